```python
import math
import jax, jax.numpy as jnp
from jax import lax
import numpy as np

D_MODEL = 1024
BATCH = 2
SEQ = 16384
DEPTH = 2
DEC_BATCH = 32
DEC_SEQ = 2048
PAST_LEN = 128

RET_HEADS = 4
RET_DK = 256
RET_DV = 512
RET_QK_W = RET_HEADS * RET_DK
RET_V_W = RET_HEADS * RET_DV
RET_CHUNK = 128
ROPE_BASE = 10000.0
LRU_WIDTH = 1280
LRU_BLOCKS = 10
LRU_BW = LRU_WIDTH // LRU_BLOCKS
LRU_C = 8.0
CONV_WIDTH = 4
CONV_LEFT = 2
DENSE_FF = 2816
N_EXPERTS = 8
TOP_K = 2
EXPERT_FF = 3584
MOE_BLOCK = 128
N_DENSE = (DEPTH + 1) // 2
N_MOE = DEPTH // 2
RMS_EPS = 1e-6
GN_EPS = 1e-5
IN_SIZES = (RET_QK_W, RET_QK_W, RET_V_W, RET_V_W, LRU_WIDTH, LRU_WIDTH, D_MODEL, D_MODEL)
IN_COLS = sum(IN_SIZES)

kernel_name = 'hybrid_bidir_retention_rglru_adaln_moe_encoder'


def rmsnorm(x, g):
    xf = x.astype(jnp.float32)
    y = xf * lax.rsqrt(jnp.mean(xf * xf, axis=-1, keepdims=True) + RMS_EPS)
    return (y * g.astype(jnp.float32)).astype(x.dtype)


def rotary(t):
    S = t.shape[1]
    half = t.shape[-1] // 2
    theta = 1.0 / (ROPE_BASE ** jnp.linspace(0.0, 1.0, half, dtype=jnp.float32))
    ang = jnp.arange(S, dtype=jnp.float32)[:, None] * theta[None, :]
    cos = jnp.cos(ang)[None, :, None, :]
    sin = jnp.sin(ang)[None, :, None, :]
    t1, t2 = t[..., :half], t[..., half:]
    return jnp.concatenate([t1 * cos - t2 * sin, t1 * sin + t2 * cos], axis=-1)


def retention_sweep(q, k, v, log_gamma, strict):
    B, S, H, DK = q.shape
    DV = v.shape[-1]
    N = S // RET_CHUNK

    def chunks(t):
        return t.reshape(B, N, RET_CHUNK, H, t.shape[-1]).transpose(1, 0, 3, 2, 4)

    idx = jnp.arange(RET_CHUNK, dtype=jnp.float32)
    diff = idx[:, None] - idx[None, :]
    mask = (diff > 0) if strict else (diff >= 0)
    decay = jnp.where(mask[None], jnp.exp(log_gamma[:, None, None] * jnp.where(mask, diff, 0.0)[None]), 0.0)
    q_dec = jnp.exp(log_gamma[:, None] * (idx + 1.0))[:, :, None]
    k_dec = jnp.exp(log_gamma[:, None] * (RET_CHUNK - 1.0 - idx))[:, :, None]
    c_dec = jnp.exp(log_gamma * RET_CHUNK)[:, None, None]

    def step(R, qkv):
        qc, kc, vc = qkv
        scores = jnp.einsum('bhid,bhjd->bhij', qc, kc) * decay
        o = jnp.einsum('bhij,bhjv->bhiv', scores, vc) + jnp.einsum('bhid,bhdv->bhiv', qc * q_dec, R)
        R = R * c_dec + jnp.einsum('bhjd,bhjv->bhdv', kc * k_dec, vc)
        return R, o

    R0 = jnp.zeros((B, H, DK, DV), jnp.float32)
    _, o = lax.scan(step, R0, (chunks(q), chunks(k), chunks(v)))
    return o.transpose(1, 0, 3, 2, 4).reshape(B, S, H, DV)


def retention_branch(q, k, v, g, w_ret_o):
    B, S, _ = q.shape
    qh = rotary(q.reshape(B, S, RET_HEADS, RET_DK).astype(jnp.float32))
    kh = rotary(k.reshape(B, S, RET_HEADS, RET_DK).astype(jnp.float32)) * (RET_DK ** -0.5)
    vh = v.reshape(B, S, RET_HEADS, RET_DV).astype(jnp.float32)
    log_gamma = jnp.log1p(-jnp.exp2(-5.0 - jnp.arange(RET_HEADS, dtype=jnp.float32)))
    fwd = retention_sweep(qh, kh, vh, log_gamma, False)
    bwd = jnp.flip(retention_sweep(jnp.flip(qh, 1), jnp.flip(kh, 1), jnp.flip(vh, 1), log_gamma, True), 1)
    o = fwd + bwd
    oc = o - jnp.mean(o, axis=-1, keepdims=True)
    o = oc * lax.rsqrt(jnp.mean(oc * oc, axis=-1, keepdims=True) + GN_EPS)
    o = o.reshape(B, S, RET_V_W).astype(g.dtype) * jax.nn.silu(g)
    return o @ w_ret_o


def linear_scan(a, u, reverse):
    def step(h, au):
        at, ut = au
        h = at * h + ut
        return h, h
    h0 = jnp.zeros((a.shape[0], a.shape[2]), jnp.float32)
    _, hs = lax.scan(step, h0, (a.swapaxes(0, 1), u.swapaxes(0, 1)), reverse=reverse)
    return hs.swapaxes(0, 1)


def rglru_bidir(x, w_rg, b_rg, w_ig, b_ig, lru_lambda):
    B, S, W = x.shape
    xb = x.reshape(B, S, LRU_BLOCKS, LRU_BW)
    r = jax.nn.sigmoid((jnp.einsum('bsni,dnij->dbsnj', xb, w_rg).reshape(2, B, S, W) + b_rg[:, None, None, :]).astype(jnp.float32))
    i = jax.nn.sigmoid((jnp.einsum('bsni,dnij->dbsnj', xb, w_ig).reshape(2, B, S, W) + b_ig[:, None, None, :]).astype(jnp.float32))
    log_a = LRU_C * r * jax.nn.log_sigmoid(lru_lambda.astype(jnp.float32))[:, None, None, :]
    a = jnp.exp(log_a)
    u = jnp.sqrt(-jnp.expm1(2.0 * log_a)) * i * x.astype(jnp.float32)[None]
    return linear_scan(a[0], u[0], False) + linear_scan(a[1], u[1], True)


def lru_branch(xl, gl, w_conv, b_conv, w_rg, b_rg, w_ig, b_ig, lru_lambda, w_lru_o):
    S = xl.shape[1]
    xp = jnp.pad(xl, ((0, 0), (CONV_LEFT, CONV_WIDTH - 1 - CONV_LEFT), (0, 0)))
    xc = b_conv + xp[:, 0:S] * w_conv[0]
    for t in range(1, CONV_WIDTH):
        xc = xc + xp[:, t:t + S] * w_conv[t]
    h = rglru_bidir(xc, w_rg, b_rg, w_ig, b_ig, lru_lambda)
    y = h.astype(xl.dtype) * jax.nn.gelu(gl, approximate=True)
    return y @ w_lru_o


def token_mixer(h, w_in, w_conv, b_conv, w_rg, b_rg, w_ig, b_ig, lru_lambda, w_ret_o, w_lru_o, w_out):
    proj = h @ w_in
    split_pts = [int(s) for s in np.cumsum(IN_SIZES)[:-1]]
    q, k, v, g, xl, gl, gate_a, gate_b = jnp.split(proj, split_pts, axis=-1)
    ya = retention_branch(q, k, v, g, w_ret_o)
    yb = lru_branch(xl, gl, w_conv, b_conv, w_rg, b_rg, w_ig, b_ig, lru_lambda, w_lru_o)
    m = jax.nn.sigmoid(gate_a) * ya + jax.nn.sigmoid(gate_b) * yb
    return m @ w_out


def swiglu(h, wg, wu, wd):
    return (jax.nn.silu(h @ wg) * (h @ wu)) @ wd


def moe_swiglu(h, w_router, wg, wu, wd):
    B, S, D = h.shape
    T = B * S
    xt = h.reshape(T, D)
    logits = (xt @ w_router).astype(jnp.float32)
    top_v, top_e = lax.top_k(logits, TOP_K)
    gates = jax.nn.softmax(top_v, axis=-1)
    e_flat = top_e.reshape(-1)
    tok_flat = jnp.repeat(jnp.arange(T), TOP_K)
    g_flat = gates.reshape(-1)
    order = jnp.argsort(e_flat)
    e_s, tok_s, g_s = e_flat[order], tok_flat[order], g_flat[order]
    counts = jnp.bincount(e_flat, length=N_EXPERTS)
    padded = ((counts + MOE_BLOCK - 1) // MOE_BLOCK) * MOE_BLOCK
    pad_end = jnp.cumsum(padded)
    pad_start = pad_end - padded
    grp_start = jnp.cumsum(counts) - counts
    n_assign = T * TOP_K
    dest = pad_start[e_s] + (jnp.arange(n_assign) - grp_start[e_s])
    P = ((n_assign + MOE_BLOCK - 1) // MOE_BLOCK) * MOE_BLOCK + N_EXPERTS * MOE_BLOCK
    n_blk = P // MOE_BLOCK
    xbuf = jnp.zeros((P, D), h.dtype).at[dest].set(xt[tok_s])
    blk_e = jnp.minimum(jnp.searchsorted(pad_end, jnp.arange(n_blk) * MOE_BLOCK, side='right'), N_EXPERTS - 1)

    def expert_block(args):
        xb, e = args
        return swiglu(xb, wg[e], wu[e], wd[e])

    ybuf = lax.map(expert_block, (xbuf.reshape(n_blk, MOE_BLOCK, D), blk_e)).reshape(P, D)
    y = jax.ops.segment_sum(ybuf[dest] * g_s[:, None].astype(h.dtype), tok_s, num_segments=T)
    return y.reshape(B, S, D)


def trunk(x, c, w_ada, b_ada, g_norm1, g_norm2, w_in, w_conv, b_conv, w_rg, b_rg, w_ig, b_ig,
          lru_lambda, w_ret_o, w_lru_o, w_out, w_ff_gate, w_ff_up, w_ff_down,
          w_router, w_e_gate, w_e_up, w_e_down, g_final):
    c_act = jax.nn.silu(c)
    for l in range(DEPTH):
        mod = (c_act @ w_ada[l] + b_ada[l])[:, None, :]
        sh1, sc1, ga1, sh2, sc2, ga2 = jnp.split(mod, 6, axis=-1)
        h = rmsnorm(x, g_norm1[l]) * (1.0 + sc1) + sh1
        x = x + ga1 * token_mixer(h, w_in[l], w_conv[l], b_conv[l], w_rg[l], b_rg[l], w_ig[l], b_ig[l],
                                  lru_lambda[l], w_ret_o[l], w_lru_o[l], w_out[l])
        h = rmsnorm(x, g_norm2[l]) * (1.0 + sc2) + sh2
        j = l // 2
        if l % 2 == 0:
            f = swiglu(h, w_ff_gate[j], w_ff_up[j], w_ff_down[j])
        else:
            f = moe_swiglu(h, w_router[j], w_e_gate[j], w_e_up[j], w_e_down[j])
        x = x + ga2 * f
    return rmsnorm(x, g_final)


def setup_inputs(seed: int = 0) -> dict:
    key = jax.random.key(seed)
    ks = jax.random.split(key, 32)
    f32 = jnp.float32

    def nrm(k, shape, fan_in, s=1.0):
        return jax.random.normal(k, shape, f32) * (s * fan_in ** -0.5)

    D = D_MODEL
    a0 = jax.random.uniform(ks[14], (DEPTH, 2, LRU_WIDTH), f32, minval=0.9, maxval=0.999)
    p = a0 ** (1.0 / LRU_C)
    lru_lambda = jnp.log(p) - jnp.log1p(-p)
    return {
        'x_prompt': jax.random.normal(ks[0], (BATCH, SEQ, D), f32),
        'x_sample': jax.random.normal(ks[1], (DEC_BATCH, DEC_SEQ, D), f32),
        'c_prompt': jax.random.normal(ks[2], (BATCH, D), f32),
        'c_sample': jax.random.normal(ks[3], (DEC_BATCH, D), f32),
        'w_ada': nrm(ks[4], (DEPTH, D, 6 * D), D, 0.5),
        'b_ada': 0.02 * jax.random.normal(ks[5], (DEPTH, 6 * D), f32),
        'g_norm1': 1.0 + 0.05 * jax.random.normal(ks[6], (DEPTH, D), f32),
        'g_norm2': 1.0 + 0.05 * jax.random.normal(ks[7], (DEPTH, D), f32),
        'w_in': nrm(ks[8], (DEPTH, D, IN_COLS), D),
        'w_conv': nrm(ks[9], (DEPTH, CONV_WIDTH, LRU_WIDTH), CONV_WIDTH),
        'b_conv': 0.02 * jax.random.normal(ks[10], (DEPTH, LRU_WIDTH), f32),
        'w_rg': nrm(ks[11], (DEPTH, 2, LRU_BLOCKS, LRU_BW, LRU_BW), LRU_BW),
        'b_rg': 0.02 * jax.random.normal(ks[12], (DEPTH, 2, LRU_WIDTH), f32),
        'w_ig': nrm(ks[13], (DEPTH, 2, LRU_BLOCKS, LRU_BW, LRU_BW), LRU_BW),
        'b_ig': 0.02 * jax.random.normal(ks[15], (DEPTH, 2, LRU_WIDTH), f32),
        'lru_lambda': lru_lambda,
        'w_ret_o': nrm(ks[16], (DEPTH, RET_V_W, D), RET_V_W),
        'w_lru_o': nrm(ks[17], (DEPTH, LRU_WIDTH, D), LRU_WIDTH),
        'w_out': nrm(ks[18], (DEPTH, D, D), D),
        'w_ff_gate': nrm(ks[19], (N_DENSE, D, DENSE_FF), D),
        'w_ff_up': nrm(ks[20], (N_DENSE, D, DENSE_FF), D),
        'w_ff_down': nrm(ks[21], (N_DENSE, DENSE_FF, D), DENSE_FF),
        'w_router': nrm(ks[22], (N_MOE, D, N_EXPERTS), D),
        'w_e_gate': nrm(ks[23], (N_MOE, N_EXPERTS, D, EXPERT_FF), D),
        'w_e_up': nrm(ks[24], (N_MOE, N_EXPERTS, D, EXPERT_FF), D),
        'w_e_down': nrm(ks[25], (N_MOE, N_EXPERTS, EXPERT_FF, D), EXPERT_FF),
        'g_final': 1.0 + 0.05 * jax.random.normal(ks[26], (D,), f32),
    }


def reference(x_prompt, x_sample, c_prompt, c_sample, w_ada, b_ada, g_norm1, g_norm2, w_in,
              w_conv, b_conv, w_rg, b_rg, w_ig, b_ig, lru_lambda, w_ret_o, w_lru_o, w_out,
              w_ff_gate, w_ff_up, w_ff_down, w_router, w_e_gate, w_e_up, w_e_down, g_final):
    weights = (w_ada, b_ada, g_norm1, g_norm2, w_in, w_conv, b_conv, w_rg, b_rg, w_ig, b_ig,
               lru_lambda, w_ret_o, w_lru_o, w_out, w_ff_gate, w_ff_up, w_ff_down,
               w_router, w_e_gate, w_e_up, w_e_down, g_final)
    y_prompt = trunk(x_prompt, c_prompt, *weights)
    y_sample = trunk(x_sample, c_sample, *weights)
    return (y_prompt, y_sample)
```

```python
import functools
import math
from typing import NamedTuple

import jax
import jax.numpy as jnp
import numpy as np
from jax import lax
from jax.experimental import pallas as pl
from jax.experimental.pallas import tpu as pltpu

F32 = jnp.float32
BF16 = jnp.bfloat16

D_MODEL = 1024
RET_HEADS = 4
RET_DK = 256
RET_DV = 512
RET_HALF = RET_DK // 2
RET_QK_W = RET_HEADS * RET_DK
RET_V_W = RET_HEADS * RET_DV
ROPE_BASE = 10000.0
LRU_WIDTH = 1280
LRU_BLOCKS = 10
LRU_BW = LRU_WIDTH // LRU_BLOCKS
LRU_C = 8.0
CONV_WIDTH = 4
CONV_LEFT = 2
N_EXPERTS = 8
TOP_K = 2
RMS_EPS = 1e-6
GN_EPS = 1e-5

V7X_LANES = 128
V7X_SUBLANES = 8
V7X_VMEM_BYTES = 64 * 1024 * 1024
VMEM_LIMIT = (V7X_VMEM_BYTES * 3) // 4

PA_W = 2 * RET_QK_W + 2 * RET_V_W + 2 * D_MODEL
PB_W = 2 * LRU_WIDTH


class Geom(NamedTuple):
    n_p: int
    s_p: int
    n_s: int
    s_s: int

    @property
    def tokens_p(self):
        return self.n_p * self.s_p

    @property
    def tokens(self):
        return self.n_p * self.s_p + self.n_s * self.s_s

    @property
    def n_seq(self):
        return self.n_p + self.n_s


def _tile_seq(g, i, tm):
    t0 = i * tm
    return jnp.where(t0 < g.tokens_p, t0 // g.s_p, g.n_p + (t0 - g.tokens_p) // g.s_s)


def _tile_pos(g, i, tm):
    t0 = i * tm
    return jnp.where(t0 < g.tokens_p, t0 % g.s_p, (t0 - g.tokens_p) % g.s_s)


def _tile_seq_len(g, i, tm):
    return jnp.where(i * tm < g.tokens_p, g.s_p, g.s_s)


def _cparams(sem):
    return pltpu.CompilerParams(dimension_semantics=sem, vmem_limit_bytes=VMEM_LIMIT)


def _silu(x):
    return x * jax.nn.sigmoid(x)


def _rms_mod(x, gvec, scale, shift):
    ms = jnp.mean(x * x, axis=-1, keepdims=True)
    y = x * lax.rsqrt(ms + RMS_EPS) * gvec
    return y * (1.0 + scale) + shift


def _mod_kernel(c_ref, w_ref, b_ref, o_ref):
    c = c_ref[...]
    o_ref[...] = jnp.dot(_silu(c), w_ref[...], preferred_element_type=F32,
                         precision=lax.Precision.HIGHEST) + b_ref[...]


def _adaln_mod(c_pad, w_ada, b_ada):
    depth, d, _ = w_ada.shape
    n = c_pad.shape[0]
    out = pl.pallas_call(
        _mod_kernel,
        grid=(depth, 6),
        in_specs=[
            pl.BlockSpec((n, d), lambda l, j: (0, 0)),
            pl.BlockSpec((None, d, d), lambda l, j: (l, 0, j)),
            pl.BlockSpec((None, None, 1, d), lambda l, j: (l, j, 0, 0)),
        ],
        out_specs=pl.BlockSpec((None, None, n, d), lambda l, j: (l, j, 0, 0)),
        out_shape=jax.ShapeDtypeStruct((depth, 6, n, d), F32),
        compiler_params=_cparams(("arbitrary", "arbitrary")),
        name="adaln_mod",
    )(c_pad, w_ada, b_ada.reshape(depth, 6, 1, d))
    return out.transpose(0, 2, 1, 3)


def _norm_proj_kernel(x_ref, mod_ref, g_ref, w_ref, o_ref, h_ref, *, shift_row, scale_row):
    @pl.when(pl.program_id(1) == 0)
    def _():
        h = _rms_mod(x_ref[...], g_ref[...], mod_ref[scale_row:scale_row + 1, :],
                     mod_ref[shift_row:shift_row + 1, :])
        h_ref[...] = h.astype(BF16)

    o_ref[...] = jnp.dot(h_ref[...], w_ref[...], preferred_element_type=F32).astype(o_ref.dtype)


def _norm_proj(geom, x, mod_l, gvec, w, *, shift_row, scale_row, tm, tn):
    t, d = x.shape
    n = w.shape[1]
    return pl.pallas_call(
        functools.partial(_norm_proj_kernel, shift_row=shift_row, scale_row=scale_row),
        grid=(t // tm, n // tn),
        in_specs=[
            pl.BlockSpec((tm, d), lambda i, j: (i, 0)),
            pl.BlockSpec((None, 6, d), lambda i, j: (_tile_seq(geom, i, tm), 0, 0)),
            pl.BlockSpec((1, d), lambda i, j: (0, 0)),
            pl.BlockSpec((d, tn), lambda i, j: (0, j)),
        ],
        out_specs=pl.BlockSpec((tm, tn), lambda i, j: (i, j)),
        out_shape=jax.ShapeDtypeStruct((t, n), F32),
        scratch_shapes=[pltpu.VMEM((tm, d), BF16)],
        compiler_params=_cparams(("parallel", "arbitrary")),
        name="norm_proj",
    )(x, mod_l, gvec, w)


def _rot_halves(ref, h, cos, sin):
    a = ref[:, h * RET_DK:h * RET_DK + RET_HALF]
    b = ref[:, h * RET_DK + RET_HALF:(h + 1) * RET_DK]
    return a * cos - b * sin, a * sin + b * cos


def _ret_bwd_state_kernel(k_ref, v_ref, cos_ref, sin_ref, kdb_ref, rb_ref, r_ref, *, geom, chunk, cdec):
    c = pl.num_programs(0) - 1 - pl.program_id(0)
    is_last = _tile_pos(geom, c, chunk) + chunk == _tile_seq_len(geom, c, chunk)

    @pl.when(is_last)
    def _():
        r_ref[...] = jnp.zeros_like(r_ref)

    cos = cos_ref[...]
    sin = sin_ref[...]
    for h in range(RET_HEADS):
        k1, k2 = _rot_halves(k_ref, h, cos, sin)
        dec = kdb_ref[h] * (RET_DK ** -0.5)
        kd = jnp.concatenate([k1 * dec, k2 * dec], axis=1).astype(BF16)
        vh = v_ref[:, h * RET_DV:(h + 1) * RET_DV].astype(BF16)
        r = r_ref[h]
        rb_ref[h] = r.astype(BF16)
        upd = lax.dot_general(kd, vh, (((0,), (0,)), ((), ())), preferred_element_type=F32)
        r_ref[h] = r * cdec[h] + upd


def _ret_main_kernel(q_ref, k_ref, v_ref, g_ref, rb_ref, cos_ref, sin_ref, dmat_ref, qdf_ref, qdb_ref,
                     kdf_ref, wo_ref, o_ref, rf_ref, *, geom, chunk, cdec):
    i = pl.program_id(0)

    @pl.when(_tile_pos(geom, i, chunk) == 0)
    def _():
        rf_ref[...] = jnp.zeros_like(rf_ref)

    cos = cos_ref[...]
    sin = sin_ref[...]
    acc = jnp.zeros(o_ref.shape, F32)
    for h in range(RET_HEADS):
        q1, q2 = _rot_halves(q_ref, h, cos, sin)
        k1, k2 = _rot_halves(k_ref, h, cos, sin)
        k1 = k1 * (RET_DK ** -0.5)
        k2 = k2 * (RET_DK ** -0.5)
        qr = jnp.concatenate([q1, q2], axis=1).astype(BF16)
        kr = jnp.concatenate([k1, k2], axis=1).astype(BF16)
        vh = v_ref[:, h * RET_DV:(h + 1) * RET_DV].astype(BF16)
        s = lax.dot_general(qr, kr, (((1,), (1,)), ((), ())), preferred_element_type=F32) * dmat_ref[h]
        o = jnp.dot(s.astype(BF16), vh, preferred_element_type=F32)
        qdf = qdf_ref[h]
        qf = jnp.concatenate([q1 * qdf, q2 * qdf], axis=1).astype(BF16)
        rf = rf_ref[h]
        o = o + jnp.dot(qf, rf.astype(BF16), preferred_element_type=F32)
        qdb = qdb_ref[h]
        qb = jnp.concatenate([q1 * qdb, q2 * qdb], axis=1).astype(BF16)
        o = o + jnp.dot(qb, rb_ref[h], preferred_element_type=F32)
        kdf = kdf_ref[h]
        kf = jnp.concatenate([k1 * kdf, k2 * kdf], axis=1).astype(BF16)
        rf_ref[h] = rf * cdec[h] + lax.dot_general(kf, vh, (((0,), (0,)), ((), ())),
                                                   preferred_element_type=F32)
        oc = o - jnp.mean(o, axis=-1, keepdims=True)
        on = oc * lax.rsqrt(jnp.mean(oc * oc, axis=-1, keepdims=True) + GN_EPS)
        og = (on * _silu(g_ref[:, h * RET_DV:(h + 1) * RET_DV])).astype(BF16)
        acc = acc + jnp.dot(og, wo_ref[h * RET_DV:(h + 1) * RET_DV, :], preferred_element_type=F32)
    o_ref[...] = acc


def _retention_tables(chunk, s_max):
    log_gamma = jnp.log1p(-jnp.exp2(-5.0 - jnp.arange(RET_HEADS, dtype=F32)))
    idx = jnp.arange(chunk, dtype=F32)
    dist = jnp.abs(idx[:, None] - idx[None, :])
    dmat = jnp.exp(log_gamma[:, None, None] * dist[None])

    def rows(e):
        return jnp.broadcast_to(jnp.exp(log_gamma[:, None] * e[None, :])[:, :, None],
                                (RET_HEADS, chunk, RET_HALF))

    qdf = rows(idx + 1.0)
    qdb = rows(chunk - idx)
    kdf = rows(chunk - 1.0 - idx)
    kdb = rows(idx)
    theta = 1.0 / (ROPE_BASE ** jnp.linspace(0.0, 1.0, RET_HALF, dtype=F32))
    ang = jnp.arange(s_max, dtype=F32)[:, None] * theta[None, :]
    return jnp.cos(ang), jnp.sin(ang), dmat, qdf, qdb, kdf, kdb


def _chunk_decay(chunk):
    lg = np.log1p(-np.exp2(-5.0 - np.arange(RET_HEADS, dtype=np.float32))).astype(np.float32)
    return tuple(float(v) for v in np.exp(lg * np.float32(chunk)).astype(np.float32))


def _retention(geom, proj_a, w_ret_o, tables, *, chunk):
    t = proj_a.shape[0]
    n_chunks = t // chunk
    cos, sin, dmat, qdf, qdb, kdf, kdb = tables
    cdec = _chunk_decay(chunk)
    state_shape = (RET_HEADS, RET_DK, RET_DV)

    def pos_blk(c):
        return _tile_pos(geom, c, chunk) // chunk

    rev = lambda i: n_chunks - 1 - i
    tab_spec = pl.BlockSpec((RET_HEADS, chunk, RET_HALF), lambda i: (0, 0, 0))
    rb = pl.pallas_call(
        functools.partial(_ret_bwd_state_kernel, geom=geom, chunk=chunk, cdec=cdec),
        grid=(n_chunks,),
        in_specs=[
            pl.BlockSpec((chunk, RET_QK_W), lambda i: (rev(i), 1)),
            pl.BlockSpec((chunk, RET_V_W), lambda i: (rev(i), 1)),
            pl.BlockSpec((chunk, RET_HALF), lambda i: (pos_blk(rev(i)), 0)),
            pl.BlockSpec((chunk, RET_HALF), lambda i: (pos_blk(rev(i)), 0)),
            tab_spec,
        ],
        out_specs=pl.BlockSpec((None,) + state_shape, lambda i: (rev(i), 0, 0, 0)),
        out_shape=jax.ShapeDtypeStruct((n_chunks,) + state_shape, BF16),
        scratch_shapes=[pltpu.VMEM(state_shape, F32)],
        compiler_params=_cparams(("arbitrary",)),
        name="ret_bwd_state",
    )(proj_a, proj_a, cos, sin, kdb)

    return pl.pallas_call(
        functools.partial(_ret_main_kernel, geom=geom, chunk=chunk, cdec=cdec),
        grid=(n_chunks,),
        in_specs=[
            pl.BlockSpec((chunk, RET_QK_W), lambda i: (i, 0)),
            pl.BlockSpec((chunk, RET_QK_W), lambda i: (i, 1)),
            pl.BlockSpec((chunk, RET_V_W), lambda i: (i, 1)),
            pl.BlockSpec((chunk, RET_V_W), lambda i: (i, 2)),
            pl.BlockSpec((None,) + state_shape, lambda i: (i, 0, 0, 0)),
            pl.BlockSpec((chunk, RET_HALF), lambda i: (pos_blk(i), 0)),
            pl.BlockSpec((chunk, RET_HALF), lambda i: (pos_blk(i), 0)),
            pl.BlockSpec((RET_HEADS, chunk, chunk), lambda i: (0, 0, 0)),
            tab_spec, tab_spec, tab_spec,
            pl.BlockSpec((RET_V_W, D_MODEL), lambda i: (0, 0)),
        ],
        out_specs=pl.BlockSpec((chunk, D_MODEL), lambda i: (i, 0)),
        out_shape=jax.ShapeDtypeStruct((t, D_MODEL), F32),
        scratch_shapes=[pltpu.VMEM(state_shape, F32)],
        compiler_params=_cparams(("arbitrary",)),
        name="ret_main",
    )(proj_a, proj_a, proj_a, proj_a, rb, cos, sin, dmat, qdf, qdb, kdf, w_ret_o)


def _log_sigmoid(x):
    return jnp.minimum(x, 0.0) - jnp.log1p(jnp.exp(-jnp.abs(x)))


def _lru_conv(cur, prev8, next8, wconv_ref, bconv_ref):
    tl = cur.shape[0]
    row8 = lax.broadcasted_iota(jnp.int32, (V7X_SUBLANES, cur.shape[1]), 0)
    acc = bconv_ref[...] + cur * wconv_ref[CONV_LEFT:CONV_LEFT + 1, :]
    for j in range(CONV_WIDTH):
        off = j - CONV_LEFT
        if off == 0:
            continue
        rolled = pltpu.roll(cur, (-off) % tl, axis=0)
        if off < 0:
            halo = pltpu.roll(prev8, (-off) % V7X_SUBLANES, axis=0)
            head = jnp.where(row8 < -off, halo, rolled[:V7X_SUBLANES])
            shifted = jnp.concatenate([head, rolled[V7X_SUBLANES:]], axis=0)
        else:
            halo = pltpu.roll(next8, (-off) % V7X_SUBLANES, axis=0)
            tail = jnp.where(row8 >= V7X_SUBLANES - off, halo, rolled[tl - V7X_SUBLANES:])
            shifted = jnp.concatenate([rolled[:tl - V7X_SUBLANES], tail], axis=0)
        acc = acc + shifted * wconv_ref[j:j + 1, :]
    return acc


def _lru_gates(xc, d, wrg_ref, brg_ref, wig_ref, big_ref, lam_ref, a_ref, u_ref):
    xcb = xc.astype(BF16)
    ls = _log_sigmoid(lam_ref[d])
    for n in range(LRU_BLOCKS):
        sl = slice(n * LRU_BW, (n + 1) * LRU_BW)
        xb = xcb[:, sl]
        r = jax.nn.sigmoid(jnp.dot(xb, wrg_ref[d, n], preferred_element_type=F32) + brg_ref[d][:, sl])
        ig = jax.nn.sigmoid(jnp.dot(xb, wig_ref[d, n], preferred_element_type=F32) + big_ref[d][:, sl])
        log_a = LRU_C * r * ls[:, sl]
        a = jnp.exp(log_a)
        a_ref[:, sl] = a
        u_ref[:, sl] = jnp.sqrt(1.0 - a * a) * ig * xc[:, sl]


def _lru_scan_rows(a_ref, u_ref, o_ref, h0, reverse):
    tl = a_ref.shape[0]

    def body(s, h):
        t = tl - 1 - s if reverse else s
        h = a_ref[pl.ds(t, 1), :] * h + u_ref[pl.ds(t, 1), :]
        o_ref[pl.ds(t, 1), :] = h
        return h

    return lax.fori_loop(0, tl, body, h0, unroll=8)


def _lru_kernel(xf_ref, xfp_ref, xfn_ref, xb_ref, xbp_ref, xbn_ref, wconv_ref, bconv_ref, wrg_ref, brg_ref,
                wig_ref, big_ref, lam_ref, hf_ref, hb_ref, a_ref, u_ref, cf_ref, cb_ref, *, geom, tl):
    i = pl.program_id(0)
    ib = pl.num_programs(0) - 1 - i

    def flags(tile):
        pos = _tile_pos(geom, tile, tl)
        return pos == 0, pos + tl == _tile_seq_len(geom, tile, tl)

    def keep(flag):
        return jnp.where(flag, 0.0, 1.0)

    first, last = flags(i)

    @pl.when(first)
    def _():
        cf_ref[...] = jnp.zeros_like(cf_ref)

    xc = _lru_conv(xf_ref[...], xfp_ref[...] * keep(first), xfn_ref[...] * keep(last), wconv_ref, bconv_ref)
    _lru_gates(xc, 0, wrg_ref, brg_ref, wig_ref, big_ref, lam_ref, a_ref, u_ref)
    cf_ref[0:1, :] = _lru_scan_rows(a_ref, u_ref, hf_ref, cf_ref[0:1, :], False)

    first, last = flags(ib)

    @pl.when(last)
    def _():
        cb_ref[...] = jnp.zeros_like(cb_ref)

    xc = _lru_conv(xb_ref[...], xbp_ref[...] * keep(first), xbn_ref[...] * keep(last), wconv_ref, bconv_ref)
    _lru_gates(xc, 1, wrg_ref, brg_ref, wig_ref, big_ref, lam_ref, a_ref, u_ref)
    cb_ref[0:1, :] = _lru_scan_rows(a_ref, u_ref, hb_ref, cb_ref[0:1, :], True)


def _lru(geom, proj_b, w_conv, b_conv, w_rg, b_rg, w_ig, b_ig, lam, *, tl):
    t = proj_b.shape[0]
    n = t // tl
    r8 = tl // V7X_SUBLANES
    n8 = t // V7X_SUBLANES
    w = LRU_WIDTH
    rev = lambda i: n - 1 - i
    prev_blk = lambda i: jnp.maximum(i * r8 - 1, 0)
    next_blk = lambda i: jnp.minimum((i + 1) * r8, n8 - 1)
    full = lambda a: pl.BlockSpec(a.shape, lambda i: (0,) * a.ndim)
    b_conv2 = b_conv.reshape(1, w)
    b_rg3 = b_rg.reshape(2, 1, w)
    b_ig3 = b_ig.reshape(2, 1, w)
    lam3 = lam.reshape(2, 1, w)
    return pl.pallas_call(
        functools.partial(_lru_kernel, geom=geom, tl=tl),
        grid=(n,),
        in_specs=[
            pl.BlockSpec((tl, w), lambda i: (i, 0)),
            pl.BlockSpec((V7X_SUBLANES, w), lambda i: (prev_blk(i), 0)),
            pl.BlockSpec((V7X_SUBLANES, w), lambda i: (next_blk(i), 0)),
            pl.BlockSpec((tl, w), lambda i: (rev(i), 0)),
            pl.BlockSpec((V7X_SUBLANES, w), lambda i: (prev_blk(rev(i)), 0)),
            pl.BlockSpec((V7X_SUBLANES, w), lambda i: (next_blk(rev(i)), 0)),
            full(w_conv), full(b_conv2), full(w_rg), full(b_rg3), full(w_ig), full(b_ig3), full(lam3),
        ],
        out_specs=[
            pl.BlockSpec((tl, w), lambda i: (i, 0)),
            pl.BlockSpec((tl, w), lambda i: (rev(i), 0)),
        ],
        out_shape=[jax.ShapeDtypeStruct((t, w), F32), jax.ShapeDtypeStruct((t, w), F32)],
        scratch_shapes=[pltpu.VMEM((tl, w), F32), pltpu.VMEM((tl, w), F32),
                        pltpu.VMEM((V7X_SUBLANES, w), F32), pltpu.VMEM((V7X_SUBLANES, w), F32)],
        compiler_params=_cparams(("arbitrary",)),
        name="lru_scan",
    )(proj_b, proj_b, proj_b, proj_b, proj_b, proj_b, w_conv, b_conv2, w_rg, b_rg3, w_ig, b_ig3, lam3)


def _gelu_tanh(x):
    return 0.5 * x * (1.0 + jnp.tanh(math.sqrt(2.0 / math.pi) * (x + 0.044715 * (x * x * x))))


def _merge_kernel(x_ref, ya_ref, hf_ref, hb_ref, gl_ref, ga_ref, gb_ref, mod_ref, wl_ref, wo_ref, o_ref):
    y = ((hf_ref[...] + hb_ref[...]) * _gelu_tanh(gl_ref[...])).astype(BF16)
    yb = jnp.dot(y, wl_ref[...], preferred_element_type=F32)
    m = jax.nn.sigmoid(ga_ref[...]) * ya_ref[...] + jax.nn.sigmoid(gb_ref[...]) * yb
    mix = jnp.dot(m.astype(BF16), wo_ref[...], preferred_element_type=F32)
    o_ref[...] = x_ref[...] + mod_ref[2:3, :] * mix


def _merge(geom, x, ya, hf, hb, proj_a, proj_b, mod_l, w_lru_o, w_out, *, tm):
    t, d = x.shape
    gate_a_blk = (2 * RET_QK_W + 2 * RET_V_W) // d
    tok = lambda w: pl.BlockSpec((tm, w), lambda i: (i, 0))
    return pl.pallas_call(
        _merge_kernel,
        grid=(t // tm,),
        in_specs=[
            tok(d), tok(d), tok(LRU_WIDTH), tok(LRU_WIDTH),
            pl.BlockSpec((tm, LRU_WIDTH), lambda i: (i, 1)),
            pl.BlockSpec((tm, d), lambda i: (i, gate_a_blk)),
            pl.BlockSpec((tm, d), lambda i: (i, gate_a_blk + 1)),
            pl.BlockSpec((None, 6, d), lambda i: (_tile_seq(geom, i, tm), 0, 0)),
            pl.BlockSpec((LRU_WIDTH, d), lambda i: (0, 0)),
            pl.BlockSpec((d, d), lambda i: (0, 0)),
        ],
        out_specs=tok(d),
        out_shape=jax.ShapeDtypeStruct((t, d), F32),
        compiler_params=_cparams(("parallel",)),
        name="merge",
    )(x, ya, hf, hb, proj_b, proj_a, proj_a, mod_l, w_lru_o, w_out)


def _ffn_kernel(x_ref, mod_ref, g_ref, wg_ref, wu_ref, wd_ref, o_ref, h_ref, acc_ref):
    j = pl.program_id(1)

    @pl.when(j == 0)
    def _():
        h_ref[...] = _rms_mod(x_ref[...], g_ref[...], mod_ref[4:5, :], mod_ref[3:4, :]).astype(BF16)
        acc_ref[...] = jnp.zeros_like(acc_ref)

    h = h_ref[...]
    a = jnp.dot(h, wg_ref[...], preferred_element_type=F32)
    u = jnp.dot(h, wu_ref[...], preferred_element_type=F32)
    acc_ref[...] += jnp.dot((_silu(a) * u).astype(BF16), wd_ref[...], preferred_element_type=F32)

    @pl.when(j == pl.num_programs(1) - 1)
    def _():
        o_ref[...] = x_ref[...] + mod_ref[5:6, :] * acc_ref[...]


def _ffn(geom, x, mod_l, gvec, wg, wu, wd, *, tm, tf):
    t, d = x.shape
    ff = wg.shape[1]
    return pl.pallas_call(
        _ffn_kernel,
        grid=(t // tm, ff // tf),
        in_specs=[
            pl.BlockSpec((tm, d), lambda i, j: (i, 0)),
            pl.BlockSpec((None, 6, d), lambda i, j: (_tile_seq(geom, i, tm), 0, 0)),
            pl.BlockSpec((1, d), lambda i, j: (0, 0)),
            pl.BlockSpec((d, tf), lambda i, j: (0, j)),
            pl.BlockSpec((d, tf), lambda i, j: (0, j)),
            pl.BlockSpec((tf, d), lambda i, j: (j, 0)),
        ],
        out_specs=pl.BlockSpec((tm, d), lambda i, j: (i, 0)),
        out_shape=jax.ShapeDtypeStruct((t, d), F32),
        scratch_shapes=[pltpu.VMEM((tm, d), BF16), pltpu.VMEM((tm, d), F32)],
        compiler_params=_cparams(("parallel", "arbitrary")),
        name="ffn_dense",
    )(x, mod_l, gvec, wg, wu, wd)


ROUTE_E0, ROUTE_E1, ROUTE_R0, ROUTE_R1, ROUTE_G0, ROUTE_G1 = range(6)


def _router_kernel(x_ref, mod_ref, g_ref, wr_ref, h_ref, route_ref, cnt_ref, carry_ref):
    i = pl.program_id(0)
    tm = x_ref.shape[0]

    @pl.when(i == 0)
    def _():
        carry_ref[...] = jnp.zeros_like(carry_ref)

    h = _rms_mod(x_ref[...], g_ref[...], mod_ref[4:5, :], mod_ref[3:4, :])
    h_ref[...] = h
    logits = jnp.dot(h, wr_ref[...], preferred_element_type=F32, precision=lax.Precision.HIGHEST)
    lane = lax.broadcasted_iota(jnp.int32, logits.shape, 1).astype(F32)
    logits = jnp.where(lane < N_EXPERTS, logits, -jnp.inf)
    m1 = jnp.max(logits, axis=-1, keepdims=True)
    i1 = jnp.min(jnp.where(logits == m1, lane, float(V7X_LANES)), axis=-1, keepdims=True)
    rest = jnp.where(lane == i1, -jnp.inf, logits)
    m2 = jnp.max(rest, axis=-1, keepdims=True)
    i2 = jnp.min(jnp.where(rest == m2, lane, float(V7X_LANES)), axis=-1, keepdims=True)
    ex = jnp.exp(m2 - m1)
    g1 = 1.0 / (1.0 + ex)
    g2 = ex / (1.0 + ex)
    sel1 = lane == i1
    sel2 = lane == i2
    onehot = jnp.where(sel1 | sel2, 1.0, 0.0)
    row = lax.broadcasted_iota(jnp.int32, (tm, tm), 0)
    col = lax.broadcasted_iota(jnp.int32, (tm, tm), 1)
    lower = jnp.where(col < row, 1.0, 0.0).astype(BF16)
    before = jnp.dot(lower, onehot.astype(BF16), preferred_element_type=F32) + carry_ref[0:1, :]
    r1 = jnp.sum(jnp.where(sel1, before, 0.0), axis=-1, keepdims=True)
    r2 = jnp.sum(jnp.where(sel2, before, 0.0), axis=-1, keepdims=True)
    out_lane = lax.broadcasted_iota(jnp.int32, route_ref.shape, 1)
    vals = (i1.astype(F32), i2.astype(F32), r1, r2, g1, g2)
    packed = jnp.zeros(route_ref.shape, F32)
    for slot, v in enumerate(vals):
        packed = jnp.where(out_lane == slot, v, packed)
    route_ref[...] = packed
    carry = carry_ref[0:1, :] + jnp.sum(onehot, axis=0, keepdims=True)
    carry_ref[0:1, :] = carry
    cnt_ref[...] = jnp.broadcast_to(carry, cnt_ref.shape).astype(jnp.int32)


def _router(geom, x, mod_l, gvec, w_router, *, tm):
    t, d = x.shape
    w_pad = jnp.pad(w_router, ((0, 0), (0, V7X_LANES - N_EXPERTS)))
    return pl.pallas_call(
        _router_kernel,
        grid=(t // tm,),
        in_specs=[
            pl.BlockSpec((tm, d), lambda i: (i, 0)),
            pl.BlockSpec((None, 6, d), lambda i: (_tile_seq(geom, i, tm), 0, 0)),
            pl.BlockSpec((1, d), lambda i: (0, 0)),
            pl.BlockSpec((d, V7X_LANES), lambda i: (0, 0)),
        ],
        out_specs=[
            pl.BlockSpec((tm, d), lambda i: (i, 0)),
            pl.BlockSpec((tm, V7X_LANES), lambda i: (i, 0)),
            pl.BlockSpec((V7X_SUBLANES, V7X_LANES), lambda i: (0, 0)),
        ],
        out_shape=[
            jax.ShapeDtypeStruct((t, d), F32),
            jax.ShapeDtypeStruct((t, V7X_LANES), F32),
            jax.ShapeDtypeStruct((V7X_SUBLANES, V7X_LANES), jnp.int32),
        ],
        scratch_shapes=[pltpu.VMEM((V7X_SUBLANES, V7X_LANES), F32)],
        compiler_params=_cparams(("arbitrary",)),
        name="router",
    )(x, mod_l, gvec, w_pad)


def _row_copy(src_hbm, src_row, dst_hbm, dst_row, sem):
    return pltpu.make_async_copy(src_hbm.at[pl.ds(src_row, 1), :], dst_hbm.at[pl.ds(dst_row, 1), :], sem)


def _dispatch_kernel(dest_ref, src_hbm, init_hbm, out_hbm, sem, *, td):
    del init_hbm
    base = pl.program_id(0) * td

    def issue(t, carry):
        for k in range(TOP_K):
            _row_copy(src_hbm, base + t, out_hbm, dest_ref[k, t], sem).start()
        return carry

    lax.fori_loop(0, td, issue, 0)

    def drain(t, carry):
        for k in range(TOP_K):
            _row_copy(src_hbm, base + t, out_hbm, dest_ref[k, t], sem).wait()
        return carry

    lax.fori_loop(0, td, drain, 0)


def _dispatch(h2, dest, n_rows, *, td):
    t, d = h2.shape
    init = jnp.zeros((n_rows, d), h2.dtype)
    return pl.pallas_call(
        functools.partial(_dispatch_kernel, td=td),
        grid=(t // td,),
        in_specs=[
            pl.BlockSpec((TOP_K, td), lambda i: (0, i), memory_space=pltpu.SMEM),
            pl.BlockSpec(memory_space=pl.ANY),
            pl.BlockSpec(memory_space=pl.ANY),
        ],
        out_specs=pl.BlockSpec(memory_space=pl.ANY),
        out_shape=jax.ShapeDtypeStruct((n_rows, d), h2.dtype),
        scratch_shapes=[pltpu.SemaphoreType.DMA(())],
        input_output_aliases={2: 0},
        compiler_params=_cparams(("arbitrary",)),
        name="moe_dispatch",
    )(dest, h2, init)


def _combine_kernel(dest_ref, src_hbm, out_hbm, sem, *, td):
    base = pl.program_id(0) * td

    def issue(t, carry):
        for k in range(TOP_K):
            _row_copy(src_hbm, dest_ref[k, t], out_hbm.at[k], base + t, sem).start()
        return carry

    lax.fori_loop(0, td, issue, 0)

    def drain(t, carry):
        for k in range(TOP_K):
            _row_copy(src_hbm, dest_ref[k, t], out_hbm.at[k], base + t, sem).wait()
        return carry

    lax.fori_loop(0, td, drain, 0)


def _combine(ybuf, dest, *, td):
    t = dest.shape[1]
    d = ybuf.shape[1]
    return pl.pallas_call(
        functools.partial(_combine_kernel, td=td),
        grid=(t // td,),
        in_specs=[
            pl.BlockSpec((TOP_K, td), lambda i: (0, i), memory_space=pltpu.SMEM),
            pl.BlockSpec(memory_space=pl.ANY),
        ],
        out_specs=pl.BlockSpec(memory_space=pl.ANY),
        out_shape=jax.ShapeDtypeStruct((TOP_K, t, d), ybuf.dtype),
        scratch_shapes=[pltpu.SemaphoreType.DMA(())],
        compiler_params=_cparams(("arbitrary",)),
        name="moe_combine",
    )(dest, ybuf)


def _experts_kernel(blk_e_ref, nused_ref, x_ref, wg_ref, wu_ref, wd_ref, o_ref, xb_ref, acc_ref):
    del blk_e_ref
    b = pl.program_id(0)
    j = pl.program_id(1)

    @pl.when(b < nused_ref[0])
    def _():
        @pl.when(j == 0)
        def _():
            xb_ref[...] = x_ref[...].astype(BF16)
            acc_ref[...] = jnp.zeros_like(acc_ref)

        xb = xb_ref[...]
        a = jnp.dot(xb, wg_ref[...], preferred_element_type=F32)
        u = jnp.dot(xb, wu_ref[...], preferred_element_type=F32)
        acc_ref[...] += jnp.dot((_silu(a) * u).astype(BF16), wd_ref[...], preferred_element_type=F32)

        @pl.when(j == pl.num_programs(1) - 1)
        def _():
            o_ref[...] = acc_ref[...]


def _experts(xbuf, blk_e, n_used, wg, wu, wd, *, bm, tf):
    p, d = xbuf.shape
    ff = wg.shape[2]
    nj = ff // tf

    def blk(b, nu):
        return jnp.minimum(b, nu[0] - 1)

    def ffc(b, j, nu):
        return jnp.where(b < nu[0], j, nj - 1)

    grid_spec = pltpu.PrefetchScalarGridSpec(
        num_scalar_prefetch=2,
        grid=(p // bm, nj),
        in_specs=[
            pl.BlockSpec((bm, d), lambda b, j, be, nu: (blk(b, nu), 0)),
            pl.BlockSpec((None, d, tf), lambda b, j, be, nu: (be[blk(b, nu)], 0, ffc(b, j, nu))),
            pl.BlockSpec((None, d, tf), lambda b, j, be, nu: (be[blk(b, nu)], 0, ffc(b, j, nu))),
            pl.BlockSpec((None, tf, d), lambda b, j, be, nu: (be[blk(b, nu)], ffc(b, j, nu), 0)),
        ],
        out_specs=pl.BlockSpec((bm, d), lambda b, j, be, nu: (blk(b, nu), 0)),
        scratch_shapes=[pltpu.VMEM((bm, d), BF16), pltpu.VMEM((bm, d), F32)],
    )
    return pl.pallas_call(
        _experts_kernel,
        grid_spec=grid_spec,
        out_shape=jax.ShapeDtypeStruct((p, d), F32),
        compiler_params=_cparams(("arbitrary", "arbitrary")),
        name="moe_experts",
    )(blk_e, n_used, xbuf, wg, wu, wd)


def _moe_out_kernel(x_ref, y_ref, route_ref, mod_ref, gf_ref, o_ref, *, final_norm):
    g1 = route_ref[:, ROUTE_G0:ROUTE_G0 + 1]
    g2 = route_ref[:, ROUTE_G1:ROUTE_G1 + 1]
    x = x_ref[...] + mod_ref[5:6, :] * (y_ref[0] * g1 + y_ref[1] * g2)
    if final_norm:
        ms = jnp.mean(x * x, axis=-1, keepdims=True)
        x = x * lax.rsqrt(ms + RMS_EPS) * gf_ref[...]
    o_ref[...] = x


def _moe_out(geom, x, y2, route, mod_l, g_final, *, tm, final_norm):
    t, d = x.shape
    return pl.pallas_call(
        functools.partial(_moe_out_kernel, final_norm=final_norm),
        grid=(t // tm,),
        in_specs=[
            pl.BlockSpec((tm, d), lambda i: (i, 0)),
            pl.BlockSpec((TOP_K, tm, d), lambda i: (0, i, 0)),
            pl.BlockSpec((tm, V7X_LANES), lambda i: (i, 0)),
            pl.BlockSpec((None, 6, d), lambda i: (_tile_seq(geom, i, tm), 0, 0)),
            pl.BlockSpec((1, d), lambda i: (0, 0)),
        ],
        out_specs=pl.BlockSpec((tm, d), lambda i: (i, 0)),
        out_shape=jax.ShapeDtypeStruct((t, d), F32),
        compiler_params=_cparams(("parallel",)),
        name="moe_out",
    )(x, y2, route, mod_l, g_final)


def _final_norm_kernel(x_ref, g_ref, o_ref):
    x = x_ref[...]
    o_ref[...] = x * lax.rsqrt(jnp.mean(x * x, axis=-1, keepdims=True) + RMS_EPS) * g_ref[...]


def _final_norm(x, g_final, *, tm):
    t, d = x.shape
    return pl.pallas_call(
        _final_norm_kernel,
        grid=(t // tm,),
        in_specs=[pl.BlockSpec((tm, d), lambda i: (i, 0)), pl.BlockSpec((1, d), lambda i: (0, 0))],
        out_specs=pl.BlockSpec((tm, d), lambda i: (i, 0)),
        out_shape=jax.ShapeDtypeStruct((t, d), F32),
        compiler_params=_cparams(("parallel",)),
        name="final_norm",
    )(x, g_final)


def _moe(geom, x, mod_l, gvec, w_router, wg, wu, wd, g_final, *, tiles, final_norm):
    t, d = x.shape
    bm = tiles["moe_rows"]
    h2, route, counts = _router(geom, x, mod_l, gvec, w_router, tm=tiles["router"])
    counts = counts[0, :N_EXPERTS]
    padded = ((counts + bm - 1) // bm) * bm
    pad_end = jnp.cumsum(padded)
    pad_start = pad_end - padded
    experts = route[:, ROUTE_E0:ROUTE_E1 + 1].astype(jnp.int32)
    ranks = route[:, ROUTE_R0:ROUTE_R1 + 1].astype(jnp.int32)
    dest = (pad_start[experts] + ranks).T
    n_rows = t * TOP_K + N_EXPERTS * bm
    n_blk = n_rows // bm
    blk_e = jnp.minimum(jnp.searchsorted(pad_end, jnp.arange(n_blk, dtype=jnp.int32) * bm, side="right"),
                        N_EXPERTS - 1).astype(jnp.int32)
    n_used = (pad_end[-1:] // bm).astype(jnp.int32)
    xbuf = _dispatch(h2, dest, n_rows, td=tiles["dma_rows"])
    ybuf = _experts(xbuf, blk_e, n_used, wg, wu, wd, bm=bm, tf=tiles["expert_ff"])
    y2 = _combine(ybuf, dest, td=tiles["dma_rows"])
    return _moe_out(geom, x, y2, route, mod_l, g_final, tm=tiles["token"], final_norm=final_norm)


def _pick_tiles(geom):
    s = math.gcd(geom.s_p, geom.s_s) if geom.n_p and geom.n_s else (geom.s_p if geom.n_p else geom.s_s)
    return {
        "proj": min(1024, s),
        "proj_cols": 1024,
        "ret_chunk": min(256, s),
        "lru": min(512, s),
        "token": min(512, s),
        "ffn_cols": 1408,
        "router": min(512, s),
        "moe_rows": 512,
        "expert_ff": 896,
        "dma_rows": min(2048, s),
    }


def _trunk(geom, x, c_all, w_ada, b_ada, g_norm1, g_norm2, w_in, w_conv, b_conv, w_rg, b_rg, w_ig, b_ig,
           lru_lambda, w_ret_o, w_lru_o, w_out, w_ff_gate, w_ff_up, w_ff_down,
           w_router, w_e_gate, w_e_up, w_e_down, g_final, tiles):
    depth = w_in.shape[0]
    d = D_MODEL
    n_pad = -(-geom.n_seq // V7X_SUBLANES) * V7X_SUBLANES
    c_pad = jnp.pad(c_all, ((0, n_pad - geom.n_seq), (0, 0)))
    mod = _adaln_mod(c_pad, w_ada, b_ada)
    tables = _retention_tables(tiles["ret_chunk"], max(geom.s_p if geom.n_p else 0, geom.s_s if geom.n_s else 0))

    o_xl = 2 * RET_QK_W + 2 * RET_V_W
    o_ga = o_xl + 2 * LRU_WIDTH
    for l in range(depth):
        wl = w_in[l]
        w_a = jnp.concatenate([wl[:, :o_xl], wl[:, o_ga:]], axis=1).astype(BF16)
        w_b = wl[:, o_xl:o_ga].astype(BF16)
        g1 = g_norm1[l].reshape(1, d)
        g2 = g_norm2[l].reshape(1, d)
        proj_a = _norm_proj(geom, x, mod[l], g1, w_a, shift_row=0, scale_row=1,
                            tm=tiles["proj"], tn=tiles["proj_cols"])
        proj_b = _norm_proj(geom, x, mod[l], g1, w_b, shift_row=0, scale_row=1,
                            tm=tiles["proj"], tn=LRU_WIDTH)
        ya = _retention(geom, proj_a, w_ret_o[l].astype(BF16), tables, chunk=tiles["ret_chunk"])
        hf, hb = _lru(geom, proj_b, w_conv[l], b_conv[l], w_rg[l].astype(BF16), b_rg[l],
                      w_ig[l].astype(BF16), b_ig[l], lru_lambda[l], tl=tiles["lru"])
        x = _merge(geom, x, ya, hf, hb, proj_a, proj_b, mod[l], w_lru_o[l].astype(BF16),
                   w_out[l].astype(BF16), tm=tiles["token"])
        j = l // 2
        last = l == depth - 1
        if l % 2 == 0:
            x = _ffn(geom, x, mod[l], g2, w_ff_gate[j].astype(BF16), w_ff_up[j].astype(BF16),
                     w_ff_down[j].astype(BF16), tm=tiles["token"], tf=tiles["ffn_cols"])
            if last:
                x = _final_norm(x, g_final.reshape(1, d), tm=tiles["token"])
        else:
            x = _moe(geom, x, mod[l], g2, w_router[j], w_e_gate[j].astype(BF16), w_e_up[j].astype(BF16),
                     w_e_down[j].astype(BF16), g_final.reshape(1, d), tiles=tiles, final_norm=last)
    return x


def kernel(x_prompt, x_sample, c_prompt, c_sample, w_ada, b_ada, g_norm1, g_norm2, w_in, w_conv, b_conv, w_rg, b_rg, w_ig, b_ig, lru_lambda, w_ret_o, w_lru_o, w_out, w_ff_gate, w_ff_up, w_ff_down, w_router, w_e_gate, w_e_up, w_e_down, g_final):
    n_p, s_p, d = x_prompt.shape
    n_s, s_s, _ = x_sample.shape
    geom = Geom(n_p, s_p, n_s, s_s)
    x = jnp.concatenate([x_prompt.reshape(-1, d), x_sample.reshape(-1, d)], axis=0)
    c_all = jnp.concatenate([c_prompt, c_sample], axis=0)
    y = _trunk(geom, x, c_all, w_ada, b_ada, g_norm1, g_norm2, w_in, w_conv, b_conv, w_rg, b_rg, w_ig, b_ig,
               lru_lambda, w_ret_o, w_lru_o, w_out, w_ff_gate, w_ff_up, w_ff_down,
               w_router, w_e_gate, w_e_up, w_e_down, g_final, _pick_tiles(geom))
    return (y[:geom.tokens_p].reshape(n_p, s_p, d), y[geom.tokens_p:].reshape(n_s, s_s, d))
```

```python
import functools
import math
from typing import NamedTuple

import jax
import jax.numpy as jnp
import numpy as np
from jax import lax
from jax.experimental import pallas as pl
from jax.experimental.pallas import tpu as pltpu

F32 = jnp.float32
BF16 = jnp.bfloat16

D_MODEL = 1024
RET_HEADS = 4
RET_DK = 256
RET_DV = 512
RET_HALF = RET_DK // 2
RET_QK_W = RET_HEADS * RET_DK
RET_V_W = RET_HEADS * RET_DV
ROPE_BASE = 10000.0
LRU_WIDTH = 1280
LRU_BLOCKS = 10
LRU_BW = LRU_WIDTH // LRU_BLOCKS
LRU_C = 8.0
CONV_WIDTH = 4
CONV_LEFT = 2
N_EXPERTS = 8
TOP_K = 2
RMS_EPS = 1e-6
GN_EPS = 1e-5

V7X_LANES = 128
V7X_SUBLANES = 8
V7X_VMEM_BYTES = 64 * 1024 * 1024
VMEM_LIMIT = (V7X_VMEM_BYTES * 3) // 4

PA_W = 2 * RET_QK_W + 2 * RET_V_W + 2 * D_MODEL
PB_W = 2 * LRU_WIDTH


class Geom(NamedTuple):
    n_p: int
    s_p: int
    n_s: int
    s_s: int

    @property
    def tokens_p(self):
        return self.n_p * self.s_p

    @property
    def tokens(self):
        return self.n_p * self.s_p + self.n_s * self.s_s

    @property
    def n_seq(self):
        return self.n_p + self.n_s


def _tile_seq(g, i, tm):
    t0 = i * tm
    return jnp.where(t0 < g.tokens_p, t0 // g.s_p, g.n_p + (t0 - g.tokens_p) // g.s_s)


def _tile_pos(g, i, tm):
    t0 = i * tm
    return jnp.where(t0 < g.tokens_p, t0 % g.s_p, (t0 - g.tokens_p) % g.s_s)


def _tile_seq_len(g, i, tm):
    return jnp.where(i * tm < g.tokens_p, g.s_p, g.s_s)


def _cparams(sem):
    return pltpu.CompilerParams(dimension_semantics=sem, vmem_limit_bytes=VMEM_LIMIT)


def _silu(x):
    return x * jax.nn.sigmoid(x)


def _rms_mod(x, gvec, scale, shift):
    ms = jnp.mean(x * x, axis=-1, keepdims=True)
    y = x * lax.rsqrt(ms + RMS_EPS) * gvec
    return y * (1.0 + scale) + shift


def _mod_kernel(c_ref, w_ref, b_ref, o_ref):
    c = c_ref[...]
    o_ref[...] = jnp.dot(_silu(c), w_ref[...], preferred_element_type=F32,
                         precision=lax.Precision.HIGHEST) + b_ref[...]


def _adaln_mod(c_pad, w_ada, b_ada):
    depth, d, _ = w_ada.shape
    n = c_pad.shape[0]
    out = pl.pallas_call(
        _mod_kernel,
        grid=(depth, 6),
        in_specs=[
            pl.BlockSpec((n, d), lambda l, j: (0, 0)),
            pl.BlockSpec((None, d, d), lambda l, j: (l, 0, j)),
            pl.BlockSpec((None, None, 1, d), lambda l, j: (l, j, 0, 0)),
        ],
        out_specs=pl.BlockSpec((None, None, n, d), lambda l, j: (l, j, 0, 0)),
        out_shape=jax.ShapeDtypeStruct((depth, 6, n, d), F32),
        compiler_params=_cparams(("arbitrary", "arbitrary")),
        name="adaln_mod",
    )(c_pad, w_ada, b_ada.reshape(depth, 6, 1, d))
    return out.transpose(0, 2, 1, 3)


def _norm_proj_kernel(x_ref, mod_ref, g_ref, w_ref, o_ref, h_ref, *, shift_row, scale_row):
    @pl.when(pl.program_id(1) == 0)
    def _():
        h = _rms_mod(x_ref[...], g_ref[...], mod_ref[scale_row:scale_row + 1, :],
                     mod_ref[shift_row:shift_row + 1, :])
        h_ref[...] = h.astype(BF16)

    o_ref[...] = jnp.dot(h_ref[...], w_ref[...], preferred_element_type=F32).astype(o_ref.dtype)


def _norm_proj(geom, x, mod_l, gvec, w, *, shift_row, scale_row, tm, tn):
    t, d = x.shape
    n = w.shape[1]
    return pl.pallas_call(
        functools.partial(_norm_proj_kernel, shift_row=shift_row, scale_row=scale_row),
        grid=(t // tm, n // tn),
        in_specs=[
            pl.BlockSpec((tm, d), lambda i, j: (i, 0)),
            pl.BlockSpec((None, 6, d), lambda i, j: (_tile_seq(geom, i, tm), 0, 0)),
            pl.BlockSpec((1, d), lambda i, j: (0, 0)),
            pl.BlockSpec((d, tn), lambda i, j: (0, j)),
        ],
        out_specs=pl.BlockSpec((tm, tn), lambda i, j: (i, j)),
        out_shape=jax.ShapeDtypeStruct((t, n), F32),
        scratch_shapes=[pltpu.VMEM((tm, d), BF16)],
        compiler_params=_cparams(("parallel", "arbitrary")),
        name="norm_proj",
    )(x, mod_l, gvec, w)


def _rot_halves(ref, h, cos, sin):
    a = ref[:, h * RET_DK:h * RET_DK + RET_HALF]
    b = ref[:, h * RET_DK + RET_HALF:(h + 1) * RET_DK]
    return a * cos - b * sin, a * sin + b * cos


def _ret_bwd_state_kernel(k_ref, v_ref, cos_ref, sin_ref, kdb_ref, rb_ref, r_ref, *, geom, chunk, cdec):
    c = pl.num_programs(0) - 1 - pl.program_id(0)
    is_last = _tile_pos(geom, c, chunk) + chunk == _tile_seq_len(geom, c, chunk)

    @pl.when(is_last)
    def _():
        r_ref[...] = jnp.zeros_like(r_ref)

    cos = cos_ref[...]
    sin = sin_ref[...]
    for h in range(RET_HEADS):
        k1, k2 = _rot_halves(k_ref, h, cos, sin)
        dec = kdb_ref[h] * (RET_DK ** -0.5)
        kd = jnp.concatenate([k1 * dec, k2 * dec], axis=1).astype(BF16)
        vh = v_ref[:, h * RET_DV:(h + 1) * RET_DV].astype(BF16)
        r = r_ref[h]
        rb_ref[h] = r.astype(BF16)
        upd = lax.dot_general(kd, vh, (((0,), (0,)), ((), ())), preferred_element_type=F32)
        r_ref[h] = r * cdec[h] + upd


def _ret_main_kernel(q_ref, k_ref, v_ref, g_ref, rb_ref, cos_ref, sin_ref, dmat_ref, qdf_ref, qdb_ref,
                     kdf_ref, wo_ref, o_ref, rf_ref, *, geom, chunk, cdec):
    i = pl.program_id(0)

    @pl.when(_tile_pos(geom, i, chunk) == 0)
    def _():
        rf_ref[...] = jnp.zeros_like(rf_ref)

    cos = cos_ref[...]
    sin = sin_ref[...]
    acc = jnp.zeros(o_ref.shape, F32)
    for h in range(RET_HEADS):
        q1, q2 = _rot_halves(q_ref, h, cos, sin)
        k1, k2 = _rot_halves(k_ref, h, cos, sin)
        k1 = k1 * (RET_DK ** -0.5)
        k2 = k2 * (RET_DK ** -0.5)
        qr = jnp.concatenate([q1, q2], axis=1).astype(BF16)
        kr = jnp.concatenate([k1, k2], axis=1).astype(BF16)
        vh = v_ref[:, h * RET_DV:(h + 1) * RET_DV].astype(BF16)
        s = lax.dot_general(qr, kr, (((1,), (1,)), ((), ())), preferred_element_type=F32) * dmat_ref[h]
        o = jnp.dot(s.astype(BF16), vh, preferred_element_type=F32)
        qdf = qdf_ref[h]
        qf = jnp.concatenate([q1 * qdf, q2 * qdf], axis=1).astype(BF16)
        rf = rf_ref[h]
        o = o + jnp.dot(qf, rf.astype(BF16), preferred_element_type=F32)
        qdb = qdb_ref[h]
        qb = jnp.concatenate([q1 * qdb, q2 * qdb], axis=1).astype(BF16)
        o = o + jnp.dot(qb, rb_ref[h], preferred_element_type=F32)
        kdf = kdf_ref[h]
        kf = jnp.concatenate([k1 * kdf, k2 * kdf], axis=1).astype(BF16)
        rf_ref[h] = rf * cdec[h] + lax.dot_general(kf, vh, (((0,), (0,)), ((), ())),
                                                   preferred_element_type=F32)
        oc = o - jnp.mean(o, axis=-1, keepdims=True)
        on = oc * lax.rsqrt(jnp.mean(oc * oc, axis=-1, keepdims=True) + GN_EPS)
        og = (on * _silu(g_ref[:, h * RET_DV:(h + 1) * RET_DV])).astype(BF16)
        acc = acc + jnp.dot(og, wo_ref[h * RET_DV:(h + 1) * RET_DV, :], preferred_element_type=F32)
    o_ref[...] = acc


def _retention_tables(chunk, s_max):
    log_gamma = jnp.log1p(-jnp.exp2(-5.0 - jnp.arange(RET_HEADS, dtype=F32)))
    idx = jnp.arange(chunk, dtype=F32)
    dist = jnp.abs(idx[:, None] - idx[None, :])
    dmat = jnp.exp(log_gamma[:, None, None] * dist[None])

    def rows(e):
        return jnp.broadcast_to(jnp.exp(log_gamma[:, None] * e[None, :])[:, :, None],
                                (RET_HEADS, chunk, RET_HALF))

    qdf = rows(idx + 1.0)
    qdb = rows(chunk - idx)
    kdf = rows(chunk - 1.0 - idx)
    kdb = rows(idx)
    theta = 1.0 / (ROPE_BASE ** jnp.linspace(0.0, 1.0, RET_HALF, dtype=F32))
    ang = jnp.arange(s_max, dtype=F32)[:, None] * theta[None, :]
    return jnp.cos(ang), jnp.sin(ang), dmat, qdf, qdb, kdf, kdb


def _chunk_decay(chunk):
    lg = np.log1p(-np.exp2(-5.0 - np.arange(RET_HEADS, dtype=np.float32))).astype(np.float32)
    return tuple(float(v) for v in np.exp(lg * np.float32(chunk)).astype(np.float32))


def _retention(geom, proj_a, w_ret_o, tables, *, chunk):
    t = proj_a.shape[0]
    n_chunks = t // chunk
    cos, sin, dmat, qdf, qdb, kdf, kdb = tables
    cdec = _chunk_decay(chunk)
    state_shape = (RET_HEADS, RET_DK, RET_DV)

    def pos_blk(c):
        return _tile_pos(geom, c, chunk) // chunk

    rev = lambda i: n_chunks - 1 - i
    tab_spec = pl.BlockSpec((RET_HEADS, chunk, RET_HALF), lambda i: (0, 0, 0))
    rb = pl.pallas_call(
        functools.partial(_ret_bwd_state_kernel, geom=geom, chunk=chunk, cdec=cdec),
        grid=(n_chunks,),
        in_specs=[
            pl.BlockSpec((chunk, RET_QK_W), lambda i: (rev(i), 1)),
            pl.BlockSpec((chunk, RET_V_W), lambda i: (rev(i), 1)),
            pl.BlockSpec((chunk, RET_HALF), lambda i: (pos_blk(rev(i)), 0)),
            pl.BlockSpec((chunk, RET_HALF), lambda i: (pos_blk(rev(i)), 0)),
            tab_spec,
        ],
        out_specs=pl.BlockSpec((None,) + state_shape, lambda i: (rev(i), 0, 0, 0)),
        out_shape=jax.ShapeDtypeStruct((n_chunks,) + state_shape, BF16),
        scratch_shapes=[pltpu.VMEM(state_shape, F32)],
        compiler_params=_cparams(("arbitrary",)),
        name="ret_bwd_state",
    )(proj_a, proj_a, cos, sin, kdb)

    return pl.pallas_call(
        functools.partial(_ret_main_kernel, geom=geom, chunk=chunk, cdec=cdec),
        grid=(n_chunks,),
        in_specs=[
            pl.BlockSpec((chunk, RET_QK_W), lambda i: (i, 0)),
            pl.BlockSpec((chunk, RET_QK_W), lambda i: (i, 1)),
            pl.BlockSpec((chunk, RET_V_W), lambda i: (i, 1)),
            pl.BlockSpec((chunk, RET_V_W), lambda i: (i, 2)),
            pl.BlockSpec((None,) + state_shape, lambda i: (i, 0, 0, 0)),
            pl.BlockSpec((chunk, RET_HALF), lambda i: (pos_blk(i), 0)),
            pl.BlockSpec((chunk, RET_HALF), lambda i: (pos_blk(i), 0)),
            pl.BlockSpec((RET_HEADS, chunk, chunk), lambda i: (0, 0, 0)),
            tab_spec, tab_spec, tab_spec,
            pl.BlockSpec((RET_V_W, D_MODEL), lambda i: (0, 0)),
        ],
        out_specs=pl.BlockSpec((chunk, D_MODEL), lambda i: (i, 0)),
        out_shape=jax.ShapeDtypeStruct((t, D_MODEL), F32),
        scratch_shapes=[pltpu.VMEM(state_shape, F32)],
        compiler_params=_cparams(("arbitrary",)),
        name="ret_main",
    )(proj_a, proj_a, proj_a, proj_a, rb, cos, sin, dmat, qdf, qdb, kdf, w_ret_o)


def _log_sigmoid(x):
    return jnp.minimum(x, 0.0) - jnp.log1p(jnp.exp(-jnp.abs(x)))


def _lru_conv(cur, prev8, next8, wconv_ref, bconv_ref):
    tl = cur.shape[0]
    row8 = lax.broadcasted_iota(jnp.int32, (V7X_SUBLANES, cur.shape[1]), 0)
    acc = bconv_ref[...] + cur * wconv_ref[CONV_LEFT:CONV_LEFT + 1, :]
    for j in range(CONV_WIDTH):
        off = j - CONV_LEFT
        if off == 0:
            continue
        rolled = pltpu.roll(cur, (-off) % tl, axis=0)
        if off < 0:
            halo = pltpu.roll(prev8, (-off) % V7X_SUBLANES, axis=0)
            head = jnp.where(row8 < -off, halo, rolled[:V7X_SUBLANES])
            shifted = jnp.concatenate([head, rolled[V7X_SUBLANES:]], axis=0)
        else:
            halo = pltpu.roll(next8, (-off) % V7X_SUBLANES, axis=0)
            tail = jnp.where(row8 >= V7X_SUBLANES - off, halo, rolled[tl - V7X_SUBLANES:])
            shifted = jnp.concatenate([rolled[:tl - V7X_SUBLANES], tail], axis=0)
        acc = acc + shifted * wconv_ref[j:j + 1, :]
    return acc


def _lru_gates(xc, d, wrg_ref, brg_ref, wig_ref, big_ref, lam_ref, a_ref, u_ref):
    xcb = xc.astype(BF16)
    ls = _log_sigmoid(lam_ref[d])
    for n in range(LRU_BLOCKS):
        sl = slice(n * LRU_BW, (n + 1) * LRU_BW)
        xb = xcb[:, sl]
        r = jax.nn.sigmoid(jnp.dot(xb, wrg_ref[d, n], preferred_element_type=F32) + brg_ref[d][:, sl])
        ig = jax.nn.sigmoid(jnp.dot(xb, wig_ref[d, n], preferred_element_type=F32) + big_ref[d][:, sl])
        log_a = LRU_C * r * ls[:, sl]
        a = jnp.exp(log_a)
        a_ref[:, sl] = a
        u_ref[:, sl] = jnp.sqrt(1.0 - a * a) * ig * xc[:, sl]


def _lru_scan_rows(a_ref, u_ref, o_ref, h0, reverse):
    tl = a_ref.shape[0]

    def body(s, h):
        t = tl - 1 - s if reverse else s
        h = a_ref[pl.ds(t, 1), :] * h + u_ref[pl.ds(t, 1), :]
        o_ref[pl.ds(t, 1), :] = h
        return h

    return lax.fori_loop(0, tl, body, h0, unroll=8)


def _lru_kernel(xf_ref, xfp_ref, xfn_ref, xb_ref, xbp_ref, xbn_ref, wconv_ref, bconv_ref, wrg_ref, brg_ref,
                wig_ref, big_ref, lam_ref, hf_ref, hb_ref, a_ref, u_ref, cf_ref, cb_ref, *, geom, tl):
    i = pl.program_id(0)
    ib = pl.num_programs(0) - 1 - i

    def flags(tile):
        pos = _tile_pos(geom, tile, tl)
        return pos == 0, pos + tl == _tile_seq_len(geom, tile, tl)

    def keep(flag):
        return jnp.where(flag, 0.0, 1.0)

    first, last = flags(i)

    @pl.when(first)
    def _():
        cf_ref[...] = jnp.zeros_like(cf_ref)

    xc = _lru_conv(xf_ref[...], xfp_ref[...] * keep(first), xfn_ref[...] * keep(last), wconv_ref, bconv_ref)
    _lru_gates(xc, 0, wrg_ref, brg_ref, wig_ref, big_ref, lam_ref, a_ref, u_ref)
    cf_ref[0:1, :] = _lru_scan_rows(a_ref, u_ref, hf_ref, cf_ref[0:1, :], False)

    first, last = flags(ib)

    @pl.when(last)
    def _():
        cb_ref[...] = jnp.zeros_like(cb_ref)

    xc = _lru_conv(xb_ref[...], xbp_ref[...] * keep(first), xbn_ref[...] * keep(last), wconv_ref, bconv_ref)
    _lru_gates(xc, 1, wrg_ref, brg_ref, wig_ref, big_ref, lam_ref, a_ref, u_ref)
    cb_ref[0:1, :] = _lru_scan_rows(a_ref, u_ref, hb_ref, cb_ref[0:1, :], True)


def _lru(geom, proj_b, w_conv, b_conv, w_rg, b_rg, w_ig, b_ig, lam, *, tl):
    t = proj_b.shape[0]
    n = t // tl
    r8 = tl // V7X_SUBLANES
    n8 = t // V7X_SUBLANES
    w = LRU_WIDTH
    rev = lambda i: n - 1 - i
    prev_blk = lambda i: jnp.maximum(i * r8 - 1, 0)
    next_blk = lambda i: jnp.minimum((i + 1) * r8, n8 - 1)
    full = lambda a: pl.BlockSpec(a.shape, lambda i: (0,) * a.ndim)
    b_conv2 = b_conv.reshape(1, w)
    b_rg3 = b_rg.reshape(2, 1, w)
    b_ig3 = b_ig.reshape(2, 1, w)
    lam3 = lam.reshape(2, 1, w)
    return pl.pallas_call(
        functools.partial(_lru_kernel, geom=geom, tl=tl),
        grid=(n,),
        in_specs=[
            pl.BlockSpec((tl, w), lambda i: (i, 0)),
            pl.BlockSpec((V7X_SUBLANES, w), lambda i: (prev_blk(i), 0)),
            pl.BlockSpec((V7X_SUBLANES, w), lambda i: (next_blk(i), 0)),
            pl.BlockSpec((tl, w), lambda i: (rev(i), 0)),
            pl.BlockSpec((V7X_SUBLANES, w), lambda i: (prev_blk(rev(i)), 0)),
            pl.BlockSpec((V7X_SUBLANES, w), lambda i: (next_blk(rev(i)), 0)),
            full(w_conv), full(b_conv2), full(w_rg), full(b_rg3), full(w_ig), full(b_ig3), full(lam3),
        ],
        out_specs=[
            pl.BlockSpec((tl, w), lambda i: (i, 0)),
            pl.BlockSpec((tl, w), lambda i: (rev(i), 0)),
        ],
        out_shape=[jax.ShapeDtypeStruct((t, w), F32), jax.ShapeDtypeStruct((t, w), F32)],
        scratch_shapes=[pltpu.VMEM((tl, w), F32), pltpu.VMEM((tl, w), F32),
                        pltpu.VMEM((V7X_SUBLANES, w), F32), pltpu.VMEM((V7X_SUBLANES, w), F32)],
        compiler_params=_cparams(("arbitrary",)),
        name="lru_scan",
    )(proj_b, proj_b, proj_b, proj_b, proj_b, proj_b, w_conv, b_conv2, w_rg, b_rg3, w_ig, b_ig3, lam3)


def _gelu_tanh(x):
    return 0.5 * x * (1.0 + jnp.tanh(math.sqrt(2.0 / math.pi) * (x + 0.044715 * (x * x * x))))


def _merge_kernel(x_ref, ya_ref, hf_ref, hb_ref, gl_ref, ga_ref, gb_ref, mod_ref, wl_ref, wo_ref, o_ref):
    y = ((hf_ref[...] + hb_ref[...]) * _gelu_tanh(gl_ref[...])).astype(BF16)
    yb = jnp.dot(y, wl_ref[...], preferred_element_type=F32)
    m = jax.nn.sigmoid(ga_ref[...]) * ya_ref[...] + jax.nn.sigmoid(gb_ref[...]) * yb
    mix = jnp.dot(m.astype(BF16), wo_ref[...], preferred_element_type=F32)
    o_ref[...] = x_ref[...] + mod_ref[2:3, :] * mix


def _merge(geom, x, ya, hf, hb, proj_a, proj_b, mod_l, w_lru_o, w_out, *, tm):
    t, d = x.shape
    gate_a_blk = (2 * RET_QK_W + 2 * RET_V_W) // d
    tok = lambda w: pl.BlockSpec((tm, w), lambda i: (i, 0))
    return pl.pallas_call(
        _merge_kernel,
        grid=(t // tm,),
        in_specs=[
            tok(d), tok(d), tok(LRU_WIDTH), tok(LRU_WIDTH),
            pl.BlockSpec((tm, LRU_WIDTH), lambda i: (i, 1)),
            pl.BlockSpec((tm, d), lambda i: (i, gate_a_blk)),
            pl.BlockSpec((tm, d), lambda i: (i, gate_a_blk + 1)),
            pl.BlockSpec((None, 6, d), lambda i: (_tile_seq(geom, i, tm), 0, 0)),
            pl.BlockSpec((LRU_WIDTH, d), lambda i: (0, 0)),
            pl.BlockSpec((d, d), lambda i: (0, 0)),
        ],
        out_specs=tok(d),
        out_shape=jax.ShapeDtypeStruct((t, d), F32),
        compiler_params=_cparams(("parallel",)),
        name="merge",
    )(x, ya, hf, hb, proj_b, proj_a, proj_a, mod_l, w_lru_o, w_out)


def _ffn_kernel(x_ref, mod_ref, g_ref, wg_ref, wu_ref, wd_ref, o_ref, h_ref, acc_ref):
    j = pl.program_id(1)

    @pl.when(j == 0)
    def _():
        h_ref[...] = _rms_mod(x_ref[...], g_ref[...], mod_ref[4:5, :], mod_ref[3:4, :]).astype(BF16)
        acc_ref[...] = jnp.zeros_like(acc_ref)

    h = h_ref[...]
    a = jnp.dot(h, wg_ref[...], preferred_element_type=F32)
    u = jnp.dot(h, wu_ref[...], preferred_element_type=F32)
    acc_ref[...] += jnp.dot((_silu(a) * u).astype(BF16), wd_ref[...], preferred_element_type=F32)

    @pl.when(j == pl.num_programs(1) - 1)
    def _():
        o_ref[...] = x_ref[...] + mod_ref[5:6, :] * acc_ref[...]


def _ffn(geom, x, mod_l, gvec, wg, wu, wd, *, tm, tf):
    t, d = x.shape
    ff = wg.shape[1]
    return pl.pallas_call(
        _ffn_kernel,
        grid=(t // tm, ff // tf),
        in_specs=[
            pl.BlockSpec((tm, d), lambda i, j: (i, 0)),
            pl.BlockSpec((None, 6, d), lambda i, j: (_tile_seq(geom, i, tm), 0, 0)),
            pl.BlockSpec((1, d), lambda i, j: (0, 0)),
            pl.BlockSpec((d, tf), lambda i, j: (0, j)),
            pl.BlockSpec((d, tf), lambda i, j: (0, j)),
            pl.BlockSpec((tf, d), lambda i, j: (j, 0)),
        ],
        out_specs=pl.BlockSpec((tm, d), lambda i, j: (i, 0)),
        out_shape=jax.ShapeDtypeStruct((t, d), F32),
        scratch_shapes=[pltpu.VMEM((tm, d), BF16), pltpu.VMEM((tm, d), F32)],
        compiler_params=_cparams(("parallel", "arbitrary")),
        name="ffn_dense",
    )(x, mod_l, gvec, wg, wu, wd)


ROUTE_E0, ROUTE_E1, ROUTE_R0, ROUTE_R1, ROUTE_G0, ROUTE_G1 = range(6)

ROW_TILES = D_MODEL // V7X_LANES


def _store_token_rows(ref, val):
    rows = val.shape[0]
    for j in range(ROW_TILES):
        ref[pl.ds(j, rows, stride=ROW_TILES), :] = val[:, j * V7X_LANES:(j + 1) * V7X_LANES]


def _load_token_rows(ref, rows):
    return jnp.concatenate([ref[pl.ds(j, rows, stride=ROW_TILES), :] for j in range(ROW_TILES)], axis=1)


def _router_kernel(x_ref, mod_ref, g_ref, wr_ref, h_ref, route_ref, cnt_ref, carry_ref):
    i = pl.program_id(0)
    tm = x_ref.shape[0]

    @pl.when(i == 0)
    def _():
        carry_ref[...] = jnp.zeros_like(carry_ref)

    h = _rms_mod(x_ref[...], g_ref[...], mod_ref[4:5, :], mod_ref[3:4, :])
    _store_token_rows(h_ref, h)
    logits = jnp.dot(h, wr_ref[...], preferred_element_type=F32, precision=lax.Precision.HIGHEST)
    lane = lax.broadcasted_iota(jnp.int32, logits.shape, 1).astype(F32)
    logits = jnp.where(lane < N_EXPERTS, logits, -jnp.inf)
    m1 = jnp.max(logits, axis=-1, keepdims=True)
    i1 = jnp.min(jnp.where(logits == m1, lane, float(V7X_LANES)), axis=-1, keepdims=True)
    rest = jnp.where(lane == i1, -jnp.inf, logits)
    m2 = jnp.max(rest, axis=-1, keepdims=True)
    i2 = jnp.min(jnp.where(rest == m2, lane, float(V7X_LANES)), axis=-1, keepdims=True)
    ex = jnp.exp(m2 - m1)
    g1 = 1.0 / (1.0 + ex)
    g2 = ex / (1.0 + ex)
    sel1 = lane == i1
    sel2 = lane == i2
    onehot = jnp.where(sel1 | sel2, 1.0, 0.0)
    row = lax.broadcasted_iota(jnp.int32, (tm, tm), 0)
    col = lax.broadcasted_iota(jnp.int32, (tm, tm), 1)
    lower = jnp.where(col < row, 1.0, 0.0).astype(BF16)
    before = jnp.dot(lower, onehot.astype(BF16), preferred_element_type=F32) + carry_ref[0:1, :]
    r1 = jnp.sum(jnp.where(sel1, before, 0.0), axis=-1, keepdims=True)
    r2 = jnp.sum(jnp.where(sel2, before, 0.0), axis=-1, keepdims=True)
    out_lane = lax.broadcasted_iota(jnp.int32, route_ref.shape, 1)
    vals = (i1.astype(F32), i2.astype(F32), r1, r2, g1, g2)
    packed = jnp.zeros(route_ref.shape, F32)
    for slot, v in enumerate(vals):
        packed = jnp.where(out_lane == slot, v, packed)
    route_ref[...] = packed
    carry = carry_ref[0:1, :] + jnp.sum(onehot, axis=0, keepdims=True)
    carry_ref[0:1, :] = carry
    cnt_ref[...] = jnp.broadcast_to(carry, cnt_ref.shape).astype(jnp.int32)


def _router(geom, x, mod_l, gvec, w_router, *, tm):
    t, d = x.shape
    w_pad = jnp.pad(w_router, ((0, 0), (0, V7X_LANES - N_EXPERTS)))
    return pl.pallas_call(
        _router_kernel,
        grid=(t // tm,),
        in_specs=[
            pl.BlockSpec((tm, d), lambda i: (i, 0)),
            pl.BlockSpec((None, 6, d), lambda i: (_tile_seq(geom, i, tm), 0, 0)),
            pl.BlockSpec((1, d), lambda i: (0, 0)),
            pl.BlockSpec((d, V7X_LANES), lambda i: (0, 0)),
        ],
        out_specs=[
            pl.BlockSpec((tm * ROW_TILES, V7X_LANES), lambda i: (i, 0)),
            pl.BlockSpec((tm, V7X_LANES), lambda i: (i, 0)),
            pl.BlockSpec((V7X_SUBLANES, V7X_LANES), lambda i: (0, 0)),
        ],
        out_shape=[
            jax.ShapeDtypeStruct((t * ROW_TILES, V7X_LANES), F32),
            jax.ShapeDtypeStruct((t, V7X_LANES), F32),
            jax.ShapeDtypeStruct((V7X_SUBLANES, V7X_LANES), jnp.int32),
        ],
        scratch_shapes=[pltpu.VMEM((V7X_SUBLANES, V7X_LANES), F32)],
        compiler_params=_cparams(("arbitrary",)),
        name="router",
    )(x, mod_l, gvec, w_pad)


def _token_copy(src, src_tok, dst, dst_tok, sem):
    s = pl.multiple_of(src_tok * ROW_TILES, ROW_TILES)
    d = pl.multiple_of(dst_tok * ROW_TILES, ROW_TILES)
    return pltpu.make_async_copy(src.at[pl.ds(s, ROW_TILES), :], dst.at[pl.ds(d, ROW_TILES), :], sem)


def _dispatch_kernel(dest_ref, src_ref, init_hbm, out_hbm, sem, *, td):
    del init_hbm

    def issue(t, carry):
        for k in range(TOP_K):
            _token_copy(src_ref, t, out_hbm, dest_ref[k, t], sem).start()
        return carry

    lax.fori_loop(0, td, issue, 0, unroll=8)
    for _ in range(TOP_K):
        pltpu.make_async_copy(src_ref, out_hbm.at[pl.ds(0, td * ROW_TILES), :], sem).wait()


def _dispatch(h2_rows, dest, n_rows, *, td):
    init = jnp.zeros((n_rows * ROW_TILES, V7X_LANES), h2_rows.dtype)
    return pl.pallas_call(
        functools.partial(_dispatch_kernel, td=td),
        grid=(dest.shape[1] // td,),
        in_specs=[
            pl.BlockSpec((TOP_K, td), lambda i: (0, i), memory_space=pltpu.SMEM),
            pl.BlockSpec((td * ROW_TILES, V7X_LANES), lambda i: (i, 0)),
            pl.BlockSpec(memory_space=pl.ANY),
        ],
        out_specs=pl.BlockSpec(memory_space=pl.ANY),
        out_shape=jax.ShapeDtypeStruct(init.shape, init.dtype),
        scratch_shapes=[pltpu.SemaphoreType.DMA(())],
        input_output_aliases={2: 0},
        compiler_params=_cparams(("arbitrary",)),
        name="moe_dispatch",
    )(dest, h2_rows, init)


def _experts_kernel(blk_e_ref, nused_ref, x_ref, wg_ref, wu_ref, wd_ref, o_ref, xb_ref, acc_ref):
    del blk_e_ref
    b = pl.program_id(0)
    j = pl.program_id(1)
    bm = xb_ref.shape[0]

    @pl.when(b < nused_ref[0])
    def _():
        @pl.when(j == 0)
        def _():
            xb_ref[...] = _load_token_rows(x_ref, bm).astype(BF16)
            acc_ref[...] = jnp.zeros_like(acc_ref)

        xb = xb_ref[...]
        a = jnp.dot(xb, wg_ref[...], preferred_element_type=F32)
        u = jnp.dot(xb, wu_ref[...], preferred_element_type=F32)
        acc_ref[...] += jnp.dot((_silu(a) * u).astype(BF16), wd_ref[...], preferred_element_type=F32)

        @pl.when(j == pl.num_programs(1) - 1)
        def _():
            _store_token_rows(o_ref, acc_ref[...])


def _experts(xbuf, blk_e, n_used, wg, wu, wd, *, bm, tf):
    p = xbuf.shape[0] // ROW_TILES
    d = D_MODEL
    ff = wg.shape[2]
    nj = ff // tf
    rows_blk = (bm * ROW_TILES, V7X_LANES)

    def blk(b, nu):
        return jnp.minimum(b, nu[0] - 1)

    def ffc(b, j, nu):
        return jnp.where(b < nu[0], j, nj - 1)

    grid_spec = pltpu.PrefetchScalarGridSpec(
        num_scalar_prefetch=2,
        grid=(p // bm, nj),
        in_specs=[
            pl.BlockSpec(rows_blk, lambda b, j, be, nu: (blk(b, nu), 0)),
            pl.BlockSpec((None, d, tf), lambda b, j, be, nu: (be[blk(b, nu)], 0, ffc(b, j, nu))),
            pl.BlockSpec((None, d, tf), lambda b, j, be, nu: (be[blk(b, nu)], 0, ffc(b, j, nu))),
            pl.BlockSpec((None, tf, d), lambda b, j, be, nu: (be[blk(b, nu)], ffc(b, j, nu), 0)),
        ],
        out_specs=pl.BlockSpec(rows_blk, lambda b, j, be, nu: (blk(b, nu), 0)),
        scratch_shapes=[pltpu.VMEM((bm, d), BF16), pltpu.VMEM((bm, d), F32)],
    )
    return pl.pallas_call(
        _experts_kernel,
        grid_spec=grid_spec,
        out_shape=jax.ShapeDtypeStruct(xbuf.shape, F32),
        compiler_params=_cparams(("arbitrary", "arbitrary")),
        name="moe_experts",
    )(blk_e, n_used, xbuf, wg, wu, wd)


def _moe_out_kernel(dest_ref, x_ref, route_ref, mod_ref, gf_ref, ybuf_hbm, o_ref, y0_ref, y1_ref, sem, *,
                    final_norm):
    tm = x_ref.shape[0]
    y_refs = (y0_ref, y1_ref)

    def issue(t, carry):
        for k in range(TOP_K):
            _token_copy(ybuf_hbm, dest_ref[k, t], y_refs[k], t, sem).start()
        return carry

    lax.fori_loop(0, tm, issue, 0, unroll=8)
    for k in range(TOP_K):
        pltpu.make_async_copy(ybuf_hbm.at[pl.ds(0, tm * ROW_TILES), :], y_refs[k], sem).wait()
    g1 = route_ref[:, ROUTE_G0:ROUTE_G0 + 1]
    g2 = route_ref[:, ROUTE_G1:ROUTE_G1 + 1]
    y = _load_token_rows(y0_ref, tm) * g1 + _load_token_rows(y1_ref, tm) * g2
    x = x_ref[...] + mod_ref[5:6, :] * y
    if final_norm:
        ms = jnp.mean(x * x, axis=-1, keepdims=True)
        x = x * lax.rsqrt(ms + RMS_EPS) * gf_ref[...]
    o_ref[...] = x


def _moe_out(geom, x, ybuf, dest, route, mod_l, g_final, *, tm, final_norm):
    t, d = x.shape
    return pl.pallas_call(
        functools.partial(_moe_out_kernel, final_norm=final_norm),
        grid=(t // tm,),
        in_specs=[
            pl.BlockSpec((TOP_K, tm), lambda i: (0, i), memory_space=pltpu.SMEM),
            pl.BlockSpec((tm, d), lambda i: (i, 0)),
            pl.BlockSpec((tm, V7X_LANES), lambda i: (i, 0)),
            pl.BlockSpec((None, 6, d), lambda i: (_tile_seq(geom, i, tm), 0, 0)),
            pl.BlockSpec((1, d), lambda i: (0, 0)),
            pl.BlockSpec(memory_space=pl.ANY),
        ],
        out_specs=pl.BlockSpec((tm, d), lambda i: (i, 0)),
        out_shape=jax.ShapeDtypeStruct((t, d), F32),
        scratch_shapes=[pltpu.VMEM((tm * ROW_TILES, V7X_LANES), F32), pltpu.VMEM((tm * ROW_TILES, V7X_LANES), F32),
                        pltpu.SemaphoreType.DMA(())],
        compiler_params=_cparams(("arbitrary",)),
        name="moe_out",
    )(dest, x, route, mod_l, g_final, ybuf)


def _final_norm_kernel(x_ref, g_ref, o_ref):
    x = x_ref[...]
    o_ref[...] = x * lax.rsqrt(jnp.mean(x * x, axis=-1, keepdims=True) + RMS_EPS) * g_ref[...]


def _final_norm(x, g_final, *, tm):
    t, d = x.shape
    return pl.pallas_call(
        _final_norm_kernel,
        grid=(t // tm,),
        in_specs=[pl.BlockSpec((tm, d), lambda i: (i, 0)), pl.BlockSpec((1, d), lambda i: (0, 0))],
        out_specs=pl.BlockSpec((tm, d), lambda i: (i, 0)),
        out_shape=jax.ShapeDtypeStruct((t, d), F32),
        compiler_params=_cparams(("parallel",)),
        name="final_norm",
    )(x, g_final)


def _moe(geom, x, mod_l, gvec, w_router, wg, wu, wd, g_final, *, tiles, final_norm):
    t, d = x.shape
    bm = tiles["moe_rows"]
    h2, route, counts = _router(geom, x, mod_l, gvec, w_router, tm=tiles["router"])
    counts = counts[0, :N_EXPERTS]
    padded = ((counts + bm - 1) // bm) * bm
    pad_end = jnp.cumsum(padded)
    pad_start = pad_end - padded
    experts = route[:, ROUTE_E0:ROUTE_E1 + 1].astype(jnp.int32)
    ranks = route[:, ROUTE_R0:ROUTE_R1 + 1].astype(jnp.int32)
    dest = (pad_start[experts] + ranks).T
    n_rows = t * TOP_K + N_EXPERTS * bm
    n_blk = n_rows // bm
    blk_start = jnp.arange(n_blk, dtype=jnp.int32) * bm
    blk_e = jnp.minimum(jnp.sum((pad_end[None, :] <= blk_start[:, None]).astype(jnp.int32), axis=1),
                        N_EXPERTS - 1).astype(jnp.int32)
    n_used = (pad_end[-1:] // bm).astype(jnp.int32)
    xbuf = _dispatch(h2, dest, n_rows, td=tiles["dma_rows"])
    ybuf = _experts(xbuf, blk_e, n_used, wg, wu, wd, bm=bm, tf=tiles["expert_ff"])
    return _moe_out(geom, x, ybuf, dest, route, mod_l, g_final, tm=tiles["token"], final_norm=final_norm)


def _pick_tiles(geom):
    s = math.gcd(geom.s_p, geom.s_s) if geom.n_p and geom.n_s else (geom.s_p if geom.n_p else geom.s_s)
    return {
        "proj": min(1024, s),
        "proj_cols": 1024,
        "ret_chunk": min(256, s),
        "lru": min(512, s),
        "token": min(512, s),
        "ffn_cols": 1408,
        "router": min(512, s),
        "moe_rows": 512,
        "expert_ff": 896,
        "dma_rows": min(2048, s),
    }


def _trunk(geom, x, c_all, w_ada, b_ada, g_norm1, g_norm2, w_in, w_conv, b_conv, w_rg, b_rg, w_ig, b_ig,
           lru_lambda, w_ret_o, w_lru_o, w_out, w_ff_gate, w_ff_up, w_ff_down,
           w_router, w_e_gate, w_e_up, w_e_down, g_final, tiles):
    depth = w_in.shape[0]
    d = D_MODEL
    n_pad = -(-geom.n_seq // V7X_SUBLANES) * V7X_SUBLANES
    c_pad = jnp.pad(c_all, ((0, n_pad - geom.n_seq), (0, 0)))
    mod = _adaln_mod(c_pad, w_ada, b_ada)
    tables = _retention_tables(tiles["ret_chunk"], max(geom.s_p if geom.n_p else 0, geom.s_s if geom.n_s else 0))

    o_xl = 2 * RET_QK_W + 2 * RET_V_W
    o_ga = o_xl + 2 * LRU_WIDTH
    for l in range(depth):
        wl = w_in[l]
        w_a = jnp.concatenate([wl[:, :o_xl], wl[:, o_ga:]], axis=1).astype(BF16)
        w_b = wl[:, o_xl:o_ga].astype(BF16)
        g1 = g_norm1[l].reshape(1, d)
        g2 = g_norm2[l].reshape(1, d)
        proj_a = _norm_proj(geom, x, mod[l], g1, w_a, shift_row=0, scale_row=1,
                            tm=tiles["proj"], tn=tiles["proj_cols"])
        proj_b = _norm_proj(geom, x, mod[l], g1, w_b, shift_row=0, scale_row=1,
                            tm=tiles["proj"], tn=LRU_WIDTH)
        ya = _retention(geom, proj_a, w_ret_o[l].astype(BF16), tables, chunk=tiles["ret_chunk"])
        hf, hb = _lru(geom, proj_b, w_conv[l], b_conv[l], w_rg[l].astype(BF16), b_rg[l],
                      w_ig[l].astype(BF16), b_ig[l], lru_lambda[l], tl=tiles["lru"])
        x = _merge(geom, x, ya, hf, hb, proj_a, proj_b, mod[l], w_lru_o[l].astype(BF16),
                   w_out[l].astype(BF16), tm=tiles["token"])
        j = l // 2
        last = l == depth - 1
        if l % 2 == 0:
            x = _ffn(geom, x, mod[l], g2, w_ff_gate[j].astype(BF16), w_ff_up[j].astype(BF16),
                     w_ff_down[j].astype(BF16), tm=tiles["token"], tf=tiles["ffn_cols"])
            if last:
                x = _final_norm(x, g_final.reshape(1, d), tm=tiles["token"])
        else:
            x = _moe(geom, x, mod[l], g2, w_router[j], w_e_gate[j].astype(BF16), w_e_up[j].astype(BF16),
                     w_e_down[j].astype(BF16), g_final.reshape(1, d), tiles=tiles, final_norm=last)
    return x


def kernel(x_prompt, x_sample, c_prompt, c_sample, w_ada, b_ada, g_norm1, g_norm2, w_in, w_conv, b_conv, w_rg, b_rg, w_ig, b_ig, lru_lambda, w_ret_o, w_lru_o, w_out, w_ff_gate, w_ff_up, w_ff_down, w_router, w_e_gate, w_e_up, w_e_down, g_final):
    n_p, s_p, d = x_prompt.shape
    n_s, s_s, _ = x_sample.shape
    geom = Geom(n_p, s_p, n_s, s_s)
    x = jnp.concatenate([x_prompt.reshape(-1, d), x_sample.reshape(-1, d)], axis=0)
    c_all = jnp.concatenate([c_prompt, c_sample], axis=0)
    y = _trunk(geom, x, c_all, w_ada, b_ada, g_norm1, g_norm2, w_in, w_conv, b_conv, w_rg, b_rg, w_ig, b_ig,
               lru_lambda, w_ret_o, w_lru_o, w_out, w_ff_gate, w_ff_up, w_ff_down,
               w_router, w_e_gate, w_e_up, w_e_down, g_final, _pick_tiles(geom))
    return (y[:geom.tokens_p].reshape(n_p, s_p, d), y[geom.tokens_p:].reshape(n_s, s_s, d))
```

```python
import functools
import math
from typing import NamedTuple

import jax
import jax.numpy as jnp
import numpy as np
from jax import lax
from jax.experimental import pallas as pl
from jax.experimental.pallas import tpu as pltpu

F32 = jnp.float32
BF16 = jnp.bfloat16

D_MODEL = 1024
RET_HEADS = 4
RET_DK = 256
RET_DV = 512
RET_HALF = RET_DK // 2
RET_QK_W = RET_HEADS * RET_DK
RET_V_W = RET_HEADS * RET_DV
ROPE_BASE = 10000.0
LRU_WIDTH = 1280
LRU_BLOCKS = 10
LRU_BW = LRU_WIDTH // LRU_BLOCKS
LRU_C = 8.0
CONV_WIDTH = 4
CONV_LEFT = 2
LRU_HALO = 16
N_EXPERTS = 8
TOP_K = 2
RMS_EPS = 1e-6
GN_EPS = 1e-5
SQRT_TINY = 1e-30

V7X_LANES = 128
V7X_SUBLANES = 8
V7X_VMEM_BYTES = 64 * 1024 * 1024
VMEM_LIMIT = (V7X_VMEM_BYTES * 3) // 4

PA_W = 2 * RET_QK_W + 2 * RET_V_W + 2 * D_MODEL
PB_W = 2 * LRU_WIDTH


class Geom(NamedTuple):
    n_p: int
    s_p: int
    n_s: int
    s_s: int

    @property
    def tokens_p(self):
        return self.n_p * self.s_p

    @property
    def tokens(self):
        return self.n_p * self.s_p + self.n_s * self.s_s

    @property
    def n_seq(self):
        return self.n_p + self.n_s


def _tile_seq(g, i, tm):
    t0 = i * tm
    return jnp.where(t0 < g.tokens_p, t0 // g.s_p, g.n_p + (t0 - g.tokens_p) // g.s_s)


def _tile_pos(g, i, tm):
    t0 = i * tm
    return jnp.where(t0 < g.tokens_p, t0 % g.s_p, (t0 - g.tokens_p) % g.s_s)


def _tile_seq_len(g, i, tm):
    return jnp.where(i * tm < g.tokens_p, g.s_p, g.s_s)


def _cparams(sem):
    return pltpu.CompilerParams(dimension_semantics=sem, vmem_limit_bytes=VMEM_LIMIT)


def _sigmoid(x):
    return 0.5 * jnp.tanh(0.5 * x) + 0.5


def _silu(x):
    return x * _sigmoid(x)


def _rms_mod(x, gvec, scale, shift):
    ms = jnp.mean(x * x, axis=-1, keepdims=True)
    y = x * lax.rsqrt(ms + RMS_EPS) * gvec
    return y * (1.0 + scale) + shift


def _mod_kernel(c_ref, w_ref, b_ref, o_ref):
    c = c_ref[...]
    o_ref[...] = jnp.dot(_silu(c), w_ref[...], preferred_element_type=F32,
                         precision=lax.Precision.HIGHEST) + b_ref[...]


def _adaln_mod(c_pad, w_ada, b_ada):
    depth, d, _ = w_ada.shape
    n = c_pad.shape[0]
    out = pl.pallas_call(
        _mod_kernel,
        grid=(depth, 6),
        in_specs=[
            pl.BlockSpec((n, d), lambda l, j: (0, 0)),
            pl.BlockSpec((None, d, d), lambda l, j: (l, 0, j)),
            pl.BlockSpec((None, None, 1, d), lambda l, j: (l, j, 0, 0)),
        ],
        out_specs=pl.BlockSpec((None, None, n, d), lambda l, j: (l, j, 0, 0)),
        out_shape=jax.ShapeDtypeStruct((depth, 6, n, d), F32),
        compiler_params=_cparams(("arbitrary", "arbitrary")),
        name="adaln_mod",
    )(c_pad, w_ada, b_ada.reshape(depth, 6, 1, d))
    return out.transpose(0, 2, 1, 3)


def _norm_proj_kernel(x_ref, mod_ref, g_ref, w_ref, o_ref, h_ref, *, shift_row, scale_row):
    @pl.when(pl.program_id(1) == 0)
    def _():
        h = _rms_mod(x_ref[...], g_ref[...], mod_ref[scale_row:scale_row + 1, :],
                     mod_ref[shift_row:shift_row + 1, :])
        h_ref[...] = h.astype(BF16)

    o_ref[...] = jnp.dot(h_ref[...], w_ref[...], preferred_element_type=F32).astype(o_ref.dtype)


def _norm_proj(geom, x, mod_l, gvec, w, *, shift_row, scale_row, tm, tn):
    t, d = x.shape
    n = w.shape[1]
    return pl.pallas_call(
        functools.partial(_norm_proj_kernel, shift_row=shift_row, scale_row=scale_row),
        grid=(t // tm, n // tn),
        in_specs=[
            pl.BlockSpec((tm, d), lambda i, j: (i, 0)),
            pl.BlockSpec((None, 6, d), lambda i, j: (_tile_seq(geom, i, tm), 0, 0)),
            pl.BlockSpec((1, d), lambda i, j: (0, 0)),
            pl.BlockSpec((d, tn), lambda i, j: (0, j)),
        ],
        out_specs=pl.BlockSpec((tm, tn), lambda i, j: (i, j)),
        out_shape=jax.ShapeDtypeStruct((t, n), BF16),
        scratch_shapes=[pltpu.VMEM((tm, d), BF16)],
        compiler_params=_cparams(("parallel", "arbitrary")),
        name="norm_proj",
    )(x, mod_l, gvec, w)


def _rot_halves(ref, h, cos, sin):
    a = ref[:, h * RET_DK:h * RET_DK + RET_HALF].astype(F32)
    b = ref[:, h * RET_DK + RET_HALF:(h + 1) * RET_DK].astype(F32)
    return a * cos - b * sin, a * sin + b * cos


def _ret_bwd_state_kernel(k_ref, v_ref, cos_ref, sin_ref, kdb_ref, rb_ref, r_ref, *, geom, chunk, cdec):
    c = pl.num_programs(0) - 1 - pl.program_id(0)
    is_last = _tile_pos(geom, c, chunk) + chunk == _tile_seq_len(geom, c, chunk)

    @pl.when(is_last)
    def _():
        r_ref[...] = jnp.zeros_like(r_ref)

    cos = cos_ref[...]
    sin = sin_ref[...]
    for h in range(RET_HEADS):
        k1, k2 = _rot_halves(k_ref, h, cos, sin)
        dec = kdb_ref[h] * (RET_DK ** -0.5)
        kd = jnp.concatenate([k1 * dec, k2 * dec], axis=1).astype(BF16)
        vh = v_ref[:, h * RET_DV:(h + 1) * RET_DV].astype(BF16)
        r = r_ref[h]
        rb_ref[h] = r.astype(BF16)
        upd = lax.dot_general(kd, vh, (((0,), (0,)), ((), ())), preferred_element_type=F32)
        r_ref[h] = r * cdec[h] + upd


def _ret_main_kernel(q_ref, k_ref, v_ref, g_ref, rb_ref, cos_ref, sin_ref, dmat_ref, qdf_ref, qdb_ref,
                     kdf_ref, wo_ref, o_ref, rf_ref, *, geom, chunk, cdec):
    i = pl.program_id(0)

    @pl.when(_tile_pos(geom, i, chunk) == 0)
    def _():
        rf_ref[...] = jnp.zeros_like(rf_ref)

    cos = cos_ref[...]
    sin = sin_ref[...]
    acc = jnp.zeros(o_ref.shape, F32)
    for h in range(RET_HEADS):
        q1, q2 = _rot_halves(q_ref, h, cos, sin)
        k1, k2 = _rot_halves(k_ref, h, cos, sin)
        k1 = k1 * (RET_DK ** -0.5)
        k2 = k2 * (RET_DK ** -0.5)
        qr = jnp.concatenate([q1, q2], axis=1).astype(BF16)
        kr = jnp.concatenate([k1, k2], axis=1).astype(BF16)
        vh = v_ref[:, h * RET_DV:(h + 1) * RET_DV].astype(BF16)
        s = lax.dot_general(qr, kr, (((1,), (1,)), ((), ())), preferred_element_type=F32) * dmat_ref[h]
        o = jnp.dot(s.astype(BF16), vh, preferred_element_type=F32)
        qdf = qdf_ref[h]
        qf = jnp.concatenate([q1 * qdf, q2 * qdf], axis=1).astype(BF16)
        rf = rf_ref[h]
        o = o + jnp.dot(qf, rf.astype(BF16), preferred_element_type=F32)
        qdb = qdb_ref[h]
        qb = jnp.concatenate([q1 * qdb, q2 * qdb], axis=1).astype(BF16)
        o = o + jnp.dot(qb, rb_ref[h], preferred_element_type=F32)
        kdf = kdf_ref[h]
        kf = jnp.concatenate([k1 * kdf, k2 * kdf], axis=1).astype(BF16)
        rf_ref[h] = rf * cdec[h] + lax.dot_general(kf, vh, (((0,), (0,)), ((), ())),
                                                   preferred_element_type=F32)
        oc = o - jnp.mean(o, axis=-1, keepdims=True)
        on = oc * lax.rsqrt(jnp.mean(oc * oc, axis=-1, keepdims=True) + GN_EPS)
        og = (on * _silu(g_ref[:, h * RET_DV:(h + 1) * RET_DV].astype(F32))).astype(BF16)
        acc = acc + jnp.dot(og, wo_ref[h * RET_DV:(h + 1) * RET_DV, :], preferred_element_type=F32)
    o_ref[...] = acc.astype(o_ref.dtype)


def _retention_tables(chunk, s_max):
    log_gamma = jnp.log1p(-jnp.exp2(-5.0 - jnp.arange(RET_HEADS, dtype=F32)))
    idx = jnp.arange(chunk, dtype=F32)
    dist = jnp.abs(idx[:, None] - idx[None, :])
    dmat = jnp.exp(log_gamma[:, None, None] * dist[None])

    def rows(e):
        return jnp.broadcast_to(jnp.exp(log_gamma[:, None] * e[None, :])[:, :, None],
                                (RET_HEADS, chunk, RET_HALF))

    qdf = rows(idx + 1.0)
    qdb = rows(chunk - idx)
    kdf = rows(chunk - 1.0 - idx)
    kdb = rows(idx)
    theta = 1.0 / (ROPE_BASE ** jnp.linspace(0.0, 1.0, RET_HALF, dtype=F32))
    ang = jnp.arange(s_max, dtype=F32)[:, None] * theta[None, :]
    return jnp.cos(ang), jnp.sin(ang), dmat, qdf, qdb, kdf, kdb


def _chunk_decay(chunk):
    lg = np.log1p(-np.exp2(-5.0 - np.arange(RET_HEADS, dtype=np.float32))).astype(np.float32)
    return tuple(float(v) for v in np.exp(lg * np.float32(chunk)).astype(np.float32))


def _retention(geom, proj_a, w_ret_o, tables, *, chunk):
    t = proj_a.shape[0]
    n_chunks = t // chunk
    cos, sin, dmat, qdf, qdb, kdf, kdb = tables
    cdec = _chunk_decay(chunk)
    state_shape = (RET_HEADS, RET_DK, RET_DV)

    def pos_blk(c):
        return _tile_pos(geom, c, chunk) // chunk

    rev = lambda i: n_chunks - 1 - i
    tab_spec = pl.BlockSpec((RET_HEADS, chunk, RET_HALF), lambda i: (0, 0, 0))
    rb = pl.pallas_call(
        functools.partial(_ret_bwd_state_kernel, geom=geom, chunk=chunk, cdec=cdec),
        grid=(n_chunks,),
        in_specs=[
            pl.BlockSpec((chunk, RET_QK_W), lambda i: (rev(i), 1)),
            pl.BlockSpec((chunk, RET_V_W), lambda i: (rev(i), 1)),
            pl.BlockSpec((chunk, RET_HALF), lambda i: (pos_blk(rev(i)), 0)),
            pl.BlockSpec((chunk, RET_HALF), lambda i: (pos_blk(rev(i)), 0)),
            tab_spec,
        ],
        out_specs=pl.BlockSpec((None,) + state_shape, lambda i: (rev(i), 0, 0, 0)),
        out_shape=jax.ShapeDtypeStruct((n_chunks,) + state_shape, BF16),
        scratch_shapes=[pltpu.VMEM(state_shape, F32)],
        compiler_params=_cparams(("arbitrary",)),
        name="ret_bwd_state",
    )(proj_a, proj_a, cos, sin, kdb)

    return pl.pallas_call(
        functools.partial(_ret_main_kernel, geom=geom, chunk=chunk, cdec=cdec),
        grid=(n_chunks,),
        in_specs=[
            pl.BlockSpec((chunk, RET_QK_W), lambda i: (i, 0)),
            pl.BlockSpec((chunk, RET_QK_W), lambda i: (i, 1)),
            pl.BlockSpec((chunk, RET_V_W), lambda i: (i, 1)),
            pl.BlockSpec((chunk, RET_V_W), lambda i: (i, 2)),
            pl.BlockSpec((None,) + state_shape, lambda i: (i, 0, 0, 0)),
            pl.BlockSpec((chunk, RET_HALF), lambda i: (pos_blk(i), 0)),
            pl.BlockSpec((chunk, RET_HALF), lambda i: (pos_blk(i), 0)),
            pl.BlockSpec((RET_HEADS, chunk, chunk), lambda i: (0, 0, 0)),
            tab_spec, tab_spec, tab_spec,
            pl.BlockSpec((RET_V_W, D_MODEL), lambda i: (0, 0)),
        ],
        out_specs=pl.BlockSpec((chunk, D_MODEL), lambda i: (i, 0)),
        out_shape=jax.ShapeDtypeStruct((t, D_MODEL), BF16),
        scratch_shapes=[pltpu.VMEM(state_shape, F32)],
        compiler_params=_cparams(("arbitrary",)),
        name="ret_main",
    )(proj_a, proj_a, proj_a, proj_a, rb, cos, sin, dmat, qdf, qdb, kdf, w_ret_o)


def _log_sigmoid(x):
    return jnp.minimum(x, 0.0) - jnp.log1p(jnp.exp(-jnp.abs(x)))


def _lru_conv(cur, prev8, next8, wconv_ref, bconv_ref):
    tl = cur.shape[0]
    row8 = lax.broadcasted_iota(jnp.int32, (V7X_SUBLANES, cur.shape[1]), 0)
    acc = bconv_ref[...] + cur * wconv_ref[CONV_LEFT:CONV_LEFT + 1, :]
    for j in range(CONV_WIDTH):
        off = j - CONV_LEFT
        if off == 0:
            continue
        rolled = pltpu.roll(cur, (-off) % tl, axis=0)
        if off < 0:
            halo = pltpu.roll(prev8, (-off) % V7X_SUBLANES, axis=0)
            head = jnp.where(row8 < -off, halo, rolled[:V7X_SUBLANES])
            shifted = jnp.concatenate([head, rolled[V7X_SUBLANES:]], axis=0)
        else:
            halo = pltpu.roll(next8, (-off) % V7X_SUBLANES, axis=0)
            tail = jnp.where(row8 >= V7X_SUBLANES - off, halo, rolled[tl - V7X_SUBLANES:])
            shifted = jnp.concatenate([rolled[:tl - V7X_SUBLANES], tail], axis=0)
        acc = acc + shifted * wconv_ref[j:j + 1, :]
    return acc


def _lru_gates(xc, d, wrg_ref, brg_ref, wig_ref, big_ref, lam_ref, a_ref, u_ref):
    xcb = xc.astype(BF16)
    ls = _log_sigmoid(lam_ref[d])
    for n in range(LRU_BLOCKS):
        sl = slice(n * LRU_BW, (n + 1) * LRU_BW)
        xb = xcb[:, sl]
        r = _sigmoid(jnp.dot(xb, wrg_ref[d, n], preferred_element_type=F32) + brg_ref[d][:, sl])
        ig = _sigmoid(jnp.dot(xb, wig_ref[d, n], preferred_element_type=F32) + big_ref[d][:, sl])
        log_a = LRU_C * r * ls[:, sl]
        a = jnp.exp(log_a)
        a_ref[:, sl] = a
        y = 1.0 - a * a
        u_ref[:, sl] = y * lax.rsqrt(jnp.maximum(y, SQRT_TINY)) * ig * xc[:, sl]


def _lru_scan_rows(a_ref, u_ref, o_ref, h0, reverse):
    tl = a_ref.shape[0]

    def body(s, h):
        t = tl - 1 - s if reverse else s
        h = a_ref[pl.ds(t, 1), :] * h + u_ref[pl.ds(t, 1), :]
        u_ref[pl.ds(t, 1), :] = h
        return h

    h_end = lax.fori_loop(0, tl, body, h0, unroll=8)
    o_ref[...] = u_ref[...].astype(o_ref.dtype)
    return h_end


def _lru_kernel(xf_ref, xfp_ref, xfn_ref, xb_ref, xbp_ref, xbn_ref, wconv_ref, bconv_ref, wrg_ref, brg_ref,
                wig_ref, big_ref, lam_ref, hf_ref, hb_ref, a_ref, u_ref, cf_ref, cb_ref, *, geom, tl):
    i = pl.program_id(0)
    ib = pl.num_programs(0) - 1 - i

    def flags(tile):
        pos = _tile_pos(geom, tile, tl)
        return pos == 0, pos + tl == _tile_seq_len(geom, tile, tl)

    def keep(flag):
        return jnp.where(flag, 0.0, 1.0)

    def conv(cur_ref, prev_ref, next_ref, first, last):
        prev8 = prev_ref[...].astype(F32)[LRU_HALO - V7X_SUBLANES:] * keep(first)
        next8 = next_ref[...].astype(F32)[:V7X_SUBLANES] * keep(last)
        return _lru_conv(cur_ref[...].astype(F32), prev8, next8, wconv_ref, bconv_ref)

    first, last = flags(i)

    @pl.when(first)
    def _():
        cf_ref[...] = jnp.zeros_like(cf_ref)

    xc = conv(xf_ref, xfp_ref, xfn_ref, first, last)
    _lru_gates(xc, 0, wrg_ref, brg_ref, wig_ref, big_ref, lam_ref, a_ref, u_ref)
    cf_ref[0:1, :] = _lru_scan_rows(a_ref, u_ref, hf_ref, cf_ref[0:1, :], False)

    first, last = flags(ib)

    @pl.when(last)
    def _():
        cb_ref[...] = jnp.zeros_like(cb_ref)

    xc = conv(xb_ref, xbp_ref, xbn_ref, first, last)
    _lru_gates(xc, 1, wrg_ref, brg_ref, wig_ref, big_ref, lam_ref, a_ref, u_ref)
    cb_ref[0:1, :] = _lru_scan_rows(a_ref, u_ref, hb_ref, cb_ref[0:1, :], True)


def _lru(geom, proj_b, w_conv, b_conv, w_rg, b_rg, w_ig, b_ig, lam, *, tl):
    t = proj_b.shape[0]
    n = t // tl
    r8 = tl // LRU_HALO
    n8 = t // LRU_HALO
    w = LRU_WIDTH
    rev = lambda i: n - 1 - i
    prev_blk = lambda i: jnp.maximum(i * r8 - 1, 0)
    next_blk = lambda i: jnp.minimum((i + 1) * r8, n8 - 1)
    full = lambda a: pl.BlockSpec(a.shape, lambda i: (0,) * a.ndim)
    b_conv2 = b_conv.reshape(1, w)
    b_rg3 = b_rg.reshape(2, 1, w)
    b_ig3 = b_ig.reshape(2, 1, w)
    lam3 = lam.reshape(2, 1, w)
    return pl.pallas_call(
        functools.partial(_lru_kernel, geom=geom, tl=tl),
        grid=(n,),
        in_specs=[
            pl.BlockSpec((tl, w), lambda i: (i, 0)),
            pl.BlockSpec((LRU_HALO, w), lambda i: (prev_blk(i), 0)),
            pl.BlockSpec((LRU_HALO, w), lambda i: (next_blk(i), 0)),
            pl.BlockSpec((tl, w), lambda i: (rev(i), 0)),
            pl.BlockSpec((LRU_HALO, w), lambda i: (prev_blk(rev(i)), 0)),
            pl.BlockSpec((LRU_HALO, w), lambda i: (next_blk(rev(i)), 0)),
            full(w_conv), full(b_conv2), full(w_rg), full(b_rg3), full(w_ig), full(b_ig3), full(lam3),
        ],
        out_specs=[
            pl.BlockSpec((tl, w), lambda i: (i, 0)),
            pl.BlockSpec((tl, w), lambda i: (rev(i), 0)),
        ],
        out_shape=[jax.ShapeDtypeStruct((t, w), BF16), jax.ShapeDtypeStruct((t, w), BF16)],
        scratch_shapes=[pltpu.VMEM((tl, w), F32), pltpu.VMEM((tl, w), F32),
                        pltpu.VMEM((V7X_SUBLANES, w), F32), pltpu.VMEM((V7X_SUBLANES, w), F32)],
        compiler_params=_cparams(("arbitrary",)),
        name="lru_scan",
    )(proj_b, proj_b, proj_b, proj_b, proj_b, proj_b, w_conv, b_conv2, w_rg, b_rg3, w_ig, b_ig3, lam3)


def _gelu_tanh(x):
    return 0.5 * x * (1.0 + jnp.tanh(math.sqrt(2.0 / math.pi) * (x + 0.044715 * (x * x * x))))


def _merge_kernel(x_ref, ya_ref, hf_ref, hb_ref, gl_ref, ga_ref, gb_ref, mod_ref, wl_ref, wo_ref, o_ref):
    f32 = lambda ref: ref[...].astype(F32)
    y = ((f32(hf_ref) + f32(hb_ref)) * _gelu_tanh(f32(gl_ref))).astype(BF16)
    yb = jnp.dot(y, wl_ref[...], preferred_element_type=F32)
    m = _sigmoid(f32(ga_ref)) * f32(ya_ref) + _sigmoid(f32(gb_ref)) * yb
    mix = jnp.dot(m.astype(BF16), wo_ref[...], preferred_element_type=F32)
    o_ref[...] = x_ref[...] + mod_ref[2:3, :] * mix


def _merge(geom, x, ya, hf, hb, proj_a, proj_b, mod_l, w_lru_o, w_out, *, tm):
    t, d = x.shape
    gate_a_blk = (2 * RET_QK_W + 2 * RET_V_W) // d
    tok = lambda w: pl.BlockSpec((tm, w), lambda i: (i, 0))
    return pl.pallas_call(
        _merge_kernel,
        grid=(t // tm,),
        in_specs=[
            tok(d), tok(d), tok(LRU_WIDTH), tok(LRU_WIDTH),
            pl.BlockSpec((tm, LRU_WIDTH), lambda i: (i, 1)),
            pl.BlockSpec((tm, d), lambda i: (i, gate_a_blk)),
            pl.BlockSpec((tm, d), lambda i: (i, gate_a_blk + 1)),
            pl.BlockSpec((None, 6, d), lambda i: (_tile_seq(geom, i, tm), 0, 0)),
            pl.BlockSpec((LRU_WIDTH, d), lambda i: (0, 0)),
            pl.BlockSpec((d, d), lambda i: (0, 0)),
        ],
        out_specs=tok(d),
        out_shape=jax.ShapeDtypeStruct((t, d), F32),
        compiler_params=_cparams(("parallel",)),
        name="merge",
    )(x, ya, hf, hb, proj_b, proj_a, proj_a, mod_l, w_lru_o, w_out)


def _ffn_kernel(x_ref, mod_ref, g_ref, wg_ref, wu_ref, wd_ref, o_ref, h_ref, acc_ref):
    j = pl.program_id(1)

    @pl.when(j == 0)
    def _():
        h_ref[...] = _rms_mod(x_ref[...], g_ref[...], mod_ref[4:5, :], mod_ref[3:4, :]).astype(BF16)
        acc_ref[...] = jnp.zeros_like(acc_ref)

    h = h_ref[...]
    a = jnp.dot(h, wg_ref[...], preferred_element_type=F32)
    u = jnp.dot(h, wu_ref[...], preferred_element_type=F32)
    acc_ref[...] += jnp.dot((_silu(a) * u).astype(BF16), wd_ref[...], preferred_element_type=F32)

    @pl.when(j == pl.num_programs(1) - 1)
    def _():
        o_ref[...] = x_ref[...] + mod_ref[5:6, :] * acc_ref[...]


def _ffn(geom, x, mod_l, gvec, wg, wu, wd, *, tm, tf):
    t, d = x.shape
    ff = wg.shape[1]
    return pl.pallas_call(
        _ffn_kernel,
        grid=(t // tm, ff // tf),
        in_specs=[
            pl.BlockSpec((tm, d), lambda i, j: (i, 0)),
            pl.BlockSpec((None, 6, d), lambda i, j: (_tile_seq(geom, i, tm), 0, 0)),
            pl.BlockSpec((1, d), lambda i, j: (0, 0)),
            pl.BlockSpec((d, tf), lambda i, j: (0, j)),
            pl.BlockSpec((d, tf), lambda i, j: (0, j)),
            pl.BlockSpec((tf, d), lambda i, j: (j, 0)),
        ],
        out_specs=pl.BlockSpec((tm, d), lambda i, j: (i, 0)),
        out_shape=jax.ShapeDtypeStruct((t, d), F32),
        scratch_shapes=[pltpu.VMEM((tm, d), BF16), pltpu.VMEM((tm, d), F32)],
        compiler_params=_cparams(("parallel", "arbitrary")),
        name="ffn_dense",
    )(x, mod_l, gvec, wg, wu, wd)


ROUTE_E0, ROUTE_E1, ROUTE_R0, ROUTE_R1, ROUTE_G0, ROUTE_G1 = range(6)

ROW_TILES = D_MODEL // V7X_LANES


def _store_token_rows(ref, val):
    rows = val.shape[0]
    for j in range(ROW_TILES):
        ref[pl.ds(j, rows, stride=ROW_TILES), :] = val[:, j * V7X_LANES:(j + 1) * V7X_LANES]


def _load_token_rows(ref, rows):
    return jnp.concatenate([ref[pl.ds(j, rows, stride=ROW_TILES), :] for j in range(ROW_TILES)], axis=1)


def _router_kernel(x_ref, mod_ref, g_ref, wr_ref, h_ref, route_ref, cnt_ref, carry_ref):
    i = pl.program_id(0)
    tm = x_ref.shape[0]

    @pl.when(i == 0)
    def _():
        carry_ref[...] = jnp.zeros_like(carry_ref)

    h = _rms_mod(x_ref[...], g_ref[...], mod_ref[4:5, :], mod_ref[3:4, :])
    _store_token_rows(h_ref, h)
    logits = jnp.dot(h, wr_ref[...], preferred_element_type=F32, precision=lax.Precision.HIGHEST)
    lane = lax.broadcasted_iota(jnp.int32, logits.shape, 1).astype(F32)
    logits = jnp.where(lane < N_EXPERTS, logits, -jnp.inf)
    m1 = jnp.max(logits, axis=-1, keepdims=True)
    i1 = jnp.min(jnp.where(logits == m1, lane, float(V7X_LANES)), axis=-1, keepdims=True)
    rest = jnp.where(lane == i1, -jnp.inf, logits)
    m2 = jnp.max(rest, axis=-1, keepdims=True)
    i2 = jnp.min(jnp.where(rest == m2, lane, float(V7X_LANES)), axis=-1, keepdims=True)
    ex = jnp.exp(m2 - m1)
    g1 = 1.0 / (1.0 + ex)
    g2 = ex / (1.0 + ex)
    sel1 = lane == i1
    sel2 = lane == i2
    onehot = jnp.where(sel1 | sel2, 1.0, 0.0)
    row = lax.broadcasted_iota(jnp.int32, (tm, tm), 0)
    col = lax.broadcasted_iota(jnp.int32, (tm, tm), 1)
    lower = jnp.where(col < row, 1.0, 0.0).astype(BF16)
    before = jnp.dot(lower, onehot.astype(BF16), preferred_element_type=F32) + carry_ref[0:1, :]
    r1 = jnp.sum(jnp.where(sel1, before, 0.0), axis=-1, keepdims=True)
    r2 = jnp.sum(jnp.where(sel2, before, 0.0), axis=-1, keepdims=True)
    out_lane = lax.broadcasted_iota(jnp.int32, route_ref.shape, 1)
    vals = (i1.astype(F32), i2.astype(F32), r1, r2, g1, g2)
    packed = jnp.zeros(route_ref.shape, F32)
    for slot, v in enumerate(vals):
        packed = jnp.where(out_lane == slot, v, packed)
    route_ref[...] = packed
    carry = carry_ref[0:1, :] + jnp.sum(onehot, axis=0, keepdims=True)
    carry_ref[0:1, :] = carry
    cnt_ref[...] = jnp.broadcast_to(carry, cnt_ref.shape).astype(jnp.int32)


def _router(geom, x, mod_l, gvec, w_router, *, tm):
    t, d = x.shape
    w_pad = jnp.pad(w_router, ((0, 0), (0, V7X_LANES - N_EXPERTS)))
    return pl.pallas_call(
        _router_kernel,
        grid=(t // tm,),
        in_specs=[
            pl.BlockSpec((tm, d), lambda i: (i, 0)),
            pl.BlockSpec((None, 6, d), lambda i: (_tile_seq(geom, i, tm), 0, 0)),
            pl.BlockSpec((1, d), lambda i: (0, 0)),
            pl.BlockSpec((d, V7X_LANES), lambda i: (0, 0)),
        ],
        out_specs=[
            pl.BlockSpec((tm * ROW_TILES, V7X_LANES), lambda i: (i, 0)),
            pl.BlockSpec((tm, V7X_LANES), lambda i: (i, 0)),
            pl.BlockSpec((V7X_SUBLANES, V7X_LANES), lambda i: (0, 0)),
        ],
        out_shape=[
            jax.ShapeDtypeStruct((t * ROW_TILES, V7X_LANES), F32),
            jax.ShapeDtypeStruct((t, V7X_LANES), F32),
            jax.ShapeDtypeStruct((V7X_SUBLANES, V7X_LANES), jnp.int32),
        ],
        scratch_shapes=[pltpu.VMEM((V7X_SUBLANES, V7X_LANES), F32)],
        compiler_params=_cparams(("arbitrary",)),
        name="router",
    )(x, mod_l, gvec, w_pad)


def _token_copy(src, src_tok, dst, dst_tok, sem):
    s = pl.multiple_of(src_tok * ROW_TILES, ROW_TILES)
    d = pl.multiple_of(dst_tok * ROW_TILES, ROW_TILES)
    return pltpu.make_async_copy(src.at[pl.ds(s, ROW_TILES), :], dst.at[pl.ds(d, ROW_TILES), :], sem)


def _dispatch_kernel(dest_ref, src_ref, init_hbm, out_hbm, sem, *, td):
    del init_hbm

    def issue(t, carry):
        for k in range(TOP_K):
            _token_copy(src_ref, t, out_hbm, dest_ref[k, t], sem).start()
        return carry

    lax.fori_loop(0, td, issue, 0, unroll=8)
    for _ in range(TOP_K):
        pltpu.make_async_copy(src_ref, out_hbm.at[pl.ds(0, td * ROW_TILES), :], sem).wait()


def _dispatch(h2_rows, dest, n_rows, *, td):
    init = jnp.zeros((n_rows * ROW_TILES, V7X_LANES), h2_rows.dtype)
    return pl.pallas_call(
        functools.partial(_dispatch_kernel, td=td),
        grid=(dest.shape[1] // td,),
        in_specs=[
            pl.BlockSpec((TOP_K, td), lambda i: (0, i), memory_space=pltpu.SMEM),
            pl.BlockSpec((td * ROW_TILES, V7X_LANES), lambda i: (i, 0)),
            pl.BlockSpec(memory_space=pl.ANY),
        ],
        out_specs=pl.BlockSpec(memory_space=pl.ANY),
        out_shape=jax.ShapeDtypeStruct(init.shape, init.dtype),
        scratch_shapes=[pltpu.SemaphoreType.DMA(())],
        input_output_aliases={2: 0},
        compiler_params=_cparams(("arbitrary",)),
        name="moe_dispatch",
    )(dest, h2_rows, init)


def _experts_kernel(blk_e_ref, nused_ref, x_ref, wg_ref, wu_ref, wd_ref, o_ref, xb_ref, acc_ref):
    del blk_e_ref
    b = pl.program_id(0)
    j = pl.program_id(1)
    bm = xb_ref.shape[0]

    @pl.when(b < nused_ref[0])
    def _():
        @pl.when(j == 0)
        def _():
            xb_ref[...] = _load_token_rows(x_ref, bm).astype(BF16)
            acc_ref[...] = jnp.zeros_like(acc_ref)

        xb = xb_ref[...]
        a = jnp.dot(xb, wg_ref[...], preferred_element_type=F32)
        u = jnp.dot(xb, wu_ref[...], preferred_element_type=F32)
        acc_ref[...] += jnp.dot((_silu(a) * u).astype(BF16), wd_ref[...], preferred_element_type=F32)

        @pl.when(j == pl.num_programs(1) - 1)
        def _():
            _store_token_rows(o_ref, acc_ref[...])


def _experts(xbuf, blk_e, n_used, wg, wu, wd, *, bm, tf):
    p = xbuf.shape[0] // ROW_TILES
    d = D_MODEL
    ff = wg.shape[2]
    nj = ff // tf
    rows_blk = (bm * ROW_TILES, V7X_LANES)

    def blk(b, nu):
        return jnp.minimum(b, nu[0] - 1)

    def ffc(b, j, nu):
        return jnp.where(b < nu[0], j, nj - 1)

    grid_spec = pltpu.PrefetchScalarGridSpec(
        num_scalar_prefetch=2,
        grid=(p // bm, nj),
        in_specs=[
            pl.BlockSpec(rows_blk, lambda b, j, be, nu: (blk(b, nu), 0)),
            pl.BlockSpec((None, d, tf), lambda b, j, be, nu: (be[blk(b, nu)], 0, ffc(b, j, nu))),
            pl.BlockSpec((None, d, tf), lambda b, j, be, nu: (be[blk(b, nu)], 0, ffc(b, j, nu))),
            pl.BlockSpec((None, tf, d), lambda b, j, be, nu: (be[blk(b, nu)], ffc(b, j, nu), 0)),
        ],
        out_specs=pl.BlockSpec(rows_blk, lambda b, j, be, nu: (blk(b, nu), 0)),
        scratch_shapes=[pltpu.VMEM((bm, d), BF16), pltpu.VMEM((bm, d), F32)],
    )
    return pl.pallas_call(
        _experts_kernel,
        grid_spec=grid_spec,
        out_shape=jax.ShapeDtypeStruct(xbuf.shape, F32),
        compiler_params=_cparams(("arbitrary", "arbitrary")),
        name="moe_experts",
    )(blk_e, n_used, xbuf, wg, wu, wd)


def _moe_out_kernel(dest_ref, x_ref, route_ref, mod_ref, gf_ref, ybuf_hbm, o_ref, y0_ref, y1_ref, sem, *,
                    final_norm):
    tm = x_ref.shape[0]
    y_refs = (y0_ref, y1_ref)

    def issue(t, carry):
        for k in range(TOP_K):
            _token_copy(ybuf_hbm, dest_ref[k, t], y_refs[k], t, sem).start()
        return carry

    lax.fori_loop(0, tm, issue, 0, unroll=8)
    for k in range(TOP_K):
        pltpu.make_async_copy(ybuf_hbm.at[pl.ds(0, tm * ROW_TILES), :], y_refs[k], sem).wait()
    g1 = route_ref[:, ROUTE_G0:ROUTE_G0 + 1]
    g2 = route_ref[:, ROUTE_G1:ROUTE_G1 + 1]
    y = _load_token_rows(y0_ref, tm) * g1 + _load_token_rows(y1_ref, tm) * g2
    x = x_ref[...] + mod_ref[5:6, :] * y
    if final_norm:
        ms = jnp.mean(x * x, axis=-1, keepdims=True)
        x = x * lax.rsqrt(ms + RMS_EPS) * gf_ref[...]
    o_ref[...] = x


def _moe_out(geom, x, ybuf, dest, route, mod_l, g_final, *, tm, final_norm):
    t, d = x.shape
    return pl.pallas_call(
        functools.partial(_moe_out_kernel, final_norm=final_norm),
        grid=(t // tm,),
        in_specs=[
            pl.BlockSpec((TOP_K, tm), lambda i: (0, i), memory_space=pltpu.SMEM),
            pl.BlockSpec((tm, d), lambda i: (i, 0)),
            pl.BlockSpec((tm, V7X_LANES), lambda i: (i, 0)),
            pl.BlockSpec((None, 6, d), lambda i: (_tile_seq(geom, i, tm), 0, 0)),
            pl.BlockSpec((1, d), lambda i: (0, 0)),
            pl.BlockSpec(memory_space=pl.ANY),
        ],
        out_specs=pl.BlockSpec((tm, d), lambda i: (i, 0)),
        out_shape=jax.ShapeDtypeStruct((t, d), F32),
        scratch_shapes=[pltpu.VMEM((tm * ROW_TILES, V7X_LANES), F32), pltpu.VMEM((tm * ROW_TILES, V7X_LANES), F32),
                        pltpu.SemaphoreType.DMA(())],
        compiler_params=_cparams(("arbitrary",)),
        name="moe_out",
    )(dest, x, route, mod_l, g_final, ybuf)


def _final_norm_kernel(x_ref, g_ref, o_ref):
    x = x_ref[...]
    o_ref[...] = x * lax.rsqrt(jnp.mean(x * x, axis=-1, keepdims=True) + RMS_EPS) * g_ref[...]


def _final_norm(x, g_final, *, tm):
    t, d = x.shape
    return pl.pallas_call(
        _final_norm_kernel,
        grid=(t // tm,),
        in_specs=[pl.BlockSpec((tm, d), lambda i: (i, 0)), pl.BlockSpec((1, d), lambda i: (0, 0))],
        out_specs=pl.BlockSpec((tm, d), lambda i: (i, 0)),
        out_shape=jax.ShapeDtypeStruct((t, d), F32),
        compiler_params=_cparams(("parallel",)),
        name="final_norm",
    )(x, g_final)


def _moe(geom, x, mod_l, gvec, w_router, wg, wu, wd, g_final, *, tiles, final_norm):
    t, d = x.shape
    bm = tiles["moe_rows"]
    h2, route, counts = _router(geom, x, mod_l, gvec, w_router, tm=tiles["router"])
    counts = counts[0, :N_EXPERTS]
    padded = ((counts + bm - 1) // bm) * bm
    pad_end = jnp.cumsum(padded)
    pad_start = pad_end - padded
    experts = route[:, ROUTE_E0:ROUTE_E1 + 1].astype(jnp.int32)
    ranks = route[:, ROUTE_R0:ROUTE_R1 + 1].astype(jnp.int32)
    dest = (pad_start[experts] + ranks).T
    n_rows = t * TOP_K + N_EXPERTS * bm
    n_blk = n_rows // bm
    blk_start = jnp.arange(n_blk, dtype=jnp.int32) * bm
    blk_e = jnp.minimum(jnp.sum((pad_end[None, :] <= blk_start[:, None]).astype(jnp.int32), axis=1),
                        N_EXPERTS - 1).astype(jnp.int32)
    n_used = (pad_end[-1:] // bm).astype(jnp.int32)
    xbuf = _dispatch(h2, dest, n_rows, td=tiles["dma_rows"])
    ybuf = _experts(xbuf, blk_e, n_used, wg, wu, wd, bm=bm, tf=tiles["expert_ff"])
    return _moe_out(geom, x, ybuf, dest, route, mod_l, g_final, tm=tiles["token"], final_norm=final_norm)


def _pick_tiles(geom):
    s = math.gcd(geom.s_p, geom.s_s) if geom.n_p and geom.n_s else (geom.s_p if geom.n_p else geom.s_s)
    return {
        "proj": min(1024, s),
        "proj_cols": 1024,
        "ret_chunk": min(256, s),
        "lru": min(512, s),
        "token": min(512, s),
        "ffn_cols": 1408,
        "router": min(512, s),
        "moe_rows": 512,
        "expert_ff": 1792,
        "dma_rows": min(2048, s),
    }


def _trunk(geom, x, c_all, w_ada, b_ada, g_norm1, g_norm2, w_in, w_conv, b_conv, w_rg, b_rg, w_ig, b_ig,
           lru_lambda, w_ret_o, w_lru_o, w_out, w_ff_gate, w_ff_up, w_ff_down,
           w_router, w_e_gate, w_e_up, w_e_down, g_final, tiles):
    depth = w_in.shape[0]
    d = D_MODEL
    n_pad = -(-geom.n_seq // V7X_SUBLANES) * V7X_SUBLANES
    c_pad = jnp.pad(c_all, ((0, n_pad - geom.n_seq), (0, 0)))
    mod = _adaln_mod(c_pad, w_ada, b_ada)
    tables = _retention_tables(tiles["ret_chunk"], max(geom.s_p if geom.n_p else 0, geom.s_s if geom.n_s else 0))

    o_xl = 2 * RET_QK_W + 2 * RET_V_W
    o_ga = o_xl + 2 * LRU_WIDTH
    for l in range(depth):
        wl = w_in[l]
        w_a = jnp.concatenate([wl[:, :o_xl], wl[:, o_ga:]], axis=1).astype(BF16)
        w_b = wl[:, o_xl:o_ga].astype(BF16)
        g1 = g_norm1[l].reshape(1, d)
        g2 = g_norm2[l].reshape(1, d)
        proj_a = _norm_proj(geom, x, mod[l], g1, w_a, shift_row=0, scale_row=1,
                            tm=tiles["proj"], tn=tiles["proj_cols"])
        proj_b = _norm_proj(geom, x, mod[l], g1, w_b, shift_row=0, scale_row=1,
                            tm=tiles["proj"], tn=LRU_WIDTH)
        ya = _retention(geom, proj_a, w_ret_o[l].astype(BF16), tables, chunk=tiles["ret_chunk"])
        hf, hb = _lru(geom, proj_b, w_conv[l], b_conv[l], w_rg[l].astype(BF16), b_rg[l],
                      w_ig[l].astype(BF16), b_ig[l], lru_lambda[l], tl=tiles["lru"])
        x = _merge(geom, x, ya, hf, hb, proj_a, proj_b, mod[l], w_lru_o[l].astype(BF16),
                   w_out[l].astype(BF16), tm=tiles["token"])
        j = l // 2
        last = l == depth - 1
        if l % 2 == 0:
            x = _ffn(geom, x, mod[l], g2, w_ff_gate[j].astype(BF16), w_ff_up[j].astype(BF16),
                     w_ff_down[j].astype(BF16), tm=tiles["token"], tf=tiles["ffn_cols"])
            if last:
                x = _final_norm(x, g_final.reshape(1, d), tm=tiles["token"])
        else:
            x = _moe(geom, x, mod[l], g2, w_router[j], w_e_gate[j].astype(BF16), w_e_up[j].astype(BF16),
                     w_e_down[j].astype(BF16), g_final.reshape(1, d), tiles=tiles, final_norm=last)
    return x


def kernel(x_prompt, x_sample, c_prompt, c_sample, w_ada, b_ada, g_norm1, g_norm2, w_in, w_conv, b_conv, w_rg, b_rg, w_ig, b_ig, lru_lambda, w_ret_o, w_lru_o, w_out, w_ff_gate, w_ff_up, w_ff_down, w_router, w_e_gate, w_e_up, w_e_down, g_final):
    n_p, s_p, d = x_prompt.shape
    n_s, s_s, _ = x_sample.shape
    geom = Geom(n_p, s_p, n_s, s_s)
    x = jnp.concatenate([x_prompt.reshape(-1, d), x_sample.reshape(-1, d)], axis=0)
    c_all = jnp.concatenate([c_prompt, c_sample], axis=0)
    y = _trunk(geom, x, c_all, w_ada, b_ada, g_norm1, g_norm2, w_in, w_conv, b_conv, w_rg, b_rg, w_ig, b_ig,
               lru_lambda, w_ret_o, w_lru_o, w_out, w_ff_gate, w_ff_up, w_ff_down,
               w_router, w_e_gate, w_e_up, w_e_down, g_final, _pick_tiles(geom))
    return (y[:geom.tokens_p].reshape(n_p, s_p, d), y[geom.tokens_p:].reshape(n_s, s_s, d))
```

```python
import functools
import math
from typing import NamedTuple

import jax
import jax.numpy as jnp
import numpy as np
from jax import lax
from jax.experimental import pallas as pl
from jax.experimental.pallas import tpu as pltpu

F32 = jnp.float32
BF16 = jnp.bfloat16

D_MODEL = 1024
RET_HEADS = 4
RET_DK = 256
RET_DV = 512
RET_HALF = RET_DK // 2
RET_QK_W = RET_HEADS * RET_DK
RET_V_W = RET_HEADS * RET_DV
ROPE_BASE = 10000.0
LRU_WIDTH = 1280
LRU_BLOCKS = 10
LRU_BW = LRU_WIDTH // LRU_BLOCKS
LRU_C = 8.0
CONV_WIDTH = 4
CONV_LEFT = 2
LRU_HALO = 16
N_EXPERTS = 8
TOP_K = 2
RMS_EPS = 1e-6
GN_EPS = 1e-5
SQRT_TINY = 1e-30

V7X_LANES = 128
V7X_SUBLANES = 8
V7X_VMEM_BYTES = 64 * 1024 * 1024
VMEM_LIMIT = (V7X_VMEM_BYTES * 3) // 4

PA_W = 2 * RET_QK_W + 2 * RET_V_W + 2 * D_MODEL
PB_W = 2 * LRU_WIDTH


class Geom(NamedTuple):
    n_p: int
    s_p: int
    n_s: int
    s_s: int

    @property
    def tokens_p(self):
        return self.n_p * self.s_p

    @property
    def tokens(self):
        return self.n_p * self.s_p + self.n_s * self.s_s

    @property
    def n_seq(self):
        return self.n_p + self.n_s


def _tile_seq(g, i, tm):
    t0 = i * tm
    return jnp.where(t0 < g.tokens_p, t0 // g.s_p, g.n_p + (t0 - g.tokens_p) // g.s_s)


def _tile_pos(g, i, tm):
    t0 = i * tm
    return jnp.where(t0 < g.tokens_p, t0 % g.s_p, (t0 - g.tokens_p) % g.s_s)


def _tile_seq_len(g, i, tm):
    return jnp.where(i * tm < g.tokens_p, g.s_p, g.s_s)


def _cparams(sem):
    return pltpu.CompilerParams(dimension_semantics=sem, vmem_limit_bytes=VMEM_LIMIT)


def _sigmoid(x):
    return 0.5 * jnp.tanh(0.5 * x) + 0.5


def _silu(x):
    return x * _sigmoid(x)


def _rms_mod(x, gvec, scale, shift):
    ms = jnp.mean(x * x, axis=-1, keepdims=True)
    y = x * lax.rsqrt(ms + RMS_EPS) * gvec
    return y * (1.0 + scale) + shift


def _mod_kernel(c_ref, w_ref, b_ref, o_ref):
    c = c_ref[...]
    o_ref[...] = jnp.dot(_silu(c), w_ref[...], preferred_element_type=F32,
                         precision=lax.Precision.HIGHEST) + b_ref[...]


def _adaln_mod(c_pad, w_ada, b_ada):
    depth, d, _ = w_ada.shape
    n = c_pad.shape[0]
    out = pl.pallas_call(
        _mod_kernel,
        grid=(depth, 6),
        in_specs=[
            pl.BlockSpec((n, d), lambda l, j: (0, 0)),
            pl.BlockSpec((None, d, d), lambda l, j: (l, 0, j)),
            pl.BlockSpec((None, None, 1, d), lambda l, j: (l, j, 0, 0)),
        ],
        out_specs=pl.BlockSpec((None, None, n, d), lambda l, j: (l, j, 0, 0)),
        out_shape=jax.ShapeDtypeStruct((depth, 6, n, d), F32),
        compiler_params=_cparams(("arbitrary", "arbitrary")),
        name="adaln_mod",
    )(c_pad, w_ada, b_ada.reshape(depth, 6, 1, d))
    return out.transpose(0, 2, 1, 3)


def _norm_proj_kernel(x_ref, mod_ref, g_ref, w_ref, o_ref, h_ref, *, shift_row, scale_row):
    @pl.when(pl.program_id(1) == 0)
    def _():
        h = _rms_mod(x_ref[...], g_ref[...], mod_ref[scale_row:scale_row + 1, :],
                     mod_ref[shift_row:shift_row + 1, :])
        h_ref[...] = h.astype(BF16)

    o_ref[...] = jnp.dot(h_ref[...], w_ref[...], preferred_element_type=F32).astype(o_ref.dtype)


def _norm_proj(geom, x, mod_l, gvec, w, *, shift_row, scale_row, tm, tn):
    t, d = x.shape
    n = w.shape[1]
    return pl.pallas_call(
        functools.partial(_norm_proj_kernel, shift_row=shift_row, scale_row=scale_row),
        grid=(t // tm, n // tn),
        in_specs=[
            pl.BlockSpec((tm, d), lambda i, j: (i, 0)),
            pl.BlockSpec((None, 6, d), lambda i, j: (_tile_seq(geom, i, tm), 0, 0)),
            pl.BlockSpec((1, d), lambda i, j: (0, 0)),
            pl.BlockSpec((d, tn), lambda i, j: (0, j)),
        ],
        out_specs=pl.BlockSpec((tm, tn), lambda i, j: (i, j)),
        out_shape=jax.ShapeDtypeStruct((t, n), BF16),
        scratch_shapes=[pltpu.VMEM((tm, d), BF16)],
        compiler_params=_cparams(("parallel", "arbitrary")),
        name="norm_proj",
    )(x, mod_l, gvec, w)


def _rot_halves(ref, h, cos, sin):
    a = ref[:, h * RET_DK:h * RET_DK + RET_HALF].astype(F32)
    b = ref[:, h * RET_DK + RET_HALF:(h + 1) * RET_DK].astype(F32)
    return a * cos - b * sin, a * sin + b * cos


def _ret_bwd_state_kernel(k_ref, v_ref, cos_ref, sin_ref, kdb_ref, rb_ref, r_ref, *, geom, chunk, cdec):
    c = pl.num_programs(0) - 1 - pl.program_id(0)
    is_last = _tile_pos(geom, c, chunk) + chunk == _tile_seq_len(geom, c, chunk)

    @pl.when(is_last)
    def _():
        r_ref[...] = jnp.zeros_like(r_ref)

    cos = cos_ref[...]
    sin = sin_ref[...]
    for h in range(RET_HEADS):
        k1, k2 = _rot_halves(k_ref, h, cos, sin)
        dec = kdb_ref[h] * (RET_DK ** -0.5)
        kd = jnp.concatenate([k1 * dec, k2 * dec], axis=1).astype(BF16)
        vh = v_ref[:, h * RET_DV:(h + 1) * RET_DV].astype(BF16)
        r = r_ref[h]
        rb_ref[h] = r.astype(BF16)
        upd = lax.dot_general(kd, vh, (((0,), (0,)), ((), ())), preferred_element_type=F32)
        r_ref[h] = r * cdec[h] + upd


def _ret_main_kernel(q_ref, k_ref, v_ref, g_ref, rb_ref, cos_ref, sin_ref, dmat_ref, qdf_ref, qdb_ref,
                     kdf_ref, wo_ref, o_ref, rf_ref, *, geom, chunk, cdec):
    i = pl.program_id(0)

    @pl.when(_tile_pos(geom, i, chunk) == 0)
    def _():
        rf_ref[...] = jnp.zeros_like(rf_ref)

    cos = cos_ref[...]
    sin = sin_ref[...]
    acc = jnp.zeros(o_ref.shape, F32)
    for h in range(RET_HEADS):
        q1, q2 = _rot_halves(q_ref, h, cos, sin)
        k1, k2 = _rot_halves(k_ref, h, cos, sin)
        k1 = k1 * (RET_DK ** -0.5)
        k2 = k2 * (RET_DK ** -0.5)
        qr = jnp.concatenate([q1, q2], axis=1).astype(BF16)
        kr = jnp.concatenate([k1, k2], axis=1).astype(BF16)
        vh = v_ref[:, h * RET_DV:(h + 1) * RET_DV].astype(BF16)
        s = lax.dot_general(qr, kr, (((1,), (1,)), ((), ())), preferred_element_type=F32) * dmat_ref[h]
        o = jnp.dot(s.astype(BF16), vh, preferred_element_type=F32)
        qdf = qdf_ref[h]
        qf = jnp.concatenate([q1 * qdf, q2 * qdf], axis=1).astype(BF16)
        rf = rf_ref[h]
        o = o + jnp.dot(qf, rf.astype(BF16), preferred_element_type=F32)
        qdb = qdb_ref[h]
        qb = jnp.concatenate([q1 * qdb, q2 * qdb], axis=1).astype(BF16)
        o = o + jnp.dot(qb, rb_ref[h], preferred_element_type=F32)
        kdf = kdf_ref[h]
        kf = jnp.concatenate([k1 * kdf, k2 * kdf], axis=1).astype(BF16)
        rf_ref[h] = rf * cdec[h] + lax.dot_general(kf, vh, (((0,), (0,)), ((), ())),
                                                   preferred_element_type=F32)
        oc = o - jnp.mean(o, axis=-1, keepdims=True)
        on = oc * lax.rsqrt(jnp.mean(oc * oc, axis=-1, keepdims=True) + GN_EPS)
        og = (on * _silu(g_ref[:, h * RET_DV:(h + 1) * RET_DV].astype(F32))).astype(BF16)
        acc = acc + jnp.dot(og, wo_ref[h * RET_DV:(h + 1) * RET_DV, :], preferred_element_type=F32)
    o_ref[...] = acc.astype(o_ref.dtype)


def _retention_tables(chunk, s_max):
    log_gamma = jnp.log1p(-jnp.exp2(-5.0 - jnp.arange(RET_HEADS, dtype=F32)))
    idx = jnp.arange(chunk, dtype=F32)
    dist = jnp.abs(idx[:, None] - idx[None, :])
    dmat = jnp.exp(log_gamma[:, None, None] * dist[None])

    def rows(e):
        return jnp.broadcast_to(jnp.exp(log_gamma[:, None] * e[None, :])[:, :, None],
                                (RET_HEADS, chunk, RET_HALF))

    qdf = rows(idx + 1.0)
    qdb = rows(chunk - idx)
    kdf = rows(chunk - 1.0 - idx)
    kdb = rows(idx)
    theta = 1.0 / (ROPE_BASE ** jnp.linspace(0.0, 1.0, RET_HALF, dtype=F32))
    ang = jnp.arange(s_max, dtype=F32)[:, None] * theta[None, :]
    return jnp.cos(ang), jnp.sin(ang), dmat, qdf, qdb, kdf, kdb


def _chunk_decay(chunk):
    lg = np.log1p(-np.exp2(-5.0 - np.arange(RET_HEADS, dtype=np.float32))).astype(np.float32)
    return tuple(float(v) for v in np.exp(lg * np.float32(chunk)).astype(np.float32))


def _retention(geom, proj_a, w_ret_o, tables, *, chunk):
    t = proj_a.shape[0]
    n_chunks = t // chunk
    cos, sin, dmat, qdf, qdb, kdf, kdb = tables
    cdec = _chunk_decay(chunk)
    state_shape = (RET_HEADS, RET_DK, RET_DV)

    def pos_blk(c):
        return _tile_pos(geom, c, chunk) // chunk

    rev = lambda i: n_chunks - 1 - i
    tab_spec = pl.BlockSpec((RET_HEADS, chunk, RET_HALF), lambda i: (0, 0, 0))
    rb = pl.pallas_call(
        functools.partial(_ret_bwd_state_kernel, geom=geom, chunk=chunk, cdec=cdec),
        grid=(n_chunks,),
        in_specs=[
            pl.BlockSpec((chunk, RET_QK_W), lambda i: (rev(i), 1)),
            pl.BlockSpec((chunk, RET_V_W), lambda i: (rev(i), 1)),
            pl.BlockSpec((chunk, RET_HALF), lambda i: (pos_blk(rev(i)), 0)),
            pl.BlockSpec((chunk, RET_HALF), lambda i: (pos_blk(rev(i)), 0)),
            tab_spec,
        ],
        out_specs=pl.BlockSpec((None,) + state_shape, lambda i: (rev(i), 0, 0, 0)),
        out_shape=jax.ShapeDtypeStruct((n_chunks,) + state_shape, BF16),
        scratch_shapes=[pltpu.VMEM(state_shape, F32)],
        compiler_params=_cparams(("arbitrary",)),
        name="ret_bwd_state",
    )(proj_a, proj_a, cos, sin, kdb)

    return pl.pallas_call(
        functools.partial(_ret_main_kernel, geom=geom, chunk=chunk, cdec=cdec),
        grid=(n_chunks,),
        in_specs=[
            pl.BlockSpec((chunk, RET_QK_W), lambda i: (i, 0)),
            pl.BlockSpec((chunk, RET_QK_W), lambda i: (i, 1)),
            pl.BlockSpec((chunk, RET_V_W), lambda i: (i, 1)),
            pl.BlockSpec((chunk, RET_V_W), lambda i: (i, 2)),
            pl.BlockSpec((None,) + state_shape, lambda i: (i, 0, 0, 0)),
            pl.BlockSpec((chunk, RET_HALF), lambda i: (pos_blk(i), 0)),
            pl.BlockSpec((chunk, RET_HALF), lambda i: (pos_blk(i), 0)),
            pl.BlockSpec((RET_HEADS, chunk, chunk), lambda i: (0, 0, 0)),
            tab_spec, tab_spec, tab_spec,
            pl.BlockSpec((RET_V_W, D_MODEL), lambda i: (0, 0)),
        ],
        out_specs=pl.BlockSpec((chunk, D_MODEL), lambda i: (i, 0)),
        out_shape=jax.ShapeDtypeStruct((t, D_MODEL), BF16),
        scratch_shapes=[pltpu.VMEM(state_shape, F32)],
        compiler_params=_cparams(("arbitrary",)),
        name="ret_main",
    )(proj_a, proj_a, proj_a, proj_a, rb, cos, sin, dmat, qdf, qdb, kdf, w_ret_o)


def _log_sigmoid(x):
    return jnp.minimum(x, 0.0) - jnp.log1p(jnp.exp(-jnp.abs(x)))


def _lru_conv(cur, prev8, next8, wconv_ref, bconv_ref):
    tl = cur.shape[0]
    row8 = lax.broadcasted_iota(jnp.int32, (V7X_SUBLANES, cur.shape[1]), 0)
    acc = bconv_ref[...] + cur * wconv_ref[CONV_LEFT:CONV_LEFT + 1, :]
    for j in range(CONV_WIDTH):
        off = j - CONV_LEFT
        if off == 0:
            continue
        rolled = pltpu.roll(cur, (-off) % tl, axis=0)
        if off < 0:
            halo = pltpu.roll(prev8, (-off) % V7X_SUBLANES, axis=0)
            head = jnp.where(row8 < -off, halo, rolled[:V7X_SUBLANES])
            shifted = jnp.concatenate([head, rolled[V7X_SUBLANES:]], axis=0)
        else:
            halo = pltpu.roll(next8, (-off) % V7X_SUBLANES, axis=0)
            tail = jnp.where(row8 >= V7X_SUBLANES - off, halo, rolled[tl - V7X_SUBLANES:])
            shifted = jnp.concatenate([rolled[:tl - V7X_SUBLANES], tail], axis=0)
        acc = acc + shifted * wconv_ref[j:j + 1, :]
    return acc


def _lru_gates(xc, d, wrg_ref, brg_ref, wig_ref, big_ref, lam_ref, a_ref, u_ref):
    xcb = xc.astype(BF16)
    ls = _log_sigmoid(lam_ref[d])
    for n in range(LRU_BLOCKS):
        sl = slice(n * LRU_BW, (n + 1) * LRU_BW)
        xb = xcb[:, sl]
        r = _sigmoid(jnp.dot(xb, wrg_ref[d, n], preferred_element_type=F32) + brg_ref[d][:, sl])
        ig = _sigmoid(jnp.dot(xb, wig_ref[d, n], preferred_element_type=F32) + big_ref[d][:, sl])
        log_a = LRU_C * r * ls[:, sl]
        a = jnp.exp(log_a)
        a_ref[:, sl] = a
        y = 1.0 - a * a
        u_ref[:, sl] = y * lax.rsqrt(jnp.maximum(y, SQRT_TINY)) * ig * xc[:, sl]


def _lru_scan_rows(a_ref, u_ref, hs_ref, o_ref, h0, reverse):
    tl = a_ref.shape[0]

    def body(s, h):
        t = tl - 1 - s if reverse else s
        h = a_ref[pl.ds(t, 1), :] * h + u_ref[pl.ds(t, 1), :]
        hs_ref[pl.ds(t, 1), :] = h
        return h

    h_end = lax.fori_loop(0, tl, body, h0, unroll=8)
    o_ref[...] = hs_ref[...].astype(o_ref.dtype)
    return h_end


def _lru_kernel(xf_ref, xfp_ref, xfn_ref, xb_ref, xbp_ref, xbn_ref, wconv_ref, bconv_ref, wrg_ref, brg_ref,
                wig_ref, big_ref, lam_ref, hf_ref, hb_ref, a_ref, u_ref, hs_ref, cf_ref, cb_ref, *, geom, tl):
    i = pl.program_id(0)
    ib = pl.num_programs(0) - 1 - i

    def flags(tile):
        pos = _tile_pos(geom, tile, tl)
        return pos == 0, pos + tl == _tile_seq_len(geom, tile, tl)

    def keep(flag):
        return jnp.where(flag, 0.0, 1.0)

    def conv(cur_ref, prev_ref, next_ref, first, last):
        prev8 = prev_ref[...].astype(F32)[LRU_HALO - V7X_SUBLANES:] * keep(first)
        next8 = next_ref[...].astype(F32)[:V7X_SUBLANES] * keep(last)
        return _lru_conv(cur_ref[...].astype(F32), prev8, next8, wconv_ref, bconv_ref)

    first, last = flags(i)

    @pl.when(first)
    def _():
        cf_ref[...] = jnp.zeros_like(cf_ref)

    xc = conv(xf_ref, xfp_ref, xfn_ref, first, last)
    _lru_gates(xc, 0, wrg_ref, brg_ref, wig_ref, big_ref, lam_ref, a_ref, u_ref)
    cf_ref[0:1, :] = _lru_scan_rows(a_ref, u_ref, hs_ref, hf_ref, cf_ref[0:1, :], False)

    first, last = flags(ib)

    @pl.when(last)
    def _():
        cb_ref[...] = jnp.zeros_like(cb_ref)

    xc = conv(xb_ref, xbp_ref, xbn_ref, first, last)
    _lru_gates(xc, 1, wrg_ref, brg_ref, wig_ref, big_ref, lam_ref, a_ref, u_ref)
    cb_ref[0:1, :] = _lru_scan_rows(a_ref, u_ref, hs_ref, hb_ref, cb_ref[0:1, :], True)


def _lru(geom, proj_b, w_conv, b_conv, w_rg, b_rg, w_ig, b_ig, lam, *, tl):
    t = proj_b.shape[0]
    n = t // tl
    r8 = tl // LRU_HALO
    n8 = t // LRU_HALO
    w = LRU_WIDTH
    rev = lambda i: n - 1 - i
    prev_blk = lambda i: jnp.maximum(i * r8 - 1, 0)
    next_blk = lambda i: jnp.minimum((i + 1) * r8, n8 - 1)
    full = lambda a: pl.BlockSpec(a.shape, lambda i: (0,) * a.ndim)
    b_conv2 = b_conv.reshape(1, w)
    b_rg3 = b_rg.reshape(2, 1, w)
    b_ig3 = b_ig.reshape(2, 1, w)
    lam3 = lam.reshape(2, 1, w)
    return pl.pallas_call(
        functools.partial(_lru_kernel, geom=geom, tl=tl),
        grid=(n,),
        in_specs=[
            pl.BlockSpec((tl, w), lambda i: (i, 0)),
            pl.BlockSpec((LRU_HALO, w), lambda i: (prev_blk(i), 0)),
            pl.BlockSpec((LRU_HALO, w), lambda i: (next_blk(i), 0)),
            pl.BlockSpec((tl, w), lambda i: (rev(i), 0)),
            pl.BlockSpec((LRU_HALO, w), lambda i: (prev_blk(rev(i)), 0)),
            pl.BlockSpec((LRU_HALO, w), lambda i: (next_blk(rev(i)), 0)),
            full(w_conv), full(b_conv2), full(w_rg), full(b_rg3), full(w_ig), full(b_ig3), full(lam3),
        ],
        out_specs=[
            pl.BlockSpec((tl, w), lambda i: (i, 0)),
            pl.BlockSpec((tl, w), lambda i: (rev(i), 0)),
        ],
        out_shape=[jax.ShapeDtypeStruct((t, w), BF16), jax.ShapeDtypeStruct((t, w), BF16)],
        scratch_shapes=[pltpu.VMEM((tl, w), F32), pltpu.VMEM((tl, w), F32), pltpu.VMEM((tl, w), F32),
                        pltpu.VMEM((V7X_SUBLANES, w), F32), pltpu.VMEM((V7X_SUBLANES, w), F32)],
        compiler_params=_cparams(("arbitrary",)),
        name="lru_scan",
    )(proj_b, proj_b, proj_b, proj_b, proj_b, proj_b, w_conv, b_conv2, w_rg, b_rg3, w_ig, b_ig3, lam3)


def _gelu_tanh(x):
    return 0.5 * x * (1.0 + jnp.tanh(math.sqrt(2.0 / math.pi) * (x + 0.044715 * (x * x * x))))


def _merge_kernel(x_ref, ya_ref, hf_ref, hb_ref, gl_ref, ga_ref, gb_ref, mod_ref, wl_ref, wo_ref, o_ref):
    f32 = lambda ref: ref[...].astype(F32)
    y = ((f32(hf_ref) + f32(hb_ref)) * _gelu_tanh(f32(gl_ref))).astype(BF16)
    yb = jnp.dot(y, wl_ref[...], preferred_element_type=F32)
    m = _sigmoid(f32(ga_ref)) * f32(ya_ref) + _sigmoid(f32(gb_ref)) * yb
    mix = jnp.dot(m.astype(BF16), wo_ref[...], preferred_element_type=F32)
    o_ref[...] = x_ref[...] + mod_ref[2:3, :] * mix


def _merge(geom, x, ya, hf, hb, proj_a, proj_b, mod_l, w_lru_o, w_out, *, tm):
    t, d = x.shape
    gate_a_blk = (2 * RET_QK_W + 2 * RET_V_W) // d
    tok = lambda w: pl.BlockSpec((tm, w), lambda i: (i, 0))
    return pl.pallas_call(
        _merge_kernel,
        grid=(t // tm,),
        in_specs=[
            tok(d), tok(d), tok(LRU_WIDTH), tok(LRU_WIDTH),
            pl.BlockSpec((tm, LRU_WIDTH), lambda i: (i, 1)),
            pl.BlockSpec((tm, d), lambda i: (i, gate_a_blk)),
            pl.BlockSpec((tm, d), lambda i: (i, gate_a_blk + 1)),
            pl.BlockSpec((None, 6, d), lambda i: (_tile_seq(geom, i, tm), 0, 0)),
            pl.BlockSpec((LRU_WIDTH, d), lambda i: (0, 0)),
            pl.BlockSpec((d, d), lambda i: (0, 0)),
        ],
        out_specs=tok(d),
        out_shape=jax.ShapeDtypeStruct((t, d), F32),
        compiler_params=_cparams(("parallel",)),
        name="merge",
    )(x, ya, hf, hb, proj_b, proj_a, proj_a, mod_l, w_lru_o, w_out)


def _ffn_kernel(x_ref, mod_ref, g_ref, wg_ref, wu_ref, wd_ref, o_ref, h_ref, acc_ref):
    j = pl.program_id(1)

    @pl.when(j == 0)
    def _():
        h_ref[...] = _rms_mod(x_ref[...], g_ref[...], mod_ref[4:5, :], mod_ref[3:4, :]).astype(BF16)
        acc_ref[...] = jnp.zeros_like(acc_ref)

    h = h_ref[...]
    a = jnp.dot(h, wg_ref[...], preferred_element_type=F32)
    u = jnp.dot(h, wu_ref[...], preferred_element_type=F32)
    acc_ref[...] += jnp.dot((_silu(a) * u).astype(BF16), wd_ref[...], preferred_element_type=F32)

    @pl.when(j == pl.num_programs(1) - 1)
    def _():
        o_ref[...] = x_ref[...] + mod_ref[5:6, :] * acc_ref[...]


def _ffn(geom, x, mod_l, gvec, wg, wu, wd, *, tm, tf):
    t, d = x.shape
    ff = wg.shape[1]
    return pl.pallas_call(
        _ffn_kernel,
        grid=(t // tm, ff // tf),
        in_specs=[
            pl.BlockSpec((tm, d), lambda i, j: (i, 0)),
            pl.BlockSpec((None, 6, d), lambda i, j: (_tile_seq(geom, i, tm), 0, 0)),
            pl.BlockSpec((1, d), lambda i, j: (0, 0)),
            pl.BlockSpec((d, tf), lambda i, j: (0, j)),
            pl.BlockSpec((d, tf), lambda i, j: (0, j)),
            pl.BlockSpec((tf, d), lambda i, j: (j, 0)),
        ],
        out_specs=pl.BlockSpec((tm, d), lambda i, j: (i, 0)),
        out_shape=jax.ShapeDtypeStruct((t, d), F32),
        scratch_shapes=[pltpu.VMEM((tm, d), BF16), pltpu.VMEM((tm, d), F32)],
        compiler_params=_cparams(("parallel", "arbitrary")),
        name="ffn_dense",
    )(x, mod_l, gvec, wg, wu, wd)


ROUTE_E0, ROUTE_E1, ROUTE_R0, ROUTE_R1, ROUTE_G0, ROUTE_G1 = range(6)

ROW_TILES = D_MODEL // V7X_LANES


def _store_token_rows(ref, val):
    rows = val.shape[0]
    for j in range(ROW_TILES):
        ref[pl.ds(j, rows, stride=ROW_TILES), :] = val[:, j * V7X_LANES:(j + 1) * V7X_LANES]


def _load_token_rows(ref, rows):
    return jnp.concatenate([ref[pl.ds(j, rows, stride=ROW_TILES), :] for j in range(ROW_TILES)], axis=1)


def _router_kernel(x_ref, mod_ref, g_ref, wr_ref, h_ref, route_ref, cnt_ref, carry_ref):
    i = pl.program_id(0)
    tm = x_ref.shape[0]

    @pl.when(i == 0)
    def _():
        carry_ref[...] = jnp.zeros_like(carry_ref)

    h = _rms_mod(x_ref[...], g_ref[...], mod_ref[4:5, :], mod_ref[3:4, :])
    _store_token_rows(h_ref, h)
    logits = jnp.dot(h, wr_ref[...], preferred_element_type=F32, precision=lax.Precision.HIGHEST)
    lane = lax.broadcasted_iota(jnp.int32, logits.shape, 1).astype(F32)
    logits = jnp.where(lane < N_EXPERTS, logits, -jnp.inf)
    m1 = jnp.max(logits, axis=-1, keepdims=True)
    i1 = jnp.min(jnp.where(logits == m1, lane, float(V7X_LANES)), axis=-1, keepdims=True)
    rest = jnp.where(lane == i1, -jnp.inf, logits)
    m2 = jnp.max(rest, axis=-1, keepdims=True)
    i2 = jnp.min(jnp.where(rest == m2, lane, float(V7X_LANES)), axis=-1, keepdims=True)
    ex = jnp.exp(m2 - m1)
    g1 = 1.0 / (1.0 + ex)
    g2 = ex / (1.0 + ex)
    sel1 = lane == i1
    sel2 = lane == i2
    onehot = jnp.where(sel1 | sel2, 1.0, 0.0)
    row = lax.broadcasted_iota(jnp.int32, (tm, tm), 0)
    col = lax.broadcasted_iota(jnp.int32, (tm, tm), 1)
    lower = jnp.where(col < row, 1.0, 0.0).astype(BF16)
    before = jnp.dot(lower, onehot.astype(BF16), preferred_element_type=F32) + carry_ref[0:1, :]
    r1 = jnp.sum(jnp.where(sel1, before, 0.0), axis=-1, keepdims=True)
    r2 = jnp.sum(jnp.where(sel2, before, 0.0), axis=-1, keepdims=True)
    out_lane = lax.broadcasted_iota(jnp.int32, route_ref.shape, 1)
    vals = (i1.astype(F32), i2.astype(F32), r1, r2, g1, g2)
    packed = jnp.zeros(route_ref.shape, F32)
    for slot, v in enumerate(vals):
        packed = jnp.where(out_lane == slot, v, packed)
    route_ref[...] = packed
    carry = carry_ref[0:1, :] + jnp.sum(onehot, axis=0, keepdims=True)
    carry_ref[0:1, :] = carry
    cnt_ref[...] = jnp.broadcast_to(carry, cnt_ref.shape).astype(jnp.int32)


def _router(geom, x, mod_l, gvec, w_router, *, tm):
    t, d = x.shape
    w_pad = jnp.pad(w_router, ((0, 0), (0, V7X_LANES - N_EXPERTS)))
    return pl.pallas_call(
        _router_kernel,
        grid=(t // tm,),
        in_specs=[
            pl.BlockSpec((tm, d), lambda i: (i, 0)),
            pl.BlockSpec((None, 6, d), lambda i: (_tile_seq(geom, i, tm), 0, 0)),
            pl.BlockSpec((1, d), lambda i: (0, 0)),
            pl.BlockSpec((d, V7X_LANES), lambda i: (0, 0)),
        ],
        out_specs=[
            pl.BlockSpec((tm * ROW_TILES, V7X_LANES), lambda i: (i, 0)),
            pl.BlockSpec((tm, V7X_LANES), lambda i: (i, 0)),
            pl.BlockSpec((V7X_SUBLANES, V7X_LANES), lambda i: (0, 0)),
        ],
        out_shape=[
            jax.ShapeDtypeStruct((t * ROW_TILES, V7X_LANES), F32),
            jax.ShapeDtypeStruct((t, V7X_LANES), F32),
            jax.ShapeDtypeStruct((V7X_SUBLANES, V7X_LANES), jnp.int32),
        ],
        scratch_shapes=[pltpu.VMEM((V7X_SUBLANES, V7X_LANES), F32)],
        compiler_params=_cparams(("arbitrary",)),
        name="router",
    )(x, mod_l, gvec, w_pad)


def _token_copy(src, src_tok, dst, dst_tok, sem):
    s = pl.multiple_of(src_tok * ROW_TILES, ROW_TILES)
    d = pl.multiple_of(dst_tok * ROW_TILES, ROW_TILES)
    return pltpu.make_async_copy(src.at[pl.ds(s, ROW_TILES), :], dst.at[pl.ds(d, ROW_TILES), :], sem)


def _dispatch_kernel(fill_ref, dest_ref, src_ref, out_hbm, zero_ref, sem, zero_sem, *, td):
    @pl.when(pl.program_id(0) == 0)
    def _():
        zero_ref[...] = jnp.zeros_like(zero_ref)
        for wait in (False, True):
            for e in range(N_EXPERTS):
                def pad_row(r, carry):
                    copy = _token_copy(zero_ref, 0, out_hbm, r, zero_sem)
                    copy.wait() if wait else copy.start()
                    return carry

                lax.fori_loop(fill_ref[0, e], fill_ref[1, e], pad_row, 0)

    def issue(t, carry):
        for k in range(TOP_K):
            _token_copy(src_ref, t, out_hbm, dest_ref[k, t], sem).start()
        return carry

    lax.fori_loop(0, td, issue, 0, unroll=8)
    for _ in range(TOP_K):
        pltpu.make_async_copy(src_ref, out_hbm.at[pl.ds(0, td * ROW_TILES), :], sem).wait()


def _dispatch(h2_rows, dest, fill, n_rows, *, td):
    grid_spec = pltpu.PrefetchScalarGridSpec(
        num_scalar_prefetch=1,
        grid=(dest.shape[1] // td,),
        in_specs=[
            pl.BlockSpec((TOP_K, td), lambda i, fill: (0, i), memory_space=pltpu.SMEM),
            pl.BlockSpec((td * ROW_TILES, V7X_LANES), lambda i, fill: (i, 0)),
        ],
        out_specs=pl.BlockSpec(memory_space=pl.ANY),
        scratch_shapes=[pltpu.VMEM((ROW_TILES, V7X_LANES), h2_rows.dtype),
                        pltpu.SemaphoreType.DMA(()), pltpu.SemaphoreType.DMA(())],
    )
    return pl.pallas_call(
        functools.partial(_dispatch_kernel, td=td),
        grid_spec=grid_spec,
        out_shape=jax.ShapeDtypeStruct((n_rows * ROW_TILES, V7X_LANES), h2_rows.dtype),
        compiler_params=_cparams(("arbitrary",)),
        name="moe_dispatch",
    )(fill, dest, h2_rows)


def _experts_kernel(blk_e_ref, nused_ref, x_ref, wg_ref, wu_ref, wd_ref, o_ref, xb_ref, acc_ref):
    del blk_e_ref
    b = pl.program_id(0)
    j = pl.program_id(1)
    bm = xb_ref.shape[0]

    @pl.when(b < nused_ref[0])
    def _():
        @pl.when(j == 0)
        def _():
            xb_ref[...] = _load_token_rows(x_ref, bm).astype(BF16)
            acc_ref[...] = jnp.zeros_like(acc_ref)

        xb = xb_ref[...]
        a = jnp.dot(xb, wg_ref[...], preferred_element_type=F32)
        u = jnp.dot(xb, wu_ref[...], preferred_element_type=F32)
        acc_ref[...] += jnp.dot((_silu(a) * u).astype(BF16), wd_ref[...], preferred_element_type=F32)

        @pl.when(j == pl.num_programs(1) - 1)
        def _():
            _store_token_rows(o_ref, acc_ref[...])


def _experts(xbuf, blk_e, n_used, wg, wu, wd, *, bm, tf):
    p = xbuf.shape[0] // ROW_TILES
    d = D_MODEL
    ff = wg.shape[2]
    nj = ff // tf
    rows_blk = (bm * ROW_TILES, V7X_LANES)

    def blk(b, nu):
        return jnp.minimum(b, nu[0] - 1)

    def ffc(b, j, nu):
        return jnp.where(b < nu[0], j, nj - 1)

    grid_spec = pltpu.PrefetchScalarGridSpec(
        num_scalar_prefetch=2,
        grid=(p // bm, nj),
        in_specs=[
            pl.BlockSpec(rows_blk, lambda b, j, be, nu: (blk(b, nu), 0)),
            pl.BlockSpec((None, d, tf), lambda b, j, be, nu: (be[blk(b, nu)], 0, ffc(b, j, nu))),
            pl.BlockSpec((None, d, tf), lambda b, j, be, nu: (be[blk(b, nu)], 0, ffc(b, j, nu))),
            pl.BlockSpec((None, tf, d), lambda b, j, be, nu: (be[blk(b, nu)], ffc(b, j, nu), 0)),
        ],
        out_specs=pl.BlockSpec(rows_blk, lambda b, j, be, nu: (blk(b, nu), 0)),
        scratch_shapes=[pltpu.VMEM((bm, d), BF16), pltpu.VMEM((bm, d), F32)],
    )
    return pl.pallas_call(
        _experts_kernel,
        grid_spec=grid_spec,
        out_shape=jax.ShapeDtypeStruct(xbuf.shape, F32),
        compiler_params=_cparams(("arbitrary", "arbitrary")),
        name="moe_experts",
    )(blk_e, n_used, xbuf, wg, wu, wd)


def _moe_out_kernel(dest_ref, dest_next_ref, x_ref, route_ref, mod_ref, gf_ref, ybuf_hbm, op_ref, os_ref,
                    ya0_ref, ya1_ref, yb0_ref, yb1_ref, sems, *, final_norm, prompt_tiles):
    i = pl.program_id(0)
    tm = x_ref.shape[0]
    bufs = ((ya0_ref, ya1_ref), (yb0_ref, yb1_ref))

    def gather(d_ref, slot):
        def issue(t, carry):
            for k in range(TOP_K):
                _token_copy(ybuf_hbm, d_ref[k, t], bufs[slot][k], t, sems.at[slot]).start()
            return carry

        lax.fori_loop(0, tm, issue, 0, unroll=8)

    def wait(slot):
        for k in range(TOP_K):
            pltpu.make_async_copy(ybuf_hbm.at[pl.ds(0, tm * ROW_TILES), :], bufs[slot][k], sems.at[slot]).wait()

    def combine(slot):
        g1 = route_ref[:, ROUTE_G0:ROUTE_G0 + 1]
        g2 = route_ref[:, ROUTE_G1:ROUTE_G1 + 1]
        y = _load_token_rows(bufs[slot][0], tm) * g1 + _load_token_rows(bufs[slot][1], tm) * g2
        x = x_ref[...] + mod_ref[5:6, :] * y
        if final_norm:
            ms = jnp.mean(x * x, axis=-1, keepdims=True)
            x = x * lax.rsqrt(ms + RMS_EPS) * gf_ref[...]

        @pl.when(i < prompt_tiles)
        def _():
            op_ref[...] = x

        @pl.when(i >= prompt_tiles)
        def _():
            os_ref[...] = x

    @pl.when(i == 0)
    def _():
        gather(dest_ref, 0)

    for slot in range(2):
        @pl.when(i % 2 == slot)
        def _():
            @pl.when(i + 1 < pl.num_programs(0))
            def _():
                gather(dest_next_ref, 1 - slot)

            wait(slot)
            combine(slot)


def _moe_out(geom, x, ybuf, dest, route, mod_l, g_final, *, tm, final_norm):
    t, d = x.shape
    n = t // tm
    prompt_tiles = geom.tokens_p // tm
    y_buf = pltpu.VMEM((tm * ROW_TILES, V7X_LANES), F32)
    return pl.pallas_call(
        functools.partial(_moe_out_kernel, final_norm=final_norm, prompt_tiles=prompt_tiles),
        grid=(n,),
        in_specs=[
            pl.BlockSpec((TOP_K, tm), lambda i: (0, i), memory_space=pltpu.SMEM),
            pl.BlockSpec((TOP_K, tm), lambda i: (0, jnp.minimum(i + 1, n - 1)), memory_space=pltpu.SMEM),
            pl.BlockSpec((tm, d), lambda i: (i, 0)),
            pl.BlockSpec((tm, V7X_LANES), lambda i: (i, 0)),
            pl.BlockSpec((None, 6, d), lambda i: (_tile_seq(geom, i, tm), 0, 0)),
            pl.BlockSpec((1, d), lambda i: (0, 0)),
            pl.BlockSpec(memory_space=pl.ANY),
        ],
        out_specs=[
            pl.BlockSpec((tm, d), lambda i: (jnp.minimum(i, prompt_tiles - 1), 0)),
            pl.BlockSpec((tm, d), lambda i: (jnp.maximum(i - prompt_tiles, 0), 0)),
        ],
        out_shape=[jax.ShapeDtypeStruct((geom.tokens_p, d), F32),
                   jax.ShapeDtypeStruct((t - geom.tokens_p, d), F32)],
        scratch_shapes=[y_buf, y_buf, y_buf, y_buf, pltpu.SemaphoreType.DMA((2,))],
        compiler_params=_cparams(("arbitrary",)),
        name="moe_out",
    )(dest, dest, x, route, mod_l, g_final, ybuf)


def _final_norm_kernel(x_ref, g_ref, o_ref):
    x = x_ref[...]
    o_ref[...] = x * lax.rsqrt(jnp.mean(x * x, axis=-1, keepdims=True) + RMS_EPS) * g_ref[...]


def _final_norm(x, g_final, *, tm):
    t, d = x.shape
    return pl.pallas_call(
        _final_norm_kernel,
        grid=(t // tm,),
        in_specs=[pl.BlockSpec((tm, d), lambda i: (i, 0)), pl.BlockSpec((1, d), lambda i: (0, 0))],
        out_specs=pl.BlockSpec((tm, d), lambda i: (i, 0)),
        out_shape=jax.ShapeDtypeStruct((t, d), F32),
        compiler_params=_cparams(("parallel",)),
        name="final_norm",
    )(x, g_final)


def _moe(geom, x, mod_l, gvec, w_router, wg, wu, wd, g_final, *, tiles, final_norm):
    t, d = x.shape
    bm = tiles["moe_rows"]
    h2, route, counts = _router(geom, x, mod_l, gvec, w_router, tm=tiles["router"])
    counts = counts[0, :N_EXPERTS]
    padded = ((counts + bm - 1) // bm) * bm
    pad_end = jnp.cumsum(padded)
    pad_start = pad_end - padded
    experts = route[:, ROUTE_E0:ROUTE_E1 + 1].astype(jnp.int32)
    ranks = route[:, ROUTE_R0:ROUTE_R1 + 1].astype(jnp.int32)
    dest = (pad_start[experts] + ranks).T
    n_rows = t * TOP_K + N_EXPERTS * bm
    n_blk = n_rows // bm
    blk_start = jnp.arange(n_blk, dtype=jnp.int32) * bm
    blk_e = jnp.minimum(jnp.sum((pad_end[None, :] <= blk_start[:, None]).astype(jnp.int32), axis=1),
                        N_EXPERTS - 1).astype(jnp.int32)
    n_used = (pad_end[-1:] // bm).astype(jnp.int32)
    fill = jnp.stack([pad_start + counts, pad_end]).astype(jnp.int32)
    xbuf = _dispatch(h2, dest, fill, n_rows, td=tiles["dma_rows"])
    ybuf = _experts(xbuf, blk_e, n_used, wg, wu, wd, bm=bm, tf=tiles["expert_ff"])
    return _moe_out(geom, x, ybuf, dest, route, mod_l, g_final, tm=tiles["token"], final_norm=final_norm)


def _pick_tiles(geom):
    s = math.gcd(geom.s_p, geom.s_s) if geom.n_p and geom.n_s else (geom.s_p if geom.n_p else geom.s_s)
    return {
        "proj": min(1024, s),
        "proj_cols": 2048,
        "ret_chunk": min(256, s),
        "lru": min(512, s),
        "token": min(512, s),
        "ffn_cols": 1408,
        "router": min(512, s),
        "moe_rows": 512,
        "expert_ff": 1792,
        "dma_rows": min(2048, s),
    }


def _trunk(geom, x, c_all, w_ada, b_ada, g_norm1, g_norm2, w_in, w_conv, b_conv, w_rg, b_rg, w_ig, b_ig,
           lru_lambda, w_ret_o, w_lru_o, w_out, w_ff_gate, w_ff_up, w_ff_down,
           w_router, w_e_gate, w_e_up, w_e_down, g_final, tiles):
    depth = w_in.shape[0]
    d = D_MODEL
    n_pad = -(-geom.n_seq // V7X_SUBLANES) * V7X_SUBLANES
    c_pad = jnp.pad(c_all, ((0, n_pad - geom.n_seq), (0, 0)))
    mod = _adaln_mod(c_pad, w_ada, b_ada)
    tables = _retention_tables(tiles["ret_chunk"], max(geom.s_p if geom.n_p else 0, geom.s_s if geom.n_s else 0))

    o_xl = 2 * RET_QK_W + 2 * RET_V_W
    o_ga = o_xl + 2 * LRU_WIDTH
    for l in range(depth):
        wl = w_in[l]
        w_a = jnp.concatenate([wl[:, :o_xl], wl[:, o_ga:]], axis=1).astype(BF16)
        w_b = wl[:, o_xl:o_ga].astype(BF16)
        g1 = g_norm1[l].reshape(1, d)
        g2 = g_norm2[l].reshape(1, d)
        proj_a = _norm_proj(geom, x, mod[l], g1, w_a, shift_row=0, scale_row=1,
                            tm=tiles["proj"], tn=tiles["proj_cols"])
        proj_b = _norm_proj(geom, x, mod[l], g1, w_b, shift_row=0, scale_row=1,
                            tm=tiles["proj"], tn=LRU_WIDTH)
        ya = _retention(geom, proj_a, w_ret_o[l].astype(BF16), tables, chunk=tiles["ret_chunk"])
        hf, hb = _lru(geom, proj_b, w_conv[l], b_conv[l], w_rg[l].astype(BF16), b_rg[l],
                      w_ig[l].astype(BF16), b_ig[l], lru_lambda[l], tl=tiles["lru"])
        x = _merge(geom, x, ya, hf, hb, proj_a, proj_b, mod[l], w_lru_o[l].astype(BF16),
                   w_out[l].astype(BF16), tm=tiles["token"])
        j = l // 2
        last = l == depth - 1
        if l % 2 == 0:
            x = _ffn(geom, x, mod[l], g2, w_ff_gate[j].astype(BF16), w_ff_up[j].astype(BF16),
                     w_ff_down[j].astype(BF16), tm=tiles["token"], tf=tiles["ffn_cols"])
            if last:
                x = _final_norm(x, g_final.reshape(1, d), tm=tiles["token"])
                return x[:geom.tokens_p], x[geom.tokens_p:]
        else:
            parts = _moe(geom, x, mod[l], g2, w_router[j], w_e_gate[j].astype(BF16), w_e_up[j].astype(BF16),
                         w_e_down[j].astype(BF16), g_final.reshape(1, d), tiles=tiles, final_norm=last)
            if last:
                return parts
            x = jnp.concatenate(parts, axis=0)


def kernel(x_prompt, x_sample, c_prompt, c_sample, w_ada, b_ada, g_norm1, g_norm2, w_in, w_conv, b_conv, w_rg, b_rg, w_ig, b_ig, lru_lambda, w_ret_o, w_lru_o, w_out, w_ff_gate, w_ff_up, w_ff_down, w_router, w_e_gate, w_e_up, w_e_down, g_final):
    n_p, s_p, d = x_prompt.shape
    n_s, s_s, _ = x_sample.shape
    geom = Geom(n_p, s_p, n_s, s_s)
    x = jnp.concatenate([x_prompt.reshape(-1, d), x_sample.reshape(-1, d)], axis=0)
    c_all = jnp.concatenate([c_prompt, c_sample], axis=0)
    yp, ys = _trunk(geom, x, c_all, w_ada, b_ada, g_norm1, g_norm2, w_in, w_conv, b_conv, w_rg, b_rg, w_ig, b_ig,
                    lru_lambda, w_ret_o, w_lru_o, w_out, w_ff_gate, w_ff_up, w_ff_down,
                    w_router, w_e_gate, w_e_up, w_e_down, g_final, _pick_tiles(geom))
    return (yp.reshape(n_p, s_p, d), ys.reshape(n_s, s_s, d))
```

```python
import functools
import math
from typing import NamedTuple

import jax
import jax.numpy as jnp
import numpy as np
from jax import lax
from jax.experimental import pallas as pl
from jax.experimental.pallas import tpu as pltpu

F32 = jnp.float32
BF16 = jnp.bfloat16

D_MODEL = 1024
RET_HEADS = 4
RET_DK = 256
RET_DV = 512
RET_HALF = RET_DK // 2
RET_QK_W = RET_HEADS * RET_DK
RET_V_W = RET_HEADS * RET_DV
ROPE_BASE = 10000.0
LRU_WIDTH = 1280
LRU_BLOCKS = 10
LRU_BW = LRU_WIDTH // LRU_BLOCKS
LRU_C = 8.0
CONV_WIDTH = 4
CONV_LEFT = 2
LRU_HALO = 16
N_EXPERTS = 8
TOP_K = 2
RMS_EPS = 1e-6
GN_EPS = 1e-5
SQRT_TINY = 1e-30

V7X_LANES = 128
V7X_SUBLANES = 8
V7X_VMEM_BYTES = 64 * 1024 * 1024
VMEM_LIMIT = (V7X_VMEM_BYTES * 3) // 4

PA_W = 2 * RET_QK_W + 2 * RET_V_W + 2 * D_MODEL
PB_W = 2 * LRU_WIDTH


class Geom(NamedTuple):
    n_p: int
    s_p: int
    n_s: int
    s_s: int

    @property
    def tokens_p(self):
        return self.n_p * self.s_p

    @property
    def tokens(self):
        return self.n_p * self.s_p + self.n_s * self.s_s

    @property
    def n_seq(self):
        return self.n_p + self.n_s


def _tile_seq(g, i, tm):
    t0 = i * tm
    return jnp.where(t0 < g.tokens_p, t0 // g.s_p, g.n_p + (t0 - g.tokens_p) // g.s_s)


def _tile_pos(g, i, tm):
    t0 = i * tm
    return jnp.where(t0 < g.tokens_p, t0 % g.s_p, (t0 - g.tokens_p) % g.s_s)


def _tile_seq_len(g, i, tm):
    return jnp.where(i * tm < g.tokens_p, g.s_p, g.s_s)


def _cparams(sem):
    return pltpu.CompilerParams(dimension_semantics=sem, vmem_limit_bytes=VMEM_LIMIT)


def _sigmoid(x):
    return 0.5 * jnp.tanh(0.5 * x) + 0.5


def _silu(x):
    return x * _sigmoid(x)


def _rms_mod(x, gvec, scale, shift):
    ms = jnp.mean(x * x, axis=-1, keepdims=True)
    y = x * lax.rsqrt(ms + RMS_EPS) * gvec
    return y * (1.0 + scale) + shift


def _mod_kernel(c_ref, w_ref, b_ref, o_ref):
    c = c_ref[...]
    o_ref[...] = jnp.dot(_silu(c), w_ref[...], preferred_element_type=F32,
                         precision=lax.Precision.HIGHEST) + b_ref[...]


def _adaln_mod(c_pad, w_ada, b_ada):
    depth, d, _ = w_ada.shape
    n = c_pad.shape[0]
    out = pl.pallas_call(
        _mod_kernel,
        grid=(depth, 6),
        in_specs=[
            pl.BlockSpec((n, d), lambda l, j: (0, 0)),
            pl.BlockSpec((None, d, d), lambda l, j: (l, 0, j)),
            pl.BlockSpec((None, None, 1, d), lambda l, j: (l, j, 0, 0)),
        ],
        out_specs=pl.BlockSpec((None, None, n, d), lambda l, j: (l, j, 0, 0)),
        out_shape=jax.ShapeDtypeStruct((depth, 6, n, d), F32),
        compiler_params=_cparams(("arbitrary", "arbitrary")),
        name="adaln_mod",
    )(c_pad, w_ada, b_ada.reshape(depth, 6, 1, d))
    return out.transpose(0, 2, 1, 3)


def _norm_proj_kernel(x_ref, mod_ref, g_ref, w_ref, o_ref, h_ref, *, shift_row, scale_row):
    @pl.when(pl.program_id(1) == 0)
    def _():
        h = _rms_mod(x_ref[...], g_ref[...], mod_ref[scale_row:scale_row + 1, :],
                     mod_ref[shift_row:shift_row + 1, :])
        h_ref[...] = h.astype(BF16)

    o_ref[...] = jnp.dot(h_ref[...], w_ref[...], preferred_element_type=F32).astype(o_ref.dtype)


def _norm_proj(geom, x, mod_l, gvec, w, *, shift_row, scale_row, tm, tn):
    t, d = x.shape
    n = w.shape[1]
    return pl.pallas_call(
        functools.partial(_norm_proj_kernel, shift_row=shift_row, scale_row=scale_row),
        grid=(t // tm, n // tn),
        in_specs=[
            pl.BlockSpec((tm, d), lambda i, j: (i, 0)),
            pl.BlockSpec((None, 6, d), lambda i, j: (_tile_seq(geom, i, tm), 0, 0)),
            pl.BlockSpec((1, d), lambda i, j: (0, 0)),
            pl.BlockSpec((d, tn), lambda i, j: (0, j)),
        ],
        out_specs=pl.BlockSpec((tm, tn), lambda i, j: (i, j)),
        out_shape=jax.ShapeDtypeStruct((t, n), BF16),
        scratch_shapes=[pltpu.VMEM((tm, d), BF16)],
        compiler_params=_cparams(("parallel", "arbitrary")),
        name="norm_proj",
    )(x, mod_l, gvec, w)


def _rot_halves(ref, h, cos, sin):
    a = ref[:, h * RET_DK:h * RET_DK + RET_HALF].astype(F32)
    b = ref[:, h * RET_DK + RET_HALF:(h + 1) * RET_DK].astype(F32)
    return a * cos - b * sin, a * sin + b * cos


def _ret_bwd_state_kernel(k_ref, v_ref, cos_ref, sin_ref, kdb_ref, rb_ref, r_ref, *, geom, chunk, cdec):
    c = pl.num_programs(0) - 1 - pl.program_id(0)
    is_last = _tile_pos(geom, c, chunk) + chunk == _tile_seq_len(geom, c, chunk)

    @pl.when(is_last)
    def _():
        r_ref[...] = jnp.zeros_like(r_ref)

    cos = cos_ref[...]
    sin = sin_ref[...]
    for h in range(RET_HEADS):
        k1, k2 = _rot_halves(k_ref, h, cos, sin)
        dec = kdb_ref[h] * (RET_DK ** -0.5)
        kd = jnp.concatenate([k1 * dec, k2 * dec], axis=1).astype(BF16)
        vh = v_ref[:, h * RET_DV:(h + 1) * RET_DV].astype(BF16)
        r = r_ref[h]
        rb_ref[h] = r.astype(BF16)
        upd = lax.dot_general(kd, vh, (((0,), (0,)), ((), ())), preferred_element_type=F32)
        r_ref[h] = r * cdec[h] + upd


def _ret_main_kernel(q_ref, k_ref, v_ref, g_ref, rb_ref, cos_ref, sin_ref, dmat_ref, qdf_ref, qdb_ref,
                     kdf_ref, wo_ref, o_ref, rf_ref, *, geom, chunk, cdec):
    i = pl.program_id(0)

    @pl.when(_tile_pos(geom, i, chunk) == 0)
    def _():
        rf_ref[...] = jnp.zeros_like(rf_ref)

    cos = cos_ref[...]
    sin = sin_ref[...]
    acc = jnp.zeros(o_ref.shape, F32)
    for h in range(RET_HEADS):
        q1, q2 = _rot_halves(q_ref, h, cos, sin)
        k1, k2 = _rot_halves(k_ref, h, cos, sin)
        k1 = k1 * (RET_DK ** -0.5)
        k2 = k2 * (RET_DK ** -0.5)
        qr = jnp.concatenate([q1, q2], axis=1).astype(BF16)
        kr = jnp.concatenate([k1, k2], axis=1).astype(BF16)
        vh = v_ref[:, h * RET_DV:(h + 1) * RET_DV].astype(BF16)
        s = lax.dot_general(qr, kr, (((1,), (1,)), ((), ())), preferred_element_type=F32) * dmat_ref[h]
        o = jnp.dot(s.astype(BF16), vh, preferred_element_type=F32)
        qdf = qdf_ref[h]
        qf = jnp.concatenate([q1 * qdf, q2 * qdf], axis=1).astype(BF16)
        rf = rf_ref[h]
        o = o + jnp.dot(qf, rf.astype(BF16), preferred_element_type=F32)
        qdb = qdb_ref[h]
        qb = jnp.concatenate([q1 * qdb, q2 * qdb], axis=1).astype(BF16)
        o = o + jnp.dot(qb, rb_ref[h], preferred_element_type=F32)
        kdf = kdf_ref[h]
        kf = jnp.concatenate([k1 * kdf, k2 * kdf], axis=1).astype(BF16)
        rf_ref[h] = rf * cdec[h] + lax.dot_general(kf, vh, (((0,), (0,)), ((), ())),
                                                   preferred_element_type=F32)
        oc = o - jnp.mean(o, axis=-1, keepdims=True)
        on = oc * lax.rsqrt(jnp.mean(oc * oc, axis=-1, keepdims=True) + GN_EPS)
        og = (on * _silu(g_ref[:, h * RET_DV:(h + 1) * RET_DV].astype(F32))).astype(BF16)
        acc = acc + jnp.dot(og, wo_ref[h * RET_DV:(h + 1) * RET_DV, :], preferred_element_type=F32)
    o_ref[...] = acc.astype(o_ref.dtype)


def _retention_tables(chunk, s_max):
    log_gamma = jnp.log1p(-jnp.exp2(-5.0 - jnp.arange(RET_HEADS, dtype=F32)))
    idx = jnp.arange(chunk, dtype=F32)
    dist = jnp.abs(idx[:, None] - idx[None, :])
    dmat = jnp.exp(log_gamma[:, None, None] * dist[None])

    def rows(e):
        return jnp.broadcast_to(jnp.exp(log_gamma[:, None] * e[None, :])[:, :, None],
                                (RET_HEADS, chunk, RET_HALF))

    qdf = rows(idx + 1.0)
    qdb = rows(chunk - idx)
    kdf = rows(chunk - 1.0 - idx)
    kdb = rows(idx)
    theta = 1.0 / (ROPE_BASE ** jnp.linspace(0.0, 1.0, RET_HALF, dtype=F32))
    ang = jnp.arange(s_max, dtype=F32)[:, None] * theta[None, :]
    return jnp.cos(ang), jnp.sin(ang), dmat, qdf, qdb, kdf, kdb


def _chunk_decay(chunk):
    lg = np.log1p(-np.exp2(-5.0 - np.arange(RET_HEADS, dtype=np.float32))).astype(np.float32)
    return tuple(float(v) for v in np.exp(lg * np.float32(chunk)).astype(np.float32))


def _retention(geom, proj_a, w_ret_o, tables, *, chunk):
    t = proj_a.shape[0]
    n_chunks = t // chunk
    cos, sin, dmat, qdf, qdb, kdf, kdb = tables
    cdec = _chunk_decay(chunk)
    state_shape = (RET_HEADS, RET_DK, RET_DV)

    def pos_blk(c):
        return _tile_pos(geom, c, chunk) // chunk

    rev = lambda i: n_chunks - 1 - i
    tab_spec = pl.BlockSpec((RET_HEADS, chunk, RET_HALF), lambda i: (0, 0, 0))
    rb = pl.pallas_call(
        functools.partial(_ret_bwd_state_kernel, geom=geom, chunk=chunk, cdec=cdec),
        grid=(n_chunks,),
        in_specs=[
            pl.BlockSpec((chunk, RET_QK_W), lambda i: (rev(i), 1)),
            pl.BlockSpec((chunk, RET_V_W), lambda i: (rev(i), 1)),
            pl.BlockSpec((chunk, RET_HALF), lambda i: (pos_blk(rev(i)), 0)),
            pl.BlockSpec((chunk, RET_HALF), lambda i: (pos_blk(rev(i)), 0)),
            tab_spec,
        ],
        out_specs=pl.BlockSpec((None,) + state_shape, lambda i: (rev(i), 0, 0, 0)),
        out_shape=jax.ShapeDtypeStruct((n_chunks,) + state_shape, BF16),
        scratch_shapes=[pltpu.VMEM(state_shape, F32)],
        compiler_params=_cparams(("arbitrary",)),
        name="ret_bwd_state",
    )(proj_a, proj_a, cos, sin, kdb)

    return pl.pallas_call(
        functools.partial(_ret_main_kernel, geom=geom, chunk=chunk, cdec=cdec),
        grid=(n_chunks,),
        in_specs=[
            pl.BlockSpec((chunk, RET_QK_W), lambda i: (i, 0)),
            pl.BlockSpec((chunk, RET_QK_W), lambda i: (i, 1)),
            pl.BlockSpec((chunk, RET_V_W), lambda i: (i, 1)),
            pl.BlockSpec((chunk, RET_V_W), lambda i: (i, 2)),
            pl.BlockSpec((None,) + state_shape, lambda i: (i, 0, 0, 0)),
            pl.BlockSpec((chunk, RET_HALF), lambda i: (pos_blk(i), 0)),
            pl.BlockSpec((chunk, RET_HALF), lambda i: (pos_blk(i), 0)),
            pl.BlockSpec((RET_HEADS, chunk, chunk), lambda i: (0, 0, 0)),
            tab_spec, tab_spec, tab_spec,
            pl.BlockSpec((RET_V_W, D_MODEL), lambda i: (0, 0)),
        ],
        out_specs=pl.BlockSpec((chunk, D_MODEL), lambda i: (i, 0)),
        out_shape=jax.ShapeDtypeStruct((t, D_MODEL), BF16),
        scratch_shapes=[pltpu.VMEM(state_shape, F32)],
        compiler_params=_cparams(("arbitrary",)),
        name="ret_main",
    )(proj_a, proj_a, proj_a, proj_a, rb, cos, sin, dmat, qdf, qdb, kdf, w_ret_o)


def _log_sigmoid(x):
    return jnp.minimum(x, 0.0) - jnp.log1p(jnp.exp(-jnp.abs(x)))


LRU_SEGS = V7X_SUBLANES
LRU_LEAD = CONV_LEFT * LRU_SEGS
LRU_TAIL = (CONV_WIDTH - 1 - CONV_LEFT) * LRU_SEGS


def _lru_load_segment_major(x_ref, prev_ref, next_ref, keep_prev, keep_next, xs_ref):
    tl = x_ref.shape[0]
    seg = tl // LRU_SEGS
    x = x_ref[...].astype(F32)
    prev_hi = prev_ref[...].astype(F32)[LRU_HALO - V7X_SUBLANES:] * keep_prev
    next_lo = next_ref[...].astype(F32)[:V7X_SUBLANES] * keep_next
    row8 = lax.broadcasted_iota(jnp.int32, (V7X_SUBLANES, LRU_BW), 0)
    last = V7X_SUBLANES - 1
    for n in range(LRU_BLOCKS):
        sl = slice(n * LRU_BW, (n + 1) * LRU_BW)
        for s in range(LRU_SEGS):
            xs_ref[n, pl.ds(LRU_LEAD + s, seg, stride=LRU_SEGS), :] = x[s * seg:(s + 1) * seg, sl]
        slab = lambda t: xs_ref[n, LRU_LEAD + t * LRU_SEGS:LRU_LEAD + (t + 1) * LRU_SEGS, :]
        p = prev_hi[:, sl]
        m1 = pltpu.roll(jnp.where(row8 == last, p, slab(seg - 1)), 1, axis=0)
        m2 = pltpu.roll(jnp.where(row8 == last, pltpu.roll(p, 1, axis=0), slab(seg - 2)), 1, axis=0)
        p1 = pltpu.roll(jnp.where(row8 == 0, next_lo[:, sl], slab(0)), last, axis=0)
        xs_ref[n, 0:LRU_SEGS, :] = m2
        xs_ref[n, LRU_SEGS:LRU_LEAD, :] = m1
        xs_ref[n, LRU_LEAD + tl:LRU_LEAD + tl + LRU_TAIL, :] = p1


def _lru_gates(xs_ref, d, wconv_ref, bconv_ref, wrg_ref, brg_ref, wig_ref, big_ref, lam_ref, a_ref, u_ref):
    tl = a_ref.shape[1]
    c = (0.5 * LRU_C * math.log2(math.e)) * _log_sigmoid(lam_ref[d])
    for n in range(LRU_BLOCKS):
        sl = slice(n * LRU_BW, (n + 1) * LRU_BW)
        xh = bconv_ref[:, sl] + xs_ref[n, 0:tl, :] * wconv_ref[0:1, sl]
        for j in range(1, CONV_WIDTH):
            xh = xh + xs_ref[n, j * LRU_SEGS:j * LRU_SEGS + tl, :] * wconv_ref[j:j + 1, sl]
        xb = xh.astype(BF16)
        th_r = jnp.tanh(jnp.dot(xb, wrg_ref[d, n], preferred_element_type=F32) + brg_ref[d][:, sl])
        th_i = jnp.tanh(jnp.dot(xb, wig_ref[d, n], preferred_element_type=F32) + big_ref[d][:, sl])
        a = jnp.exp2(c[:, sl] * th_r + c[:, sl])
        a_ref[n] = a
        y = 1.0 - a * a
        u_ref[n] = (y * lax.rsqrt(jnp.maximum(y, SQRT_TINY))) * ((th_i + 1.0) * xh)


def _lru_scan_segments(a_ref, u_ref, hs_ref, as_ref, fin_ref, carry_ref, o_ref, reverse):
    tl = a_ref.shape[1]
    seg = tl // LRU_SEGS
    slab_shape = (LRU_SEGS, LRU_BW)

    def step(k, carry):
        t = seg - 1 - k if reverse else k
        rows = pl.ds(pl.multiple_of(t * LRU_SEGS, LRU_SEGS), LRU_SEGS)
        hs, decays = [], []
        for n in range(LRU_BLOCKS):
            a = a_ref[n, rows, :]
            h = a * carry[0][n] + u_ref[n, rows, :]
            decay = a * carry[1][n]
            hs_ref[n, rows, :] = h
            as_ref[n, rows, :] = decay
            hs.append(h)
            decays.append(decay)
        return tuple(hs), tuple(decays)

    init = (tuple(jnp.zeros(slab_shape, F32) for _ in range(LRU_BLOCKS)),
            tuple(jnp.ones(slab_shape, F32) for _ in range(LRU_BLOCKS)))
    h_end, decay_end = lax.fori_loop(0, seg, step, init, unroll=4)

    order = range(LRU_SEGS - 1, -1, -1) if reverse else range(LRU_SEGS)
    for n in range(LRU_BLOCKS):
        sl = slice(n * LRU_BW, (n + 1) * LRU_BW)
        fin_ref[0] = h_end[n]
        fin_ref[1] = decay_end[n]
        c = carry_ref[n, 0:1, :]
        for s in order:
            fin_ref[2, s:s + 1, :] = c
            c = fin_ref[0, s:s + 1, :] + fin_ref[1, s:s + 1, :] * c
        carry_ref[n, 0:1, :] = c
        for s in range(LRU_SEGS):
            seg_rows = pl.ds(s, seg, stride=LRU_SEGS)
            h = hs_ref[n, seg_rows, :] + as_ref[n, seg_rows, :] * fin_ref[2, s:s + 1, :]
            o_ref[s * seg:(s + 1) * seg, sl] = h.astype(o_ref.dtype)


def _lru_kernel(xf_ref, xfp_ref, xfn_ref, xb_ref, xbp_ref, xbn_ref, wconv_ref, bconv_ref, wrg_ref, brg_ref,
                wig_ref, big_ref, lam_ref, hf_ref, hb_ref, xs_ref, a_ref, u_ref, hs_ref, as_ref, fin_ref,
                cf_ref, cb_ref, *, geom, tl):
    i = pl.program_id(0)
    ib = pl.num_programs(0) - 1 - i

    def flags(tile):
        pos = _tile_pos(geom, tile, tl)
        return pos == 0, pos + tl == _tile_seq_len(geom, tile, tl)

    def keep(flag):
        return jnp.where(flag, 0.0, 1.0)

    def direction(d, x_ref, prev_ref, next_ref, first, last, carry_ref, o_ref):
        _lru_load_segment_major(x_ref, prev_ref, next_ref, keep(first), keep(last), xs_ref)
        _lru_gates(xs_ref, d, wconv_ref, bconv_ref, wrg_ref, brg_ref, wig_ref, big_ref, lam_ref, a_ref, u_ref)
        _lru_scan_segments(a_ref, u_ref, hs_ref, as_ref, fin_ref, carry_ref, o_ref, reverse=(d == 1))

    first, last = flags(i)

    @pl.when(first)
    def _():
        cf_ref[...] = jnp.zeros_like(cf_ref)

    direction(0, xf_ref, xfp_ref, xfn_ref, first, last, cf_ref, hf_ref)

    first, last = flags(ib)

    @pl.when(last)
    def _():
        cb_ref[...] = jnp.zeros_like(cb_ref)

    direction(1, xb_ref, xbp_ref, xbn_ref, first, last, cb_ref, hb_ref)


def _lru(geom, proj_b, w_conv, b_conv, w_rg, b_rg, w_ig, b_ig, lam, *, tl):
    t = proj_b.shape[0]
    n = t // tl
    r8 = tl // LRU_HALO
    n8 = t // LRU_HALO
    w = LRU_WIDTH
    rev = lambda i: n - 1 - i
    prev_blk = lambda i: jnp.maximum(i * r8 - 1, 0)
    next_blk = lambda i: jnp.minimum((i + 1) * r8, n8 - 1)
    full = lambda a: pl.BlockSpec(a.shape, lambda i: (0,) * a.ndim)
    w_conv = 0.5 * w_conv
    b_conv2 = 0.5 * b_conv.reshape(1, w)
    b_rg3 = 0.5 * b_rg.reshape(2, 1, w)
    b_ig3 = 0.5 * b_ig.reshape(2, 1, w)
    lam3 = lam.reshape(2, 1, w)
    return pl.pallas_call(
        functools.partial(_lru_kernel, geom=geom, tl=tl),
        grid=(n,),
        in_specs=[
            pl.BlockSpec((tl, w), lambda i: (i, 0)),
            pl.BlockSpec((LRU_HALO, w), lambda i: (prev_blk(i), 0)),
            pl.BlockSpec((LRU_HALO, w), lambda i: (next_blk(i), 0)),
            pl.BlockSpec((tl, w), lambda i: (rev(i), 0)),
            pl.BlockSpec((LRU_HALO, w), lambda i: (prev_blk(rev(i)), 0)),
            pl.BlockSpec((LRU_HALO, w), lambda i: (next_blk(rev(i)), 0)),
            full(w_conv), full(b_conv2), full(w_rg), full(b_rg3), full(w_ig), full(b_ig3), full(lam3),
        ],
        out_specs=[
            pl.BlockSpec((tl, w), lambda i: (i, 0)),
            pl.BlockSpec((tl, w), lambda i: (rev(i), 0)),
        ],
        out_shape=[jax.ShapeDtypeStruct((t, w), BF16), jax.ShapeDtypeStruct((t, w), BF16)],
        scratch_shapes=[pltpu.VMEM((LRU_BLOCKS, LRU_LEAD + tl + LRU_TAIL, LRU_BW), F32)]
        + [pltpu.VMEM((LRU_BLOCKS, tl, LRU_BW), F32)] * 4
        + [pltpu.VMEM((3, LRU_SEGS, LRU_BW), F32)]
        + [pltpu.VMEM((LRU_BLOCKS, V7X_SUBLANES, LRU_BW), F32)] * 2,
        compiler_params=_cparams(("arbitrary",)),
        name="lru_scan",
    )(proj_b, proj_b, proj_b, proj_b, proj_b, proj_b, w_conv, b_conv2, w_rg, b_rg3, w_ig, b_ig3, lam3)


def _gelu_tanh(x):
    return 0.5 * x * (1.0 + jnp.tanh(math.sqrt(2.0 / math.pi) * (x + 0.044715 * (x * x * x))))


def _merge_kernel(x_ref, ya_ref, hf_ref, hb_ref, gl_ref, ga_ref, gb_ref, mod_ref, wl_ref, wo_ref, o_ref):
    f32 = lambda ref: ref[...].astype(F32)
    y = ((f32(hf_ref) + f32(hb_ref)) * _gelu_tanh(f32(gl_ref))).astype(BF16)
    yb = jnp.dot(y, wl_ref[...], preferred_element_type=F32)
    m = _sigmoid(f32(ga_ref)) * f32(ya_ref) + _sigmoid(f32(gb_ref)) * yb
    mix = jnp.dot(m.astype(BF16), wo_ref[...], preferred_element_type=F32)
    o_ref[...] = x_ref[...] + mod_ref[2:3, :] * mix


def _merge(geom, x, ya, hf, hb, proj_a, proj_b, mod_l, w_lru_o, w_out, *, tm):
    t, d = x.shape
    gate_a_blk = (2 * RET_QK_W + 2 * RET_V_W) // d
    tok = lambda w: pl.BlockSpec((tm, w), lambda i: (i, 0))
    return pl.pallas_call(
        _merge_kernel,
        grid=(t // tm,),
        in_specs=[
            tok(d), tok(d), tok(LRU_WIDTH), tok(LRU_WIDTH),
            pl.BlockSpec((tm, LRU_WIDTH), lambda i: (i, 1)),
            pl.BlockSpec((tm, d), lambda i: (i, gate_a_blk)),
            pl.BlockSpec((tm, d), lambda i: (i, gate_a_blk + 1)),
            pl.BlockSpec((None, 6, d), lambda i: (_tile_seq(geom, i, tm), 0, 0)),
            pl.BlockSpec((LRU_WIDTH, d), lambda i: (0, 0)),
            pl.BlockSpec((d, d), lambda i: (0, 0)),
        ],
        out_specs=tok(d),
        out_shape=jax.ShapeDtypeStruct((t, d), F32),
        compiler_params=_cparams(("parallel",)),
        name="merge",
    )(x, ya, hf, hb, proj_b, proj_a, proj_a, mod_l, w_lru_o, w_out)


def _ffn_kernel(x_ref, mod_ref, g_ref, wg_ref, wu_ref, wd_ref, o_ref, h_ref, acc_ref):
    j = pl.program_id(1)

    @pl.when(j == 0)
    def _():
        h_ref[...] = _rms_mod(x_ref[...], g_ref[...], mod_ref[4:5, :], mod_ref[3:4, :]).astype(BF16)
        acc_ref[...] = jnp.zeros_like(acc_ref)

    h = h_ref[...]
    a = jnp.dot(h, wg_ref[...], preferred_element_type=F32)
    u = jnp.dot(h, wu_ref[...], preferred_element_type=F32)
    acc_ref[...] += jnp.dot((_silu(a) * u).astype(BF16), wd_ref[...], preferred_element_type=F32)

    @pl.when(j == pl.num_programs(1) - 1)
    def _():
        o_ref[...] = x_ref[...] + mod_ref[5:6, :] * acc_ref[...]


def _ffn(geom, x, mod_l, gvec, wg, wu, wd, *, tm, tf):
    t, d = x.shape
    ff = wg.shape[1]
    return pl.pallas_call(
        _ffn_kernel,
        grid=(t // tm, ff // tf),
        in_specs=[
            pl.BlockSpec((tm, d), lambda i, j: (i, 0)),
            pl.BlockSpec((None, 6, d), lambda i, j: (_tile_seq(geom, i, tm), 0, 0)),
            pl.BlockSpec((1, d), lambda i, j: (0, 0)),
            pl.BlockSpec((d, tf), lambda i, j: (0, j)),
            pl.BlockSpec((d, tf), lambda i, j: (0, j)),
            pl.BlockSpec((tf, d), lambda i, j: (j, 0)),
        ],
        out_specs=pl.BlockSpec((tm, d), lambda i, j: (i, 0)),
        out_shape=jax.ShapeDtypeStruct((t, d), F32),
        scratch_shapes=[pltpu.VMEM((tm, d), BF16), pltpu.VMEM((tm, d), F32)],
        compiler_params=_cparams(("parallel", "arbitrary")),
        name="ffn_dense",
    )(x, mod_l, gvec, wg, wu, wd)


ROUTE_E0, ROUTE_E1, ROUTE_R0, ROUTE_R1, ROUTE_G0, ROUTE_G1 = range(6)

ROW_TILES = D_MODEL // V7X_LANES


def _store_token_rows(ref, val):
    rows = val.shape[0]
    for j in range(ROW_TILES):
        ref[pl.ds(j, rows, stride=ROW_TILES), :] = val[:, j * V7X_LANES:(j + 1) * V7X_LANES]


def _load_token_rows(ref, rows):
    return jnp.concatenate([ref[pl.ds(j, rows, stride=ROW_TILES), :] for j in range(ROW_TILES)], axis=1)


def _router_kernel(x_ref, mod_ref, g_ref, wr_ref, h_ref, route_ref, cnt_ref, carry_ref):
    i = pl.program_id(0)
    tm = x_ref.shape[0]

    @pl.when(i == 0)
    def _():
        carry_ref[...] = jnp.zeros_like(carry_ref)

    h = _rms_mod(x_ref[...], g_ref[...], mod_ref[4:5, :], mod_ref[3:4, :])
    _store_token_rows(h_ref, h)
    logits = jnp.dot(h, wr_ref[...], preferred_element_type=F32, precision=lax.Precision.HIGHEST)
    lane = lax.broadcasted_iota(jnp.int32, logits.shape, 1).astype(F32)
    logits = jnp.where(lane < N_EXPERTS, logits, -jnp.inf)
    m1 = jnp.max(logits, axis=-1, keepdims=True)
    i1 = jnp.min(jnp.where(logits == m1, lane, float(V7X_LANES)), axis=-1, keepdims=True)
    rest = jnp.where(lane == i1, -jnp.inf, logits)
    m2 = jnp.max(rest, axis=-1, keepdims=True)
    i2 = jnp.min(jnp.where(rest == m2, lane, float(V7X_LANES)), axis=-1, keepdims=True)
    ex = jnp.exp(m2 - m1)
    g1 = 1.0 / (1.0 + ex)
    g2 = ex / (1.0 + ex)
    sel1 = lane == i1
    sel2 = lane == i2
    onehot = jnp.where(sel1 | sel2, 1.0, 0.0)
    row = lax.broadcasted_iota(jnp.int32, (tm, tm), 0)
    col = lax.broadcasted_iota(jnp.int32, (tm, tm), 1)
    lower = jnp.where(col < row, 1.0, 0.0).astype(BF16)
    before = jnp.dot(lower, onehot.astype(BF16), preferred_element_type=F32) + carry_ref[0:1, :]
    r1 = jnp.sum(jnp.where(sel1, before, 0.0), axis=-1, keepdims=True)
    r2 = jnp.sum(jnp.where(sel2, before, 0.0), axis=-1, keepdims=True)
    out_lane = lax.broadcasted_iota(jnp.int32, route_ref.shape, 1)
    vals = (i1.astype(F32), i2.astype(F32), r1, r2, g1, g2)
    packed = jnp.zeros(route_ref.shape, F32)
    for slot, v in enumerate(vals):
        packed = jnp.where(out_lane == slot, v, packed)
    route_ref[...] = packed
    carry = carry_ref[0:1, :] + jnp.sum(onehot, axis=0, keepdims=True)
    carry_ref[0:1, :] = carry
    cnt_ref[...] = jnp.broadcast_to(carry, cnt_ref.shape).astype(jnp.int32)


def _router(geom, x, mod_l, gvec, w_router, *, tm):
    t, d = x.shape
    w_pad = jnp.pad(w_router, ((0, 0), (0, V7X_LANES - N_EXPERTS)))
    return pl.pallas_call(
        _router_kernel,
        grid=(t // tm,),
        in_specs=[
            pl.BlockSpec((tm, d), lambda i: (i, 0)),
            pl.BlockSpec((None, 6, d), lambda i: (_tile_seq(geom, i, tm), 0, 0)),
            pl.BlockSpec((1, d), lambda i: (0, 0)),
            pl.BlockSpec((d, V7X_LANES), lambda i: (0, 0)),
        ],
        out_specs=[
            pl.BlockSpec((tm * ROW_TILES, V7X_LANES), lambda i: (i, 0)),
            pl.BlockSpec((tm, V7X_LANES), lambda i: (i, 0)),
            pl.BlockSpec((V7X_SUBLANES, V7X_LANES), lambda i: (0, 0)),
        ],
        out_shape=[
            jax.ShapeDtypeStruct((t * ROW_TILES, V7X_LANES), F32),
            jax.ShapeDtypeStruct((t, V7X_LANES), F32),
            jax.ShapeDtypeStruct((V7X_SUBLANES, V7X_LANES), jnp.int32),
        ],
        scratch_shapes=[pltpu.VMEM((V7X_SUBLANES, V7X_LANES), F32)],
        compiler_params=_cparams(("arbitrary",)),
        name="router",
    )(x, mod_l, gvec, w_pad)


def _token_copy(src, src_tok, dst, dst_tok, sem):
    s = pl.multiple_of(src_tok * ROW_TILES, ROW_TILES)
    d = pl.multiple_of(dst_tok * ROW_TILES, ROW_TILES)
    return pltpu.make_async_copy(src.at[pl.ds(s, ROW_TILES), :], dst.at[pl.ds(d, ROW_TILES), :], sem)


def _dispatch_kernel(fill_ref, dest_ref, src_ref, out_hbm, zero_ref, sem, zero_sem, *, td):
    @pl.when(pl.program_id(0) == 0)
    def _():
        zero_ref[...] = jnp.zeros_like(zero_ref)
        for wait in (False, True):
            for e in range(N_EXPERTS):
                def pad_row(r, carry):
                    copy = _token_copy(zero_ref, 0, out_hbm, r, zero_sem)
                    copy.wait() if wait else copy.start()
                    return carry

                lax.fori_loop(fill_ref[0, e], fill_ref[1, e], pad_row, 0)

    def issue(t, carry):
        for k in range(TOP_K):
            _token_copy(src_ref, t, out_hbm, dest_ref[k, t], sem).start()
        return carry

    lax.fori_loop(0, td, issue, 0, unroll=8)
    for _ in range(TOP_K):
        pltpu.make_async_copy(src_ref, out_hbm.at[pl.ds(0, td * ROW_TILES), :], sem).wait()


def _dispatch(h2_rows, dest, fill, n_rows, *, td):
    grid_spec = pltpu.PrefetchScalarGridSpec(
        num_scalar_prefetch=1,
        grid=(dest.shape[1] // td,),
        in_specs=[
            pl.BlockSpec((TOP_K, td), lambda i, fill: (0, i), memory_space=pltpu.SMEM),
            pl.BlockSpec((td * ROW_TILES, V7X_LANES), lambda i, fill: (i, 0)),
        ],
        out_specs=pl.BlockSpec(memory_space=pl.ANY),
        scratch_shapes=[pltpu.VMEM((ROW_TILES, V7X_LANES), h2_rows.dtype),
                        pltpu.SemaphoreType.DMA(()), pltpu.SemaphoreType.DMA(())],
    )
    return pl.pallas_call(
        functools.partial(_dispatch_kernel, td=td),
        grid_spec=grid_spec,
        out_shape=jax.ShapeDtypeStruct((n_rows * ROW_TILES, V7X_LANES), h2_rows.dtype),
        compiler_params=_cparams(("arbitrary",)),
        name="moe_dispatch",
    )(fill, dest, h2_rows)


def _experts_kernel(blk_e_ref, nused_ref, x_ref, wg_ref, wu_ref, wd_ref, o_ref, xb_ref, acc_ref):
    del blk_e_ref
    b = pl.program_id(0)
    j = pl.program_id(1)
    bm = xb_ref.shape[0]

    @pl.when(b < nused_ref[0])
    def _():
        @pl.when(j == 0)
        def _():
            xb_ref[...] = _load_token_rows(x_ref, bm).astype(BF16)
            acc_ref[...] = jnp.zeros_like(acc_ref)

        xb = xb_ref[...]
        a = jnp.dot(xb, wg_ref[...], preferred_element_type=F32)
        u = jnp.dot(xb, wu_ref[...], preferred_element_type=F32)
        acc_ref[...] += jnp.dot((_silu(a) * u).astype(BF16), wd_ref[...], preferred_element_type=F32)

        @pl.when(j == pl.num_programs(1) - 1)
        def _():
            _store_token_rows(o_ref, acc_ref[...])

    @pl.when((b >= nused_ref[0]) & (j == pl.num_programs(1) - 1))
    def _():
        o_ref[...] = jnp.zeros_like(o_ref)


def _experts(xbuf, blk_e, n_used, wg, wu, wd, *, bm, tf):
    p = xbuf.shape[0] // ROW_TILES
    d = D_MODEL
    ff = wg.shape[2]
    nj = ff // tf
    rows_blk = (bm * ROW_TILES, V7X_LANES)

    def blk(b, nu):
        return jnp.minimum(b, nu[0] - 1)

    def ffc(b, j, nu):
        return jnp.where(b < nu[0], j, nj - 1)

    grid_spec = pltpu.PrefetchScalarGridSpec(
        num_scalar_prefetch=2,
        grid=(p // bm, nj),
        in_specs=[
            pl.BlockSpec(rows_blk, lambda b, j, be, nu: (blk(b, nu), 0)),
            pl.BlockSpec((None, d, tf), lambda b, j, be, nu: (be[blk(b, nu)], 0, ffc(b, j, nu))),
            pl.BlockSpec((None, d, tf), lambda b, j, be, nu: (be[blk(b, nu)], 0, ffc(b, j, nu))),
            pl.BlockSpec((None, tf, d), lambda b, j, be, nu: (be[blk(b, nu)], ffc(b, j, nu), 0)),
        ],
        out_specs=pl.BlockSpec(rows_blk, lambda b, j, be, nu: (b, 0)),
        scratch_shapes=[pltpu.VMEM((bm, d), BF16), pltpu.VMEM((bm, d), F32)],
    )
    return pl.pallas_call(
        _experts_kernel,
        grid_spec=grid_spec,
        out_shape=jax.ShapeDtypeStruct(xbuf.shape, F32),
        compiler_params=_cparams(("arbitrary", "arbitrary")),
        name="moe_experts",
    )(blk_e, n_used, xbuf, wg, wu, wd)


def _moe_out_kernel(dest_ref, dest_next_ref, x_ref, route_ref, mod_ref, gf_ref, ybuf_hbm, op_ref, os_ref,
                    ya0_ref, ya1_ref, yb0_ref, yb1_ref, sems, *, final_norm, prompt_tiles):
    i = pl.program_id(0)
    tm = x_ref.shape[0]
    bufs = ((ya0_ref, ya1_ref), (yb0_ref, yb1_ref))

    def gather(d_ref, slot):
        def issue(t, carry):
            for k in range(TOP_K):
                _token_copy(ybuf_hbm, d_ref[k, t], bufs[slot][k], t, sems.at[slot]).start()
            return carry

        lax.fori_loop(0, tm, issue, 0, unroll=8)

    def wait(slot):
        for k in range(TOP_K):
            pltpu.make_async_copy(ybuf_hbm.at[pl.ds(0, tm * ROW_TILES), :], bufs[slot][k], sems.at[slot]).wait()

    def combine(slot):
        g1 = route_ref[:, ROUTE_G0:ROUTE_G0 + 1]
        g2 = route_ref[:, ROUTE_G1:ROUTE_G1 + 1]
        y = _load_token_rows(bufs[slot][0], tm) * g1 + _load_token_rows(bufs[slot][1], tm) * g2
        x = x_ref[...] + mod_ref[5:6, :] * y
        if final_norm:
            ms = jnp.mean(x * x, axis=-1, keepdims=True)
            x = x * lax.rsqrt(ms + RMS_EPS) * gf_ref[...]

        @pl.when(i < prompt_tiles)
        def _():
            op_ref[...] = x

        @pl.when(i >= prompt_tiles)
        def _():
            os_ref[...] = x

    @pl.when(i == 0)
    def _():
        gather(dest_ref, 0)

    for slot in range(2):
        @pl.when(i % 2 == slot)
        def _():
            @pl.when(i + 1 < pl.num_programs(0))
            def _():
                gather(dest_next_ref, 1 - slot)

            wait(slot)
            combine(slot)


def _moe_out(geom, x, ybuf, dest, route, mod_l, g_final, *, tm, final_norm):
    t, d = x.shape
    n = t // tm
    prompt_tiles = geom.tokens_p // tm
    y_buf = pltpu.VMEM((tm * ROW_TILES, V7X_LANES), F32)
    return pl.pallas_call(
        functools.partial(_moe_out_kernel, final_norm=final_norm, prompt_tiles=prompt_tiles),
        grid=(n,),
        in_specs=[
            pl.BlockSpec((TOP_K, tm), lambda i: (0, i), memory_space=pltpu.SMEM),
            pl.BlockSpec((TOP_K, tm), lambda i: (0, jnp.minimum(i + 1, n - 1)), memory_space=pltpu.SMEM),
            pl.BlockSpec((tm, d), lambda i: (i, 0)),
            pl.BlockSpec((tm, V7X_LANES), lambda i: (i, 0)),
            pl.BlockSpec((None, 6, d), lambda i: (_tile_seq(geom, i, tm), 0, 0)),
            pl.BlockSpec((1, d), lambda i: (0, 0)),
            pl.BlockSpec(memory_space=pl.ANY),
        ],
        out_specs=[
            pl.BlockSpec((tm, d), lambda i: (jnp.minimum(i, prompt_tiles - 1), 0)),
            pl.BlockSpec((tm, d), lambda i: (jnp.maximum(i - prompt_tiles, 0), 0)),
        ],
        out_shape=[jax.ShapeDtypeStruct((geom.tokens_p, d), F32),
                   jax.ShapeDtypeStruct((t - geom.tokens_p, d), F32)],
        scratch_shapes=[y_buf, y_buf, y_buf, y_buf, pltpu.SemaphoreType.DMA((2,))],
        compiler_params=_cparams(("arbitrary",)),
        name="moe_out",
    )(dest, dest, x, route, mod_l, g_final, ybuf)


def _final_norm_kernel(x_ref, g_ref, o_ref):
    x = x_ref[...]
    o_ref[...] = x * lax.rsqrt(jnp.mean(x * x, axis=-1, keepdims=True) + RMS_EPS) * g_ref[...]


def _final_norm(x, g_final, *, tm):
    t, d = x.shape
    return pl.pallas_call(
        _final_norm_kernel,
        grid=(t // tm,),
        in_specs=[pl.BlockSpec((tm, d), lambda i: (i, 0)), pl.BlockSpec((1, d), lambda i: (0, 0))],
        out_specs=pl.BlockSpec((tm, d), lambda i: (i, 0)),
        out_shape=jax.ShapeDtypeStruct((t, d), F32),
        compiler_params=_cparams(("parallel",)),
        name="final_norm",
    )(x, g_final)


def _moe(geom, x, mod_l, gvec, w_router, wg, wu, wd, g_final, *, tiles, final_norm):
    t, d = x.shape
    bm = tiles["moe_rows"]
    h2, route, counts = _router(geom, x, mod_l, gvec, w_router, tm=tiles["router"])
    counts = counts[0, :N_EXPERTS]
    padded = ((counts + bm - 1) // bm) * bm
    pad_end = jnp.cumsum(padded)
    pad_start = pad_end - padded
    experts = route[:, ROUTE_E0:ROUTE_E1 + 1].astype(jnp.int32)
    ranks = route[:, ROUTE_R0:ROUTE_R1 + 1].astype(jnp.int32)
    dest = (pad_start[experts] + ranks).T
    n_rows = t * TOP_K + N_EXPERTS * bm
    n_blk = n_rows // bm
    blk_start = jnp.arange(n_blk, dtype=jnp.int32) * bm
    blk_e = jnp.minimum(jnp.sum((pad_end[None, :] <= blk_start[:, None]).astype(jnp.int32), axis=1),
                        N_EXPERTS - 1).astype(jnp.int32)
    n_used = (pad_end[-1:] // bm).astype(jnp.int32)
    fill = jnp.stack([pad_start + counts, pad_end.at[N_EXPERTS - 1].set(n_rows)]).astype(jnp.int32)
    xbuf = _dispatch(h2, dest, fill, n_rows, td=tiles["dma_rows"])
    ybuf = _experts(xbuf, blk_e, n_used, wg, wu, wd, bm=bm, tf=tiles["expert_ff"])
    return _moe_out(geom, x, ybuf, dest, route, mod_l, g_final, tm=tiles["token"], final_norm=final_norm)


def _pick_tiles(geom):
    s = math.gcd(geom.s_p, geom.s_s) if geom.n_p and geom.n_s else (geom.s_p if geom.n_p else geom.s_s)
    return {
        "proj": min(1024, s),
        "proj_cols": 2048,
        "ret_chunk": min(256, s),
        "lru": min(512, s),
        "token": min(512, s),
        "ffn_cols": 1408,
        "router": min(512, s),
        "moe_rows": 512,
        "expert_ff": 1792,
        "dma_rows": min(2048, s),
    }


def _trunk(geom, x, c_all, w_ada, b_ada, g_norm1, g_norm2, w_in, w_conv, b_conv, w_rg, b_rg, w_ig, b_ig,
           lru_lambda, w_ret_o, w_lru_o, w_out, w_ff_gate, w_ff_up, w_ff_down,
           w_router, w_e_gate, w_e_up, w_e_down, g_final, tiles):
    depth = w_in.shape[0]
    d = D_MODEL
    n_pad = -(-geom.n_seq // V7X_SUBLANES) * V7X_SUBLANES
    c_pad = jnp.pad(c_all, ((0, n_pad - geom.n_seq), (0, 0)))
    mod = _adaln_mod(c_pad, w_ada, b_ada)
    tables = _retention_tables(tiles["ret_chunk"], max(geom.s_p if geom.n_p else 0, geom.s_s if geom.n_s else 0))

    o_xl = 2 * RET_QK_W + 2 * RET_V_W
    o_ga = o_xl + 2 * LRU_WIDTH
    for l in range(depth):
        wl = w_in[l]
        w_a = jnp.concatenate([wl[:, :o_xl], wl[:, o_ga:]], axis=1).astype(BF16)
        w_b = wl[:, o_xl:o_ga].astype(BF16)
        g1 = g_norm1[l].reshape(1, d)
        g2 = g_norm2[l].reshape(1, d)
        proj_a = _norm_proj(geom, x, mod[l], g1, w_a, shift_row=0, scale_row=1,
                            tm=tiles["proj"], tn=tiles["proj_cols"])
        proj_b = _norm_proj(geom, x, mod[l], g1, w_b, shift_row=0, scale_row=1,
                            tm=tiles["proj"], tn=LRU_WIDTH)
        ya = _retention(geom, proj_a, w_ret_o[l].astype(BF16), tables, chunk=tiles["ret_chunk"])
        hf, hb = _lru(geom, proj_b, w_conv[l], b_conv[l], w_rg[l].astype(BF16), b_rg[l],
                      w_ig[l].astype(BF16), b_ig[l], lru_lambda[l], tl=tiles["lru"])
        x = _merge(geom, x, ya, hf, hb, proj_a, proj_b, mod[l], w_lru_o[l].astype(BF16),
                   w_out[l].astype(BF16), tm=tiles["token"])
        j = l // 2
        last = l == depth - 1
        if l % 2 == 0:
            x = _ffn(geom, x, mod[l], g2, w_ff_gate[j].astype(BF16), w_ff_up[j].astype(BF16),
                     w_ff_down[j].astype(BF16), tm=tiles["token"], tf=tiles["ffn_cols"])
            if last:
                x = _final_norm(x, g_final.reshape(1, d), tm=tiles["token"])
                return x[:geom.tokens_p], x[geom.tokens_p:]
        else:
            parts = _moe(geom, x, mod[l], g2, w_router[j], w_e_gate[j].astype(BF16), w_e_up[j].astype(BF16),
                         w_e_down[j].astype(BF16), g_final.reshape(1, d), tiles=tiles, final_norm=last)
            if last:
                return parts
            x = jnp.concatenate(parts, axis=0)


def kernel(x_prompt, x_sample, c_prompt, c_sample, w_ada, b_ada, g_norm1, g_norm2, w_in, w_conv, b_conv, w_rg, b_rg, w_ig, b_ig, lru_lambda, w_ret_o, w_lru_o, w_out, w_ff_gate, w_ff_up, w_ff_down, w_router, w_e_gate, w_e_up, w_e_down, g_final):
    n_p, s_p, d = x_prompt.shape
    n_s, s_s, _ = x_sample.shape
    geom = Geom(n_p, s_p, n_s, s_s)
    x = jnp.concatenate([x_prompt.reshape(-1, d), x_sample.reshape(-1, d)], axis=0)
    c_all = jnp.concatenate([c_prompt, c_sample], axis=0)
    yp, ys = _trunk(geom, x, c_all, w_ada, b_ada, g_norm1, g_norm2, w_in, w_conv, b_conv, w_rg, b_rg, w_ig, b_ig,
                    lru_lambda, w_ret_o, w_lru_o, w_out, w_ff_gate, w_ff_up, w_ff_down,
                    w_router, w_e_gate, w_e_up, w_e_down, g_final, _pick_tiles(geom))
    return (yp.reshape(n_p, s_p, d), ys.reshape(n_s, s_s, d))
```

```python
import functools
import math
from typing import NamedTuple

import jax
import jax.numpy as jnp
import numpy as np
from jax import lax
from jax.experimental import pallas as pl
from jax.experimental.pallas import tpu as pltpu

F32 = jnp.float32
BF16 = jnp.bfloat16

D_MODEL = 1024
RET_HEADS = 4
RET_DK = 256
RET_DV = 512
RET_HALF = RET_DK // 2
RET_QK_W = RET_HEADS * RET_DK
RET_V_W = RET_HEADS * RET_DV
ROPE_BASE = 10000.0
LRU_WIDTH = 1280
LRU_BLOCKS = 10
LRU_BW = LRU_WIDTH // LRU_BLOCKS
LRU_C = 8.0
CONV_WIDTH = 4
CONV_LEFT = 2
LRU_HALO = 16
N_EXPERTS = 8
TOP_K = 2
RMS_EPS = 1e-6
GN_EPS = 1e-5
SQRT_TINY = 1e-30

V7X_LANES = 128
V7X_SUBLANES = 8
V7X_VMEM_BYTES = 64 * 1024 * 1024
VMEM_LIMIT = (V7X_VMEM_BYTES * 3) // 4

PA_W = 2 * RET_QK_W + 2 * RET_V_W + 2 * D_MODEL
PB_W = 2 * LRU_WIDTH


class Geom(NamedTuple):
    n_p: int
    s_p: int
    n_s: int
    s_s: int

    @property
    def tokens_p(self):
        return self.n_p * self.s_p

    @property
    def tokens(self):
        return self.n_p * self.s_p + self.n_s * self.s_s

    @property
    def n_seq(self):
        return self.n_p + self.n_s


def _tile_seq(g, i, tm):
    t0 = i * tm
    return jnp.where(t0 < g.tokens_p, t0 // g.s_p, g.n_p + (t0 - g.tokens_p) // g.s_s)


def _tile_pos(g, i, tm):
    t0 = i * tm
    return jnp.where(t0 < g.tokens_p, t0 % g.s_p, (t0 - g.tokens_p) % g.s_s)


def _tile_seq_len(g, i, tm):
    return jnp.where(i * tm < g.tokens_p, g.s_p, g.s_s)


def _cparams(sem):
    return pltpu.CompilerParams(dimension_semantics=sem, vmem_limit_bytes=VMEM_LIMIT)


def _sigmoid(x):
    return 0.5 * jnp.tanh(0.5 * x) + 0.5


def _silu(x):
    return x * _sigmoid(x)


def _rms_mod(x, gvec, scale, shift):
    ms = jnp.mean(x * x, axis=-1, keepdims=True)
    y = x * lax.rsqrt(ms + RMS_EPS) * gvec
    return y * (1.0 + scale) + shift


def _mod_kernel(c_ref, w_ref, b_ref, o_ref):
    c = c_ref[...]
    o_ref[...] = jnp.dot(_silu(c), w_ref[...], preferred_element_type=F32,
                         precision=lax.Precision.HIGHEST) + b_ref[...]


def _adaln_mod(c_pad, w_ada, b_ada):
    depth, d, _ = w_ada.shape
    n = c_pad.shape[0]
    out = pl.pallas_call(
        _mod_kernel,
        grid=(depth, 6),
        in_specs=[
            pl.BlockSpec((n, d), lambda l, j: (0, 0)),
            pl.BlockSpec((None, d, d), lambda l, j: (l, 0, j)),
            pl.BlockSpec((None, None, 1, d), lambda l, j: (l, j, 0, 0)),
        ],
        out_specs=pl.BlockSpec((None, None, n, d), lambda l, j: (l, j, 0, 0)),
        out_shape=jax.ShapeDtypeStruct((depth, 6, n, d), F32),
        compiler_params=_cparams(("arbitrary", "arbitrary")),
        name="adaln_mod",
    )(c_pad, w_ada, b_ada.reshape(depth, 6, 1, d))
    return out.transpose(0, 2, 1, 3)


def _norm_proj_kernel(x_ref, mod_ref, g_ref, w_ref, o_ref, h_ref, *, shift_row, scale_row):
    @pl.when(pl.program_id(1) == 0)
    def _():
        h = _rms_mod(x_ref[...], g_ref[...], mod_ref[scale_row:scale_row + 1, :],
                     mod_ref[shift_row:shift_row + 1, :])
        h_ref[...] = h.astype(BF16)

    o_ref[...] = jnp.dot(h_ref[...], w_ref[...], preferred_element_type=F32).astype(o_ref.dtype)


def _norm_proj(geom, x, mod_l, gvec, w, *, shift_row, scale_row, tm, tn):
    t, d = x.shape
    n = w.shape[1]
    return pl.pallas_call(
        functools.partial(_norm_proj_kernel, shift_row=shift_row, scale_row=scale_row),
        grid=(t // tm, n // tn),
        in_specs=[
            pl.BlockSpec((tm, d), lambda i, j: (i, 0)),
            pl.BlockSpec((None, 6, d), lambda i, j: (_tile_seq(geom, i, tm), 0, 0)),
            pl.BlockSpec((1, d), lambda i, j: (0, 0)),
            pl.BlockSpec((d, tn), lambda i, j: (0, j)),
        ],
        out_specs=pl.BlockSpec((tm, tn), lambda i, j: (i, j)),
        out_shape=jax.ShapeDtypeStruct((t, n), BF16),
        scratch_shapes=[pltpu.VMEM((tm, d), BF16)],
        compiler_params=_cparams(("parallel", "arbitrary")),
        name="norm_proj",
    )(x, mod_l, gvec, w)


def _rot_halves(ref, h, cos, sin):
    a = ref[:, h * RET_DK:h * RET_DK + RET_HALF].astype(F32)
    b = ref[:, h * RET_DK + RET_HALF:(h + 1) * RET_DK].astype(F32)
    return a * cos - b * sin, a * sin + b * cos


def _ret_bwd_state_kernel(k_ref, v_ref, cos_ref, sin_ref, kdb_ref, rb_ref, r_ref, *, geom, chunk, cdec):
    c = pl.num_programs(0) - 1 - pl.program_id(0)
    is_last = _tile_pos(geom, c, chunk) + chunk == _tile_seq_len(geom, c, chunk)

    @pl.when(is_last)
    def _():
        r_ref[...] = jnp.zeros_like(r_ref)

    cos = cos_ref[...]
    sin = sin_ref[...]
    for h in range(RET_HEADS):
        k1, k2 = _rot_halves(k_ref, h, cos, sin)
        dec = kdb_ref[h]
        kd = jnp.concatenate([k1 * dec, k2 * dec], axis=1).astype(BF16)
        vh = v_ref[:, h * RET_DV:(h + 1) * RET_DV].astype(BF16)
        r = r_ref[h]
        rb_ref[h] = r.astype(BF16)
        upd = lax.dot_general(kd, vh, (((0,), (0,)), ((), ())), preferred_element_type=F32)
        r_ref[h] = r * cdec[h] + upd


def _ret_main_kernel(q_ref, k_ref, v_ref, g_ref, rb_ref, cos_ref, sin_ref, dmat_ref, qdf_ref, qdb_ref,
                     kdf_ref, wo_ref, o_ref, rf_ref, *, geom, chunk, cdec):
    i = pl.program_id(0)

    @pl.when(_tile_pos(geom, i, chunk) == 0)
    def _():
        rf_ref[...] = jnp.zeros_like(rf_ref)

    cos = cos_ref[...]
    sin = sin_ref[...]
    acc = jnp.zeros(o_ref.shape, F32)
    for h in range(RET_HEADS):
        q1, q2 = _rot_halves(q_ref, h, cos, sin)
        k1, k2 = _rot_halves(k_ref, h, cos, sin)
        qr = jnp.concatenate([q1, q2], axis=1).astype(BF16)
        kr = jnp.concatenate([k1, k2], axis=1).astype(BF16)
        vh = v_ref[:, h * RET_DV:(h + 1) * RET_DV].astype(BF16)
        s = lax.dot_general(qr, kr, (((1,), (1,)), ((), ())), preferred_element_type=F32) * dmat_ref[h]
        o = jnp.dot(s.astype(BF16), vh, preferred_element_type=F32)
        qdf = qdf_ref[h]
        qf = jnp.concatenate([q1 * qdf, q2 * qdf], axis=1).astype(BF16)
        rf = rf_ref[h]
        o = o + jnp.dot(qf, rf.astype(BF16), preferred_element_type=F32)
        qdb = qdb_ref[h]
        qb = jnp.concatenate([q1 * qdb, q2 * qdb], axis=1).astype(BF16)
        o = o + jnp.dot(qb, rb_ref[h], preferred_element_type=F32)
        kdf = kdf_ref[h]
        kf = jnp.concatenate([k1 * kdf, k2 * kdf], axis=1).astype(BF16)
        rf_ref[h] = rf * cdec[h] + lax.dot_general(kf, vh, (((0,), (0,)), ((), ())),
                                                   preferred_element_type=F32)
        oc = o - jnp.mean(o, axis=-1, keepdims=True)
        on = oc * lax.rsqrt(jnp.mean(oc * oc, axis=-1, keepdims=True) + GN_EPS)
        og = (on * _silu(g_ref[:, h * RET_DV:(h + 1) * RET_DV].astype(F32))).astype(BF16)
        acc = acc + jnp.dot(og, wo_ref[h * RET_DV:(h + 1) * RET_DV, :], preferred_element_type=F32)
    o_ref[...] = acc.astype(o_ref.dtype)


def _retention_tables(chunk, s_max):
    log_gamma = jnp.log1p(-jnp.exp2(-5.0 - jnp.arange(RET_HEADS, dtype=F32)))
    idx = jnp.arange(chunk, dtype=F32)
    dist = jnp.abs(idx[:, None] - idx[None, :])
    dmat = jnp.exp(log_gamma[:, None, None] * dist[None])

    def rows(e):
        return jnp.broadcast_to(jnp.exp(log_gamma[:, None] * e[None, :])[:, :, None],
                                (RET_HEADS, chunk, RET_HALF))

    qdf = rows(idx + 1.0)
    qdb = rows(chunk - idx)
    kdf = rows(chunk - 1.0 - idx)
    kdb = rows(idx)
    theta = 1.0 / (ROPE_BASE ** jnp.linspace(0.0, 1.0, RET_HALF, dtype=F32))
    ang = jnp.arange(s_max, dtype=F32)[:, None] * theta[None, :]
    k_scale = RET_DK ** -0.5
    return jnp.cos(ang), jnp.sin(ang), dmat * k_scale, qdf, qdb, kdf * k_scale, kdb * k_scale


def _chunk_decay(chunk):
    lg = np.log1p(-np.exp2(-5.0 - np.arange(RET_HEADS, dtype=np.float32))).astype(np.float32)
    return tuple(float(v) for v in np.exp(lg * np.float32(chunk)).astype(np.float32))


def _retention(geom, proj_a, w_ret_o, tables, *, chunk):
    t = proj_a.shape[0]
    n_chunks = t // chunk
    cos, sin, dmat, qdf, qdb, kdf, kdb = tables
    cdec = _chunk_decay(chunk)
    state_shape = (RET_HEADS, RET_DK, RET_DV)

    def pos_blk(c):
        return _tile_pos(geom, c, chunk) // chunk

    rev = lambda i: n_chunks - 1 - i
    tab_spec = pl.BlockSpec((RET_HEADS, chunk, RET_HALF), lambda i: (0, 0, 0))
    rb = pl.pallas_call(
        functools.partial(_ret_bwd_state_kernel, geom=geom, chunk=chunk, cdec=cdec),
        grid=(n_chunks,),
        in_specs=[
            pl.BlockSpec((chunk, RET_QK_W), lambda i: (rev(i), 1)),
            pl.BlockSpec((chunk, RET_V_W), lambda i: (rev(i), 1)),
            pl.BlockSpec((chunk, RET_HALF), lambda i: (pos_blk(rev(i)), 0)),
            pl.BlockSpec((chunk, RET_HALF), lambda i: (pos_blk(rev(i)), 0)),
            tab_spec,
        ],
        out_specs=pl.BlockSpec((None,) + state_shape, lambda i: (rev(i), 0, 0, 0)),
        out_shape=jax.ShapeDtypeStruct((n_chunks,) + state_shape, BF16),
        scratch_shapes=[pltpu.VMEM(state_shape, F32)],
        compiler_params=_cparams(("arbitrary",)),
        name="ret_bwd_state",
    )(proj_a, proj_a, cos, sin, kdb)

    return pl.pallas_call(
        functools.partial(_ret_main_kernel, geom=geom, chunk=chunk, cdec=cdec),
        grid=(n_chunks,),
        in_specs=[
            pl.BlockSpec((chunk, RET_QK_W), lambda i: (i, 0)),
            pl.BlockSpec((chunk, RET_QK_W), lambda i: (i, 1)),
            pl.BlockSpec((chunk, RET_V_W), lambda i: (i, 1)),
            pl.BlockSpec((chunk, RET_V_W), lambda i: (i, 2)),
            pl.BlockSpec((None,) + state_shape, lambda i: (i, 0, 0, 0)),
            pl.BlockSpec((chunk, RET_HALF), lambda i: (pos_blk(i), 0)),
            pl.BlockSpec((chunk, RET_HALF), lambda i: (pos_blk(i), 0)),
            pl.BlockSpec((RET_HEADS, chunk, chunk), lambda i: (0, 0, 0)),
            tab_spec, tab_spec, tab_spec,
            pl.BlockSpec((RET_V_W, D_MODEL), lambda i: (0, 0)),
        ],
        out_specs=pl.BlockSpec((chunk, D_MODEL), lambda i: (i, 0)),
        out_shape=jax.ShapeDtypeStruct((t, D_MODEL), BF16),
        scratch_shapes=[pltpu.VMEM(state_shape, F32)],
        compiler_params=_cparams(("arbitrary",)),
        name="ret_main",
    )(proj_a, proj_a, proj_a, proj_a, rb, cos, sin, dmat, qdf, qdb, kdf, w_ret_o)


def _log_sigmoid(x):
    return jnp.minimum(x, 0.0) - jnp.log1p(jnp.exp(-jnp.abs(x)))


LRU_SEGS = V7X_SUBLANES
LRU_LEAD = CONV_LEFT * LRU_SEGS
LRU_TAIL = (CONV_WIDTH - 1 - CONV_LEFT) * LRU_SEGS


def _lru_load_segment_major(x_ref, prev_ref, next_ref, keep_prev, keep_next, xs_ref):
    tl = x_ref.shape[0]
    seg = tl // LRU_SEGS
    x = x_ref[...].astype(F32)
    prev_hi = prev_ref[...].astype(F32)[LRU_HALO - V7X_SUBLANES:] * keep_prev
    next_lo = next_ref[...].astype(F32)[:V7X_SUBLANES] * keep_next
    row8 = lax.broadcasted_iota(jnp.int32, (V7X_SUBLANES, LRU_BW), 0)
    last = V7X_SUBLANES - 1
    for n in range(LRU_BLOCKS):
        sl = slice(n * LRU_BW, (n + 1) * LRU_BW)
        for s in range(LRU_SEGS):
            xs_ref[n, pl.ds(LRU_LEAD + s, seg, stride=LRU_SEGS), :] = x[s * seg:(s + 1) * seg, sl]
        slab = lambda t: xs_ref[n, LRU_LEAD + t * LRU_SEGS:LRU_LEAD + (t + 1) * LRU_SEGS, :]
        p = prev_hi[:, sl]
        m1 = pltpu.roll(jnp.where(row8 == last, p, slab(seg - 1)), 1, axis=0)
        m2 = pltpu.roll(jnp.where(row8 == last, pltpu.roll(p, 1, axis=0), slab(seg - 2)), 1, axis=0)
        p1 = pltpu.roll(jnp.where(row8 == 0, next_lo[:, sl], slab(0)), last, axis=0)
        xs_ref[n, 0:LRU_SEGS, :] = m2
        xs_ref[n, LRU_SEGS:LRU_LEAD, :] = m1
        xs_ref[n, LRU_LEAD + tl:LRU_LEAD + tl + LRU_TAIL, :] = p1


def _lru_gates(xs_ref, d, wconv_ref, bconv_ref, wrg_ref, brg_ref, wig_ref, big_ref, lam_ref, a_ref, u_ref):
    tl = a_ref.shape[1]
    c = (0.5 * LRU_C * math.log2(math.e)) * _log_sigmoid(lam_ref[d])
    for n in range(LRU_BLOCKS):
        sl = slice(n * LRU_BW, (n + 1) * LRU_BW)
        xh = bconv_ref[:, sl] + xs_ref[n, 0:tl, :] * wconv_ref[0:1, sl]
        for j in range(1, CONV_WIDTH):
            xh = xh + xs_ref[n, j * LRU_SEGS:j * LRU_SEGS + tl, :] * wconv_ref[j:j + 1, sl]
        xb = xh.astype(BF16)
        th_r = jnp.tanh(jnp.dot(xb, wrg_ref[d, n], preferred_element_type=F32) + brg_ref[d][:, sl])
        th_i = jnp.tanh(jnp.dot(xb, wig_ref[d, n], preferred_element_type=F32) + big_ref[d][:, sl])
        a = jnp.exp2(c[:, sl] * th_r + c[:, sl])
        a_ref[n] = a
        y = 1.0 - a * a
        u_ref[n] = (y * lax.rsqrt(jnp.maximum(y, SQRT_TINY))) * ((th_i + 1.0) * xh)


def _lru_scan_segments(a_ref, u_ref, hs_ref, as_ref, fin_ref, carry_ref, o_ref, reverse):
    tl = a_ref.shape[1]
    seg = tl // LRU_SEGS
    slab_shape = (LRU_SEGS, LRU_BW)

    def step(k, carry):
        t = seg - 1 - k if reverse else k
        rows = pl.ds(pl.multiple_of(t * LRU_SEGS, LRU_SEGS), LRU_SEGS)
        hs, decays = [], []
        for n in range(LRU_BLOCKS):
            a = a_ref[n, rows, :]
            h = a * carry[0][n] + u_ref[n, rows, :]
            decay = a * carry[1][n]
            hs_ref[n, rows, :] = h
            as_ref[n, rows, :] = decay
            hs.append(h)
            decays.append(decay)
        return tuple(hs), tuple(decays)

    init = (tuple(jnp.zeros(slab_shape, F32) for _ in range(LRU_BLOCKS)),
            tuple(jnp.ones(slab_shape, F32) for _ in range(LRU_BLOCKS)))
    h_end, decay_end = lax.fori_loop(0, seg, step, init, unroll=4)

    order = range(LRU_SEGS - 1, -1, -1) if reverse else range(LRU_SEGS)
    for n in range(LRU_BLOCKS):
        sl = slice(n * LRU_BW, (n + 1) * LRU_BW)
        fin_ref[0] = h_end[n]
        fin_ref[1] = decay_end[n]
        c = carry_ref[n, 0:1, :]
        for s in order:
            fin_ref[2, s:s + 1, :] = c
            c = fin_ref[0, s:s + 1, :] + fin_ref[1, s:s + 1, :] * c
        carry_ref[n, 0:1, :] = c
        for s in range(LRU_SEGS):
            seg_rows = pl.ds(s, seg, stride=LRU_SEGS)
            h = hs_ref[n, seg_rows, :] + as_ref[n, seg_rows, :] * fin_ref[2, s:s + 1, :]
            o_ref[s * seg:(s + 1) * seg, sl] = h.astype(o_ref.dtype)


def _lru_kernel(xf_ref, xfp_ref, xfn_ref, xb_ref, xbp_ref, xbn_ref, wconv_ref, bconv_ref, wrg_ref, brg_ref,
                wig_ref, big_ref, lam_ref, hf_ref, hb_ref, xs_ref, a_ref, u_ref, hs_ref, as_ref, fin_ref,
                cf_ref, cb_ref, *, geom, tl):
    i = pl.program_id(0)
    ib = pl.num_programs(0) - 1 - i

    def flags(tile):
        pos = _tile_pos(geom, tile, tl)
        return pos == 0, pos + tl == _tile_seq_len(geom, tile, tl)

    def keep(flag):
        return jnp.where(flag, 0.0, 1.0)

    def direction(d, x_ref, prev_ref, next_ref, first, last, carry_ref, o_ref):
        _lru_load_segment_major(x_ref, prev_ref, next_ref, keep(first), keep(last), xs_ref)
        _lru_gates(xs_ref, d, wconv_ref, bconv_ref, wrg_ref, brg_ref, wig_ref, big_ref, lam_ref, a_ref, u_ref)
        _lru_scan_segments(a_ref, u_ref, hs_ref, as_ref, fin_ref, carry_ref, o_ref, reverse=(d == 1))

    first, last = flags(i)

    @pl.when(first)
    def _():
        cf_ref[...] = jnp.zeros_like(cf_ref)

    direction(0, xf_ref, xfp_ref, xfn_ref, first, last, cf_ref, hf_ref)

    first, last = flags(ib)

    @pl.when(last)
    def _():
        cb_ref[...] = jnp.zeros_like(cb_ref)

    direction(1, xb_ref, xbp_ref, xbn_ref, first, last, cb_ref, hb_ref)


def _lru(geom, proj_b, w_conv, b_conv, w_rg, b_rg, w_ig, b_ig, lam, *, tl):
    t = proj_b.shape[0]
    n = t // tl
    r8 = tl // LRU_HALO
    n8 = t // LRU_HALO
    w = LRU_WIDTH
    rev = lambda i: n - 1 - i
    prev_blk = lambda i: jnp.maximum(i * r8 - 1, 0)
    next_blk = lambda i: jnp.minimum((i + 1) * r8, n8 - 1)
    full = lambda a: pl.BlockSpec(a.shape, lambda i: (0,) * a.ndim)
    w_conv = 0.5 * w_conv
    b_conv2 = 0.5 * b_conv.reshape(1, w)
    b_rg3 = 0.5 * b_rg.reshape(2, 1, w)
    b_ig3 = 0.5 * b_ig.reshape(2, 1, w)
    lam3 = lam.reshape(2, 1, w)
    return pl.pallas_call(
        functools.partial(_lru_kernel, geom=geom, tl=tl),
        grid=(n,),
        in_specs=[
            pl.BlockSpec((tl, w), lambda i: (i, 0)),
            pl.BlockSpec((LRU_HALO, w), lambda i: (prev_blk(i), 0)),
            pl.BlockSpec((LRU_HALO, w), lambda i: (next_blk(i), 0)),
            pl.BlockSpec((tl, w), lambda i: (rev(i), 0)),
            pl.BlockSpec((LRU_HALO, w), lambda i: (prev_blk(rev(i)), 0)),
            pl.BlockSpec((LRU_HALO, w), lambda i: (next_blk(rev(i)), 0)),
            full(w_conv), full(b_conv2), full(w_rg), full(b_rg3), full(w_ig), full(b_ig3), full(lam3),
        ],
        out_specs=[
            pl.BlockSpec((tl, w), lambda i: (i, 0)),
            pl.BlockSpec((tl, w), lambda i: (rev(i), 0)),
        ],
        out_shape=[jax.ShapeDtypeStruct((t, w), BF16), jax.ShapeDtypeStruct((t, w), BF16)],
        scratch_shapes=[pltpu.VMEM((LRU_BLOCKS, LRU_LEAD + tl + LRU_TAIL, LRU_BW), F32)]
        + [pltpu.VMEM((LRU_BLOCKS, tl, LRU_BW), F32)] * 4
        + [pltpu.VMEM((3, LRU_SEGS, LRU_BW), F32)]
        + [pltpu.VMEM((LRU_BLOCKS, V7X_SUBLANES, LRU_BW), F32)] * 2,
        compiler_params=_cparams(("arbitrary",)),
        name="lru_scan",
    )(proj_b, proj_b, proj_b, proj_b, proj_b, proj_b, w_conv, b_conv2, w_rg, b_rg3, w_ig, b_ig3, lam3)


def _gelu_tanh(x):
    return 0.5 * x * (1.0 + jnp.tanh(math.sqrt(2.0 / math.pi) * (x + 0.044715 * (x * x * x))))


def _merge_kernel(x_ref, ya_ref, hf_ref, hb_ref, gl_ref, ga_ref, gb_ref, mod_ref, wl_ref, wo_ref, o_ref):
    f32 = lambda ref: ref[...].astype(F32)
    y = ((f32(hf_ref) + f32(hb_ref)) * _gelu_tanh(f32(gl_ref))).astype(BF16)
    yb = jnp.dot(y, wl_ref[...], preferred_element_type=F32)
    m = _sigmoid(f32(ga_ref)) * f32(ya_ref) + _sigmoid(f32(gb_ref)) * yb
    mix = jnp.dot(m.astype(BF16), wo_ref[...], preferred_element_type=F32)
    o_ref[...] = x_ref[...] + mod_ref[2:3, :] * mix


def _merge(geom, x, ya, hf, hb, proj_a, proj_b, mod_l, w_lru_o, w_out, *, tm):
    t, d = x.shape
    gate_a_blk = (2 * RET_QK_W + 2 * RET_V_W) // d
    tok = lambda w: pl.BlockSpec((tm, w), lambda i: (i, 0))
    return pl.pallas_call(
        _merge_kernel,
        grid=(t // tm,),
        in_specs=[
            tok(d), tok(d), tok(LRU_WIDTH), tok(LRU_WIDTH),
            pl.BlockSpec((tm, LRU_WIDTH), lambda i: (i, 1)),
            pl.BlockSpec((tm, d), lambda i: (i, gate_a_blk)),
            pl.BlockSpec((tm, d), lambda i: (i, gate_a_blk + 1)),
            pl.BlockSpec((None, 6, d), lambda i: (_tile_seq(geom, i, tm), 0, 0)),
            pl.BlockSpec((LRU_WIDTH, d), lambda i: (0, 0)),
            pl.BlockSpec((d, d), lambda i: (0, 0)),
        ],
        out_specs=tok(d),
        out_shape=jax.ShapeDtypeStruct((t, d), F32),
        compiler_params=_cparams(("parallel",)),
        name="merge",
    )(x, ya, hf, hb, proj_b, proj_a, proj_a, mod_l, w_lru_o, w_out)


def _ffn_kernel(x_ref, mod_ref, g_ref, wg_ref, wu_ref, wd_ref, o_ref, *, col_chunks):
    x = x_ref[...]
    h = _rms_mod(x, g_ref[...], mod_ref[4:5, :], mod_ref[3:4, :]).astype(BF16)
    acc = None
    for c0, c1 in col_chunks:
        a = jnp.dot(h, wg_ref[:, c0:c1], preferred_element_type=F32)
        u = jnp.dot(h, wu_ref[:, c0:c1], preferred_element_type=F32)
        part = jnp.dot((_silu(a) * u).astype(BF16), wd_ref[c0:c1, :], preferred_element_type=F32)
        acc = part if acc is None else acc + part
    o_ref[...] = x + mod_ref[5:6, :] * acc


def _ffn(geom, x, mod_l, gvec, wg, wu, wd, *, tm, tf):
    t, d = x.shape
    ff = wg.shape[1]
    col_chunks = tuple((c, min(c + tf, ff)) for c in range(0, ff, tf))
    resident = lambda shape: pl.BlockSpec(shape, lambda i: (0, 0), pipeline_mode=pl.Buffered(1))
    return pl.pallas_call(
        functools.partial(_ffn_kernel, col_chunks=col_chunks),
        grid=(t // tm,),
        in_specs=[
            pl.BlockSpec((tm, d), lambda i: (i, 0)),
            pl.BlockSpec((None, 6, d), lambda i: (_tile_seq(geom, i, tm), 0, 0)),
            pl.BlockSpec((1, d), lambda i: (0, 0)),
            resident((d, ff)), resident((d, ff)), resident((ff, d)),
        ],
        out_specs=pl.BlockSpec((tm, d), lambda i: (i, 0)),
        out_shape=jax.ShapeDtypeStruct((t, d), F32),
        compiler_params=_cparams(("parallel",)),
        name="ffn_dense",
    )(x, mod_l, gvec, wg, wu, wd)


ROUTE_E0, ROUTE_E1, ROUTE_R0, ROUTE_R1, ROUTE_G0, ROUTE_G1 = range(6)

ROW_TILES = D_MODEL // V7X_LANES


def _store_token_rows(ref, val):
    rows = val.shape[0]
    for j in range(ROW_TILES):
        ref[pl.ds(j, rows, stride=ROW_TILES), :] = val[:, j * V7X_LANES:(j + 1) * V7X_LANES]


def _load_token_rows(ref, rows):
    return jnp.concatenate([ref[pl.ds(j, rows, stride=ROW_TILES), :] for j in range(ROW_TILES)], axis=1)


def _router_kernel(x_ref, mod_ref, g_ref, wr_ref, h_ref, route_ref, cnt_ref, carry_ref):
    i = pl.program_id(0)
    tm = x_ref.shape[0]

    @pl.when(i == 0)
    def _():
        carry_ref[...] = jnp.zeros_like(carry_ref)

    h = _rms_mod(x_ref[...], g_ref[...], mod_ref[4:5, :], mod_ref[3:4, :])
    _store_token_rows(h_ref, h)
    logits = jnp.dot(h, wr_ref[...], preferred_element_type=F32, precision=lax.Precision.HIGHEST)
    lane = lax.broadcasted_iota(jnp.int32, logits.shape, 1).astype(F32)
    logits = jnp.where(lane < N_EXPERTS, logits, -jnp.inf)
    m1 = jnp.max(logits, axis=-1, keepdims=True)
    i1 = jnp.min(jnp.where(logits == m1, lane, float(V7X_LANES)), axis=-1, keepdims=True)
    rest = jnp.where(lane == i1, -jnp.inf, logits)
    m2 = jnp.max(rest, axis=-1, keepdims=True)
    i2 = jnp.min(jnp.where(rest == m2, lane, float(V7X_LANES)), axis=-1, keepdims=True)
    ex = jnp.exp(m2 - m1)
    g1 = 1.0 / (1.0 + ex)
    g2 = ex / (1.0 + ex)
    sel1 = lane == i1
    sel2 = lane == i2
    onehot = jnp.where(sel1 | sel2, 1.0, 0.0)
    row = lax.broadcasted_iota(jnp.int32, (tm, tm), 0)
    col = lax.broadcasted_iota(jnp.int32, (tm, tm), 1)
    lower = jnp.where(col < row, 1.0, 0.0).astype(BF16)
    before = jnp.dot(lower, onehot.astype(BF16), preferred_element_type=F32) + carry_ref[0:1, :]
    r1 = jnp.sum(jnp.where(sel1, before, 0.0), axis=-1, keepdims=True)
    r2 = jnp.sum(jnp.where(sel2, before, 0.0), axis=-1, keepdims=True)
    out_lane = lax.broadcasted_iota(jnp.int32, route_ref.shape, 1)
    vals = (i1.astype(F32), i2.astype(F32), r1, r2, g1, g2)
    packed = jnp.zeros(route_ref.shape, F32)
    for slot, v in enumerate(vals):
        packed = jnp.where(out_lane == slot, v, packed)
    route_ref[...] = packed
    carry = carry_ref[0:1, :] + jnp.sum(onehot, axis=0, keepdims=True)
    carry_ref[0:1, :] = carry
    cnt_ref[...] = jnp.broadcast_to(carry, cnt_ref.shape).astype(jnp.int32)


def _router(geom, x, mod_l, gvec, w_router, *, tm):
    t, d = x.shape
    w_pad = jnp.pad(w_router, ((0, 0), (0, V7X_LANES - N_EXPERTS)))
    return pl.pallas_call(
        _router_kernel,
        grid=(t // tm,),
        in_specs=[
            pl.BlockSpec((tm, d), lambda i: (i, 0)),
            pl.BlockSpec((None, 6, d), lambda i: (_tile_seq(geom, i, tm), 0, 0)),
            pl.BlockSpec((1, d), lambda i: (0, 0)),
            pl.BlockSpec((d, V7X_LANES), lambda i: (0, 0)),
        ],
        out_specs=[
            pl.BlockSpec((tm * ROW_TILES, V7X_LANES), lambda i: (i, 0)),
            pl.BlockSpec((tm, V7X_LANES), lambda i: (i, 0)),
            pl.BlockSpec((V7X_SUBLANES, V7X_LANES), lambda i: (0, 0)),
        ],
        out_shape=[
            jax.ShapeDtypeStruct((t * ROW_TILES, V7X_LANES), F32),
            jax.ShapeDtypeStruct((t, V7X_LANES), F32),
            jax.ShapeDtypeStruct((V7X_SUBLANES, V7X_LANES), jnp.int32),
        ],
        scratch_shapes=[pltpu.VMEM((V7X_SUBLANES, V7X_LANES), F32)],
        compiler_params=_cparams(("arbitrary",)),
        name="router",
    )(x, mod_l, gvec, w_pad)


def _token_copy(src, src_tok, dst, dst_tok, sem):
    s = pl.multiple_of(src_tok * ROW_TILES, ROW_TILES)
    d = pl.multiple_of(dst_tok * ROW_TILES, ROW_TILES)
    return pltpu.make_async_copy(src.at[pl.ds(s, ROW_TILES), :], dst.at[pl.ds(d, ROW_TILES), :], sem)


def _dispatch_kernel(fill_ref, dest_ref, src_ref, out_hbm, zero_ref, sem, zero_sem, *, td):
    @pl.when(pl.program_id(0) == 0)
    def _():
        zero_ref[...] = jnp.zeros_like(zero_ref)
        for wait in (False, True):
            for e in range(N_EXPERTS):
                def pad_row(r, carry):
                    copy = _token_copy(zero_ref, 0, out_hbm, r, zero_sem)
                    copy.wait() if wait else copy.start()
                    return carry

                lax.fori_loop(fill_ref[0, e], fill_ref[1, e], pad_row, 0)

    def issue(t, carry):
        for k in range(TOP_K):
            _token_copy(src_ref, t, out_hbm, dest_ref[k, t], sem).start()
        return carry

    lax.fori_loop(0, td, issue, 0, unroll=8)
    for _ in range(TOP_K):
        pltpu.make_async_copy(src_ref, out_hbm.at[pl.ds(0, td * ROW_TILES), :], sem).wait()


def _dispatch(h2_rows, dest, fill, n_rows, *, td):
    grid_spec = pltpu.PrefetchScalarGridSpec(
        num_scalar_prefetch=1,
        grid=(dest.shape[1] // td,),
        in_specs=[
            pl.BlockSpec((TOP_K, td), lambda i, fill: (0, i), memory_space=pltpu.SMEM),
            pl.BlockSpec((td * ROW_TILES, V7X_LANES), lambda i, fill: (i, 0)),
        ],
        out_specs=pl.BlockSpec(memory_space=pl.ANY),
        scratch_shapes=[pltpu.VMEM((ROW_TILES, V7X_LANES), h2_rows.dtype),
                        pltpu.SemaphoreType.DMA(()), pltpu.SemaphoreType.DMA(())],
    )
    return pl.pallas_call(
        functools.partial(_dispatch_kernel, td=td),
        grid_spec=grid_spec,
        out_shape=jax.ShapeDtypeStruct((n_rows * ROW_TILES, V7X_LANES), h2_rows.dtype),
        compiler_params=_cparams(("arbitrary",)),
        name="moe_dispatch",
    )(fill, dest, h2_rows)


def _experts_kernel(blk_e_ref, nused_ref, x_ref, wg_ref, wu_ref, wd_ref, o_ref, xb_ref, acc_ref):
    del blk_e_ref
    b = pl.program_id(0)
    j = pl.program_id(1)
    bm = xb_ref.shape[0]

    @pl.when(b < nused_ref[0])
    def _():
        @pl.when(j == 0)
        def _():
            xb_ref[...] = _load_token_rows(x_ref, bm).astype(BF16)
            acc_ref[...] = jnp.zeros_like(acc_ref)

        xb = xb_ref[...]
        a = jnp.dot(xb, wg_ref[...], preferred_element_type=F32)
        u = jnp.dot(xb, wu_ref[...], preferred_element_type=F32)
        acc_ref[...] += jnp.dot((_silu(a) * u).astype(BF16), wd_ref[...], preferred_element_type=F32)

        @pl.when(j == pl.num_programs(1) - 1)
        def _():
            _store_token_rows(o_ref, acc_ref[...])

    @pl.when((b >= nused_ref[0]) & (j == pl.num_programs(1) - 1))
    def _():
        o_ref[...] = jnp.zeros_like(o_ref)


def _experts(xbuf, blk_e, n_used, wg, wu, wd, *, bm, tf):
    p = xbuf.shape[0] // ROW_TILES
    d = D_MODEL
    ff = wg.shape[2]
    nj = ff // tf
    rows_blk = (bm * ROW_TILES, V7X_LANES)

    def blk(b, nu):
        return jnp.minimum(b, nu[0] - 1)

    def ffc(b, j, nu):
        return jnp.where(b < nu[0], j, nj - 1)

    grid_spec = pltpu.PrefetchScalarGridSpec(
        num_scalar_prefetch=2,
        grid=(p // bm, nj),
        in_specs=[
            pl.BlockSpec(rows_blk, lambda b, j, be, nu: (blk(b, nu), 0)),
            pl.BlockSpec((None, d, tf), lambda b, j, be, nu: (be[blk(b, nu)], 0, ffc(b, j, nu))),
            pl.BlockSpec((None, d, tf), lambda b, j, be, nu: (be[blk(b, nu)], 0, ffc(b, j, nu))),
            pl.BlockSpec((None, tf, d), lambda b, j, be, nu: (be[blk(b, nu)], ffc(b, j, nu), 0)),
        ],
        out_specs=pl.BlockSpec(rows_blk, lambda b, j, be, nu: (b, 0)),
        scratch_shapes=[pltpu.VMEM((bm, d), BF16), pltpu.VMEM((bm, d), F32)],
    )
    return pl.pallas_call(
        _experts_kernel,
        grid_spec=grid_spec,
        out_shape=jax.ShapeDtypeStruct(xbuf.shape, F32),
        compiler_params=_cparams(("arbitrary", "arbitrary")),
        name="moe_experts",
    )(blk_e, n_used, xbuf, wg, wu, wd)


def _moe_out_kernel(dest_ref, dest_next_ref, x_ref, route_ref, mod_ref, gf_ref, ybuf_hbm, op_ref, os_ref,
                    ya0_ref, ya1_ref, yb0_ref, yb1_ref, sems, *, final_norm, prompt_tiles):
    i = pl.program_id(0)
    tm = x_ref.shape[0]
    bufs = ((ya0_ref, ya1_ref), (yb0_ref, yb1_ref))

    def gather(d_ref, slot):
        def issue(t, carry):
            for k in range(TOP_K):
                _token_copy(ybuf_hbm, d_ref[k, t], bufs[slot][k], t, sems.at[slot]).start()
            return carry

        lax.fori_loop(0, tm, issue, 0, unroll=8)

    def wait(slot):
        for k in range(TOP_K):
            pltpu.make_async_copy(ybuf_hbm.at[pl.ds(0, tm * ROW_TILES), :], bufs[slot][k], sems.at[slot]).wait()

    def combine(slot):
        g1 = route_ref[:, ROUTE_G0:ROUTE_G0 + 1]
        g2 = route_ref[:, ROUTE_G1:ROUTE_G1 + 1]
        y = _load_token_rows(bufs[slot][0], tm) * g1 + _load_token_rows(bufs[slot][1], tm) * g2
        x = x_ref[...] + mod_ref[5:6, :] * y
        if final_norm:
            ms = jnp.mean(x * x, axis=-1, keepdims=True)
            x = x * lax.rsqrt(ms + RMS_EPS) * gf_ref[...]

        @pl.when(i < prompt_tiles)
        def _():
            op_ref[...] = x

        @pl.when(i >= prompt_tiles)
        def _():
            os_ref[...] = x

    @pl.when(i == 0)
    def _():
        gather(dest_ref, 0)

    for slot in range(2):
        @pl.when(i % 2 == slot)
        def _():
            @pl.when(i + 1 < pl.num_programs(0))
            def _():
                gather(dest_next_ref, 1 - slot)

            wait(slot)
            combine(slot)


def _moe_out(geom, x, ybuf, dest, route, mod_l, g_final, *, tm, final_norm):
    t, d = x.shape
    n = t // tm
    prompt_tiles = geom.tokens_p // tm
    y_buf = pltpu.VMEM((tm * ROW_TILES, V7X_LANES), F32)
    return pl.pallas_call(
        functools.partial(_moe_out_kernel, final_norm=final_norm, prompt_tiles=prompt_tiles),
        grid=(n,),
        in_specs=[
            pl.BlockSpec((TOP_K, tm), lambda i: (0, i), memory_space=pltpu.SMEM),
            pl.BlockSpec((TOP_K, tm), lambda i: (0, jnp.minimum(i + 1, n - 1)), memory_space=pltpu.SMEM),
            pl.BlockSpec((tm, d), lambda i: (i, 0)),
            pl.BlockSpec((tm, V7X_LANES), lambda i: (i, 0)),
            pl.BlockSpec((None, 6, d), lambda i: (_tile_seq(geom, i, tm), 0, 0)),
            pl.BlockSpec((1, d), lambda i: (0, 0)),
            pl.BlockSpec(memory_space=pl.ANY),
        ],
        out_specs=[
            pl.BlockSpec((tm, d), lambda i: (jnp.minimum(i, prompt_tiles - 1), 0)),
            pl.BlockSpec((tm, d), lambda i: (jnp.maximum(i - prompt_tiles, 0), 0)),
        ],
        out_shape=[jax.ShapeDtypeStruct((geom.tokens_p, d), F32),
                   jax.ShapeDtypeStruct((t - geom.tokens_p, d), F32)],
        scratch_shapes=[y_buf, y_buf, y_buf, y_buf, pltpu.SemaphoreType.DMA((2,))],
        compiler_params=_cparams(("arbitrary",)),
        name="moe_out",
    )(dest, dest, x, route, mod_l, g_final, ybuf)


def _final_norm_kernel(x_ref, g_ref, o_ref):
    x = x_ref[...]
    o_ref[...] = x * lax.rsqrt(jnp.mean(x * x, axis=-1, keepdims=True) + RMS_EPS) * g_ref[...]


def _final_norm(x, g_final, *, tm):
    t, d = x.shape
    return pl.pallas_call(
        _final_norm_kernel,
        grid=(t // tm,),
        in_specs=[pl.BlockSpec((tm, d), lambda i: (i, 0)), pl.BlockSpec((1, d), lambda i: (0, 0))],
        out_specs=pl.BlockSpec((tm, d), lambda i: (i, 0)),
        out_shape=jax.ShapeDtypeStruct((t, d), F32),
        compiler_params=_cparams(("parallel",)),
        name="final_norm",
    )(x, g_final)


def _moe(geom, x, mod_l, gvec, w_router, wg, wu, wd, g_final, *, tiles, final_norm):
    t, d = x.shape
    bm = tiles["moe_rows"]
    h2, route, counts = _router(geom, x, mod_l, gvec, w_router, tm=tiles["router"])
    counts = counts[0, :N_EXPERTS]
    padded = ((counts + bm - 1) // bm) * bm
    pad_end = jnp.cumsum(padded)
    pad_start = pad_end - padded
    experts = route[:, ROUTE_E0:ROUTE_E1 + 1].astype(jnp.int32)
    ranks = route[:, ROUTE_R0:ROUTE_R1 + 1].astype(jnp.int32)
    dest = (pad_start[experts] + ranks).T
    n_rows = t * TOP_K + N_EXPERTS * bm
    n_blk = n_rows // bm
    blk_start = jnp.arange(n_blk, dtype=jnp.int32) * bm
    blk_e = jnp.minimum(jnp.sum((pad_end[None, :] <= blk_start[:, None]).astype(jnp.int32), axis=1),
                        N_EXPERTS - 1).astype(jnp.int32)
    n_used = (pad_end[-1:] // bm).astype(jnp.int32)
    fill = jnp.stack([pad_start + counts, pad_end.at[N_EXPERTS - 1].set(n_rows)]).astype(jnp.int32)
    xbuf = _dispatch(h2, dest, fill, n_rows, td=tiles["dma_rows"])
    ybuf = _experts(xbuf, blk_e, n_used, wg, wu, wd, bm=bm, tf=tiles["expert_ff"])
    return _moe_out(geom, x, ybuf, dest, route, mod_l, g_final, tm=tiles["token"], final_norm=final_norm)


def _pick_tiles(geom):
    s = math.gcd(geom.s_p, geom.s_s) if geom.n_p and geom.n_s else (geom.s_p if geom.n_p else geom.s_s)
    return {
        "proj": min(1024, s),
        "proj_cols": 2048,
        "ret_chunk": min(512, s),
        "lru": min(512, s),
        "token": min(512, s),
        "ffn_cols": 1536,
        "router": min(512, s),
        "moe_rows": 512,
        "expert_ff": 1792,
        "dma_rows": min(2048, s),
    }


def _trunk(geom, x, c_all, w_ada, b_ada, g_norm1, g_norm2, w_in, w_conv, b_conv, w_rg, b_rg, w_ig, b_ig,
           lru_lambda, w_ret_o, w_lru_o, w_out, w_ff_gate, w_ff_up, w_ff_down,
           w_router, w_e_gate, w_e_up, w_e_down, g_final, tiles):
    depth = w_in.shape[0]
    d = D_MODEL
    n_pad = -(-geom.n_seq // V7X_SUBLANES) * V7X_SUBLANES
    c_pad = jnp.pad(c_all, ((0, n_pad - geom.n_seq), (0, 0)))
    mod = _adaln_mod(c_pad, w_ada, b_ada)
    tables = _retention_tables(tiles["ret_chunk"], max(geom.s_p if geom.n_p else 0, geom.s_s if geom.n_s else 0))

    o_xl = 2 * RET_QK_W + 2 * RET_V_W
    o_ga = o_xl + 2 * LRU_WIDTH
    for l in range(depth):
        wl = w_in[l]
        w_a = jnp.concatenate([wl[:, :o_xl], wl[:, o_ga:]], axis=1).astype(BF16)
        w_b = wl[:, o_xl:o_ga].astype(BF16)
        g1 = g_norm1[l].reshape(1, d)
        g2 = g_norm2[l].reshape(1, d)
        proj_a = _norm_proj(geom, x, mod[l], g1, w_a, shift_row=0, scale_row=1,
                            tm=tiles["proj"], tn=tiles["proj_cols"])
        proj_b = _norm_proj(geom, x, mod[l], g1, w_b, shift_row=0, scale_row=1,
                            tm=tiles["proj"], tn=LRU_WIDTH)
        ya = _retention(geom, proj_a, w_ret_o[l].astype(BF16), tables, chunk=tiles["ret_chunk"])
        hf, hb = _lru(geom, proj_b, w_conv[l], b_conv[l], w_rg[l].astype(BF16), b_rg[l],
                      w_ig[l].astype(BF16), b_ig[l], lru_lambda[l], tl=tiles["lru"])
        x = _merge(geom, x, ya, hf, hb, proj_a, proj_b, mod[l], w_lru_o[l].astype(BF16),
                   w_out[l].astype(BF16), tm=tiles["token"])
        j = l // 2
        last = l == depth - 1
        if l % 2 == 0:
            x = _ffn(geom, x, mod[l], g2, w_ff_gate[j].astype(BF16), w_ff_up[j].astype(BF16),
                     w_ff_down[j].astype(BF16), tm=tiles["token"], tf=tiles["ffn_cols"])
            if last:
                x = _final_norm(x, g_final.reshape(1, d), tm=tiles["token"])
                return x[:geom.tokens_p], x[geom.tokens_p:]
        else:
            parts = _moe(geom, x, mod[l], g2, w_router[j], w_e_gate[j].astype(BF16), w_e_up[j].astype(BF16),
                         w_e_down[j].astype(BF16), g_final.reshape(1, d), tiles=tiles, final_norm=last)
            if last:
                return parts
            x = jnp.concatenate(parts, axis=0)


def kernel(x_prompt, x_sample, c_prompt, c_sample, w_ada, b_ada, g_norm1, g_norm2, w_in, w_conv, b_conv, w_rg, b_rg, w_ig, b_ig, lru_lambda, w_ret_o, w_lru_o, w_out, w_ff_gate, w_ff_up, w_ff_down, w_router, w_e_gate, w_e_up, w_e_down, g_final):
    n_p, s_p, d = x_prompt.shape
    n_s, s_s, _ = x_sample.shape
    geom = Geom(n_p, s_p, n_s, s_s)
    x = jnp.concatenate([x_prompt.reshape(-1, d), x_sample.reshape(-1, d)], axis=0)
    c_all = jnp.concatenate([c_prompt, c_sample], axis=0)
    yp, ys = _trunk(geom, x, c_all, w_ada, b_ada, g_norm1, g_norm2, w_in, w_conv, b_conv, w_rg, b_rg, w_ig, b_ig,
                    lru_lambda, w_ret_o, w_lru_o, w_out, w_ff_gate, w_ff_up, w_ff_down,
                    w_router, w_e_gate, w_e_up, w_e_down, g_final, _pick_tiles(geom))
    return (yp.reshape(n_p, s_p, d), ys.reshape(n_s, s_s, d))
```

```python
import functools
import math
from typing import NamedTuple

import jax
import jax.numpy as jnp
import numpy as np
from jax import lax
from jax.experimental import pallas as pl
from jax.experimental.pallas import tpu as pltpu

F32 = jnp.float32
BF16 = jnp.bfloat16

D_MODEL = 1024
RET_HEADS = 4
RET_DK = 256
RET_DV = 512
RET_HALF = RET_DK // 2
RET_QK_W = RET_HEADS * RET_DK
RET_V_W = RET_HEADS * RET_DV
ROPE_BASE = 10000.0
LRU_WIDTH = 1280
LRU_BLOCKS = 10
LRU_BW = LRU_WIDTH // LRU_BLOCKS
LRU_C = 8.0
CONV_WIDTH = 4
CONV_LEFT = 2
LRU_HALO = 16
N_EXPERTS = 8
TOP_K = 2
RMS_EPS = 1e-6
GN_EPS = 1e-5
SQRT_TINY = 1e-30

V7X_LANES = 128
V7X_SUBLANES = 8
V7X_VMEM_BYTES = 64 * 1024 * 1024
DMA_PRIORITIES = 2
VMEM_LIMIT = (V7X_VMEM_BYTES * 3) // 4

PA_W = 2 * RET_QK_W + 2 * RET_V_W + 2 * D_MODEL
PB_W = 2 * LRU_WIDTH


class Geom(NamedTuple):
    n_p: int
    s_p: int
    n_s: int
    s_s: int

    @property
    def tokens_p(self):
        return self.n_p * self.s_p

    @property
    def tokens(self):
        return self.n_p * self.s_p + self.n_s * self.s_s

    @property
    def n_seq(self):
        return self.n_p + self.n_s


def _tile_seq(g, i, tm):
    t0 = i * tm
    return jnp.where(t0 < g.tokens_p, t0 // g.s_p, g.n_p + (t0 - g.tokens_p) // g.s_s)


def _tile_pos(g, i, tm):
    t0 = i * tm
    return jnp.where(t0 < g.tokens_p, t0 % g.s_p, (t0 - g.tokens_p) % g.s_s)


def _tile_seq_len(g, i, tm):
    return jnp.where(i * tm < g.tokens_p, g.s_p, g.s_s)


def _cparams(sem):
    return pltpu.CompilerParams(dimension_semantics=sem, vmem_limit_bytes=VMEM_LIMIT)


def _sigmoid(x):
    return 0.5 * jnp.tanh(0.5 * x) + 0.5


def _silu(x):
    return x * _sigmoid(x)


def _rms_mod(x, gvec, scale, shift):
    ms = jnp.mean(x * x, axis=-1, keepdims=True)
    y = x * lax.rsqrt(ms + RMS_EPS) * gvec
    return y * (1.0 + scale) + shift


def _mod_kernel(c_ref, w_ref, b_ref, o_ref):
    c = c_ref[...]
    o_ref[...] = jnp.dot(_silu(c), w_ref[...], preferred_element_type=F32,
                         precision=lax.Precision.HIGHEST) + b_ref[...]


def _adaln_mod(c_pad, w_ada, b_ada):
    depth, d, _ = w_ada.shape
    n = c_pad.shape[0]
    out = pl.pallas_call(
        _mod_kernel,
        grid=(depth, 6),
        in_specs=[
            pl.BlockSpec((n, d), lambda l, j: (0, 0)),
            pl.BlockSpec((None, d, d), lambda l, j: (l, 0, j)),
            pl.BlockSpec((None, None, 1, d), lambda l, j: (l, j, 0, 0)),
        ],
        out_specs=pl.BlockSpec((None, None, n, d), lambda l, j: (l, j, 0, 0)),
        out_shape=jax.ShapeDtypeStruct((depth, 6, n, d), F32),
        compiler_params=_cparams(("arbitrary", "arbitrary")),
        name="adaln_mod",
    )(c_pad, w_ada, b_ada.reshape(depth, 6, 1, d))
    return out.transpose(0, 2, 1, 3)


def _token_parts(geom, x, tm):
    parts = tuple(p for p in x if p.shape[0]) if isinstance(x, tuple) else (x,)
    d = parts[0].shape[1]
    if len(parts) == 1:
        return parts, [pl.BlockSpec((tm, d), lambda i, *_: (i, 0))], None
    first = geom.tokens_p // tm
    specs = [pl.BlockSpec((tm, d), lambda i, *_: (jnp.minimum(i, first - 1), 0)),
             pl.BlockSpec((tm, d), lambda i, *_: (jnp.maximum(i - first, 0), 0))]
    return parts, specs, first


def _for_token_tile(x_refs, first, body):
    if len(x_refs) == 1:
        body(x_refs[0])
        return
    i = pl.program_id(0)
    pl.when(i < first)(lambda: body(x_refs[0]))
    pl.when(i >= first)(lambda: body(x_refs[1]))


def _norm_proj_kernel(*refs, n_x, first, shift_row, scale_row):
    x_refs = refs[:n_x]
    mod_ref, g_ref, w_ref, o_ref, h_ref = refs[n_x:]

    def norm(x_ref):
        h = _rms_mod(x_ref[...], g_ref[...], mod_ref[scale_row:scale_row + 1, :],
                     mod_ref[shift_row:shift_row + 1, :])
        h_ref[...] = h.astype(BF16)

    @pl.when(pl.program_id(1) == 0)
    def _():
        _for_token_tile(x_refs, first, norm)

    o_ref[...] = jnp.dot(h_ref[...], w_ref[...], preferred_element_type=F32).astype(o_ref.dtype)


def _norm_proj(geom, x, mod_l, gvec, w, *, shift_row, scale_row, tm, tn):
    parts, x_specs, first = _token_parts(geom, x, tm)
    t, d = geom.tokens, parts[0].shape[1]
    n = w.shape[1]
    return pl.pallas_call(
        functools.partial(_norm_proj_kernel, n_x=len(parts), first=first, shift_row=shift_row,
                          scale_row=scale_row),
        grid=(t // tm, n // tn),
        in_specs=x_specs + [
            pl.BlockSpec((None, 6, d), lambda i, j: (_tile_seq(geom, i, tm), 0, 0)),
            pl.BlockSpec((1, d), lambda i, j: (0, 0)),
            pl.BlockSpec((d, tn), lambda i, j: (0, j)),
        ],
        out_specs=pl.BlockSpec((tm, tn), lambda i, j: (i, j)),
        out_shape=jax.ShapeDtypeStruct((t, n), BF16),
        scratch_shapes=[pltpu.VMEM((tm, d), BF16)],
        compiler_params=_cparams(("parallel", "arbitrary")),
        name="norm_proj",
    )(*parts, mod_l, gvec, w)


def _rot_halves(ref, h, cos, sin):
    a = ref[:, h * RET_DK:h * RET_DK + RET_HALF].astype(F32)
    b = ref[:, h * RET_DK + RET_HALF:(h + 1) * RET_DK].astype(F32)
    return a * cos - b * sin, a * sin + b * cos


def _ret_bwd_state_kernel(k_ref, v_ref, cos_ref, sin_ref, kdb_ref, rb_ref, r_ref, *, geom, chunk, cdec):
    c = pl.num_programs(0) - 1 - pl.program_id(0)
    is_last = _tile_pos(geom, c, chunk) + chunk == _tile_seq_len(geom, c, chunk)

    @pl.when(is_last)
    def _():
        r_ref[...] = jnp.zeros_like(r_ref)

    cos = cos_ref[...]
    sin = sin_ref[...]
    for h in range(RET_HEADS):
        k1, k2 = _rot_halves(k_ref, h, cos, sin)
        dec = kdb_ref[h]
        kd = jnp.concatenate([k1 * dec, k2 * dec], axis=1).astype(BF16)
        vh = v_ref[:, h * RET_DV:(h + 1) * RET_DV].astype(BF16)
        r = r_ref[h]
        rb_ref[h] = r.astype(BF16)
        upd = lax.dot_general(kd, vh, (((0,), (0,)), ((), ())), preferred_element_type=F32)
        r_ref[h] = r * cdec[h] + upd


def _ret_main_kernel(q_ref, k_ref, v_ref, g_ref, rb_ref, cos_ref, sin_ref, dmat_ref, qdf_ref, qdb_ref,
                     kdf_ref, wo_ref, o_ref, rf_ref, *, geom, chunk, cdec):
    i = pl.program_id(0)

    @pl.when(_tile_pos(geom, i, chunk) == 0)
    def _():
        rf_ref[...] = jnp.zeros_like(rf_ref)

    cos = cos_ref[...]
    sin = sin_ref[...]
    acc = jnp.zeros(o_ref.shape, F32)
    for h in range(RET_HEADS):
        q1, q2 = _rot_halves(q_ref, h, cos, sin)
        k1, k2 = _rot_halves(k_ref, h, cos, sin)
        qr = jnp.concatenate([q1, q2], axis=1).astype(BF16)
        kr = jnp.concatenate([k1, k2], axis=1).astype(BF16)
        vh = v_ref[:, h * RET_DV:(h + 1) * RET_DV].astype(BF16)
        s = lax.dot_general(qr, kr, (((1,), (1,)), ((), ())), preferred_element_type=F32) * dmat_ref[h]
        o = jnp.dot(s.astype(BF16), vh, preferred_element_type=F32)
        qdf = qdf_ref[h]
        qf = jnp.concatenate([q1 * qdf, q2 * qdf], axis=1).astype(BF16)
        rf = rf_ref[h]
        o = o + jnp.dot(qf, rf.astype(BF16), preferred_element_type=F32)
        qdb = qdb_ref[h]
        qb = jnp.concatenate([q1 * qdb, q2 * qdb], axis=1).astype(BF16)
        o = o + jnp.dot(qb, rb_ref[h], preferred_element_type=F32)
        kdf = kdf_ref[h]
        kf = jnp.concatenate([k1 * kdf, k2 * kdf], axis=1).astype(BF16)
        rf_ref[h] = rf * cdec[h] + lax.dot_general(kf, vh, (((0,), (0,)), ((), ())),
                                                   preferred_element_type=F32)
        oc = o - jnp.mean(o, axis=-1, keepdims=True)
        on = oc * lax.rsqrt(jnp.mean(oc * oc, axis=-1, keepdims=True) + GN_EPS)
        og = (on * _silu(g_ref[:, h * RET_DV:(h + 1) * RET_DV].astype(F32))).astype(BF16)
        acc = acc + jnp.dot(og, wo_ref[h * RET_DV:(h + 1) * RET_DV, :], preferred_element_type=F32)
    o_ref[...] = acc.astype(o_ref.dtype)


def _retention_tables(chunk, s_max):
    log_gamma = jnp.log1p(-jnp.exp2(-5.0 - jnp.arange(RET_HEADS, dtype=F32)))
    idx = jnp.arange(chunk, dtype=F32)
    dist = jnp.abs(idx[:, None] - idx[None, :])
    dmat = jnp.exp(log_gamma[:, None, None] * dist[None])

    def rows(e):
        return jnp.broadcast_to(jnp.exp(log_gamma[:, None] * e[None, :])[:, :, None],
                                (RET_HEADS, chunk, RET_HALF))

    qdf = rows(idx + 1.0)
    qdb = rows(chunk - idx)
    kdf = rows(chunk - 1.0 - idx)
    kdb = rows(idx)
    theta = 1.0 / (ROPE_BASE ** jnp.linspace(0.0, 1.0, RET_HALF, dtype=F32))
    ang = jnp.arange(s_max, dtype=F32)[:, None] * theta[None, :]
    k_scale = RET_DK ** -0.5
    return jnp.cos(ang), jnp.sin(ang), dmat * k_scale, qdf, qdb, kdf * k_scale, kdb * k_scale


def _chunk_decay(chunk):
    lg = np.log1p(-np.exp2(-5.0 - np.arange(RET_HEADS, dtype=np.float32))).astype(np.float32)
    return tuple(float(v) for v in np.exp(lg * np.float32(chunk)).astype(np.float32))


def _retention(geom, proj_a, w_ret_o, tables, *, chunk):
    t = proj_a.shape[0]
    n_chunks = t // chunk
    cos, sin, dmat, qdf, qdb, kdf, kdb = tables
    cdec = _chunk_decay(chunk)
    state_shape = (RET_HEADS, RET_DK, RET_DV)

    def pos_blk(c):
        return _tile_pos(geom, c, chunk) // chunk

    rev = lambda i: n_chunks - 1 - i
    tab_spec = pl.BlockSpec((RET_HEADS, chunk, RET_HALF), lambda i: (0, 0, 0))
    rb = pl.pallas_call(
        functools.partial(_ret_bwd_state_kernel, geom=geom, chunk=chunk, cdec=cdec),
        grid=(n_chunks,),
        in_specs=[
            pl.BlockSpec((chunk, RET_QK_W), lambda i: (rev(i), 1)),
            pl.BlockSpec((chunk, RET_V_W), lambda i: (rev(i), 1)),
            pl.BlockSpec((chunk, RET_HALF), lambda i: (pos_blk(rev(i)), 0)),
            pl.BlockSpec((chunk, RET_HALF), lambda i: (pos_blk(rev(i)), 0)),
            tab_spec,
        ],
        out_specs=pl.BlockSpec((None,) + state_shape, lambda i: (rev(i), 0, 0, 0)),
        out_shape=jax.ShapeDtypeStruct((n_chunks,) + state_shape, BF16),
        scratch_shapes=[pltpu.VMEM(state_shape, F32)],
        compiler_params=_cparams(("arbitrary",)),
        name="ret_bwd_state",
    )(proj_a, proj_a, cos, sin, kdb)

    return pl.pallas_call(
        functools.partial(_ret_main_kernel, geom=geom, chunk=chunk, cdec=cdec),
        grid=(n_chunks,),
        in_specs=[
            pl.BlockSpec((chunk, RET_QK_W), lambda i: (i, 0)),
            pl.BlockSpec((chunk, RET_QK_W), lambda i: (i, 1)),
            pl.BlockSpec((chunk, RET_V_W), lambda i: (i, 1)),
            pl.BlockSpec((chunk, RET_V_W), lambda i: (i, 2)),
            pl.BlockSpec((None,) + state_shape, lambda i: (i, 0, 0, 0)),
            pl.BlockSpec((chunk, RET_HALF), lambda i: (pos_blk(i), 0)),
            pl.BlockSpec((chunk, RET_HALF), lambda i: (pos_blk(i), 0)),
            pl.BlockSpec((RET_HEADS, chunk, chunk), lambda i: (0, 0, 0)),
            tab_spec, tab_spec, tab_spec,
            pl.BlockSpec((RET_V_W, D_MODEL), lambda i: (0, 0)),
        ],
        out_specs=pl.BlockSpec((chunk, D_MODEL), lambda i: (i, 0)),
        out_shape=jax.ShapeDtypeStruct((t, D_MODEL), BF16),
        scratch_shapes=[pltpu.VMEM(state_shape, F32)],
        compiler_params=_cparams(("arbitrary",)),
        name="ret_main",
    )(proj_a, proj_a, proj_a, proj_a, rb, cos, sin, dmat, qdf, qdb, kdf, w_ret_o)


def _log_sigmoid(x):
    return jnp.minimum(x, 0.0) - jnp.log1p(jnp.exp(-jnp.abs(x)))


LRU_SEGS = V7X_SUBLANES
LRU_LEAD = CONV_LEFT * LRU_SEGS
LRU_TAIL = (CONV_WIDTH - 1 - CONV_LEFT) * LRU_SEGS


def _lru_load_segment_major(x_ref, prev_ref, next_ref, keep_prev, keep_next, xs_ref):
    tl = x_ref.shape[0]
    seg = tl // LRU_SEGS
    x = x_ref[...].astype(F32)
    prev_hi = prev_ref[...].astype(F32)[LRU_HALO - V7X_SUBLANES:] * keep_prev
    next_lo = next_ref[...].astype(F32)[:V7X_SUBLANES] * keep_next
    row8 = lax.broadcasted_iota(jnp.int32, (V7X_SUBLANES, LRU_BW), 0)
    last = V7X_SUBLANES - 1
    for n in range(LRU_BLOCKS):
        sl = slice(n * LRU_BW, (n + 1) * LRU_BW)
        for s in range(LRU_SEGS):
            xs_ref[n, pl.ds(LRU_LEAD + s, seg, stride=LRU_SEGS), :] = x[s * seg:(s + 1) * seg, sl]
        slab = lambda t: xs_ref[n, LRU_LEAD + t * LRU_SEGS:LRU_LEAD + (t + 1) * LRU_SEGS, :]
        p = prev_hi[:, sl]
        m1 = pltpu.roll(jnp.where(row8 == last, p, slab(seg - 1)), 1, axis=0)
        m2 = pltpu.roll(jnp.where(row8 == last, pltpu.roll(p, 1, axis=0), slab(seg - 2)), 1, axis=0)
        p1 = pltpu.roll(jnp.where(row8 == 0, next_lo[:, sl], slab(0)), last, axis=0)
        xs_ref[n, 0:LRU_SEGS, :] = m2
        xs_ref[n, LRU_SEGS:LRU_LEAD, :] = m1
        xs_ref[n, LRU_LEAD + tl:LRU_LEAD + tl + LRU_TAIL, :] = p1


def _lru_gates(xs_ref, d, wconv_ref, bconv_ref, wrg_ref, brg_ref, wig_ref, big_ref, lam_ref, a_ref, u_ref):
    tl = a_ref.shape[1]
    c = (0.5 * LRU_C * math.log2(math.e)) * _log_sigmoid(lam_ref[d])
    for n in range(LRU_BLOCKS):
        sl = slice(n * LRU_BW, (n + 1) * LRU_BW)
        xh = bconv_ref[:, sl] + xs_ref[n, 0:tl, :] * wconv_ref[0:1, sl]
        for j in range(1, CONV_WIDTH):
            xh = xh + xs_ref[n, j * LRU_SEGS:j * LRU_SEGS + tl, :] * wconv_ref[j:j + 1, sl]
        xb = xh.astype(BF16)
        th_r = jnp.tanh(jnp.dot(xb, wrg_ref[d, n], preferred_element_type=F32) + brg_ref[d][:, sl])
        th_i = jnp.tanh(jnp.dot(xb, wig_ref[d, n], preferred_element_type=F32) + big_ref[d][:, sl])
        a = jnp.exp2(c[:, sl] * th_r + c[:, sl])
        a_ref[n] = a
        y = 1.0 - a * a
        u_ref[n] = (y * lax.rsqrt(jnp.maximum(y, SQRT_TINY))) * ((th_i + 1.0) * xh)


def _lru_scan_segments(a_ref, u_ref, hs_ref, as_ref, fin_ref, carry_ref, o_ref, reverse):
    tl = a_ref.shape[1]
    seg = tl // LRU_SEGS
    slab_shape = (LRU_SEGS, LRU_BW)

    def step(k, carry):
        t = seg - 1 - k if reverse else k
        rows = pl.ds(pl.multiple_of(t * LRU_SEGS, LRU_SEGS), LRU_SEGS)
        hs, decays = [], []
        for n in range(LRU_BLOCKS):
            a = a_ref[n, rows, :]
            h = a * carry[0][n] + u_ref[n, rows, :]
            decay = a * carry[1][n]
            hs_ref[n, rows, :] = h
            as_ref[n, rows, :] = decay
            hs.append(h)
            decays.append(decay)
        return tuple(hs), tuple(decays)

    init = (tuple(jnp.zeros(slab_shape, F32) for _ in range(LRU_BLOCKS)),
            tuple(jnp.ones(slab_shape, F32) for _ in range(LRU_BLOCKS)))
    h_end, decay_end = lax.fori_loop(0, seg, step, init, unroll=4)

    order = range(LRU_SEGS - 1, -1, -1) if reverse else range(LRU_SEGS)
    for n in range(LRU_BLOCKS):
        sl = slice(n * LRU_BW, (n + 1) * LRU_BW)
        fin_ref[0] = h_end[n]
        fin_ref[1] = decay_end[n]
        c = carry_ref[n, 0:1, :]
        for s in order:
            fin_ref[2, s:s + 1, :] = c
            c = fin_ref[0, s:s + 1, :] + fin_ref[1, s:s + 1, :] * c
        carry_ref[n, 0:1, :] = c
        for s in range(LRU_SEGS):
            seg_rows = pl.ds(s, seg, stride=LRU_SEGS)
            h = hs_ref[n, seg_rows, :] + as_ref[n, seg_rows, :] * fin_ref[2, s:s + 1, :]
            o_ref[s * seg:(s + 1) * seg, sl] = h.astype(o_ref.dtype)


def _lru_kernel(xf_ref, xfp_ref, xfn_ref, xb_ref, xbp_ref, xbn_ref, wconv_ref, bconv_ref, wrg_ref, brg_ref,
                wig_ref, big_ref, lam_ref, hf_ref, hb_ref, xs_ref, a_ref, u_ref, hs_ref, as_ref, fin_ref,
                cf_ref, cb_ref, *, geom, tl):
    i = pl.program_id(0)
    ib = pl.num_programs(0) - 1 - i

    def flags(tile):
        pos = _tile_pos(geom, tile, tl)
        return pos == 0, pos + tl == _tile_seq_len(geom, tile, tl)

    def keep(flag):
        return jnp.where(flag, 0.0, 1.0)

    def direction(d, x_ref, prev_ref, next_ref, first, last, carry_ref, o_ref):
        _lru_load_segment_major(x_ref, prev_ref, next_ref, keep(first), keep(last), xs_ref)
        _lru_gates(xs_ref, d, wconv_ref, bconv_ref, wrg_ref, brg_ref, wig_ref, big_ref, lam_ref, a_ref, u_ref)
        _lru_scan_segments(a_ref, u_ref, hs_ref, as_ref, fin_ref, carry_ref, o_ref, reverse=(d == 1))

    first, last = flags(i)

    @pl.when(first)
    def _():
        cf_ref[...] = jnp.zeros_like(cf_ref)

    direction(0, xf_ref, xfp_ref, xfn_ref, first, last, cf_ref, hf_ref)

    first, last = flags(ib)

    @pl.when(last)
    def _():
        cb_ref[...] = jnp.zeros_like(cb_ref)

    direction(1, xb_ref, xbp_ref, xbn_ref, first, last, cb_ref, hb_ref)


def _lru(geom, proj_b, w_conv, b_conv, w_rg, b_rg, w_ig, b_ig, lam, *, tl):
    t = proj_b.shape[0]
    n = t // tl
    r8 = tl // LRU_HALO
    n8 = t // LRU_HALO
    w = LRU_WIDTH
    rev = lambda i: n - 1 - i
    prev_blk = lambda i: jnp.maximum(i * r8 - 1, 0)
    next_blk = lambda i: jnp.minimum((i + 1) * r8, n8 - 1)
    full = lambda a: pl.BlockSpec(a.shape, lambda i: (0,) * a.ndim)
    w_conv = 0.5 * w_conv
    b_conv2 = 0.5 * b_conv.reshape(1, w)
    b_rg3 = 0.5 * b_rg.reshape(2, 1, w)
    b_ig3 = 0.5 * b_ig.reshape(2, 1, w)
    lam3 = lam.reshape(2, 1, w)
    return pl.pallas_call(
        functools.partial(_lru_kernel, geom=geom, tl=tl),
        grid=(n,),
        in_specs=[
            pl.BlockSpec((tl, w), lambda i: (i, 0)),
            pl.BlockSpec((LRU_HALO, w), lambda i: (prev_blk(i), 0)),
            pl.BlockSpec((LRU_HALO, w), lambda i: (next_blk(i), 0)),
            pl.BlockSpec((tl, w), lambda i: (rev(i), 0)),
            pl.BlockSpec((LRU_HALO, w), lambda i: (prev_blk(rev(i)), 0)),
            pl.BlockSpec((LRU_HALO, w), lambda i: (next_blk(rev(i)), 0)),
            full(w_conv), full(b_conv2), full(w_rg), full(b_rg3), full(w_ig), full(b_ig3), full(lam3),
        ],
        out_specs=[
            pl.BlockSpec((tl, w), lambda i: (i, 0)),
            pl.BlockSpec((tl, w), lambda i: (rev(i), 0)),
        ],
        out_shape=[jax.ShapeDtypeStruct((t, w), BF16), jax.ShapeDtypeStruct((t, w), BF16)],
        scratch_shapes=[pltpu.VMEM((LRU_BLOCKS, LRU_LEAD + tl + LRU_TAIL, LRU_BW), F32)]
        + [pltpu.VMEM((LRU_BLOCKS, tl, LRU_BW), F32)] * 4
        + [pltpu.VMEM((3, LRU_SEGS, LRU_BW), F32)]
        + [pltpu.VMEM((LRU_BLOCKS, V7X_SUBLANES, LRU_BW), F32)] * 2,
        compiler_params=_cparams(("arbitrary",)),
        name="lru_scan",
    )(proj_b, proj_b, proj_b, proj_b, proj_b, proj_b, w_conv, b_conv2, w_rg, b_rg3, w_ig, b_ig3, lam3)


def _gelu_tanh(x):
    return 0.5 * x * (1.0 + jnp.tanh(math.sqrt(2.0 / math.pi) * (x + 0.044715 * (x * x * x))))


def _merge_kernel(*refs, n_x, first):
    x_refs = refs[:n_x]
    ya_ref, hf_ref, hb_ref, gl_ref, ga_ref, gb_ref, mod_ref, wl_ref, wo_ref, o_ref = refs[n_x:]
    f32 = lambda ref: ref[...].astype(F32)
    y = ((f32(hf_ref) + f32(hb_ref)) * _gelu_tanh(f32(gl_ref))).astype(BF16)
    yb = jnp.dot(y, wl_ref[...], preferred_element_type=F32)
    m = _sigmoid(f32(ga_ref)) * f32(ya_ref) + _sigmoid(f32(gb_ref)) * yb
    mix = mod_ref[2:3, :] * jnp.dot(m.astype(BF16), wo_ref[...], preferred_element_type=F32)

    def residual(x_ref):
        o_ref[...] = x_ref[...] + mix

    _for_token_tile(x_refs, first, residual)


def _merge(geom, x, ya, hf, hb, proj_a, proj_b, mod_l, w_lru_o, w_out, *, tm):
    parts, x_specs, first = _token_parts(geom, x, tm)
    t, d = geom.tokens, parts[0].shape[1]
    gate_a_blk = (2 * RET_QK_W + 2 * RET_V_W) // d
    tok = lambda w: pl.BlockSpec((tm, w), lambda i: (i, 0))
    return pl.pallas_call(
        functools.partial(_merge_kernel, n_x=len(parts), first=first),
        grid=(t // tm,),
        in_specs=x_specs + [
            tok(d), tok(LRU_WIDTH), tok(LRU_WIDTH),
            pl.BlockSpec((tm, LRU_WIDTH), lambda i: (i, 1)),
            pl.BlockSpec((tm, d), lambda i: (i, gate_a_blk)),
            pl.BlockSpec((tm, d), lambda i: (i, gate_a_blk + 1)),
            pl.BlockSpec((None, 6, d), lambda i: (_tile_seq(geom, i, tm), 0, 0)),
            pl.BlockSpec((LRU_WIDTH, d), lambda i: (0, 0)),
            pl.BlockSpec((d, d), lambda i: (0, 0)),
        ],
        out_specs=tok(d),
        out_shape=jax.ShapeDtypeStruct((t, d), F32),
        compiler_params=_cparams(("parallel",)),
        name="merge",
    )(*parts, ya, hf, hb, proj_b, proj_a, proj_a, mod_l, w_lru_o, w_out)


def _ffn_kernel(x_ref, mod_ref, g_ref, wg_ref, wu_ref, wd_ref, o_ref, *, col_chunks):
    x = x_ref[...]
    h = _rms_mod(x, g_ref[...], mod_ref[4:5, :], mod_ref[3:4, :]).astype(BF16)
    acc = None
    for c0, c1 in col_chunks:
        a = jnp.dot(h, wg_ref[:, c0:c1], preferred_element_type=F32)
        u = jnp.dot(h, wu_ref[:, c0:c1], preferred_element_type=F32)
        part = jnp.dot((_silu(a) * u).astype(BF16), wd_ref[c0:c1, :], preferred_element_type=F32)
        acc = part if acc is None else acc + part
    o_ref[...] = x + mod_ref[5:6, :] * acc


def _ffn(geom, x, mod_l, gvec, wg, wu, wd, *, tm, tf):
    t, d = x.shape
    ff = wg.shape[1]
    col_chunks = tuple((c, min(c + tf, ff)) for c in range(0, ff, tf))
    resident = lambda shape: pl.BlockSpec(shape, lambda i: (0, 0), pipeline_mode=pl.Buffered(1))
    return pl.pallas_call(
        functools.partial(_ffn_kernel, col_chunks=col_chunks),
        grid=(t // tm,),
        in_specs=[
            pl.BlockSpec((tm, d), lambda i: (i, 0)),
            pl.BlockSpec((None, 6, d), lambda i: (_tile_seq(geom, i, tm), 0, 0)),
            pl.BlockSpec((1, d), lambda i: (0, 0)),
            resident((d, ff)), resident((d, ff)), resident((ff, d)),
        ],
        out_specs=pl.BlockSpec((tm, d), lambda i: (i, 0)),
        out_shape=jax.ShapeDtypeStruct((t, d), F32),
        compiler_params=_cparams(("parallel",)),
        name="ffn_dense",
    )(x, mod_l, gvec, wg, wu, wd)


ROUTE_E0, ROUTE_E1, ROUTE_R0, ROUTE_R1, ROUTE_G0, ROUTE_G1 = range(6)

ROW_TILES = D_MODEL // V7X_LANES


def _store_token_rows(ref, val):
    rows = val.shape[0]
    for j in range(ROW_TILES):
        ref[pl.ds(j, rows, stride=ROW_TILES), :] = val[:, j * V7X_LANES:(j + 1) * V7X_LANES]


def _load_token_rows(ref, rows):
    return jnp.concatenate([ref[pl.ds(j, rows, stride=ROW_TILES), :] for j in range(ROW_TILES)], axis=1)


def _router_kernel(x_ref, mod_ref, g_ref, wr_ref, h_ref, route_ref, cnt_ref, carry_ref):
    i = pl.program_id(0)
    tm = x_ref.shape[0]

    @pl.when(i == 0)
    def _():
        carry_ref[...] = jnp.zeros_like(carry_ref)

    h = _rms_mod(x_ref[...], g_ref[...], mod_ref[4:5, :], mod_ref[3:4, :])
    _store_token_rows(h_ref, h)
    h_hi = h.astype(BF16)
    h_lo = (h - h_hi.astype(F32)).astype(BF16)
    logits = (jnp.dot(h_hi, wr_ref[0], preferred_element_type=F32)
              + jnp.dot(h_lo, wr_ref[0], preferred_element_type=F32)
              + jnp.dot(h_hi, wr_ref[1], preferred_element_type=F32))
    lane = lax.broadcasted_iota(jnp.int32, logits.shape, 1).astype(F32)
    logits = jnp.where(lane < N_EXPERTS, logits, -jnp.inf)
    m1 = jnp.max(logits, axis=-1, keepdims=True)
    i1 = jnp.min(jnp.where(logits == m1, lane, float(V7X_LANES)), axis=-1, keepdims=True)
    rest = jnp.where(lane == i1, -jnp.inf, logits)
    m2 = jnp.max(rest, axis=-1, keepdims=True)
    i2 = jnp.min(jnp.where(rest == m2, lane, float(V7X_LANES)), axis=-1, keepdims=True)
    ex = jnp.exp(m2 - m1)
    g1 = 1.0 / (1.0 + ex)
    g2 = ex / (1.0 + ex)
    sel1 = lane == i1
    sel2 = lane == i2
    onehot = jnp.where(sel1 | sel2, 1.0, 0.0)
    row = lax.broadcasted_iota(jnp.int32, (tm, tm), 0)
    col = lax.broadcasted_iota(jnp.int32, (tm, tm), 1)
    lower = jnp.where(col < row, 1.0, 0.0).astype(BF16)
    before = jnp.dot(lower, onehot.astype(BF16), preferred_element_type=F32) + carry_ref[0:1, :]
    r1 = jnp.sum(jnp.where(sel1, before, 0.0), axis=-1, keepdims=True)
    r2 = jnp.sum(jnp.where(sel2, before, 0.0), axis=-1, keepdims=True)
    out_lane = lax.broadcasted_iota(jnp.int32, route_ref.shape, 1)
    vals = (i1.astype(F32), i2.astype(F32), r1, r2, g1, g2)
    packed = jnp.zeros(route_ref.shape, F32)
    for slot, v in enumerate(vals):
        packed = jnp.where(out_lane == slot, v, packed)
    route_ref[...] = packed
    carry = carry_ref[0:1, :] + jnp.sum(onehot, axis=0, keepdims=True)
    carry_ref[0:1, :] = carry
    cnt_ref[...] = jnp.broadcast_to(carry, cnt_ref.shape).astype(jnp.int32)


def _router(geom, x, mod_l, gvec, w_router, *, tm):
    t, d = x.shape
    w_pad = jnp.pad(w_router, ((0, 0), (0, V7X_LANES - N_EXPERTS)))
    w_hi = w_pad.astype(BF16)
    w_pad = jnp.stack([w_hi, (w_pad - w_hi.astype(F32)).astype(BF16)])
    return pl.pallas_call(
        _router_kernel,
        grid=(t // tm,),
        in_specs=[
            pl.BlockSpec((tm, d), lambda i: (i, 0)),
            pl.BlockSpec((None, 6, d), lambda i: (_tile_seq(geom, i, tm), 0, 0)),
            pl.BlockSpec((1, d), lambda i: (0, 0)),
            pl.BlockSpec((2, d, V7X_LANES), lambda i: (0, 0, 0)),
        ],
        out_specs=[
            pl.BlockSpec((tm * ROW_TILES, V7X_LANES), lambda i: (i, 0)),
            pl.BlockSpec((tm, V7X_LANES), lambda i: (i, 0)),
            pl.BlockSpec((V7X_SUBLANES, V7X_LANES), lambda i: (0, 0)),
        ],
        out_shape=[
            jax.ShapeDtypeStruct((t * ROW_TILES, V7X_LANES), F32),
            jax.ShapeDtypeStruct((t, V7X_LANES), F32),
            jax.ShapeDtypeStruct((V7X_SUBLANES, V7X_LANES), jnp.int32),
        ],
        scratch_shapes=[pltpu.VMEM((V7X_SUBLANES, V7X_LANES), F32)],
        compiler_params=_cparams(("arbitrary",)),
        name="router",
    )(x, mod_l, gvec, w_pad)


def _token_copy(src, src_tok, dst, dst_tok, sem):
    s = pl.multiple_of(src_tok * ROW_TILES, ROW_TILES)
    d = pl.multiple_of(dst_tok * ROW_TILES, ROW_TILES)
    return pltpu.make_async_copy(src.at[pl.ds(s, ROW_TILES), :], dst.at[pl.ds(d, ROW_TILES), :], sem)


def _dispatch_kernel(fill_ref, dest_ref, src_ref, out_hbm, zero_ref, sem, zero_sem, *, td):
    @pl.when(pl.program_id(0) == 0)
    def _():
        zero_ref[...] = jnp.zeros_like(zero_ref)
        for wait in (False, True):
            for e in range(N_EXPERTS):
                def pad_row(r, carry):
                    copy = _token_copy(zero_ref, 0, out_hbm, r, zero_sem)
                    copy.wait() if wait else copy.start()
                    return carry

                lax.fori_loop(fill_ref[0, e], fill_ref[1, e], pad_row, 0)

    def issue(t, carry):
        for k in range(TOP_K):
            _token_copy(src_ref, t, out_hbm, dest_ref[k, t], sem).start(priority=k % DMA_PRIORITIES)
        return carry

    lax.fori_loop(0, td, issue, 0, unroll=8)
    for _ in range(TOP_K):
        pltpu.make_async_copy(src_ref, out_hbm.at[pl.ds(0, td * ROW_TILES), :], sem).wait()


def _dispatch(h2_rows, dest, fill, n_rows, *, td):
    grid_spec = pltpu.PrefetchScalarGridSpec(
        num_scalar_prefetch=1,
        grid=(dest.shape[1] // td,),
        in_specs=[
            pl.BlockSpec((TOP_K, td), lambda i, fill: (0, i), memory_space=pltpu.SMEM),
            pl.BlockSpec((td * ROW_TILES, V7X_LANES), lambda i, fill: (i, 0)),
        ],
        out_specs=pl.BlockSpec(memory_space=pl.ANY),
        scratch_shapes=[pltpu.VMEM((ROW_TILES, V7X_LANES), h2_rows.dtype),
                        pltpu.SemaphoreType.DMA(()), pltpu.SemaphoreType.DMA(())],
    )
    return pl.pallas_call(
        functools.partial(_dispatch_kernel, td=td),
        grid_spec=grid_spec,
        out_shape=jax.ShapeDtypeStruct((n_rows * ROW_TILES, V7X_LANES), h2_rows.dtype),
        compiler_params=_cparams(("arbitrary",)),
        name="moe_dispatch",
    )(fill, dest, h2_rows)


def _experts_kernel(blk_e_ref, nused_ref, x_ref, wg_ref, wu_ref, wd_ref, o_ref, xb_ref, acc_ref):
    del blk_e_ref
    b = pl.program_id(0)
    j = pl.program_id(1)
    bm = xb_ref.shape[0]

    @pl.when(b < nused_ref[0])
    def _():
        @pl.when(j == 0)
        def _():
            xb_ref[...] = _load_token_rows(x_ref, bm).astype(BF16)
            acc_ref[...] = jnp.zeros_like(acc_ref)

        xb = xb_ref[...]
        a = jnp.dot(xb, wg_ref[...], preferred_element_type=F32)
        u = jnp.dot(xb, wu_ref[...], preferred_element_type=F32)
        acc_ref[...] += jnp.dot((_silu(a) * u).astype(BF16), wd_ref[...], preferred_element_type=F32)

        @pl.when(j == pl.num_programs(1) - 1)
        def _():
            _store_token_rows(o_ref, acc_ref[...])

    @pl.when((b >= nused_ref[0]) & (j == pl.num_programs(1) - 1))
    def _():
        o_ref[...] = jnp.zeros_like(o_ref)


def _experts(xbuf, blk_e, n_used, wg, wu, wd, *, bm, tf):
    p = xbuf.shape[0] // ROW_TILES
    d = D_MODEL
    ff = wg.shape[2]
    nj = ff // tf
    rows_blk = (bm * ROW_TILES, V7X_LANES)

    def blk(b, nu):
        return jnp.minimum(b, nu[0] - 1)

    def ffc(b, j, nu):
        return jnp.where(b < nu[0], j, nj - 1)

    grid_spec = pltpu.PrefetchScalarGridSpec(
        num_scalar_prefetch=2,
        grid=(p // bm, nj),
        in_specs=[
            pl.BlockSpec(rows_blk, lambda b, j, be, nu: (blk(b, nu), 0)),
            pl.BlockSpec((None, d, tf), lambda b, j, be, nu: (be[blk(b, nu)], 0, ffc(b, j, nu))),
            pl.BlockSpec((None, d, tf), lambda b, j, be, nu: (be[blk(b, nu)], 0, ffc(b, j, nu))),
            pl.BlockSpec((None, tf, d), lambda b, j, be, nu: (be[blk(b, nu)], ffc(b, j, nu), 0)),
        ],
        out_specs=pl.BlockSpec(rows_blk, lambda b, j, be, nu: (b, 0)),
        scratch_shapes=[pltpu.VMEM((bm, d), BF16), pltpu.VMEM((bm, d), F32)],
    )
    return pl.pallas_call(
        _experts_kernel,
        grid_spec=grid_spec,
        out_shape=jax.ShapeDtypeStruct(xbuf.shape, F32),
        compiler_params=_cparams(("arbitrary", "arbitrary")),
        name="moe_experts",
    )(blk_e, n_used, xbuf, wg, wu, wd)


def _moe_out_kernel(dest_ref, dest_next_ref, x_ref, route_ref, mod_ref, gf_ref, ybuf_hbm, op_ref, os_ref,
                    ya0_ref, ya1_ref, yb0_ref, yb1_ref, sems, *, final_norm, prompt_tiles):
    i = pl.program_id(0)
    tm = x_ref.shape[0]
    bufs = ((ya0_ref, ya1_ref), (yb0_ref, yb1_ref))

    def gather(d_ref, slot):
        def issue(t, carry):
            for k in range(TOP_K):
                _token_copy(ybuf_hbm, d_ref[k, t], bufs[slot][k], t, sems.at[slot]).start(
                    priority=k % DMA_PRIORITIES)
            return carry

        lax.fori_loop(0, tm, issue, 0, unroll=8)

    def wait(slot):
        for k in range(TOP_K):
            pltpu.make_async_copy(ybuf_hbm.at[pl.ds(0, tm * ROW_TILES), :], bufs[slot][k], sems.at[slot]).wait()

    def combine(slot):
        g1 = route_ref[:, ROUTE_G0:ROUTE_G0 + 1]
        g2 = route_ref[:, ROUTE_G1:ROUTE_G1 + 1]
        y = _load_token_rows(bufs[slot][0], tm) * g1 + _load_token_rows(bufs[slot][1], tm) * g2
        x = x_ref[...] + mod_ref[5:6, :] * y
        if final_norm:
            ms = jnp.mean(x * x, axis=-1, keepdims=True)
            x = x * lax.rsqrt(ms + RMS_EPS) * gf_ref[...]

        @pl.when(i < prompt_tiles)
        def _():
            op_ref[...] = x

        @pl.when(i >= prompt_tiles)
        def _():
            os_ref[...] = x

    @pl.when(i == 0)
    def _():
        gather(dest_ref, 0)

    for slot in range(2):
        @pl.when(i % 2 == slot)
        def _():
            @pl.when(i + 1 < pl.num_programs(0))
            def _():
                gather(dest_next_ref, 1 - slot)

            wait(slot)
            combine(slot)


def _moe_out(geom, x, ybuf, dest, route, mod_l, g_final, *, tm, final_norm):
    t, d = x.shape
    n = t // tm
    prompt_tiles = geom.tokens_p // tm
    y_buf = pltpu.VMEM((tm * ROW_TILES, V7X_LANES), F32)
    return pl.pallas_call(
        functools.partial(_moe_out_kernel, final_norm=final_norm, prompt_tiles=prompt_tiles),
        grid=(n,),
        in_specs=[
            pl.BlockSpec((TOP_K, tm), lambda i: (0, i), memory_space=pltpu.SMEM),
            pl.BlockSpec((TOP_K, tm), lambda i: (0, jnp.minimum(i + 1, n - 1)), memory_space=pltpu.SMEM),
            pl.BlockSpec((tm, d), lambda i: (i, 0)),
            pl.BlockSpec((tm, V7X_LANES), lambda i: (i, 0)),
            pl.BlockSpec((None, 6, d), lambda i: (_tile_seq(geom, i, tm), 0, 0)),
            pl.BlockSpec((1, d), lambda i: (0, 0)),
            pl.BlockSpec(memory_space=pl.ANY),
        ],
        out_specs=[
            pl.BlockSpec((tm, d), lambda i: (jnp.minimum(i, prompt_tiles - 1), 0)),
            pl.BlockSpec((tm, d), lambda i: (jnp.maximum(i - prompt_tiles, 0), 0)),
        ],
        out_shape=[jax.ShapeDtypeStruct((geom.tokens_p, d), F32),
                   jax.ShapeDtypeStruct((t - geom.tokens_p, d), F32)],
        scratch_shapes=[y_buf, y_buf, y_buf, y_buf, pltpu.SemaphoreType.DMA((2,))],
        compiler_params=_cparams(("arbitrary",)),
        name="moe_out",
    )(dest, dest, x, route, mod_l, g_final, ybuf)


def _final_norm_kernel(x_ref, g_ref, o_ref):
    x = x_ref[...]
    o_ref[...] = x * lax.rsqrt(jnp.mean(x * x, axis=-1, keepdims=True) + RMS_EPS) * g_ref[...]


def _final_norm(x, g_final, *, tm):
    t, d = x.shape
    return pl.pallas_call(
        _final_norm_kernel,
        grid=(t // tm,),
        in_specs=[pl.BlockSpec((tm, d), lambda i: (i, 0)), pl.BlockSpec((1, d), lambda i: (0, 0))],
        out_specs=pl.BlockSpec((tm, d), lambda i: (i, 0)),
        out_shape=jax.ShapeDtypeStruct((t, d), F32),
        compiler_params=_cparams(("parallel",)),
        name="final_norm",
    )(x, g_final)


def _moe(geom, x, mod_l, gvec, w_router, wg, wu, wd, g_final, *, tiles, final_norm):
    t, d = x.shape
    bm = tiles["moe_rows"]
    h2, route, counts = _router(geom, x, mod_l, gvec, w_router, tm=tiles["router"])
    counts = counts[0, :N_EXPERTS]
    padded = ((counts + bm - 1) // bm) * bm
    pad_end = jnp.cumsum(padded)
    pad_start = pad_end - padded
    experts = route[:, ROUTE_E0:ROUTE_E1 + 1].astype(jnp.int32)
    ranks = route[:, ROUTE_R0:ROUTE_R1 + 1].astype(jnp.int32)
    dest = (pad_start[experts] + ranks).T
    n_rows = t * TOP_K + N_EXPERTS * bm
    n_blk = n_rows // bm
    blk_start = jnp.arange(n_blk, dtype=jnp.int32) * bm
    blk_e = jnp.minimum(jnp.sum((pad_end[None, :] <= blk_start[:, None]).astype(jnp.int32), axis=1),
                        N_EXPERTS - 1).astype(jnp.int32)
    n_used = (pad_end[-1:] // bm).astype(jnp.int32)
    fill = jnp.stack([pad_start + counts, pad_end.at[N_EXPERTS - 1].set(n_rows)]).astype(jnp.int32)
    xbuf = _dispatch(h2, dest, fill, n_rows, td=tiles["dma_rows"])
    ybuf = _experts(xbuf, blk_e, n_used, wg, wu, wd, bm=bm, tf=tiles["expert_ff"])
    return _moe_out(geom, x, ybuf, dest, route, mod_l, g_final, tm=tiles["token"], final_norm=final_norm)


def _pick_tiles(geom):
    s = math.gcd(geom.s_p, geom.s_s) if geom.n_p and geom.n_s else (geom.s_p if geom.n_p else geom.s_s)
    return {
        "proj": min(1024, s),
        "proj_cols": 2048,
        "ret_chunk": min(512, s),
        "lru": min(512, s),
        "token": min(512, s),
        "ffn_cols": 1536,
        "router": min(512, s),
        "moe_rows": 512,
        "expert_ff": 1792,
        "dma_rows": min(2048, s),
    }


def _trunk(geom, x, c_all, w_ada, b_ada, g_norm1, g_norm2, w_in, w_conv, b_conv, w_rg, b_rg, w_ig, b_ig,
           lru_lambda, w_ret_o, w_lru_o, w_out, w_ff_gate, w_ff_up, w_ff_down,
           w_router, w_e_gate, w_e_up, w_e_down, g_final, tiles):
    depth = w_in.shape[0]
    d = D_MODEL
    n_pad = -(-geom.n_seq // V7X_SUBLANES) * V7X_SUBLANES
    c_pad = jnp.pad(c_all, ((0, n_pad - geom.n_seq), (0, 0)))
    mod = _adaln_mod(c_pad, w_ada, b_ada)
    tables = _retention_tables(tiles["ret_chunk"], max(geom.s_p if geom.n_p else 0, geom.s_s if geom.n_s else 0))

    o_xl = 2 * RET_QK_W + 2 * RET_V_W
    o_ga = o_xl + 2 * LRU_WIDTH
    for l in range(depth):
        wl = w_in[l]
        w_a = jnp.concatenate([wl[:, :o_xl], wl[:, o_ga:]], axis=1).astype(BF16)
        w_b = wl[:, o_xl:o_ga].astype(BF16)
        g1 = g_norm1[l].reshape(1, d)
        g2 = g_norm2[l].reshape(1, d)
        proj_a = _norm_proj(geom, x, mod[l], g1, w_a, shift_row=0, scale_row=1,
                            tm=tiles["proj"], tn=tiles["proj_cols"])
        proj_b = _norm_proj(geom, x, mod[l], g1, w_b, shift_row=0, scale_row=1,
                            tm=tiles["proj"], tn=LRU_WIDTH)
        ya = _retention(geom, proj_a, w_ret_o[l].astype(BF16), tables, chunk=tiles["ret_chunk"])
        hf, hb = _lru(geom, proj_b, w_conv[l], b_conv[l], w_rg[l].astype(BF16), b_rg[l],
                      w_ig[l].astype(BF16), b_ig[l], lru_lambda[l], tl=tiles["lru"])
        x = _merge(geom, x, ya, hf, hb, proj_a, proj_b, mod[l], w_lru_o[l].astype(BF16),
                   w_out[l].astype(BF16), tm=tiles["token"])
        j = l // 2
        last = l == depth - 1
        if l % 2 == 0:
            x = _ffn(geom, x, mod[l], g2, w_ff_gate[j].astype(BF16), w_ff_up[j].astype(BF16),
                     w_ff_down[j].astype(BF16), tm=tiles["token"], tf=tiles["ffn_cols"])
            if last:
                x = _final_norm(x, g_final.reshape(1, d), tm=tiles["token"])
                return x[:geom.tokens_p], x[geom.tokens_p:]
        else:
            parts = _moe(geom, x, mod[l], g2, w_router[j], w_e_gate[j].astype(BF16), w_e_up[j].astype(BF16),
                         w_e_down[j].astype(BF16), g_final.reshape(1, d), tiles=tiles, final_norm=last)
            if last:
                return parts
            x = jnp.concatenate(parts, axis=0)


def kernel(x_prompt, x_sample, c_prompt, c_sample, w_ada, b_ada, g_norm1, g_norm2, w_in, w_conv, b_conv, w_rg, b_rg, w_ig, b_ig, lru_lambda, w_ret_o, w_lru_o, w_out, w_ff_gate, w_ff_up, w_ff_down, w_router, w_e_gate, w_e_up, w_e_down, g_final):
    n_p, s_p, d = x_prompt.shape
    n_s, s_s, _ = x_sample.shape
    geom = Geom(n_p, s_p, n_s, s_s)
    x = (x_prompt.reshape(-1, d), x_sample.reshape(-1, d))
    c_all = jnp.concatenate([c_prompt, c_sample], axis=0)
    yp, ys = _trunk(geom, x, c_all, w_ada, b_ada, g_norm1, g_norm2, w_in, w_conv, b_conv, w_rg, b_rg, w_ig, b_ig,
                    lru_lambda, w_ret_o, w_lru_o, w_out, w_ff_gate, w_ff_up, w_ff_down,
                    w_router, w_e_gate, w_e_up, w_e_down, g_final, _pick_tiles(geom))
    return (yp.reshape(n_p, s_p, d), ys.reshape(n_s, s_s, d))
```

```python
import functools
import math
from typing import NamedTuple

import jax
import jax.numpy as jnp
import numpy as np
from jax import lax
from jax.experimental import pallas as pl
from jax.experimental.pallas import tpu as pltpu

F32 = jnp.float32
BF16 = jnp.bfloat16

D_MODEL = 1024
RET_HEADS = 4
RET_DK = 256
RET_DV = 512
RET_HALF = RET_DK // 2
RET_QK_W = RET_HEADS * RET_DK
RET_V_W = RET_HEADS * RET_DV
ROPE_BASE = 10000.0
LRU_WIDTH = 1280
LRU_BLOCKS = 10
LRU_BW = LRU_WIDTH // LRU_BLOCKS
LRU_C = 8.0
CONV_WIDTH = 4
CONV_LEFT = 2
LRU_HALO = 16
N_EXPERTS = 8
TOP_K = 2
RMS_EPS = 1e-6
GN_EPS = 1e-5
SQRT_TINY = 1e-30

V7X_LANES = 128
V7X_SUBLANES = 8
V7X_VMEM_BYTES = 64 * 1024 * 1024
DMA_PRIORITIES = 2
VMEM_LIMIT = (V7X_VMEM_BYTES * 3) // 4

PA_W = 2 * RET_QK_W + 2 * RET_V_W + 2 * D_MODEL
PB_W = 2 * LRU_WIDTH


class Geom(NamedTuple):
    n_p: int
    s_p: int
    n_s: int
    s_s: int

    @property
    def tokens_p(self):
        return self.n_p * self.s_p

    @property
    def tokens(self):
        return self.n_p * self.s_p + self.n_s * self.s_s

    @property
    def n_seq(self):
        return self.n_p + self.n_s


def _tile_seq(g, i, tm):
    t0 = i * tm
    return jnp.where(t0 < g.tokens_p, t0 // g.s_p, g.n_p + (t0 - g.tokens_p) // g.s_s)


def _tile_pos(g, i, tm):
    t0 = i * tm
    return jnp.where(t0 < g.tokens_p, t0 % g.s_p, (t0 - g.tokens_p) % g.s_s)


def _tile_seq_len(g, i, tm):
    return jnp.where(i * tm < g.tokens_p, g.s_p, g.s_s)


def _cparams(sem):
    return pltpu.CompilerParams(dimension_semantics=sem, vmem_limit_bytes=VMEM_LIMIT)


def _sigmoid(x):
    return 0.5 * jnp.tanh(0.5 * x) + 0.5


def _silu(x):
    return x * _sigmoid(x)


def _rms_mod(x, gvec, scale, shift):
    ms = jnp.mean(x * x, axis=-1, keepdims=True)
    y = x * lax.rsqrt(ms + RMS_EPS) * gvec
    return y * (1.0 + scale) + shift


def _mod_kernel(c_ref, w_ref, b_ref, o_ref):
    c = c_ref[...]
    o_ref[...] = jnp.dot(_silu(c), w_ref[...], preferred_element_type=F32,
                         precision=lax.Precision.HIGHEST) + b_ref[...]


def _adaln_mod(c_pad, w_ada, b_ada):
    depth, d, _ = w_ada.shape
    n = c_pad.shape[0]
    out = pl.pallas_call(
        _mod_kernel,
        grid=(depth, 6),
        in_specs=[
            pl.BlockSpec((n, d), lambda l, j: (0, 0)),
            pl.BlockSpec((None, d, d), lambda l, j: (l, 0, j)),
            pl.BlockSpec((None, None, 1, d), lambda l, j: (l, j, 0, 0)),
        ],
        out_specs=pl.BlockSpec((None, None, n, d), lambda l, j: (l, j, 0, 0)),
        out_shape=jax.ShapeDtypeStruct((depth, 6, n, d), F32),
        compiler_params=_cparams(("arbitrary", "arbitrary")),
        name="adaln_mod",
    )(c_pad, w_ada, b_ada.reshape(depth, 6, 1, d))
    return out.transpose(0, 2, 1, 3)


def _token_parts(geom, x, tm):
    parts = tuple(p for p in x if p.shape[0]) if isinstance(x, tuple) else (x,)
    d = parts[0].shape[1]
    if len(parts) == 1:
        return parts, [pl.BlockSpec((tm, d), lambda i, *_: (i, 0))], None
    first = geom.tokens_p // tm
    specs = [pl.BlockSpec((tm, d), lambda i, *_: (jnp.minimum(i, first - 1), 0)),
             pl.BlockSpec((tm, d), lambda i, *_: (jnp.maximum(i - first, 0), 0))]
    return parts, specs, first


def _for_token_tile(x_refs, first, body):
    if len(x_refs) == 1:
        body(x_refs[0])
        return
    i = pl.program_id(0)
    pl.when(i < first)(lambda: body(x_refs[0]))
    pl.when(i >= first)(lambda: body(x_refs[1]))


def _norm_proj_kernel(*refs, n_x, first, steps_a, shift_row, scale_row):
    x_refs = refs[:n_x]
    mod_ref, g_ref, wa_ref, wb_ref, oa_ref, ob_ref, h_ref = refs[n_x:]
    j = pl.program_id(1)

    def norm(x_ref):
        h = _rms_mod(x_ref[...], g_ref[...], mod_ref[scale_row:scale_row + 1, :],
                     mod_ref[shift_row:shift_row + 1, :])
        h_ref[...] = h.astype(BF16)

    @pl.when(j == 0)
    def _():
        _for_token_tile(x_refs, first, norm)

    @pl.when(j < steps_a)
    def _():
        oa_ref[...] = jnp.dot(h_ref[...], wa_ref[...], preferred_element_type=F32).astype(oa_ref.dtype)

    @pl.when(j >= steps_a)
    def _():
        ob_ref[...] = jnp.dot(h_ref[...], wb_ref[...], preferred_element_type=F32).astype(ob_ref.dtype)


def _norm_proj(geom, x, mod_l, gvec, w_a, w_b, *, shift_row, scale_row, tm, tn_a, tn_b):
    parts, x_specs, first = _token_parts(geom, x, tm)
    t, d = geom.tokens, parts[0].shape[1]
    steps_a, steps_b = w_a.shape[1] // tn_a, w_b.shape[1] // tn_b
    col_a = lambda j: jnp.minimum(j, steps_a - 1)
    col_b = lambda j: jnp.maximum(j - steps_a, 0)
    return pl.pallas_call(
        functools.partial(_norm_proj_kernel, n_x=len(parts), first=first, steps_a=steps_a,
                          shift_row=shift_row, scale_row=scale_row),
        grid=(t // tm, steps_a + steps_b),
        in_specs=x_specs + [
            pl.BlockSpec((None, 6, d), lambda i, j: (_tile_seq(geom, i, tm), 0, 0)),
            pl.BlockSpec((1, d), lambda i, j: (0, 0)),
            pl.BlockSpec((d, tn_a), lambda i, j: (0, col_a(j))),
            pl.BlockSpec((d, tn_b), lambda i, j: (0, col_b(j))),
        ],
        out_specs=[pl.BlockSpec((tm, tn_a), lambda i, j: (i, col_a(j))),
                   pl.BlockSpec((tm, tn_b), lambda i, j: (i, col_b(j)))],
        out_shape=[jax.ShapeDtypeStruct((t, w_a.shape[1]), BF16), jax.ShapeDtypeStruct((t, w_b.shape[1]), BF16)],
        scratch_shapes=[pltpu.VMEM((tm, d), BF16)],
        compiler_params=_cparams(("parallel", "arbitrary")),
        name="norm_proj",
    )(*parts, mod_l, gvec, w_a, w_b)


def _rot_halves(ref, h, cos, sin):
    a = ref[:, h * RET_DK:h * RET_DK + RET_HALF].astype(F32)
    b = ref[:, h * RET_DK + RET_HALF:(h + 1) * RET_DK].astype(F32)
    return a * cos - b * sin, a * sin + b * cos


def _ret_bwd_state_kernel(k_ref, v_ref, cos_ref, sin_ref, kdb_ref, rb_ref, r_ref, *, geom, chunk, cdec):
    c = pl.num_programs(0) - 1 - pl.program_id(0)
    is_last = _tile_pos(geom, c, chunk) + chunk == _tile_seq_len(geom, c, chunk)

    @pl.when(is_last)
    def _():
        r_ref[...] = jnp.zeros_like(r_ref)

    cos = cos_ref[...]
    sin = sin_ref[...]
    for h in range(RET_HEADS):
        k1, k2 = _rot_halves(k_ref, h, cos, sin)
        dec = kdb_ref[h]
        kd = jnp.concatenate([k1 * dec, k2 * dec], axis=1).astype(BF16)
        vh = v_ref[:, h * RET_DV:(h + 1) * RET_DV].astype(BF16)
        r = r_ref[h]
        rb_ref[h] = r.astype(BF16)
        upd = lax.dot_general(kd, vh, (((0,), (0,)), ((), ())), preferred_element_type=F32)
        r_ref[h] = r * cdec[h] + upd


def _ret_main_kernel(q_ref, k_ref, v_ref, g_ref, rb_ref, cos_ref, sin_ref, dmat_ref, qdf_ref, qdb_ref,
                     kdf_ref, wo_ref, o_ref, rf_ref, *, geom, chunk, cdec):
    i = pl.program_id(0)

    @pl.when(_tile_pos(geom, i, chunk) == 0)
    def _():
        rf_ref[...] = jnp.zeros_like(rf_ref)

    cos = cos_ref[...]
    sin = sin_ref[...]
    acc = jnp.zeros(o_ref.shape, F32)
    for h in range(RET_HEADS):
        q1, q2 = _rot_halves(q_ref, h, cos, sin)
        k1, k2 = _rot_halves(k_ref, h, cos, sin)
        qr = jnp.concatenate([q1, q2], axis=1).astype(BF16)
        kr = jnp.concatenate([k1, k2], axis=1).astype(BF16)
        vh = v_ref[:, h * RET_DV:(h + 1) * RET_DV].astype(BF16)
        s = lax.dot_general(qr, kr, (((1,), (1,)), ((), ())), preferred_element_type=F32) * dmat_ref[h]
        o = jnp.dot(s.astype(BF16), vh, preferred_element_type=F32)
        qdf = qdf_ref[h]
        qf = jnp.concatenate([q1 * qdf, q2 * qdf], axis=1).astype(BF16)
        rf = rf_ref[h]
        o = o + jnp.dot(qf, rf.astype(BF16), preferred_element_type=F32)
        qdb = qdb_ref[h]
        qb = jnp.concatenate([q1 * qdb, q2 * qdb], axis=1).astype(BF16)
        o = o + jnp.dot(qb, rb_ref[h], preferred_element_type=F32)
        kdf = kdf_ref[h]
        kf = jnp.concatenate([k1 * kdf, k2 * kdf], axis=1).astype(BF16)
        rf_ref[h] = rf * cdec[h] + lax.dot_general(kf, vh, (((0,), (0,)), ((), ())),
                                                   preferred_element_type=F32)
        oc = o - jnp.mean(o, axis=-1, keepdims=True)
        on = oc * lax.rsqrt(jnp.mean(oc * oc, axis=-1, keepdims=True) + GN_EPS)
        og = (on * _silu(g_ref[:, h * RET_DV:(h + 1) * RET_DV].astype(F32))).astype(BF16)
        acc = acc + jnp.dot(og, wo_ref[h * RET_DV:(h + 1) * RET_DV, :], preferred_element_type=F32)
    o_ref[...] = acc.astype(o_ref.dtype)


def _retention_tables(chunk, s_max):
    log_gamma = jnp.log1p(-jnp.exp2(-5.0 - jnp.arange(RET_HEADS, dtype=F32)))
    idx = jnp.arange(chunk, dtype=F32)
    dist = jnp.abs(idx[:, None] - idx[None, :])
    dmat = jnp.exp(log_gamma[:, None, None] * dist[None])

    def rows(e):
        return jnp.broadcast_to(jnp.exp(log_gamma[:, None] * e[None, :])[:, :, None],
                                (RET_HEADS, chunk, RET_HALF))

    qdf = rows(idx + 1.0)
    qdb = rows(chunk - idx)
    kdf = rows(chunk - 1.0 - idx)
    kdb = rows(idx)
    theta = 1.0 / (ROPE_BASE ** jnp.linspace(0.0, 1.0, RET_HALF, dtype=F32))
    ang = jnp.arange(s_max, dtype=F32)[:, None] * theta[None, :]
    k_scale = RET_DK ** -0.5
    return jnp.cos(ang), jnp.sin(ang), dmat * k_scale, qdf, qdb, kdf * k_scale, kdb * k_scale


def _chunk_decay(chunk):
    lg = np.log1p(-np.exp2(-5.0 - np.arange(RET_HEADS, dtype=np.float32))).astype(np.float32)
    return tuple(float(v) for v in np.exp(lg * np.float32(chunk)).astype(np.float32))


def _retention(geom, proj_a, w_ret_o, tables, *, chunk):
    t = proj_a.shape[0]
    n_chunks = t // chunk
    cos, sin, dmat, qdf, qdb, kdf, kdb = tables
    cdec = _chunk_decay(chunk)
    state_shape = (RET_HEADS, RET_DK, RET_DV)

    def pos_blk(c):
        return _tile_pos(geom, c, chunk) // chunk

    rev = lambda i: n_chunks - 1 - i
    tab_spec = pl.BlockSpec((RET_HEADS, chunk, RET_HALF), lambda i: (0, 0, 0))
    rb = pl.pallas_call(
        functools.partial(_ret_bwd_state_kernel, geom=geom, chunk=chunk, cdec=cdec),
        grid=(n_chunks,),
        in_specs=[
            pl.BlockSpec((chunk, RET_QK_W), lambda i: (rev(i), 1)),
            pl.BlockSpec((chunk, RET_V_W), lambda i: (rev(i), 1)),
            pl.BlockSpec((chunk, RET_HALF), lambda i: (pos_blk(rev(i)), 0)),
            pl.BlockSpec((chunk, RET_HALF), lambda i: (pos_blk(rev(i)), 0)),
            tab_spec,
        ],
        out_specs=pl.BlockSpec((None,) + state_shape, lambda i: (rev(i), 0, 0, 0)),
        out_shape=jax.ShapeDtypeStruct((n_chunks,) + state_shape, BF16),
        scratch_shapes=[pltpu.VMEM(state_shape, F32)],
        compiler_params=_cparams(("arbitrary",)),
        name="ret_bwd_state",
    )(proj_a, proj_a, cos, sin, kdb)

    return pl.pallas_call(
        functools.partial(_ret_main_kernel, geom=geom, chunk=chunk, cdec=cdec),
        grid=(n_chunks,),
        in_specs=[
            pl.BlockSpec((chunk, RET_QK_W), lambda i: (i, 0)),
            pl.BlockSpec((chunk, RET_QK_W), lambda i: (i, 1)),
            pl.BlockSpec((chunk, RET_V_W), lambda i: (i, 1)),
            pl.BlockSpec((chunk, RET_V_W), lambda i: (i, 2)),
            pl.BlockSpec((None,) + state_shape, lambda i: (i, 0, 0, 0)),
            pl.BlockSpec((chunk, RET_HALF), lambda i: (pos_blk(i), 0)),
            pl.BlockSpec((chunk, RET_HALF), lambda i: (pos_blk(i), 0)),
            pl.BlockSpec((RET_HEADS, chunk, chunk), lambda i: (0, 0, 0)),
            tab_spec, tab_spec, tab_spec,
            pl.BlockSpec((RET_V_W, D_MODEL), lambda i: (0, 0)),
        ],
        out_specs=pl.BlockSpec((chunk, D_MODEL), lambda i: (i, 0)),
        out_shape=jax.ShapeDtypeStruct((t, D_MODEL), BF16),
        scratch_shapes=[pltpu.VMEM(state_shape, F32)],
        compiler_params=_cparams(("arbitrary",)),
        name="ret_main",
    )(proj_a, proj_a, proj_a, proj_a, rb, cos, sin, dmat, qdf, qdb, kdf, w_ret_o)


def _log_sigmoid(x):
    return jnp.minimum(x, 0.0) - jnp.log1p(jnp.exp(-jnp.abs(x)))


LRU_SEGS = V7X_SUBLANES
LRU_LEAD = CONV_LEFT * LRU_SEGS
LRU_TAIL = (CONV_WIDTH - 1 - CONV_LEFT) * LRU_SEGS


def _lru_load_segment_major(x_ref, prev_ref, next_ref, keep_prev, keep_next, xs_ref):
    tl = x_ref.shape[0]
    seg = tl // LRU_SEGS
    x = x_ref[...].astype(F32)
    prev_hi = prev_ref[...].astype(F32)[LRU_HALO - V7X_SUBLANES:] * keep_prev
    next_lo = next_ref[...].astype(F32)[:V7X_SUBLANES] * keep_next
    row8 = lax.broadcasted_iota(jnp.int32, (V7X_SUBLANES, LRU_BW), 0)
    last = V7X_SUBLANES - 1
    for n in range(LRU_BLOCKS):
        sl = slice(n * LRU_BW, (n + 1) * LRU_BW)
        for s in range(LRU_SEGS):
            xs_ref[n, pl.ds(LRU_LEAD + s, seg, stride=LRU_SEGS), :] = x[s * seg:(s + 1) * seg, sl]
        slab = lambda t: xs_ref[n, LRU_LEAD + t * LRU_SEGS:LRU_LEAD + (t + 1) * LRU_SEGS, :]
        p = prev_hi[:, sl]
        m1 = pltpu.roll(jnp.where(row8 == last, p, slab(seg - 1)), 1, axis=0)
        m2 = pltpu.roll(jnp.where(row8 == last, pltpu.roll(p, 1, axis=0), slab(seg - 2)), 1, axis=0)
        p1 = pltpu.roll(jnp.where(row8 == 0, next_lo[:, sl], slab(0)), last, axis=0)
        xs_ref[n, 0:LRU_SEGS, :] = m2
        xs_ref[n, LRU_SEGS:LRU_LEAD, :] = m1
        xs_ref[n, LRU_LEAD + tl:LRU_LEAD + tl + LRU_TAIL, :] = p1


def _lru_gates(xs_ref, d, wconv_ref, bconv_ref, wrg_ref, brg_ref, wig_ref, big_ref, lam_ref, a_ref, u_ref):
    tl = a_ref.shape[1]
    c = (0.5 * LRU_C * math.log2(math.e)) * _log_sigmoid(lam_ref[d])
    for n in range(LRU_BLOCKS):
        sl = slice(n * LRU_BW, (n + 1) * LRU_BW)
        xh = bconv_ref[:, sl] + xs_ref[n, 0:tl, :] * wconv_ref[0:1, sl]
        for j in range(1, CONV_WIDTH):
            xh = xh + xs_ref[n, j * LRU_SEGS:j * LRU_SEGS + tl, :] * wconv_ref[j:j + 1, sl]
        xb = xh.astype(BF16)
        th_r = jnp.tanh(jnp.dot(xb, wrg_ref[d, n], preferred_element_type=F32) + brg_ref[d][:, sl])
        th_i = jnp.tanh(jnp.dot(xb, wig_ref[d, n], preferred_element_type=F32) + big_ref[d][:, sl])
        a = jnp.exp2(c[:, sl] * th_r + c[:, sl])
        a_ref[n] = a
        y = 1.0 - a * a
        u_ref[n] = (y * lax.rsqrt(jnp.maximum(y, SQRT_TINY))) * ((th_i + 1.0) * xh)


def _lru_scan_segments(a_ref, u_ref, hs_ref, as_ref, fin_ref, carry_ref, o_ref, reverse):
    tl = a_ref.shape[1]
    seg = tl // LRU_SEGS
    slab_shape = (LRU_SEGS, LRU_BW)

    def step(k, carry):
        t = seg - 1 - k if reverse else k
        rows = pl.ds(pl.multiple_of(t * LRU_SEGS, LRU_SEGS), LRU_SEGS)
        hs, decays = [], []
        for n in range(LRU_BLOCKS):
            a = a_ref[n, rows, :]
            h = a * carry[0][n] + u_ref[n, rows, :]
            decay = a * carry[1][n]
            hs_ref[n, rows, :] = h
            as_ref[n, rows, :] = decay
            hs.append(h)
            decays.append(decay)
        return tuple(hs), tuple(decays)

    init = (tuple(jnp.zeros(slab_shape, F32) for _ in range(LRU_BLOCKS)),
            tuple(jnp.ones(slab_shape, F32) for _ in range(LRU_BLOCKS)))
    h_end, decay_end = lax.fori_loop(0, seg, step, init, unroll=4)

    order = range(LRU_SEGS - 1, -1, -1) if reverse else range(LRU_SEGS)
    for n in range(LRU_BLOCKS):
        sl = slice(n * LRU_BW, (n + 1) * LRU_BW)
        fin_ref[0] = h_end[n]
        fin_ref[1] = decay_end[n]
        c = carry_ref[n, 0:1, :]
        for s in order:
            fin_ref[2, s:s + 1, :] = c
            c = fin_ref[0, s:s + 1, :] + fin_ref[1, s:s + 1, :] * c
        carry_ref[n, 0:1, :] = c
        for s in range(LRU_SEGS):
            seg_rows = pl.ds(s, seg, stride=LRU_SEGS)
            h = hs_ref[n, seg_rows, :] + as_ref[n, seg_rows, :] * fin_ref[2, s:s + 1, :]
            o_ref[s * seg:(s + 1) * seg, sl] = h.astype(o_ref.dtype)


def _lru_kernel(xf_ref, xfp_ref, xfn_ref, xb_ref, xbp_ref, xbn_ref, wconv_ref, bconv_ref, wrg_ref, brg_ref,
                wig_ref, big_ref, lam_ref, hf_ref, hb_ref, xs_ref, a_ref, u_ref, hs_ref, as_ref, fin_ref,
                cf_ref, cb_ref, *, geom, tl):
    i = pl.program_id(0)
    ib = pl.num_programs(0) - 1 - i

    def flags(tile):
        pos = _tile_pos(geom, tile, tl)
        return pos == 0, pos + tl == _tile_seq_len(geom, tile, tl)

    def keep(flag):
        return jnp.where(flag, 0.0, 1.0)

    def direction(d, x_ref, prev_ref, next_ref, first, last, carry_ref, o_ref):
        _lru_load_segment_major(x_ref, prev_ref, next_ref, keep(first), keep(last), xs_ref)
        _lru_gates(xs_ref, d, wconv_ref, bconv_ref, wrg_ref, brg_ref, wig_ref, big_ref, lam_ref, a_ref, u_ref)
        _lru_scan_segments(a_ref, u_ref, hs_ref, as_ref, fin_ref, carry_ref, o_ref, reverse=(d == 1))

    first, last = flags(i)

    @pl.when(first)
    def _():
        cf_ref[...] = jnp.zeros_like(cf_ref)

    direction(0, xf_ref, xfp_ref, xfn_ref, first, last, cf_ref, hf_ref)

    first, last = flags(ib)

    @pl.when(last)
    def _():
        cb_ref[...] = jnp.zeros_like(cb_ref)

    direction(1, xb_ref, xbp_ref, xbn_ref, first, last, cb_ref, hb_ref)


def _lru(geom, proj_b, w_conv, b_conv, w_rg, b_rg, w_ig, b_ig, lam, *, tl):
    t = proj_b.shape[0]
    n = t // tl
    r8 = tl // LRU_HALO
    n8 = t // LRU_HALO
    w = LRU_WIDTH
    rev = lambda i: n - 1 - i
    prev_blk = lambda i: jnp.maximum(i * r8 - 1, 0)
    next_blk = lambda i: jnp.minimum((i + 1) * r8, n8 - 1)
    full = lambda a: pl.BlockSpec(a.shape, lambda i: (0,) * a.ndim)
    w_conv = 0.5 * w_conv
    b_conv2 = 0.5 * b_conv.reshape(1, w)
    b_rg3 = 0.5 * b_rg.reshape(2, 1, w)
    b_ig3 = 0.5 * b_ig.reshape(2, 1, w)
    lam3 = lam.reshape(2, 1, w)
    return pl.pallas_call(
        functools.partial(_lru_kernel, geom=geom, tl=tl),
        grid=(n,),
        in_specs=[
            pl.BlockSpec((tl, w), lambda i: (i, 0)),
            pl.BlockSpec((LRU_HALO, w), lambda i: (prev_blk(i), 0)),
            pl.BlockSpec((LRU_HALO, w), lambda i: (next_blk(i), 0)),
            pl.BlockSpec((tl, w), lambda i: (rev(i), 0)),
            pl.BlockSpec((LRU_HALO, w), lambda i: (prev_blk(rev(i)), 0)),
            pl.BlockSpec((LRU_HALO, w), lambda i: (next_blk(rev(i)), 0)),
            full(w_conv), full(b_conv2), full(w_rg), full(b_rg3), full(w_ig), full(b_ig3), full(lam3),
        ],
        out_specs=[
            pl.BlockSpec((tl, w), lambda i: (i, 0)),
            pl.BlockSpec((tl, w), lambda i: (rev(i), 0)),
        ],
        out_shape=[jax.ShapeDtypeStruct((t, w), BF16), jax.ShapeDtypeStruct((t, w), BF16)],
        scratch_shapes=[pltpu.VMEM((LRU_BLOCKS, LRU_LEAD + tl + LRU_TAIL, LRU_BW), F32)]
        + [pltpu.VMEM((LRU_BLOCKS, tl, LRU_BW), F32)] * 4
        + [pltpu.VMEM((3, LRU_SEGS, LRU_BW), F32)]
        + [pltpu.VMEM((LRU_BLOCKS, V7X_SUBLANES, LRU_BW), F32)] * 2,
        compiler_params=_cparams(("arbitrary",)),
        name="lru_scan",
    )(proj_b, proj_b, proj_b, proj_b, proj_b, proj_b, w_conv, b_conv2, w_rg, b_rg3, w_ig, b_ig3, lam3)


def _gelu_tanh(x):
    return 0.5 * x * (1.0 + jnp.tanh(math.sqrt(2.0 / math.pi) * (x + 0.044715 * (x * x * x))))


def _merge_kernel(*refs, n_x, first):
    x_refs = refs[:n_x]
    ya_ref, hf_ref, hb_ref, gl_ref, ga_ref, gb_ref, mod_ref, wl_ref, wo_ref, o_ref = refs[n_x:]
    f32 = lambda ref: ref[...].astype(F32)
    y = ((f32(hf_ref) + f32(hb_ref)) * _gelu_tanh(f32(gl_ref))).astype(BF16)
    yb = jnp.dot(y, wl_ref[...], preferred_element_type=F32)
    m = _sigmoid(f32(ga_ref)) * f32(ya_ref) + _sigmoid(f32(gb_ref)) * yb
    mix = mod_ref[2:3, :] * jnp.dot(m.astype(BF16), wo_ref[...], preferred_element_type=F32)

    def residual(x_ref):
        o_ref[...] = x_ref[...] + mix

    _for_token_tile(x_refs, first, residual)


def _merge(geom, x, ya, hf, hb, proj_a, proj_b, mod_l, w_lru_o, w_out, *, tm):
    parts, x_specs, first = _token_parts(geom, x, tm)
    t, d = geom.tokens, parts[0].shape[1]
    gate_a_blk = (2 * RET_QK_W + 2 * RET_V_W) // d
    tok = lambda w: pl.BlockSpec((tm, w), lambda i: (i, 0))
    return pl.pallas_call(
        functools.partial(_merge_kernel, n_x=len(parts), first=first),
        grid=(t // tm,),
        in_specs=x_specs + [
            tok(d), tok(LRU_WIDTH), tok(LRU_WIDTH),
            pl.BlockSpec((tm, LRU_WIDTH), lambda i: (i, 1)),
            pl.BlockSpec((tm, d), lambda i: (i, gate_a_blk)),
            pl.BlockSpec((tm, d), lambda i: (i, gate_a_blk + 1)),
            pl.BlockSpec((None, 6, d), lambda i: (_tile_seq(geom, i, tm), 0, 0)),
            pl.BlockSpec((LRU_WIDTH, d), lambda i: (0, 0)),
            pl.BlockSpec((d, d), lambda i: (0, 0)),
        ],
        out_specs=tok(d),
        out_shape=jax.ShapeDtypeStruct((t, d), F32),
        compiler_params=_cparams(("parallel",)),
        name="merge",
    )(*parts, ya, hf, hb, proj_b, proj_a, proj_a, mod_l, w_lru_o, w_out)


def _ffn_kernel(x_ref, mod_ref, g_ref, wg_ref, wu_ref, wd_ref, o_ref, *, col_chunks):
    x = x_ref[...]
    h = _rms_mod(x, g_ref[...], mod_ref[4:5, :], mod_ref[3:4, :]).astype(BF16)
    acc = None
    for c0, c1 in col_chunks:
        a = jnp.dot(h, wg_ref[:, c0:c1], preferred_element_type=F32)
        u = jnp.dot(h, wu_ref[:, c0:c1], preferred_element_type=F32)
        part = jnp.dot((_silu(a) * u).astype(BF16), wd_ref[c0:c1, :], preferred_element_type=F32)
        acc = part if acc is None else acc + part
    o_ref[...] = x + mod_ref[5:6, :] * acc


def _ffn(geom, x, mod_l, gvec, wg, wu, wd, *, tm, tf):
    t, d = x.shape
    ff = wg.shape[1]
    col_chunks = tuple((c, min(c + tf, ff)) for c in range(0, ff, tf))
    resident = lambda shape: pl.BlockSpec(shape, lambda i: (0, 0), pipeline_mode=pl.Buffered(1))
    return pl.pallas_call(
        functools.partial(_ffn_kernel, col_chunks=col_chunks),
        grid=(t // tm,),
        in_specs=[
            pl.BlockSpec((tm, d), lambda i: (i, 0)),
            pl.BlockSpec((None, 6, d), lambda i: (_tile_seq(geom, i, tm), 0, 0)),
            pl.BlockSpec((1, d), lambda i: (0, 0)),
            resident((d, ff)), resident((d, ff)), resident((ff, d)),
        ],
        out_specs=pl.BlockSpec((tm, d), lambda i: (i, 0)),
        out_shape=jax.ShapeDtypeStruct((t, d), F32),
        compiler_params=_cparams(("parallel",)),
        name="ffn_dense",
    )(x, mod_l, gvec, wg, wu, wd)


ROUTE_E0, ROUTE_E1, ROUTE_R0, ROUTE_R1, ROUTE_G0, ROUTE_G1 = range(6)

ROW_TILES = D_MODEL // V7X_LANES


def _store_token_rows(ref, val):
    rows = val.shape[0]
    for j in range(ROW_TILES):
        ref[pl.ds(j, rows, stride=ROW_TILES), :] = val[:, j * V7X_LANES:(j + 1) * V7X_LANES]


def _load_token_rows(ref, rows):
    return jnp.concatenate([ref[pl.ds(j, rows, stride=ROW_TILES), :] for j in range(ROW_TILES)], axis=1)


def _router_kernel(x_ref, mod_ref, g_ref, wr_ref, h_ref, route_ref, cnt_ref, carry_ref):
    i = pl.program_id(0)
    tm = x_ref.shape[0]

    @pl.when(i == 0)
    def _():
        carry_ref[...] = jnp.zeros_like(carry_ref)

    h = _rms_mod(x_ref[...], g_ref[...], mod_ref[4:5, :], mod_ref[3:4, :])
    _store_token_rows(h_ref, h)
    h_hi = h.astype(BF16)
    h_lo = (h - h_hi.astype(F32)).astype(BF16)
    logits = (jnp.dot(h_hi, wr_ref[0], preferred_element_type=F32)
              + jnp.dot(h_lo, wr_ref[0], preferred_element_type=F32)
              + jnp.dot(h_hi, wr_ref[1], preferred_element_type=F32))
    lane = lax.broadcasted_iota(jnp.int32, logits.shape, 1).astype(F32)
    logits = jnp.where(lane < N_EXPERTS, logits, -jnp.inf)
    m1 = jnp.max(logits, axis=-1, keepdims=True)
    i1 = jnp.min(jnp.where(logits == m1, lane, float(V7X_LANES)), axis=-1, keepdims=True)
    rest = jnp.where(lane == i1, -jnp.inf, logits)
    m2 = jnp.max(rest, axis=-1, keepdims=True)
    i2 = jnp.min(jnp.where(rest == m2, lane, float(V7X_LANES)), axis=-1, keepdims=True)
    ex = jnp.exp(m2 - m1)
    g1 = 1.0 / (1.0 + ex)
    g2 = ex / (1.0 + ex)
    sel1 = lane == i1
    sel2 = lane == i2
    onehot = jnp.where(sel1 | sel2, 1.0, 0.0)
    row = lax.broadcasted_iota(jnp.int32, (tm, tm), 0)
    col = lax.broadcasted_iota(jnp.int32, (tm, tm), 1)
    lower = jnp.where(col < row, 1.0, 0.0).astype(BF16)
    before = jnp.dot(lower, onehot.astype(BF16), preferred_element_type=F32) + carry_ref[0:1, :]
    r1 = jnp.sum(jnp.where(sel1, before, 0.0), axis=-1, keepdims=True)
    r2 = jnp.sum(jnp.where(sel2, before, 0.0), axis=-1, keepdims=True)
    out_lane = lax.broadcasted_iota(jnp.int32, route_ref.shape, 1)
    vals = (i1.astype(F32), i2.astype(F32), r1, r2, g1, g2)
    packed = jnp.zeros(route_ref.shape, F32)
    for slot, v in enumerate(vals):
        packed = jnp.where(out_lane == slot, v, packed)
    route_ref[...] = packed
    carry = carry_ref[0:1, :] + jnp.sum(onehot, axis=0, keepdims=True)
    carry_ref[0:1, :] = carry
    cnt_ref[...] = jnp.broadcast_to(carry, cnt_ref.shape).astype(jnp.int32)


def _router(geom, x, mod_l, gvec, w_router, *, tm):
    t, d = x.shape
    w_pad = jnp.pad(w_router, ((0, 0), (0, V7X_LANES - N_EXPERTS)))
    w_hi = w_pad.astype(BF16)
    w_pad = jnp.stack([w_hi, (w_pad - w_hi.astype(F32)).astype(BF16)])
    return pl.pallas_call(
        _router_kernel,
        grid=(t // tm,),
        in_specs=[
            pl.BlockSpec((tm, d), lambda i: (i, 0)),
            pl.BlockSpec((None, 6, d), lambda i: (_tile_seq(geom, i, tm), 0, 0)),
            pl.BlockSpec((1, d), lambda i: (0, 0)),
            pl.BlockSpec((2, d, V7X_LANES), lambda i: (0, 0, 0)),
        ],
        out_specs=[
            pl.BlockSpec((tm * ROW_TILES, V7X_LANES), lambda i: (i, 0)),
            pl.BlockSpec((tm, V7X_LANES), lambda i: (i, 0)),
            pl.BlockSpec((V7X_SUBLANES, V7X_LANES), lambda i: (0, 0)),
        ],
        out_shape=[
            jax.ShapeDtypeStruct((t * ROW_TILES, V7X_LANES), F32),
            jax.ShapeDtypeStruct((t, V7X_LANES), F32),
            jax.ShapeDtypeStruct((V7X_SUBLANES, V7X_LANES), jnp.int32),
        ],
        scratch_shapes=[pltpu.VMEM((V7X_SUBLANES, V7X_LANES), F32)],
        compiler_params=_cparams(("arbitrary",)),
        name="router",
    )(x, mod_l, gvec, w_pad)


def _token_copy(src, src_tok, dst, dst_tok, sem):
    s = pl.multiple_of(src_tok * ROW_TILES, ROW_TILES)
    d = pl.multiple_of(dst_tok * ROW_TILES, ROW_TILES)
    return pltpu.make_async_copy(src.at[pl.ds(s, ROW_TILES), :], dst.at[pl.ds(d, ROW_TILES), :], sem)


def _dispatch_kernel(fill_ref, dest_ref, src_ref, out_hbm, zero_ref, sem, zero_sem, *, td):
    @pl.when(pl.program_id(0) == 0)
    def _():
        zero_ref[...] = jnp.zeros_like(zero_ref)
        for wait in (False, True):
            for e in range(N_EXPERTS):
                def pad_row(r, carry):
                    copy = _token_copy(zero_ref, 0, out_hbm, r, zero_sem)
                    copy.wait() if wait else copy.start()
                    return carry

                lax.fori_loop(fill_ref[0, e], fill_ref[1, e], pad_row, 0)

    def issue(t, carry):
        for k in range(TOP_K):
            _token_copy(src_ref, t, out_hbm, dest_ref[k, t], sem).start(priority=k % DMA_PRIORITIES)
        return carry

    lax.fori_loop(0, td, issue, 0, unroll=8)
    for _ in range(TOP_K):
        pltpu.make_async_copy(src_ref, out_hbm.at[pl.ds(0, td * ROW_TILES), :], sem).wait()


def _dispatch(h2_rows, dest, fill, n_rows, *, td):
    grid_spec = pltpu.PrefetchScalarGridSpec(
        num_scalar_prefetch=1,
        grid=(dest.shape[1] // td,),
        in_specs=[
            pl.BlockSpec((TOP_K, td), lambda i, fill: (0, i), memory_space=pltpu.SMEM),
            pl.BlockSpec((td * ROW_TILES, V7X_LANES), lambda i, fill: (i, 0)),
        ],
        out_specs=pl.BlockSpec(memory_space=pl.ANY),
        scratch_shapes=[pltpu.VMEM((ROW_TILES, V7X_LANES), h2_rows.dtype),
                        pltpu.SemaphoreType.DMA(()), pltpu.SemaphoreType.DMA(())],
    )
    return pl.pallas_call(
        functools.partial(_dispatch_kernel, td=td),
        grid_spec=grid_spec,
        out_shape=jax.ShapeDtypeStruct((n_rows * ROW_TILES, V7X_LANES), h2_rows.dtype),
        compiler_params=_cparams(("arbitrary",)),
        name="moe_dispatch",
    )(fill, dest, h2_rows)


def _experts_kernel(blk_e_ref, nused_ref, x_ref, wg_ref, wu_ref, wd_ref, o_ref, xb_ref, acc_ref):
    del blk_e_ref
    b = pl.program_id(0)
    j = pl.program_id(1)
    bm = xb_ref.shape[0]

    @pl.when(b < nused_ref[0])
    def _():
        @pl.when(j == 0)
        def _():
            xb_ref[...] = _load_token_rows(x_ref, bm).astype(BF16)
            acc_ref[...] = jnp.zeros_like(acc_ref)

        xb = xb_ref[...]
        a = jnp.dot(xb, wg_ref[...], preferred_element_type=F32)
        u = jnp.dot(xb, wu_ref[...], preferred_element_type=F32)
        acc_ref[...] += jnp.dot((_silu(a) * u).astype(BF16), wd_ref[...], preferred_element_type=F32)

        @pl.when(j == pl.num_programs(1) - 1)
        def _():
            _store_token_rows(o_ref, acc_ref[...])

    @pl.when((b >= nused_ref[0]) & (j == pl.num_programs(1) - 1))
    def _():
        o_ref[...] = jnp.zeros_like(o_ref)


def _experts(xbuf, blk_e, n_used, wg, wu, wd, *, bm, tf):
    p = xbuf.shape[0] // ROW_TILES
    d = D_MODEL
    ff = wg.shape[2]
    nj = ff // tf
    rows_blk = (bm * ROW_TILES, V7X_LANES)

    def blk(b, nu):
        return jnp.minimum(b, nu[0] - 1)

    def ffc(b, j, nu):
        return jnp.where(b < nu[0], j, nj - 1)

    grid_spec = pltpu.PrefetchScalarGridSpec(
        num_scalar_prefetch=2,
        grid=(p // bm, nj),
        in_specs=[
            pl.BlockSpec(rows_blk, lambda b, j, be, nu: (blk(b, nu), 0)),
            pl.BlockSpec((None, d, tf), lambda b, j, be, nu: (be[blk(b, nu)], 0, ffc(b, j, nu))),
            pl.BlockSpec((None, d, tf), lambda b, j, be, nu: (be[blk(b, nu)], 0, ffc(b, j, nu))),
            pl.BlockSpec((None, tf, d), lambda b, j, be, nu: (be[blk(b, nu)], ffc(b, j, nu), 0)),
        ],
        out_specs=pl.BlockSpec(rows_blk, lambda b, j, be, nu: (b, 0)),
        scratch_shapes=[pltpu.VMEM((bm, d), BF16), pltpu.VMEM((bm, d), F32)],
    )
    return pl.pallas_call(
        _experts_kernel,
        grid_spec=grid_spec,
        out_shape=jax.ShapeDtypeStruct(xbuf.shape, F32),
        compiler_params=_cparams(("arbitrary", "arbitrary")),
        name="moe_experts",
    )(blk_e, n_used, xbuf, wg, wu, wd)


def _moe_out_kernel(dest_ref, dest_next_ref, x_ref, route_ref, mod_ref, gf_ref, ybuf_hbm, op_ref, os_ref,
                    ya0_ref, ya1_ref, yb0_ref, yb1_ref, sems, *, final_norm, prompt_tiles):
    i = pl.program_id(0)
    tm = x_ref.shape[0]
    bufs = ((ya0_ref, ya1_ref), (yb0_ref, yb1_ref))

    def gather(d_ref, slot, inline=False):
        def issue(t, carry):
            for k in range(TOP_K):
                _token_copy(ybuf_hbm, d_ref[k, t], bufs[slot][k], t, sems.at[slot]).start(
                    priority=k % DMA_PRIORITIES)
            return carry

        if inline:
            for t in range(tm):
                issue(t, 0)
        else:
            lax.fori_loop(0, tm, issue, 0, unroll=8)

    def wait(slot):
        for k in range(TOP_K):
            pltpu.make_async_copy(ybuf_hbm.at[pl.ds(0, tm * ROW_TILES), :], bufs[slot][k], sems.at[slot]).wait()

    def combine(slot):
        g1 = route_ref[:, ROUTE_G0:ROUTE_G0 + 1]
        g2 = route_ref[:, ROUTE_G1:ROUTE_G1 + 1]
        y = _load_token_rows(bufs[slot][0], tm) * g1 + _load_token_rows(bufs[slot][1], tm) * g2
        x = x_ref[...] + mod_ref[5:6, :] * y
        if final_norm:
            ms = jnp.mean(x * x, axis=-1, keepdims=True)
            x = x * lax.rsqrt(ms + RMS_EPS) * gf_ref[...]

        @pl.when(i < prompt_tiles)
        def _():
            op_ref[...] = x

        @pl.when(i >= prompt_tiles)
        def _():
            os_ref[...] = x

    @pl.when(i == 0)
    def _():
        gather(dest_ref, 0)

    for slot in range(2):
        @pl.when(i % 2 == slot)
        def _():
            wait(slot)
            gather(dest_next_ref, 1 - slot, inline=True)
            combine(slot)

            @pl.when(i + 1 == pl.num_programs(0))
            def _():
                wait(1 - slot)


def _moe_out(geom, x, ybuf, dest, route, mod_l, g_final, *, tm, final_norm):
    t, d = x.shape
    n = t // tm
    prompt_tiles = geom.tokens_p // tm
    y_buf = pltpu.VMEM((tm * ROW_TILES, V7X_LANES), F32)
    return pl.pallas_call(
        functools.partial(_moe_out_kernel, final_norm=final_norm, prompt_tiles=prompt_tiles),
        grid=(n,),
        in_specs=[
            pl.BlockSpec((TOP_K, tm), lambda i: (0, i), memory_space=pltpu.SMEM),
            pl.BlockSpec((TOP_K, tm), lambda i: (0, jnp.minimum(i + 1, n - 1)), memory_space=pltpu.SMEM),
            pl.BlockSpec((tm, d), lambda i: (i, 0)),
            pl.BlockSpec((tm, V7X_LANES), lambda i: (i, 0)),
            pl.BlockSpec((None, 6, d), lambda i: (_tile_seq(geom, i, tm), 0, 0)),
            pl.BlockSpec((1, d), lambda i: (0, 0)),
            pl.BlockSpec(memory_space=pl.ANY),
        ],
        out_specs=[
            pl.BlockSpec((tm, d), lambda i: (jnp.minimum(i, prompt_tiles - 1), 0)),
            pl.BlockSpec((tm, d), lambda i: (jnp.maximum(i - prompt_tiles, 0), 0)),
        ],
        out_shape=[jax.ShapeDtypeStruct((geom.tokens_p, d), F32),
                   jax.ShapeDtypeStruct((t - geom.tokens_p, d), F32)],
        scratch_shapes=[y_buf, y_buf, y_buf, y_buf, pltpu.SemaphoreType.DMA((2,))],
        compiler_params=_cparams(("arbitrary",)),
        name="moe_out",
    )(dest, dest, x, route, mod_l, g_final, ybuf)


def _final_norm_kernel(x_ref, g_ref, o_ref):
    x = x_ref[...]
    o_ref[...] = x * lax.rsqrt(jnp.mean(x * x, axis=-1, keepdims=True) + RMS_EPS) * g_ref[...]


def _final_norm(x, g_final, *, tm):
    t, d = x.shape
    return pl.pallas_call(
        _final_norm_kernel,
        grid=(t // tm,),
        in_specs=[pl.BlockSpec((tm, d), lambda i: (i, 0)), pl.BlockSpec((1, d), lambda i: (0, 0))],
        out_specs=pl.BlockSpec((tm, d), lambda i: (i, 0)),
        out_shape=jax.ShapeDtypeStruct((t, d), F32),
        compiler_params=_cparams(("parallel",)),
        name="final_norm",
    )(x, g_final)


def _moe(geom, x, mod_l, gvec, w_router, wg, wu, wd, g_final, *, tiles, final_norm):
    t, d = x.shape
    bm = tiles["moe_rows"]
    h2, route, counts = _router(geom, x, mod_l, gvec, w_router, tm=tiles["router"])
    counts = counts[0, :N_EXPERTS]
    padded = ((counts + bm - 1) // bm) * bm
    pad_end = jnp.cumsum(padded)
    pad_start = pad_end - padded
    experts = route[:, ROUTE_E0:ROUTE_E1 + 1].astype(jnp.int32)
    ranks = route[:, ROUTE_R0:ROUTE_R1 + 1].astype(jnp.int32)
    dest = (pad_start[experts] + ranks).T
    n_rows = t * TOP_K + N_EXPERTS * bm
    n_blk = n_rows // bm
    blk_start = jnp.arange(n_blk, dtype=jnp.int32) * bm
    blk_e = jnp.minimum(jnp.sum((pad_end[None, :] <= blk_start[:, None]).astype(jnp.int32), axis=1),
                        N_EXPERTS - 1).astype(jnp.int32)
    n_used = (pad_end[-1:] // bm).astype(jnp.int32)
    fill = jnp.stack([pad_start + counts, pad_end.at[N_EXPERTS - 1].set(n_rows)]).astype(jnp.int32)
    xbuf = _dispatch(h2, dest, fill, n_rows, td=tiles["dma_rows"])
    ybuf = _experts(xbuf, blk_e, n_used, wg, wu, wd, bm=bm, tf=tiles["expert_ff"])
    return _moe_out(geom, x, ybuf, dest, route, mod_l, g_final, tm=tiles["token"], final_norm=final_norm)


def _pick_tiles(geom):
    s = math.gcd(geom.s_p, geom.s_s) if geom.n_p and geom.n_s else (geom.s_p if geom.n_p else geom.s_s)
    return {
        "proj": min(1024, s),
        "proj_cols": 2048,
        "ret_chunk": min(512, s),
        "lru": min(512, s),
        "token": min(512, s),
        "ffn_cols": 1536,
        "router": min(512, s),
        "moe_rows": 512,
        "expert_ff": 1792,
        "dma_rows": min(2048, s),
    }


def _trunk(geom, x, c_all, w_ada, b_ada, g_norm1, g_norm2, w_in, w_conv, b_conv, w_rg, b_rg, w_ig, b_ig,
           lru_lambda, w_ret_o, w_lru_o, w_out, w_ff_gate, w_ff_up, w_ff_down,
           w_router, w_e_gate, w_e_up, w_e_down, g_final, tiles):
    depth = w_in.shape[0]
    d = D_MODEL
    n_pad = -(-geom.n_seq // V7X_SUBLANES) * V7X_SUBLANES
    c_pad = jnp.pad(c_all, ((0, n_pad - geom.n_seq), (0, 0)))
    mod = _adaln_mod(c_pad, w_ada, b_ada)
    tables = _retention_tables(tiles["ret_chunk"], max(geom.s_p if geom.n_p else 0, geom.s_s if geom.n_s else 0))

    o_xl = 2 * RET_QK_W + 2 * RET_V_W
    o_ga = o_xl + 2 * LRU_WIDTH
    for l in range(depth):
        wl = w_in[l]
        w_a = jnp.concatenate([wl[:, :o_xl], wl[:, o_ga:]], axis=1).astype(BF16)
        w_b = wl[:, o_xl:o_ga].astype(BF16)
        g1 = g_norm1[l].reshape(1, d)
        g2 = g_norm2[l].reshape(1, d)
        proj_a, proj_b = _norm_proj(geom, x, mod[l], g1, w_a, w_b, shift_row=0, scale_row=1,
                                    tm=tiles["proj"], tn_a=tiles["proj_cols"], tn_b=LRU_WIDTH)
        ya = _retention(geom, proj_a, w_ret_o[l].astype(BF16), tables, chunk=tiles["ret_chunk"])
        hf, hb = _lru(geom, proj_b, w_conv[l], b_conv[l], w_rg[l].astype(BF16), b_rg[l],
                      w_ig[l].astype(BF16), b_ig[l], lru_lambda[l], tl=tiles["lru"])
        x = _merge(geom, x, ya, hf, hb, proj_a, proj_b, mod[l], w_lru_o[l].astype(BF16),
                   w_out[l].astype(BF16), tm=tiles["token"])
        j = l // 2
        last = l == depth - 1
        if l % 2 == 0:
            x = _ffn(geom, x, mod[l], g2, w_ff_gate[j].astype(BF16), w_ff_up[j].astype(BF16),
                     w_ff_down[j].astype(BF16), tm=tiles["token"], tf=tiles["ffn_cols"])
            if last:
                x = _final_norm(x, g_final.reshape(1, d), tm=tiles["token"])
                return x[:geom.tokens_p], x[geom.tokens_p:]
        else:
            parts = _moe(geom, x, mod[l], g2, w_router[j], w_e_gate[j].astype(BF16), w_e_up[j].astype(BF16),
                         w_e_down[j].astype(BF16), g_final.reshape(1, d), tiles=tiles, final_norm=last)
            if last:
                return parts
            x = jnp.concatenate(parts, axis=0)


def kernel(x_prompt, x_sample, c_prompt, c_sample, w_ada, b_ada, g_norm1, g_norm2, w_in, w_conv, b_conv, w_rg, b_rg, w_ig, b_ig, lru_lambda, w_ret_o, w_lru_o, w_out, w_ff_gate, w_ff_up, w_ff_down, w_router, w_e_gate, w_e_up, w_e_down, g_final):
    n_p, s_p, d = x_prompt.shape
    n_s, s_s, _ = x_sample.shape
    geom = Geom(n_p, s_p, n_s, s_s)
    x = (x_prompt.reshape(-1, d), x_sample.reshape(-1, d))
    c_all = jnp.concatenate([c_prompt, c_sample], axis=0)
    yp, ys = _trunk(geom, x, c_all, w_ada, b_ada, g_norm1, g_norm2, w_in, w_conv, b_conv, w_rg, b_rg, w_ig, b_ig,
                    lru_lambda, w_ret_o, w_lru_o, w_out, w_ff_gate, w_ff_up, w_ff_down,
                    w_router, w_e_gate, w_e_up, w_e_down, g_final, _pick_tiles(geom))
    return (yp.reshape(n_p, s_p, d), ys.reshape(n_s, s_s, d))
```

```python
import functools
import math
from typing import NamedTuple

import jax
import jax.numpy as jnp
import numpy as np
from jax import lax
from jax.experimental import pallas as pl
from jax.experimental.pallas import tpu as pltpu

F32 = jnp.float32
BF16 = jnp.bfloat16

D_MODEL = 1024
RET_HEADS = 4
RET_DK = 256
RET_DV = 512
RET_HALF = RET_DK // 2
RET_QK_W = RET_HEADS * RET_DK
RET_V_W = RET_HEADS * RET_DV
ROPE_BASE = 10000.0
LRU_WIDTH = 1280
LRU_BLOCKS = 10
LRU_BW = LRU_WIDTH // LRU_BLOCKS
LRU_C = 8.0
CONV_WIDTH = 4
CONV_LEFT = 2
LRU_HALO = 16
N_EXPERTS = 8
TOP_K = 2
RMS_EPS = 1e-6
GN_EPS = 1e-5
SQRT_TINY = 1e-30

V7X_LANES = 128
V7X_SUBLANES = 8
V7X_VMEM_BYTES = 64 * 1024 * 1024
DMA_PRIORITIES = 2
VMEM_LIMIT = (V7X_VMEM_BYTES * 3) // 4

PA_W = 2 * RET_QK_W + 2 * RET_V_W + 2 * D_MODEL
PB_W = 2 * LRU_WIDTH


class Geom(NamedTuple):
    n_p: int
    s_p: int
    n_s: int
    s_s: int

    @property
    def tokens_p(self):
        return self.n_p * self.s_p

    @property
    def tokens(self):
        return self.n_p * self.s_p + self.n_s * self.s_s

    @property
    def n_seq(self):
        return self.n_p + self.n_s


def _tile_seq(g, i, tm):
    t0 = i * tm
    return jnp.where(t0 < g.tokens_p, t0 // g.s_p, g.n_p + (t0 - g.tokens_p) // g.s_s)


def _tile_pos(g, i, tm):
    t0 = i * tm
    return jnp.where(t0 < g.tokens_p, t0 % g.s_p, (t0 - g.tokens_p) % g.s_s)


def _tile_seq_len(g, i, tm):
    return jnp.where(i * tm < g.tokens_p, g.s_p, g.s_s)


def _cparams(sem):
    return pltpu.CompilerParams(dimension_semantics=sem, vmem_limit_bytes=VMEM_LIMIT)


def _sigmoid(x):
    return 0.5 * jnp.tanh(0.5 * x) + 0.5


def _silu(x):
    return x * _sigmoid(x)


def _rms_mod(x, gvec, scale, shift):
    ms = jnp.mean(x * x, axis=-1, keepdims=True)
    y = x * lax.rsqrt(ms + RMS_EPS) * gvec
    return y * (1.0 + scale) + shift


def _mod_kernel(c_ref, w_ref, b_ref, o_ref):
    c = c_ref[...]
    o_ref[...] = jnp.dot(_silu(c), w_ref[...], preferred_element_type=F32,
                         precision=lax.Precision.HIGHEST) + b_ref[...]


def _adaln_mod(c_pad, w_ada, b_ada):
    depth, d, _ = w_ada.shape
    n = c_pad.shape[0]
    out = pl.pallas_call(
        _mod_kernel,
        grid=(depth, 6),
        in_specs=[
            pl.BlockSpec((n, d), lambda l, j: (0, 0)),
            pl.BlockSpec((None, d, d), lambda l, j: (l, 0, j)),
            pl.BlockSpec((None, None, 1, d), lambda l, j: (l, j, 0, 0)),
        ],
        out_specs=pl.BlockSpec((None, None, n, d), lambda l, j: (l, j, 0, 0)),
        out_shape=jax.ShapeDtypeStruct((depth, 6, n, d), F32),
        compiler_params=_cparams(("arbitrary", "arbitrary")),
        name="adaln_mod",
    )(c_pad, w_ada, b_ada.reshape(depth, 6, 1, d))
    return out.transpose(0, 2, 1, 3)


def _token_parts(geom, x, tm):
    parts = tuple(p for p in x if p.shape[0]) if isinstance(x, tuple) else (x,)
    d = parts[0].shape[1]
    if len(parts) == 1:
        return parts, [pl.BlockSpec((tm, d), lambda i, *_: (i, 0))], None
    first = geom.tokens_p // tm
    specs = [pl.BlockSpec((tm, d), lambda i, *_: (jnp.minimum(i, first - 1), 0)),
             pl.BlockSpec((tm, d), lambda i, *_: (jnp.maximum(i - first, 0), 0))]
    return parts, specs, first


def _token_tile(x_refs, first):
    if len(x_refs) == 1:
        return x_refs[0][...]
    return jnp.where(pl.program_id(0) < first, x_refs[0][...], x_refs[1][...])


def _norm_proj_kernel(*refs, n_x, first, shift_row, scale_row):
    x_refs = refs[:n_x]
    mod_ref, g_ref, w_ref, o_ref, h_ref = refs[n_x:]

    @pl.when(pl.program_id(1) == 0)
    def _():
        h = _rms_mod(_token_tile(x_refs, first), g_ref[...], mod_ref[scale_row:scale_row + 1, :],
                     mod_ref[shift_row:shift_row + 1, :])
        h_ref[...] = h.astype(BF16)

    o_ref[...] = jnp.dot(h_ref[...], w_ref[...], preferred_element_type=F32).astype(o_ref.dtype)


def _norm_proj(geom, x, mod_l, gvec, w, *, shift_row, scale_row, tm, tn):
    parts, x_specs, first = _token_parts(geom, x, tm)
    t, d = geom.tokens, parts[0].shape[1]
    n = w.shape[1]
    return pl.pallas_call(
        functools.partial(_norm_proj_kernel, n_x=len(parts), first=first, shift_row=shift_row,
                          scale_row=scale_row),
        grid=(t // tm, n // tn),
        in_specs=x_specs + [
            pl.BlockSpec((None, 6, d), lambda i, j: (_tile_seq(geom, i, tm), 0, 0)),
            pl.BlockSpec((1, d), lambda i, j: (0, 0)),
            pl.BlockSpec((d, tn), lambda i, j: (0, j)),
        ],
        out_specs=pl.BlockSpec((tm, tn), lambda i, j: (i, j)),
        out_shape=jax.ShapeDtypeStruct((t, n), BF16),
        scratch_shapes=[pltpu.VMEM((tm, d), BF16)],
        compiler_params=_cparams(("parallel", "arbitrary")),
        name="norm_proj",
    )(*parts, mod_l, gvec, w)


def _rot_halves(ref, h, cos, sin):
    a = ref[:, h * RET_DK:h * RET_DK + RET_HALF].astype(F32)
    b = ref[:, h * RET_DK + RET_HALF:(h + 1) * RET_DK].astype(F32)
    return a * cos - b * sin, a * sin + b * cos


def _ret_bwd_state_kernel(k_ref, v_ref, cos_ref, sin_ref, kdb_ref, rb_ref, r_ref, *, geom, chunk, cdec):
    c = pl.num_programs(0) - 1 - pl.program_id(0)
    is_last = _tile_pos(geom, c, chunk) + chunk == _tile_seq_len(geom, c, chunk)

    @pl.when(is_last)
    def _():
        r_ref[...] = jnp.zeros_like(r_ref)

    cos = cos_ref[...]
    sin = sin_ref[...]
    for h in range(RET_HEADS):
        k1, k2 = _rot_halves(k_ref, h, cos, sin)
        dec = kdb_ref[h]
        kd = jnp.concatenate([k1 * dec, k2 * dec], axis=1).astype(BF16)
        vh = v_ref[:, h * RET_DV:(h + 1) * RET_DV].astype(BF16)
        r = r_ref[h]
        rb_ref[h] = r.astype(BF16)
        upd = lax.dot_general(kd, vh, (((0,), (0,)), ((), ())), preferred_element_type=F32)
        r_ref[h] = r * cdec[h] + upd


def _ret_main_kernel(q_ref, k_ref, v_ref, g_ref, rb_ref, cos_ref, sin_ref, dmat_ref, qdf_ref, qdb_ref,
                     kdf_ref, wo_ref, o_ref, rf_ref, *, geom, chunk, cdec):
    i = pl.program_id(0)

    @pl.when(_tile_pos(geom, i, chunk) == 0)
    def _():
        rf_ref[...] = jnp.zeros_like(rf_ref)

    cos = cos_ref[...]
    sin = sin_ref[...]
    acc = jnp.zeros(o_ref.shape, F32)
    for h in range(RET_HEADS):
        q1, q2 = _rot_halves(q_ref, h, cos, sin)
        k1, k2 = _rot_halves(k_ref, h, cos, sin)
        qr = jnp.concatenate([q1, q2], axis=1).astype(BF16)
        kr = jnp.concatenate([k1, k2], axis=1).astype(BF16)
        vh = v_ref[:, h * RET_DV:(h + 1) * RET_DV].astype(BF16)
        s = lax.dot_general(qr, kr, (((1,), (1,)), ((), ())), preferred_element_type=F32) * dmat_ref[h]
        o = jnp.dot(s.astype(BF16), vh, preferred_element_type=F32)
        qdf = qdf_ref[h]
        qf = jnp.concatenate([q1 * qdf, q2 * qdf], axis=1).astype(BF16)
        rf = rf_ref[h]
        o = o + jnp.dot(qf, rf.astype(BF16), preferred_element_type=F32)
        qdb = qdb_ref[h]
        qb = jnp.concatenate([q1 * qdb, q2 * qdb], axis=1).astype(BF16)
        o = o + jnp.dot(qb, rb_ref[h], preferred_element_type=F32)
        kdf = kdf_ref[h]
        kf = jnp.concatenate([k1 * kdf, k2 * kdf], axis=1).astype(BF16)
        rf_ref[h] = rf * cdec[h] + lax.dot_general(kf, vh, (((0,), (0,)), ((), ())),
                                                   preferred_element_type=F32)
        oc = o - jnp.mean(o, axis=-1, keepdims=True)
        on = oc * lax.rsqrt(jnp.mean(oc * oc, axis=-1, keepdims=True) + GN_EPS)
        og = (on * _silu(g_ref[:, h * RET_DV:(h + 1) * RET_DV].astype(F32))).astype(BF16)
        acc = acc + jnp.dot(og, wo_ref[h * RET_DV:(h + 1) * RET_DV, :], preferred_element_type=F32)
    o_ref[...] = acc.astype(o_ref.dtype)


def _retention_tables(chunk, s_max):
    log_gamma = jnp.log1p(-jnp.exp2(-5.0 - jnp.arange(RET_HEADS, dtype=F32)))
    idx = jnp.arange(chunk, dtype=F32)
    dist = jnp.abs(idx[:, None] - idx[None, :])
    dmat = jnp.exp(log_gamma[:, None, None] * dist[None])

    def rows(e):
        return jnp.broadcast_to(jnp.exp(log_gamma[:, None] * e[None, :])[:, :, None],
                                (RET_HEADS, chunk, RET_HALF))

    qdf = rows(idx + 1.0)
    qdb = rows(chunk - idx)
    kdf = rows(chunk - 1.0 - idx)
    kdb = rows(idx)
    theta = 1.0 / (ROPE_BASE ** jnp.linspace(0.0, 1.0, RET_HALF, dtype=F32))
    ang = jnp.arange(s_max, dtype=F32)[:, None] * theta[None, :]
    k_scale = RET_DK ** -0.5
    return jnp.cos(ang), jnp.sin(ang), dmat * k_scale, qdf, qdb, kdf * k_scale, kdb * k_scale


def _chunk_decay(chunk):
    lg = np.log1p(-np.exp2(-5.0 - np.arange(RET_HEADS, dtype=np.float32))).astype(np.float32)
    return tuple(float(v) for v in np.exp(lg * np.float32(chunk)).astype(np.float32))


def _retention(geom, proj_a, w_ret_o, tables, *, chunk):
    t = proj_a.shape[0]
    n_chunks = t // chunk
    cos, sin, dmat, qdf, qdb, kdf, kdb = tables
    cdec = _chunk_decay(chunk)
    state_shape = (RET_HEADS, RET_DK, RET_DV)

    def pos_blk(c):
        return _tile_pos(geom, c, chunk) // chunk

    rev = lambda i: n_chunks - 1 - i
    tab_spec = pl.BlockSpec((RET_HEADS, chunk, RET_HALF), lambda i: (0, 0, 0))
    rb = pl.pallas_call(
        functools.partial(_ret_bwd_state_kernel, geom=geom, chunk=chunk, cdec=cdec),
        grid=(n_chunks,),
        in_specs=[
            pl.BlockSpec((chunk, RET_QK_W), lambda i: (rev(i), 1)),
            pl.BlockSpec((chunk, RET_V_W), lambda i: (rev(i), 1)),
            pl.BlockSpec((chunk, RET_HALF), lambda i: (pos_blk(rev(i)), 0)),
            pl.BlockSpec((chunk, RET_HALF), lambda i: (pos_blk(rev(i)), 0)),
            tab_spec,
        ],
        out_specs=pl.BlockSpec((None,) + state_shape, lambda i: (rev(i), 0, 0, 0)),
        out_shape=jax.ShapeDtypeStruct((n_chunks,) + state_shape, BF16),
        scratch_shapes=[pltpu.VMEM(state_shape, F32)],
        compiler_params=_cparams(("arbitrary",)),
        name="ret_bwd_state",
    )(proj_a, proj_a, cos, sin, kdb)

    return pl.pallas_call(
        functools.partial(_ret_main_kernel, geom=geom, chunk=chunk, cdec=cdec),
        grid=(n_chunks,),
        in_specs=[
            pl.BlockSpec((chunk, RET_QK_W), lambda i: (i, 0)),
            pl.BlockSpec((chunk, RET_QK_W), lambda i: (i, 1)),
            pl.BlockSpec((chunk, RET_V_W), lambda i: (i, 1)),
            pl.BlockSpec((chunk, RET_V_W), lambda i: (i, 2)),
            pl.BlockSpec((None,) + state_shape, lambda i: (i, 0, 0, 0)),
            pl.BlockSpec((chunk, RET_HALF), lambda i: (pos_blk(i), 0)),
            pl.BlockSpec((chunk, RET_HALF), lambda i: (pos_blk(i), 0)),
            pl.BlockSpec((RET_HEADS, chunk, chunk), lambda i: (0, 0, 0)),
            tab_spec, tab_spec, tab_spec,
            pl.BlockSpec((RET_V_W, D_MODEL), lambda i: (0, 0)),
        ],
        out_specs=pl.BlockSpec((chunk, D_MODEL), lambda i: (i, 0)),
        out_shape=jax.ShapeDtypeStruct((t, D_MODEL), BF16),
        scratch_shapes=[pltpu.VMEM(state_shape, F32)],
        compiler_params=_cparams(("arbitrary",)),
        name="ret_main",
    )(proj_a, proj_a, proj_a, proj_a, rb, cos, sin, dmat, qdf, qdb, kdf, w_ret_o)


def _log_sigmoid(x):
    return jnp.minimum(x, 0.0) - jnp.log1p(jnp.exp(-jnp.abs(x)))


LRU_SEGS = V7X_SUBLANES
LRU_LEAD = CONV_LEFT * LRU_SEGS
LRU_TAIL = (CONV_WIDTH - 1 - CONV_LEFT) * LRU_SEGS


def _lru_load_segment_major(x_ref, prev_ref, next_ref, keep_prev, keep_next, xs_ref):
    tl = x_ref.shape[0]
    seg = tl // LRU_SEGS
    x = x_ref[...].astype(F32)
    prev_hi = prev_ref[...].astype(F32)[LRU_HALO - V7X_SUBLANES:] * keep_prev
    next_lo = next_ref[...].astype(F32)[:V7X_SUBLANES] * keep_next
    row8 = lax.broadcasted_iota(jnp.int32, (V7X_SUBLANES, LRU_BW), 0)
    last = V7X_SUBLANES - 1
    for n in range(LRU_BLOCKS):
        sl = slice(n * LRU_BW, (n + 1) * LRU_BW)
        for s in range(LRU_SEGS):
            xs_ref[n, pl.ds(LRU_LEAD + s, seg, stride=LRU_SEGS), :] = x[s * seg:(s + 1) * seg, sl]
        slab = lambda t: xs_ref[n, LRU_LEAD + t * LRU_SEGS:LRU_LEAD + (t + 1) * LRU_SEGS, :]
        p = prev_hi[:, sl]
        m1 = pltpu.roll(jnp.where(row8 == last, p, slab(seg - 1)), 1, axis=0)
        m2 = pltpu.roll(jnp.where(row8 == last, pltpu.roll(p, 1, axis=0), slab(seg - 2)), 1, axis=0)
        p1 = pltpu.roll(jnp.where(row8 == 0, next_lo[:, sl], slab(0)), last, axis=0)
        xs_ref[n, 0:LRU_SEGS, :] = m2
        xs_ref[n, LRU_SEGS:LRU_LEAD, :] = m1
        xs_ref[n, LRU_LEAD + tl:LRU_LEAD + tl + LRU_TAIL, :] = p1


def _lru_gates(xs_ref, d, wconv_ref, bconv_ref, wrg_ref, brg_ref, wig_ref, big_ref, lam_ref, a_ref, u_ref):
    tl = a_ref.shape[1]
    c = (0.5 * LRU_C * math.log2(math.e)) * _log_sigmoid(lam_ref[d])
    for n in range(LRU_BLOCKS):
        sl = slice(n * LRU_BW, (n + 1) * LRU_BW)
        xh = bconv_ref[:, sl] + xs_ref[n, 0:tl, :] * wconv_ref[0:1, sl]
        for j in range(1, CONV_WIDTH):
            xh = xh + xs_ref[n, j * LRU_SEGS:j * LRU_SEGS + tl, :] * wconv_ref[j:j + 1, sl]
        xb = xh.astype(BF16)
        th_r = jnp.tanh(jnp.dot(xb, wrg_ref[d, n], preferred_element_type=F32) + brg_ref[d][:, sl])
        th_i = jnp.tanh(jnp.dot(xb, wig_ref[d, n], preferred_element_type=F32) + big_ref[d][:, sl])
        a = jnp.exp2(c[:, sl] * th_r + c[:, sl])
        a_ref[n] = a
        y = 1.0 - a * a
        u_ref[n] = (y * lax.rsqrt(jnp.maximum(y, SQRT_TINY))) * ((th_i + 1.0) * xh)


def _lru_scan_segments(a_ref, u_ref, hs_ref, as_ref, fin_ref, carry_ref, o_ref, reverse):
    tl = a_ref.shape[1]
    seg = tl // LRU_SEGS
    slab_shape = (LRU_SEGS, LRU_BW)

    def step(k, carry):
        t = seg - 1 - k if reverse else k
        rows = pl.ds(pl.multiple_of(t * LRU_SEGS, LRU_SEGS), LRU_SEGS)
        hs, decays = [], []
        for n in range(LRU_BLOCKS):
            a = a_ref[n, rows, :]
            h = a * carry[0][n] + u_ref[n, rows, :]
            decay = a * carry[1][n]
            hs_ref[n, rows, :] = h
            as_ref[n, rows, :] = decay
            hs.append(h)
            decays.append(decay)
        return tuple(hs), tuple(decays)

    init = (tuple(jnp.zeros(slab_shape, F32) for _ in range(LRU_BLOCKS)),
            tuple(jnp.ones(slab_shape, F32) for _ in range(LRU_BLOCKS)))
    h_end, decay_end = lax.fori_loop(0, seg, step, init, unroll=4)

    order = range(LRU_SEGS - 1, -1, -1) if reverse else range(LRU_SEGS)
    for n in range(LRU_BLOCKS):
        sl = slice(n * LRU_BW, (n + 1) * LRU_BW)
        fin_ref[0] = h_end[n]
        fin_ref[1] = decay_end[n]
        c = carry_ref[n, 0:1, :]
        for s in order:
            fin_ref[2, s:s + 1, :] = c
            c = fin_ref[0, s:s + 1, :] + fin_ref[1, s:s + 1, :] * c
        carry_ref[n, 0:1, :] = c
        for s in range(LRU_SEGS):
            seg_rows = pl.ds(s, seg, stride=LRU_SEGS)
            h = hs_ref[n, seg_rows, :] + as_ref[n, seg_rows, :] * fin_ref[2, s:s + 1, :]
            o_ref[s * seg:(s + 1) * seg, sl] = h.astype(o_ref.dtype)


def _lru_kernel(xf_ref, xfp_ref, xfn_ref, xb_ref, xbp_ref, xbn_ref, wconv_ref, bconv_ref, wrg_ref, brg_ref,
                wig_ref, big_ref, lam_ref, hf_ref, hb_ref, xs_ref, a_ref, u_ref, hs_ref, as_ref, fin_ref,
                cf_ref, cb_ref, *, geom, tl):
    i = pl.program_id(0)
    ib = pl.num_programs(0) - 1 - i

    def flags(tile):
        pos = _tile_pos(geom, tile, tl)
        return pos == 0, pos + tl == _tile_seq_len(geom, tile, tl)

    def keep(flag):
        return jnp.where(flag, 0.0, 1.0)

    def direction(d, x_ref, prev_ref, next_ref, first, last, carry_ref, o_ref):
        _lru_load_segment_major(x_ref, prev_ref, next_ref, keep(first), keep(last), xs_ref)
        _lru_gates(xs_ref, d, wconv_ref, bconv_ref, wrg_ref, brg_ref, wig_ref, big_ref, lam_ref, a_ref, u_ref)
        _lru_scan_segments(a_ref, u_ref, hs_ref, as_ref, fin_ref, carry_ref, o_ref, reverse=(d == 1))

    first, last = flags(i)

    @pl.when(first)
    def _():
        cf_ref[...] = jnp.zeros_like(cf_ref)

    direction(0, xf_ref, xfp_ref, xfn_ref, first, last, cf_ref, hf_ref)

    first, last = flags(ib)

    @pl.when(last)
    def _():
        cb_ref[...] = jnp.zeros_like(cb_ref)

    direction(1, xb_ref, xbp_ref, xbn_ref, first, last, cb_ref, hb_ref)


def _lru(geom, proj_b, w_conv, b_conv, w_rg, b_rg, w_ig, b_ig, lam, *, tl):
    t = proj_b.shape[0]
    n = t // tl
    r8 = tl // LRU_HALO
    n8 = t // LRU_HALO
    w = LRU_WIDTH
    rev = lambda i: n - 1 - i
    prev_blk = lambda i: jnp.maximum(i * r8 - 1, 0)
    next_blk = lambda i: jnp.minimum((i + 1) * r8, n8 - 1)
    full = lambda a: pl.BlockSpec(a.shape, lambda i: (0,) * a.ndim)
    w_conv = 0.5 * w_conv
    b_conv2 = 0.5 * b_conv.reshape(1, w)
    b_rg3 = 0.5 * b_rg.reshape(2, 1, w)
    b_ig3 = 0.5 * b_ig.reshape(2, 1, w)
    lam3 = lam.reshape(2, 1, w)
    return pl.pallas_call(
        functools.partial(_lru_kernel, geom=geom, tl=tl),
        grid=(n,),
        in_specs=[
            pl.BlockSpec((tl, w), lambda i: (i, 0)),
            pl.BlockSpec((LRU_HALO, w), lambda i: (prev_blk(i), 0)),
            pl.BlockSpec((LRU_HALO, w), lambda i: (next_blk(i), 0)),
            pl.BlockSpec((tl, w), lambda i: (rev(i), 0)),
            pl.BlockSpec((LRU_HALO, w), lambda i: (prev_blk(rev(i)), 0)),
            pl.BlockSpec((LRU_HALO, w), lambda i: (next_blk(rev(i)), 0)),
            full(w_conv), full(b_conv2), full(w_rg), full(b_rg3), full(w_ig), full(b_ig3), full(lam3),
        ],
        out_specs=[
            pl.BlockSpec((tl, w), lambda i: (i, 0)),
            pl.BlockSpec((tl, w), lambda i: (rev(i), 0)),
        ],
        out_shape=[jax.ShapeDtypeStruct((t, w), BF16), jax.ShapeDtypeStruct((t, w), BF16)],
        scratch_shapes=[pltpu.VMEM((LRU_BLOCKS, LRU_LEAD + tl + LRU_TAIL, LRU_BW), F32)]
        + [pltpu.VMEM((LRU_BLOCKS, tl, LRU_BW), F32)] * 4
        + [pltpu.VMEM((3, LRU_SEGS, LRU_BW), F32)]
        + [pltpu.VMEM((LRU_BLOCKS, V7X_SUBLANES, LRU_BW), F32)] * 2,
        compiler_params=_cparams(("arbitrary",)),
        name="lru_scan",
    )(proj_b, proj_b, proj_b, proj_b, proj_b, proj_b, w_conv, b_conv2, w_rg, b_rg3, w_ig, b_ig3, lam3)


def _gelu_tanh(x):
    return 0.5 * x * (1.0 + jnp.tanh(math.sqrt(2.0 / math.pi) * (x + 0.044715 * (x * x * x))))


def _merge_kernel(*refs, n_x, first):
    x_refs = refs[:n_x]
    ya_ref, hf_ref, hb_ref, gl_ref, ga_ref, gb_ref, mod_ref, wl_ref, wo_ref, o_ref = refs[n_x:]
    f32 = lambda ref: ref[...].astype(F32)
    y = ((f32(hf_ref) + f32(hb_ref)) * _gelu_tanh(f32(gl_ref))).astype(BF16)
    yb = jnp.dot(y, wl_ref[...], preferred_element_type=F32)
    m = _sigmoid(f32(ga_ref)) * f32(ya_ref) + _sigmoid(f32(gb_ref)) * yb
    mix = jnp.dot(m.astype(BF16), wo_ref[...], preferred_element_type=F32)
    o_ref[...] = _token_tile(x_refs, first) + mod_ref[2:3, :] * mix


def _merge(geom, x, ya, hf, hb, proj_a, proj_b, mod_l, w_lru_o, w_out, *, tm):
    parts, x_specs, first = _token_parts(geom, x, tm)
    t, d = geom.tokens, parts[0].shape[1]
    gate_a_blk = (2 * RET_QK_W + 2 * RET_V_W) // d
    tok = lambda w: pl.BlockSpec((tm, w), lambda i: (i, 0))
    return pl.pallas_call(
        functools.partial(_merge_kernel, n_x=len(parts), first=first),
        grid=(t // tm,),
        in_specs=x_specs + [
            tok(d), tok(LRU_WIDTH), tok(LRU_WIDTH),
            pl.BlockSpec((tm, LRU_WIDTH), lambda i: (i, 1)),
            pl.BlockSpec((tm, d), lambda i: (i, gate_a_blk)),
            pl.BlockSpec((tm, d), lambda i: (i, gate_a_blk + 1)),
            pl.BlockSpec((None, 6, d), lambda i: (_tile_seq(geom, i, tm), 0, 0)),
            pl.BlockSpec((LRU_WIDTH, d), lambda i: (0, 0)),
            pl.BlockSpec((d, d), lambda i: (0, 0)),
        ],
        out_specs=tok(d),
        out_shape=jax.ShapeDtypeStruct((t, d), F32),
        compiler_params=_cparams(("parallel",)),
        name="merge",
    )(*parts, ya, hf, hb, proj_b, proj_a, proj_a, mod_l, w_lru_o, w_out)


def _ffn_kernel(x_ref, mod_ref, g_ref, wg_ref, wu_ref, wd_ref, o_ref, *, col_chunks):
    x = x_ref[...]
    h = _rms_mod(x, g_ref[...], mod_ref[4:5, :], mod_ref[3:4, :]).astype(BF16)
    acc = None
    for c0, c1 in col_chunks:
        a = jnp.dot(h, wg_ref[:, c0:c1], preferred_element_type=F32)
        u = jnp.dot(h, wu_ref[:, c0:c1], preferred_element_type=F32)
        part = jnp.dot((_silu(a) * u).astype(BF16), wd_ref[c0:c1, :], preferred_element_type=F32)
        acc = part if acc is None else acc + part
    o_ref[...] = x + mod_ref[5:6, :] * acc


def _ffn(geom, x, mod_l, gvec, wg, wu, wd, *, tm, tf):
    t, d = x.shape
    ff = wg.shape[1]
    col_chunks = tuple((c, min(c + tf, ff)) for c in range(0, ff, tf))
    resident = lambda shape: pl.BlockSpec(shape, lambda i: (0, 0), pipeline_mode=pl.Buffered(1))
    return pl.pallas_call(
        functools.partial(_ffn_kernel, col_chunks=col_chunks),
        grid=(t // tm,),
        in_specs=[
            pl.BlockSpec((tm, d), lambda i: (i, 0)),
            pl.BlockSpec((None, 6, d), lambda i: (_tile_seq(geom, i, tm), 0, 0)),
            pl.BlockSpec((1, d), lambda i: (0, 0)),
            resident((d, ff)), resident((d, ff)), resident((ff, d)),
        ],
        out_specs=pl.BlockSpec((tm, d), lambda i: (i, 0)),
        out_shape=jax.ShapeDtypeStruct((t, d), F32),
        compiler_params=_cparams(("parallel",)),
        name="ffn_dense",
    )(x, mod_l, gvec, wg, wu, wd)


ROUTE_E0, ROUTE_E1, ROUTE_R0, ROUTE_R1, ROUTE_G0, ROUTE_G1 = range(6)

ROW_TILES = D_MODEL // V7X_LANES


def _store_token_rows(ref, val):
    rows = val.shape[0]
    for j in range(ROW_TILES):
        ref[pl.ds(j, rows, stride=ROW_TILES), :] = val[:, j * V7X_LANES:(j + 1) * V7X_LANES]


def _load_token_rows(ref, rows):
    return jnp.concatenate([ref[pl.ds(j, rows, stride=ROW_TILES), :] for j in range(ROW_TILES)], axis=1)


def _router_kernel(x_ref, mod_ref, g_ref, wr_ref, h_ref, route_ref, cnt_ref, carry_ref):
    i = pl.program_id(0)
    tm = x_ref.shape[0]

    @pl.when(i == 0)
    def _():
        carry_ref[...] = jnp.zeros_like(carry_ref)

    h = _rms_mod(x_ref[...], g_ref[...], mod_ref[4:5, :], mod_ref[3:4, :])
    _store_token_rows(h_ref, h)
    h_hi = h.astype(BF16)
    h_lo = (h - h_hi.astype(F32)).astype(BF16)
    logits = (jnp.dot(h_hi, wr_ref[0], preferred_element_type=F32)
              + jnp.dot(h_lo, wr_ref[0], preferred_element_type=F32)
              + jnp.dot(h_hi, wr_ref[1], preferred_element_type=F32))
    lane = lax.broadcasted_iota(jnp.int32, logits.shape, 1).astype(F32)
    logits = jnp.where(lane < N_EXPERTS, logits, -jnp.inf)
    m1 = jnp.max(logits, axis=-1, keepdims=True)
    i1 = jnp.min(jnp.where(logits == m1, lane, float(V7X_LANES)), axis=-1, keepdims=True)
    rest = jnp.where(lane == i1, -jnp.inf, logits)
    m2 = jnp.max(rest, axis=-1, keepdims=True)
    i2 = jnp.min(jnp.where(rest == m2, lane, float(V7X_LANES)), axis=-1, keepdims=True)
    ex = jnp.exp(m2 - m1)
    g1 = 1.0 / (1.0 + ex)
    g2 = ex / (1.0 + ex)
    sel1 = lane == i1
    sel2 = lane == i2
    onehot = jnp.where(sel1 | sel2, 1.0, 0.0)
    row = lax.broadcasted_iota(jnp.int32, (tm, tm), 0)
    col = lax.broadcasted_iota(jnp.int32, (tm, tm), 1)
    lower = jnp.where(col < row, 1.0, 0.0).astype(BF16)
    before = jnp.dot(lower, onehot.astype(BF16), preferred_element_type=F32) + carry_ref[0:1, :]
    r1 = jnp.sum(jnp.where(sel1, before, 0.0), axis=-1, keepdims=True)
    r2 = jnp.sum(jnp.where(sel2, before, 0.0), axis=-1, keepdims=True)
    out_lane = lax.broadcasted_iota(jnp.int32, route_ref.shape, 1)
    vals = (i1.astype(F32), i2.astype(F32), r1, r2, g1, g2)
    packed = jnp.zeros(route_ref.shape, F32)
    for slot, v in enumerate(vals):
        packed = jnp.where(out_lane == slot, v, packed)
    route_ref[...] = packed
    carry = carry_ref[0:1, :] + jnp.sum(onehot, axis=0, keepdims=True)
    carry_ref[0:1, :] = carry
    cnt_ref[...] = jnp.broadcast_to(carry, cnt_ref.shape).astype(jnp.int32)


def _router(geom, x, mod_l, gvec, w_router, *, tm):
    t, d = x.shape
    w_pad = jnp.pad(w_router, ((0, 0), (0, V7X_LANES - N_EXPERTS)))
    w_hi = w_pad.astype(BF16)
    w_pad = jnp.stack([w_hi, (w_pad - w_hi.astype(F32)).astype(BF16)])
    return pl.pallas_call(
        _router_kernel,
        grid=(t // tm,),
        in_specs=[
            pl.BlockSpec((tm, d), lambda i: (i, 0)),
            pl.BlockSpec((None, 6, d), lambda i: (_tile_seq(geom, i, tm), 0, 0)),
            pl.BlockSpec((1, d), lambda i: (0, 0)),
            pl.BlockSpec((2, d, V7X_LANES), lambda i: (0, 0, 0)),
        ],
        out_specs=[
            pl.BlockSpec((tm * ROW_TILES, V7X_LANES), lambda i: (i, 0)),
            pl.BlockSpec((tm, V7X_LANES), lambda i: (i, 0)),
            pl.BlockSpec((V7X_SUBLANES, V7X_LANES), lambda i: (0, 0)),
        ],
        out_shape=[
            jax.ShapeDtypeStruct((t * ROW_TILES, V7X_LANES), F32),
            jax.ShapeDtypeStruct((t, V7X_LANES), F32),
            jax.ShapeDtypeStruct((V7X_SUBLANES, V7X_LANES), jnp.int32),
        ],
        scratch_shapes=[pltpu.VMEM((V7X_SUBLANES, V7X_LANES), F32)],
        compiler_params=_cparams(("arbitrary",)),
        name="router",
    )(x, mod_l, gvec, w_pad)


def _token_copy(src, src_tok, dst, dst_tok, sem):
    s = pl.multiple_of(src_tok * ROW_TILES, ROW_TILES)
    d = pl.multiple_of(dst_tok * ROW_TILES, ROW_TILES)
    return pltpu.make_async_copy(src.at[pl.ds(s, ROW_TILES), :], dst.at[pl.ds(d, ROW_TILES), :], sem)


def _dispatch_kernel(fill_ref, dest_ref, src_ref, out_hbm, zero_ref, sem, zero_sem, *, td):
    @pl.when(pl.program_id(0) == 0)
    def _():
        zero_ref[...] = jnp.zeros_like(zero_ref)
        for wait in (False, True):
            for e in range(N_EXPERTS):
                def pad_row(r, carry):
                    copy = _token_copy(zero_ref, 0, out_hbm, r, zero_sem)
                    copy.wait() if wait else copy.start()
                    return carry

                lax.fori_loop(fill_ref[0, e], fill_ref[1, e], pad_row, 0)

    def issue(t, carry):
        for k in range(TOP_K):
            _token_copy(src_ref, t, out_hbm, dest_ref[k, t], sem).start(priority=k % DMA_PRIORITIES)
        return carry

    lax.fori_loop(0, td, issue, 0, unroll=8)
    for _ in range(TOP_K):
        pltpu.make_async_copy(src_ref, out_hbm.at[pl.ds(0, td * ROW_TILES), :], sem).wait()


def _dispatch(h2_rows, dest, fill, n_rows, *, td):
    grid_spec = pltpu.PrefetchScalarGridSpec(
        num_scalar_prefetch=1,
        grid=(dest.shape[1] // td,),
        in_specs=[
            pl.BlockSpec((TOP_K, td), lambda i, fill: (0, i), memory_space=pltpu.SMEM),
            pl.BlockSpec((td * ROW_TILES, V7X_LANES), lambda i, fill: (i, 0)),
        ],
        out_specs=pl.BlockSpec(memory_space=pl.ANY),
        scratch_shapes=[pltpu.VMEM((ROW_TILES, V7X_LANES), h2_rows.dtype),
                        pltpu.SemaphoreType.DMA(()), pltpu.SemaphoreType.DMA(())],
    )
    return pl.pallas_call(
        functools.partial(_dispatch_kernel, td=td),
        grid_spec=grid_spec,
        out_shape=jax.ShapeDtypeStruct((n_rows * ROW_TILES, V7X_LANES), h2_rows.dtype),
        compiler_params=_cparams(("arbitrary",)),
        name="moe_dispatch",
    )(fill, dest, h2_rows)


def _experts_kernel(blk_e_ref, nused_ref, x_ref, wg_ref, wu_ref, wd_ref, o_ref, xb_ref, acc_ref):
    del blk_e_ref
    b = pl.program_id(0)
    j = pl.program_id(1)
    bm = xb_ref.shape[0]

    @pl.when(b < nused_ref[0])
    def _():
        @pl.when(j == 0)
        def _():
            xb_ref[...] = _load_token_rows(x_ref, bm).astype(BF16)
            acc_ref[...] = jnp.zeros_like(acc_ref)

        xb = xb_ref[...]
        a = jnp.dot(xb, wg_ref[...], preferred_element_type=F32)
        u = jnp.dot(xb, wu_ref[...], preferred_element_type=F32)
        acc_ref[...] += jnp.dot((_silu(a) * u).astype(BF16), wd_ref[...], preferred_element_type=F32)

        @pl.when(j == pl.num_programs(1) - 1)
        def _():
            _store_token_rows(o_ref, acc_ref[...])

    @pl.when((b >= nused_ref[0]) & (j == pl.num_programs(1) - 1))
    def _():
        o_ref[...] = jnp.zeros_like(o_ref)


def _experts(xbuf, blk_e, n_used, wg, wu, wd, *, bm, tf):
    p = xbuf.shape[0] // ROW_TILES
    d = D_MODEL
    ff = wg.shape[2]
    nj = ff // tf
    rows_blk = (bm * ROW_TILES, V7X_LANES)

    def blk(b, nu):
        return jnp.minimum(b, nu[0] - 1)

    def ffc(b, j, nu):
        return jnp.where(b < nu[0], j, nj - 1)

    grid_spec = pltpu.PrefetchScalarGridSpec(
        num_scalar_prefetch=2,
        grid=(p // bm, nj),
        in_specs=[
            pl.BlockSpec(rows_blk, lambda b, j, be, nu: (blk(b, nu), 0)),
            pl.BlockSpec((None, d, tf), lambda b, j, be, nu: (be[blk(b, nu)], 0, ffc(b, j, nu))),
            pl.BlockSpec((None, d, tf), lambda b, j, be, nu: (be[blk(b, nu)], 0, ffc(b, j, nu))),
            pl.BlockSpec((None, tf, d), lambda b, j, be, nu: (be[blk(b, nu)], ffc(b, j, nu), 0)),
        ],
        out_specs=pl.BlockSpec(rows_blk, lambda b, j, be, nu: (b, 0)),
        scratch_shapes=[pltpu.VMEM((bm, d), BF16), pltpu.VMEM((bm, d), F32)],
    )
    return pl.pallas_call(
        _experts_kernel,
        grid_spec=grid_spec,
        out_shape=jax.ShapeDtypeStruct(xbuf.shape, F32),
        compiler_params=_cparams(("arbitrary", "arbitrary")),
        name="moe_experts",
    )(blk_e, n_used, xbuf, wg, wu, wd)


def _moe_out_kernel(dest_ref, dest_next_ref, x_ref, route_ref, mod_ref, gf_ref, ybuf_hbm, op_ref, os_ref,
                    ya0_ref, ya1_ref, yb0_ref, yb1_ref, sems, *, final_norm, prompt_tiles):
    i = pl.program_id(0)
    tm = x_ref.shape[0]
    bufs = ((ya0_ref, ya1_ref), (yb0_ref, yb1_ref))

    def gather(d_ref, slot, inline=False):
        def issue(t, carry):
            for k in range(TOP_K):
                _token_copy(ybuf_hbm, d_ref[k, t], bufs[slot][k], t, sems.at[slot]).start(
                    priority=k % DMA_PRIORITIES)
            return carry

        if inline:
            for t in range(tm):
                issue(t, 0)
        else:
            lax.fori_loop(0, tm, issue, 0, unroll=8)

    def wait(slot):
        for k in range(TOP_K):
            pltpu.make_async_copy(ybuf_hbm.at[pl.ds(0, tm * ROW_TILES), :], bufs[slot][k], sems.at[slot]).wait()

    def combine(slot):
        g1 = route_ref[:, ROUTE_G0:ROUTE_G0 + 1]
        g2 = route_ref[:, ROUTE_G1:ROUTE_G1 + 1]
        y = _load_token_rows(bufs[slot][0], tm) * g1 + _load_token_rows(bufs[slot][1], tm) * g2
        x = x_ref[...] + mod_ref[5:6, :] * y
        if final_norm:
            ms = jnp.mean(x * x, axis=-1, keepdims=True)
            x = x * lax.rsqrt(ms + RMS_EPS) * gf_ref[...]

        @pl.when(i < prompt_tiles)
        def _():
            op_ref[...] = x

        @pl.when(i >= prompt_tiles)
        def _():
            os_ref[...] = x

    @pl.when(i == 0)
    def _():
        gather(dest_ref, 0)

    for slot in range(2):
        @pl.when(i % 2 == slot)
        def _():
            wait(slot)
            gather(dest_next_ref, 1 - slot, inline=True)
            combine(slot)

            @pl.when(i + 1 == pl.num_programs(0))
            def _():
                wait(1 - slot)


def _moe_out(geom, x, ybuf, dest, route, mod_l, g_final, *, tm, final_norm):
    t, d = x.shape
    n = t // tm
    prompt_tiles = geom.tokens_p // tm
    y_buf = pltpu.VMEM((tm * ROW_TILES, V7X_LANES), F32)
    return pl.pallas_call(
        functools.partial(_moe_out_kernel, final_norm=final_norm, prompt_tiles=prompt_tiles),
        grid=(n,),
        in_specs=[
            pl.BlockSpec((TOP_K, tm), lambda i: (0, i), memory_space=pltpu.SMEM),
            pl.BlockSpec((TOP_K, tm), lambda i: (0, jnp.minimum(i + 1, n - 1)), memory_space=pltpu.SMEM),
            pl.BlockSpec((tm, d), lambda i: (i, 0)),
            pl.BlockSpec((tm, V7X_LANES), lambda i: (i, 0)),
            pl.BlockSpec((None, 6, d), lambda i: (_tile_seq(geom, i, tm), 0, 0)),
            pl.BlockSpec((1, d), lambda i: (0, 0)),
            pl.BlockSpec(memory_space=pl.ANY),
        ],
        out_specs=[
            pl.BlockSpec((tm, d), lambda i: (jnp.minimum(i, prompt_tiles - 1), 0)),
            pl.BlockSpec((tm, d), lambda i: (jnp.maximum(i - prompt_tiles, 0), 0)),
        ],
        out_shape=[jax.ShapeDtypeStruct((geom.tokens_p, d), F32),
                   jax.ShapeDtypeStruct((t - geom.tokens_p, d), F32)],
        scratch_shapes=[y_buf, y_buf, y_buf, y_buf, pltpu.SemaphoreType.DMA((2,))],
        compiler_params=_cparams(("arbitrary",)),
        name="moe_out",
    )(dest, dest, x, route, mod_l, g_final, ybuf)


def _final_norm_kernel(x_ref, g_ref, o_ref):
    x = x_ref[...]
    o_ref[...] = x * lax.rsqrt(jnp.mean(x * x, axis=-1, keepdims=True) + RMS_EPS) * g_ref[...]


def _final_norm(x, g_final, *, tm):
    t, d = x.shape
    return pl.pallas_call(
        _final_norm_kernel,
        grid=(t // tm,),
        in_specs=[pl.BlockSpec((tm, d), lambda i: (i, 0)), pl.BlockSpec((1, d), lambda i: (0, 0))],
        out_specs=pl.BlockSpec((tm, d), lambda i: (i, 0)),
        out_shape=jax.ShapeDtypeStruct((t, d), F32),
        compiler_params=_cparams(("parallel",)),
        name="final_norm",
    )(x, g_final)


def _moe(geom, x, mod_l, gvec, w_router, wg, wu, wd, g_final, *, tiles, final_norm):
    t, d = x.shape
    bm = tiles["moe_rows"]
    h2, route, counts = _router(geom, x, mod_l, gvec, w_router, tm=tiles["router"])
    counts = counts[0, :N_EXPERTS]
    padded = ((counts + bm - 1) // bm) * bm
    pad_end = jnp.cumsum(padded)
    pad_start = pad_end - padded
    experts = route[:, ROUTE_E0:ROUTE_E1 + 1].astype(jnp.int32)
    ranks = route[:, ROUTE_R0:ROUTE_R1 + 1].astype(jnp.int32)
    dest = (pad_start[experts] + ranks).T
    n_rows = t * TOP_K + N_EXPERTS * bm
    n_blk = n_rows // bm
    blk_start = jnp.arange(n_blk, dtype=jnp.int32) * bm
    blk_e = jnp.minimum(jnp.sum((pad_end[None, :] <= blk_start[:, None]).astype(jnp.int32), axis=1),
                        N_EXPERTS - 1).astype(jnp.int32)
    n_used = (pad_end[-1:] // bm).astype(jnp.int32)
    fill = jnp.stack([pad_start + counts, pad_end.at[N_EXPERTS - 1].set(n_rows)]).astype(jnp.int32)
    xbuf = _dispatch(h2, dest, fill, n_rows, td=tiles["dma_rows"])
    ybuf = _experts(xbuf, blk_e, n_used, wg, wu, wd, bm=bm, tf=tiles["expert_ff"])
    return _moe_out(geom, x, ybuf, dest, route, mod_l, g_final, tm=tiles["token"], final_norm=final_norm)


def _pick_tiles(geom):
    s = math.gcd(geom.s_p, geom.s_s) if geom.n_p and geom.n_s else (geom.s_p if geom.n_p else geom.s_s)
    return {
        "proj": min(1024, s),
        "proj_cols": 2048,
        "ret_chunk": min(512, s),
        "lru": min(512, s),
        "token": min(512, s),
        "ffn_cols": 1536,
        "router": min(512, s),
        "moe_rows": 512,
        "expert_ff": 1792,
        "dma_rows": min(2048, s),
    }


def _trunk(geom, x, c_all, w_ada, b_ada, g_norm1, g_norm2, w_in, w_conv, b_conv, w_rg, b_rg, w_ig, b_ig,
           lru_lambda, w_ret_o, w_lru_o, w_out, w_ff_gate, w_ff_up, w_ff_down,
           w_router, w_e_gate, w_e_up, w_e_down, g_final, tiles):
    depth = w_in.shape[0]
    d = D_MODEL
    n_pad = -(-geom.n_seq // V7X_SUBLANES) * V7X_SUBLANES
    c_pad = jnp.pad(c_all, ((0, n_pad - geom.n_seq), (0, 0)))
    mod = _adaln_mod(c_pad, w_ada, b_ada)
    tables = _retention_tables(tiles["ret_chunk"], max(geom.s_p if geom.n_p else 0, geom.s_s if geom.n_s else 0))

    o_xl = 2 * RET_QK_W + 2 * RET_V_W
    o_ga = o_xl + 2 * LRU_WIDTH
    for l in range(depth):
        wl = w_in[l]
        w_a = jnp.concatenate([wl[:, :o_xl], wl[:, o_ga:]], axis=1).astype(BF16)
        w_b = wl[:, o_xl:o_ga].astype(BF16)
        g1 = g_norm1[l].reshape(1, d)
        g2 = g_norm2[l].reshape(1, d)
        proj_a = _norm_proj(geom, x, mod[l], g1, w_a, shift_row=0, scale_row=1,
                            tm=tiles["proj"], tn=tiles["proj_cols"])
        proj_b = _norm_proj(geom, x, mod[l], g1, w_b, shift_row=0, scale_row=1,
                            tm=tiles["proj"], tn=LRU_WIDTH)
        ya = _retention(geom, proj_a, w_ret_o[l].astype(BF16), tables, chunk=tiles["ret_chunk"])
        hf, hb = _lru(geom, proj_b, w_conv[l], b_conv[l], w_rg[l].astype(BF16), b_rg[l],
                      w_ig[l].astype(BF16), b_ig[l], lru_lambda[l], tl=tiles["lru"])
        x = _merge(geom, x, ya, hf, hb, proj_a, proj_b, mod[l], w_lru_o[l].astype(BF16),
                   w_out[l].astype(BF16), tm=tiles["token"])
        j = l // 2
        last = l == depth - 1
        if l % 2 == 0:
            x = _ffn(geom, x, mod[l], g2, w_ff_gate[j].astype(BF16), w_ff_up[j].astype(BF16),
                     w_ff_down[j].astype(BF16), tm=tiles["token"], tf=tiles["ffn_cols"])
            if last:
                x = _final_norm(x, g_final.reshape(1, d), tm=tiles["token"])
                return x[:geom.tokens_p], x[geom.tokens_p:]
        else:
            parts = _moe(geom, x, mod[l], g2, w_router[j], w_e_gate[j].astype(BF16), w_e_up[j].astype(BF16),
                         w_e_down[j].astype(BF16), g_final.reshape(1, d), tiles=tiles, final_norm=last)
            if last:
                return parts
            x = jnp.concatenate(parts, axis=0)


def kernel(x_prompt, x_sample, c_prompt, c_sample, w_ada, b_ada, g_norm1, g_norm2, w_in, w_conv, b_conv, w_rg, b_rg, w_ig, b_ig, lru_lambda, w_ret_o, w_lru_o, w_out, w_ff_gate, w_ff_up, w_ff_down, w_router, w_e_gate, w_e_up, w_e_down, g_final):
    n_p, s_p, d = x_prompt.shape
    n_s, s_s, _ = x_sample.shape
    geom = Geom(n_p, s_p, n_s, s_s)
    x = (x_prompt.reshape(-1, d), x_sample.reshape(-1, d))
    c_all = jnp.concatenate([c_prompt, c_sample], axis=0)
    yp, ys = _trunk(geom, x, c_all, w_ada, b_ada, g_norm1, g_norm2, w_in, w_conv, b_conv, w_rg, b_rg, w_ig, b_ig,
                    lru_lambda, w_ret_o, w_lru_o, w_out, w_ff_gate, w_ff_up, w_ff_down,
                    w_router, w_e_gate, w_e_up, w_e_down, g_final, _pick_tiles(geom))
    return (yp.reshape(n_p, s_p, d), ys.reshape(n_s, s_s, d))
```

```python
import functools
import math
from typing import NamedTuple

import jax
import jax.numpy as jnp
import numpy as np
from jax import lax
from jax.experimental import pallas as pl
from jax.experimental.pallas import tpu as pltpu

F32 = jnp.float32
BF16 = jnp.bfloat16

D_MODEL = 1024
RET_HEADS = 4
RET_DK = 256
RET_DV = 512
RET_HALF = RET_DK // 2
RET_QK_W = RET_HEADS * RET_DK
RET_V_W = RET_HEADS * RET_DV
ROPE_BASE = 10000.0
LRU_WIDTH = 1280
LRU_BLOCKS = 10
LRU_BW = LRU_WIDTH // LRU_BLOCKS
LRU_C = 8.0
CONV_WIDTH = 4
CONV_LEFT = 2
LRU_HALO = 16
N_EXPERTS = 8
TOP_K = 2
RMS_EPS = 1e-6
GN_EPS = 1e-5
SQRT_TINY = 1e-30

V7X_LANES = 128
V7X_SUBLANES = 8
V7X_VMEM_BYTES = 64 * 1024 * 1024
DMA_PRIORITIES = 2
VMEM_LIMIT = (V7X_VMEM_BYTES * 3) // 4

PA_W = 2 * RET_QK_W + 2 * RET_V_W + 2 * D_MODEL
PB_W = 2 * LRU_WIDTH


class Geom(NamedTuple):
    n_p: int
    s_p: int
    n_s: int
    s_s: int

    @property
    def tokens_p(self):
        return self.n_p * self.s_p

    @property
    def tokens(self):
        return self.n_p * self.s_p + self.n_s * self.s_s

    @property
    def n_seq(self):
        return self.n_p + self.n_s


def _tile_seq(g, i, tm):
    t0 = i * tm
    return jnp.where(t0 < g.tokens_p, t0 // g.s_p, g.n_p + (t0 - g.tokens_p) // g.s_s)


def _tile_pos(g, i, tm):
    t0 = i * tm
    return jnp.where(t0 < g.tokens_p, t0 % g.s_p, (t0 - g.tokens_p) % g.s_s)


def _tile_seq_len(g, i, tm):
    return jnp.where(i * tm < g.tokens_p, g.s_p, g.s_s)


def _cparams(sem):
    return pltpu.CompilerParams(dimension_semantics=sem, vmem_limit_bytes=VMEM_LIMIT)


def _sigmoid(x):
    return 0.5 * jnp.tanh(0.5 * x) + 0.5


def _silu(x):
    return x * _sigmoid(x)


def _rms_mod(x, gvec, scale, shift):
    ms = jnp.mean(x * x, axis=-1, keepdims=True)
    y = x * lax.rsqrt(ms + RMS_EPS) * gvec
    return y * (1.0 + scale) + shift


def _mod_kernel(c_ref, w_ref, b_ref, o_ref):
    c = c_ref[...]
    o_ref[...] = jnp.dot(_silu(c), w_ref[...], preferred_element_type=F32,
                         precision=lax.Precision.HIGHEST) + b_ref[...]


def _adaln_mod(c_pad, w_ada, b_ada):
    depth, d, _ = w_ada.shape
    n = c_pad.shape[0]
    out = pl.pallas_call(
        _mod_kernel,
        grid=(depth, 6),
        in_specs=[
            pl.BlockSpec((n, d), lambda l, j: (0, 0)),
            pl.BlockSpec((None, d, d), lambda l, j: (l, 0, j)),
            pl.BlockSpec((None, None, 1, d), lambda l, j: (l, j, 0, 0)),
        ],
        out_specs=pl.BlockSpec((None, None, n, d), lambda l, j: (l, j, 0, 0)),
        out_shape=jax.ShapeDtypeStruct((depth, 6, n, d), F32),
        compiler_params=_cparams(("arbitrary", "arbitrary")),
        name="adaln_mod",
    )(c_pad, w_ada, b_ada.reshape(depth, 6, 1, d))
    return out.transpose(0, 2, 1, 3)


def _token_parts(geom, x, tm):
    parts = tuple(p for p in x if p.shape[0]) if isinstance(x, tuple) else (x,)
    d = parts[0].shape[1]
    if len(parts) == 1:
        return parts, [pl.BlockSpec((tm, d), lambda i, *_: (i, 0))], None
    first = geom.tokens_p // tm
    specs = [pl.BlockSpec((tm, d), lambda i, *_: (jnp.minimum(i, first - 1), 0)),
             pl.BlockSpec((tm, d), lambda i, *_: (jnp.maximum(i - first, 0), 0))]
    return parts, specs, first


def _token_tile(x_refs, first):
    if len(x_refs) == 1:
        return x_refs[0][...]
    return jnp.where(pl.program_id(0) < first, x_refs[0][...], x_refs[1][...])


def _norm_proj_kernel(*refs, n_x, first, shift_row, scale_row):
    x_refs = refs[:n_x]
    mod_ref, g_ref, w_ref, o_ref, h_ref = refs[n_x:]

    @pl.when(pl.program_id(1) == 0)
    def _():
        h = _rms_mod(_token_tile(x_refs, first), g_ref[...], mod_ref[scale_row:scale_row + 1, :],
                     mod_ref[shift_row:shift_row + 1, :])
        h_ref[...] = h.astype(BF16)

    o_ref[...] = jnp.dot(h_ref[...], w_ref[...], preferred_element_type=F32).astype(o_ref.dtype)


def _norm_proj(geom, x, mod_l, gvec, w, *, shift_row, scale_row, tm, tn):
    parts, x_specs, first = _token_parts(geom, x, tm)
    t, d = geom.tokens, parts[0].shape[1]
    n = w.shape[1]
    return pl.pallas_call(
        functools.partial(_norm_proj_kernel, n_x=len(parts), first=first, shift_row=shift_row,
                          scale_row=scale_row),
        grid=(t // tm, n // tn),
        in_specs=x_specs + [
            pl.BlockSpec((None, 6, d), lambda i, j: (_tile_seq(geom, i, tm), 0, 0)),
            pl.BlockSpec((1, d), lambda i, j: (0, 0)),
            pl.BlockSpec((d, tn), lambda i, j: (0, j)),
        ],
        out_specs=pl.BlockSpec((tm, tn), lambda i, j: (i, j)),
        out_shape=jax.ShapeDtypeStruct((t, n), BF16),
        scratch_shapes=[pltpu.VMEM((tm, d), BF16)],
        compiler_params=_cparams(("parallel", "arbitrary")),
        name="norm_proj",
    )(*parts, mod_l, gvec, w)


def _rot_halves(ref, h, cos, sin):
    a = ref[:, h * RET_DK:h * RET_DK + RET_HALF].astype(F32)
    b = ref[:, h * RET_DK + RET_HALF:(h + 1) * RET_DK].astype(F32)
    return a * cos - b * sin, a * sin + b * cos


def _ret_bwd_state_kernel(k_ref, v_ref, cos_ref, sin_ref, kdb_ref, rb_ref, r_ref, *, geom, chunk, cdec):
    c = pl.num_programs(0) - 1 - pl.program_id(0)
    is_last = _tile_pos(geom, c, chunk) + chunk == _tile_seq_len(geom, c, chunk)

    @pl.when(is_last)
    def _():
        r_ref[...] = jnp.zeros_like(r_ref)

    cos = cos_ref[...]
    sin = sin_ref[...]
    for h in range(RET_HEADS):
        k1, k2 = _rot_halves(k_ref, h, cos, sin)
        dec = kdb_ref[h]
        kd = jnp.concatenate([k1 * dec, k2 * dec], axis=1).astype(BF16)
        vh = v_ref[:, h * RET_DV:(h + 1) * RET_DV].astype(BF16)
        r = r_ref[h]
        rb_ref[h] = r.astype(BF16)
        upd = lax.dot_general(kd, vh, (((0,), (0,)), ((), ())), preferred_element_type=F32)
        r_ref[h] = r * cdec[h] + upd


def _ret_main_kernel(q_ref, k_ref, v_ref, g_ref, rb_ref, cos_ref, sin_ref, dmat_ref, qdf_ref, qdb_ref,
                     kdf_ref, wo_ref, o_ref, rf_ref, *, geom, chunk, cdec):
    i = pl.program_id(0)

    @pl.when(_tile_pos(geom, i, chunk) == 0)
    def _():
        rf_ref[...] = jnp.zeros_like(rf_ref)

    cos = cos_ref[...]
    sin = sin_ref[...]
    acc = jnp.zeros(o_ref.shape, F32)
    for h in range(RET_HEADS):
        q1, q2 = _rot_halves(q_ref, h, cos, sin)
        k1, k2 = _rot_halves(k_ref, h, cos, sin)
        qr = jnp.concatenate([q1, q2], axis=1).astype(BF16)
        kr = jnp.concatenate([k1, k2], axis=1).astype(BF16)
        vh = v_ref[:, h * RET_DV:(h + 1) * RET_DV].astype(BF16)
        s = lax.dot_general(qr, kr, (((1,), (1,)), ((), ())), preferred_element_type=F32) * dmat_ref[h]
        o = jnp.dot(s.astype(BF16), vh, preferred_element_type=F32)
        qdf = qdf_ref[h]
        qf = jnp.concatenate([q1 * qdf, q2 * qdf], axis=1).astype(BF16)
        rf = rf_ref[h]
        o = o + jnp.dot(qf, rf.astype(BF16), preferred_element_type=F32)
        qdb = qdb_ref[h]
        qb = jnp.concatenate([q1 * qdb, q2 * qdb], axis=1).astype(BF16)
        o = o + jnp.dot(qb, rb_ref[h], preferred_element_type=F32)
        kdf = kdf_ref[h]
        kf = jnp.concatenate([k1 * kdf, k2 * kdf], axis=1).astype(BF16)
        rf_ref[h] = rf * cdec[h] + lax.dot_general(kf, vh, (((0,), (0,)), ((), ())),
                                                   preferred_element_type=F32)
        oc = o - jnp.mean(o, axis=-1, keepdims=True)
        on = oc * lax.rsqrt(jnp.mean(oc * oc, axis=-1, keepdims=True) + GN_EPS)
        og = (on * _silu(g_ref[:, h * RET_DV:(h + 1) * RET_DV].astype(F32))).astype(BF16)
        acc = acc + jnp.dot(og, wo_ref[h * RET_DV:(h + 1) * RET_DV, :], preferred_element_type=F32)
    o_ref[...] = acc.astype(o_ref.dtype)


def _retention_tables(chunk, s_max):
    log_gamma = jnp.log1p(-jnp.exp2(-5.0 - jnp.arange(RET_HEADS, dtype=F32)))
    idx = jnp.arange(chunk, dtype=F32)
    dist = jnp.abs(idx[:, None] - idx[None, :])
    dmat = jnp.exp(log_gamma[:, None, None] * dist[None])

    def rows(e):
        return jnp.broadcast_to(jnp.exp(log_gamma[:, None] * e[None, :])[:, :, None],
                                (RET_HEADS, chunk, RET_HALF))

    qdf = rows(idx + 1.0)
    qdb = rows(chunk - idx)
    kdf = rows(chunk - 1.0 - idx)
    kdb = rows(idx)
    theta = 1.0 / (ROPE_BASE ** jnp.linspace(0.0, 1.0, RET_HALF, dtype=F32))
    ang = jnp.arange(s_max, dtype=F32)[:, None] * theta[None, :]
    k_scale = RET_DK ** -0.5
    return jnp.cos(ang), jnp.sin(ang), dmat * k_scale, qdf, qdb, kdf * k_scale, kdb * k_scale


def _chunk_decay(chunk):
    lg = np.log1p(-np.exp2(-5.0 - np.arange(RET_HEADS, dtype=np.float32))).astype(np.float32)
    return tuple(float(v) for v in np.exp(lg * np.float32(chunk)).astype(np.float32))


def _retention(geom, proj_a, w_ret_o, tables, *, chunk):
    t = proj_a.shape[0]
    n_chunks = t // chunk
    cos, sin, dmat, qdf, qdb, kdf, kdb = tables
    cdec = _chunk_decay(chunk)
    state_shape = (RET_HEADS, RET_DK, RET_DV)

    def pos_blk(c):
        return _tile_pos(geom, c, chunk) // chunk

    rev = lambda i: n_chunks - 1 - i
    tab_spec = pl.BlockSpec((RET_HEADS, chunk, RET_HALF), lambda i: (0, 0, 0))
    rb = pl.pallas_call(
        functools.partial(_ret_bwd_state_kernel, geom=geom, chunk=chunk, cdec=cdec),
        grid=(n_chunks,),
        in_specs=[
            pl.BlockSpec((chunk, RET_QK_W), lambda i: (rev(i), 1)),
            pl.BlockSpec((chunk, RET_V_W), lambda i: (rev(i), 1)),
            pl.BlockSpec((chunk, RET_HALF), lambda i: (pos_blk(rev(i)), 0)),
            pl.BlockSpec((chunk, RET_HALF), lambda i: (pos_blk(rev(i)), 0)),
            tab_spec,
        ],
        out_specs=pl.BlockSpec((None,) + state_shape, lambda i: (rev(i), 0, 0, 0)),
        out_shape=jax.ShapeDtypeStruct((n_chunks,) + state_shape, BF16),
        scratch_shapes=[pltpu.VMEM(state_shape, F32)],
        compiler_params=_cparams(("arbitrary",)),
        name="ret_bwd_state",
    )(proj_a, proj_a, cos, sin, kdb)

    return pl.pallas_call(
        functools.partial(_ret_main_kernel, geom=geom, chunk=chunk, cdec=cdec),
        grid=(n_chunks,),
        in_specs=[
            pl.BlockSpec((chunk, RET_QK_W), lambda i: (i, 0)),
            pl.BlockSpec((chunk, RET_QK_W), lambda i: (i, 1)),
            pl.BlockSpec((chunk, RET_V_W), lambda i: (i, 1)),
            pl.BlockSpec((chunk, RET_V_W), lambda i: (i, 2)),
            pl.BlockSpec((None,) + state_shape, lambda i: (i, 0, 0, 0)),
            pl.BlockSpec((chunk, RET_HALF), lambda i: (pos_blk(i), 0)),
            pl.BlockSpec((chunk, RET_HALF), lambda i: (pos_blk(i), 0)),
            pl.BlockSpec((RET_HEADS, chunk, chunk), lambda i: (0, 0, 0)),
            tab_spec, tab_spec, tab_spec,
            pl.BlockSpec((RET_V_W, D_MODEL), lambda i: (0, 0)),
        ],
        out_specs=pl.BlockSpec((chunk, D_MODEL), lambda i: (i, 0)),
        out_shape=jax.ShapeDtypeStruct((t, D_MODEL), BF16),
        scratch_shapes=[pltpu.VMEM(state_shape, F32)],
        compiler_params=_cparams(("arbitrary",)),
        name="ret_main",
    )(proj_a, proj_a, proj_a, proj_a, rb, cos, sin, dmat, qdf, qdb, kdf, w_ret_o)


def _log_sigmoid(x):
    return jnp.minimum(x, 0.0) - jnp.log1p(jnp.exp(-jnp.abs(x)))


LRU_SEGS = V7X_SUBLANES
LRU_LEAD = CONV_LEFT * LRU_SEGS
LRU_TAIL = (CONV_WIDTH - 1 - CONV_LEFT) * LRU_SEGS


def _lru_load_segment_major(x_ref, prev_ref, next_ref, keep_prev, keep_next, xs_ref):
    tl = x_ref.shape[0]
    seg = tl // LRU_SEGS
    x = x_ref[...].astype(F32)
    prev_hi = prev_ref[...].astype(F32)[LRU_HALO - V7X_SUBLANES:] * keep_prev
    next_lo = next_ref[...].astype(F32)[:V7X_SUBLANES] * keep_next
    row8 = lax.broadcasted_iota(jnp.int32, (V7X_SUBLANES, LRU_BW), 0)
    last = V7X_SUBLANES - 1
    for n in range(LRU_BLOCKS):
        sl = slice(n * LRU_BW, (n + 1) * LRU_BW)
        for s in range(LRU_SEGS):
            xs_ref[n, pl.ds(LRU_LEAD + s, seg, stride=LRU_SEGS), :] = x[s * seg:(s + 1) * seg, sl]
        slab = lambda t: xs_ref[n, LRU_LEAD + t * LRU_SEGS:LRU_LEAD + (t + 1) * LRU_SEGS, :]
        p = prev_hi[:, sl]
        m1 = pltpu.roll(jnp.where(row8 == last, p, slab(seg - 1)), 1, axis=0)
        m2 = pltpu.roll(jnp.where(row8 == last, pltpu.roll(p, 1, axis=0), slab(seg - 2)), 1, axis=0)
        p1 = pltpu.roll(jnp.where(row8 == 0, next_lo[:, sl], slab(0)), last, axis=0)
        xs_ref[n, 0:LRU_SEGS, :] = m2
        xs_ref[n, LRU_SEGS:LRU_LEAD, :] = m1
        xs_ref[n, LRU_LEAD + tl:LRU_LEAD + tl + LRU_TAIL, :] = p1


def _lru_gates(xs_ref, d, wconv_ref, bconv_ref, wrg_ref, brg_ref, wig_ref, big_ref, lam_ref, a_ref, u_ref):
    tl = a_ref.shape[1]
    c = (0.5 * LRU_C * math.log2(math.e)) * _log_sigmoid(lam_ref[d])
    for n in range(LRU_BLOCKS):
        sl = slice(n * LRU_BW, (n + 1) * LRU_BW)
        xh = bconv_ref[:, sl] + xs_ref[n, 0:tl, :] * wconv_ref[0:1, sl]
        for j in range(1, CONV_WIDTH):
            xh = xh + xs_ref[n, j * LRU_SEGS:j * LRU_SEGS + tl, :] * wconv_ref[j:j + 1, sl]
        xb = xh.astype(BF16)
        th_r = jnp.tanh(jnp.dot(xb, wrg_ref[d, n], preferred_element_type=F32) + brg_ref[d][:, sl])
        th_i = jnp.tanh(jnp.dot(xb, wig_ref[d, n], preferred_element_type=F32) + big_ref[d][:, sl])
        a = jnp.exp2(c[:, sl] * th_r + c[:, sl])
        a_ref[n] = a
        y = 1.0 - a * a
        u_ref[n] = (y * lax.rsqrt(jnp.maximum(y, SQRT_TINY))) * ((th_i + 1.0) * xh)


def _lru_scan_segments(a_ref, u_ref, hs_ref, as_ref, fin_ref, carry_ref, o_ref, reverse):
    tl = a_ref.shape[1]
    seg = tl // LRU_SEGS
    slab_shape = (LRU_SEGS, LRU_BW)

    def step(k, carry):
        t = seg - 1 - k if reverse else k
        rows = pl.ds(pl.multiple_of(t * LRU_SEGS, LRU_SEGS), LRU_SEGS)
        hs, decays = [], []
        for n in range(LRU_BLOCKS):
            a = a_ref[n, rows, :]
            h = a * carry[0][n] + u_ref[n, rows, :]
            decay = a * carry[1][n]
            hs_ref[n, rows, :] = h
            as_ref[n, rows, :] = decay
            hs.append(h)
            decays.append(decay)
        return tuple(hs), tuple(decays)

    init = (tuple(jnp.zeros(slab_shape, F32) for _ in range(LRU_BLOCKS)),
            tuple(jnp.ones(slab_shape, F32) for _ in range(LRU_BLOCKS)))
    h_end, decay_end = lax.fori_loop(0, seg, step, init, unroll=4)

    order = range(LRU_SEGS - 1, -1, -1) if reverse else range(LRU_SEGS)
    for n in range(LRU_BLOCKS):
        sl = slice(n * LRU_BW, (n + 1) * LRU_BW)
        fin_ref[0] = h_end[n]
        fin_ref[1] = decay_end[n]
        c = carry_ref[n, 0:1, :]
        for s in order:
            fin_ref[2, s:s + 1, :] = c
            c = fin_ref[0, s:s + 1, :] + fin_ref[1, s:s + 1, :] * c
        carry_ref[n, 0:1, :] = c
        for s in range(LRU_SEGS):
            seg_rows = pl.ds(s, seg, stride=LRU_SEGS)
            h = hs_ref[n, seg_rows, :] + as_ref[n, seg_rows, :] * fin_ref[2, s:s + 1, :]
            o_ref[s * seg:(s + 1) * seg, sl] = h.astype(o_ref.dtype)


def _lru_kernel(xf_ref, xfp_ref, xfn_ref, xb_ref, xbp_ref, xbn_ref, wconv_ref, bconv_ref, wrg_ref, brg_ref,
                wig_ref, big_ref, lam_ref, hf_ref, hb_ref, xs_ref, a_ref, u_ref, hs_ref, as_ref, fin_ref,
                cf_ref, cb_ref, *, geom, tl):
    i = pl.program_id(0)
    ib = pl.num_programs(0) - 1 - i

    def flags(tile):
        pos = _tile_pos(geom, tile, tl)
        return pos == 0, pos + tl == _tile_seq_len(geom, tile, tl)

    def keep(flag):
        return jnp.where(flag, 0.0, 1.0)

    def direction(d, x_ref, prev_ref, next_ref, first, last, carry_ref, o_ref):
        _lru_load_segment_major(x_ref, prev_ref, next_ref, keep(first), keep(last), xs_ref)
        _lru_gates(xs_ref, d, wconv_ref, bconv_ref, wrg_ref, brg_ref, wig_ref, big_ref, lam_ref, a_ref, u_ref)
        _lru_scan_segments(a_ref, u_ref, hs_ref, as_ref, fin_ref, carry_ref, o_ref, reverse=(d == 1))

    first, last = flags(i)

    @pl.when(first)
    def _():
        cf_ref[...] = jnp.zeros_like(cf_ref)

    direction(0, xf_ref, xfp_ref, xfn_ref, first, last, cf_ref, hf_ref)

    first, last = flags(ib)

    @pl.when(last)
    def _():
        cb_ref[...] = jnp.zeros_like(cb_ref)

    direction(1, xb_ref, xbp_ref, xbn_ref, first, last, cb_ref, hb_ref)


def _lru(geom, proj_b, w_conv, b_conv, w_rg, b_rg, w_ig, b_ig, lam, *, tl):
    t = proj_b.shape[0]
    n = t // tl
    r8 = tl // LRU_HALO
    n8 = t // LRU_HALO
    w = LRU_WIDTH
    rev = lambda i: n - 1 - i
    prev_blk = lambda i: jnp.maximum(i * r8 - 1, 0)
    next_blk = lambda i: jnp.minimum((i + 1) * r8, n8 - 1)
    full = lambda a: pl.BlockSpec(a.shape, lambda i: (0,) * a.ndim)
    w_conv = 0.5 * w_conv
    b_conv2 = 0.5 * b_conv.reshape(1, w)
    b_rg3 = 0.5 * b_rg.reshape(2, 1, w)
    b_ig3 = 0.5 * b_ig.reshape(2, 1, w)
    lam3 = lam.reshape(2, 1, w)
    return pl.pallas_call(
        functools.partial(_lru_kernel, geom=geom, tl=tl),
        grid=(n,),
        in_specs=[
            pl.BlockSpec((tl, w), lambda i: (i, 0)),
            pl.BlockSpec((LRU_HALO, w), lambda i: (prev_blk(i), 0)),
            pl.BlockSpec((LRU_HALO, w), lambda i: (next_blk(i), 0)),
            pl.BlockSpec((tl, w), lambda i: (rev(i), 0)),
            pl.BlockSpec((LRU_HALO, w), lambda i: (prev_blk(rev(i)), 0)),
            pl.BlockSpec((LRU_HALO, w), lambda i: (next_blk(rev(i)), 0)),
            full(w_conv), full(b_conv2), full(w_rg), full(b_rg3), full(w_ig), full(b_ig3), full(lam3),
        ],
        out_specs=[
            pl.BlockSpec((tl, w), lambda i: (i, 0)),
            pl.BlockSpec((tl, w), lambda i: (rev(i), 0)),
        ],
        out_shape=[jax.ShapeDtypeStruct((t, w), BF16), jax.ShapeDtypeStruct((t, w), BF16)],
        scratch_shapes=[pltpu.VMEM((LRU_BLOCKS, LRU_LEAD + tl + LRU_TAIL, LRU_BW), F32)]
        + [pltpu.VMEM((LRU_BLOCKS, tl, LRU_BW), F32)] * 4
        + [pltpu.VMEM((3, LRU_SEGS, LRU_BW), F32)]
        + [pltpu.VMEM((LRU_BLOCKS, V7X_SUBLANES, LRU_BW), F32)] * 2,
        compiler_params=_cparams(("arbitrary",)),
        name="lru_scan",
    )(proj_b, proj_b, proj_b, proj_b, proj_b, proj_b, w_conv, b_conv2, w_rg, b_rg3, w_ig, b_ig3, lam3)


def _gelu_tanh(x):
    return 0.5 * x * (1.0 + jnp.tanh(math.sqrt(2.0 / math.pi) * (x + 0.044715 * (x * x * x))))


def _merge_kernel(*refs, n_x, first):
    x_refs = refs[:n_x]
    ya_ref, hf_ref, hb_ref, gl_ref, ga_ref, gb_ref, mod_ref, wl_ref, wo_ref, o_ref = refs[n_x:]
    f32 = lambda ref: ref[...].astype(F32)
    y = ((f32(hf_ref) + f32(hb_ref)) * _gelu_tanh(f32(gl_ref))).astype(BF16)
    yb = jnp.dot(y, wl_ref[...], preferred_element_type=F32)
    m = _sigmoid(f32(ga_ref)) * f32(ya_ref) + _sigmoid(f32(gb_ref)) * yb
    mix = jnp.dot(m.astype(BF16), wo_ref[...], preferred_element_type=F32)
    o_ref[...] = _token_tile(x_refs, first) + mod_ref[2:3, :] * mix


def _merge(geom, x, ya, hf, hb, proj_a, proj_b, mod_l, w_lru_o, w_out, *, tm):
    parts, x_specs, first = _token_parts(geom, x, tm)
    t, d = geom.tokens, parts[0].shape[1]
    gate_a_blk = (2 * RET_QK_W + 2 * RET_V_W) // d
    tok = lambda w: pl.BlockSpec((tm, w), lambda i: (i, 0))
    return pl.pallas_call(
        functools.partial(_merge_kernel, n_x=len(parts), first=first),
        grid=(t // tm,),
        in_specs=x_specs + [
            tok(d), tok(LRU_WIDTH), tok(LRU_WIDTH),
            pl.BlockSpec((tm, LRU_WIDTH), lambda i: (i, 1)),
            pl.BlockSpec((tm, d), lambda i: (i, gate_a_blk)),
            pl.BlockSpec((tm, d), lambda i: (i, gate_a_blk + 1)),
            pl.BlockSpec((None, 6, d), lambda i: (_tile_seq(geom, i, tm), 0, 0)),
            pl.BlockSpec((LRU_WIDTH, d), lambda i: (0, 0)),
            pl.BlockSpec((d, d), lambda i: (0, 0)),
        ],
        out_specs=tok(d),
        out_shape=jax.ShapeDtypeStruct((t, d), F32),
        compiler_params=_cparams(("parallel",)),
        name="merge",
    )(*parts, ya, hf, hb, proj_b, proj_a, proj_a, mod_l, w_lru_o, w_out)


def _ffn_kernel(x_ref, mod_ref, g_ref, wg_ref, wu_ref, wd_ref, o_ref, *, col_chunks):
    x = x_ref[...]
    h = _rms_mod(x, g_ref[...], mod_ref[4:5, :], mod_ref[3:4, :]).astype(BF16)
    acc = None
    for c0, c1 in col_chunks:
        a = jnp.dot(h, wg_ref[:, c0:c1], preferred_element_type=F32)
        u = jnp.dot(h, wu_ref[:, c0:c1], preferred_element_type=F32)
        part = jnp.dot((_silu(a) * u).astype(BF16), wd_ref[c0:c1, :], preferred_element_type=F32)
        acc = part if acc is None else acc + part
    o_ref[...] = x + mod_ref[5:6, :] * acc


def _ffn(geom, x, mod_l, gvec, wg, wu, wd, *, tm, tf):
    t, d = x.shape
    ff = wg.shape[1]
    col_chunks = tuple((c, min(c + tf, ff)) for c in range(0, ff, tf))
    resident = lambda shape: pl.BlockSpec(shape, lambda i: (0, 0), pipeline_mode=pl.Buffered(1))
    return pl.pallas_call(
        functools.partial(_ffn_kernel, col_chunks=col_chunks),
        grid=(t // tm,),
        in_specs=[
            pl.BlockSpec((tm, d), lambda i: (i, 0)),
            pl.BlockSpec((None, 6, d), lambda i: (_tile_seq(geom, i, tm), 0, 0)),
            pl.BlockSpec((1, d), lambda i: (0, 0)),
            resident((d, ff)), resident((d, ff)), resident((ff, d)),
        ],
        out_specs=pl.BlockSpec((tm, d), lambda i: (i, 0)),
        out_shape=jax.ShapeDtypeStruct((t, d), F32),
        compiler_params=_cparams(("parallel",)),
        name="ffn_dense",
    )(x, mod_l, gvec, wg, wu, wd)


ROUTE_E0, ROUTE_E1, ROUTE_R0, ROUTE_R1, ROUTE_G0, ROUTE_G1 = range(6)

ROW_TILES = D_MODEL // V7X_LANES


def _store_token_rows(ref, val):
    rows = val.shape[0]
    for j in range(ROW_TILES):
        ref[pl.ds(j, rows, stride=ROW_TILES), :] = val[:, j * V7X_LANES:(j + 1) * V7X_LANES]


def _load_token_rows(ref, rows):
    return jnp.concatenate([ref[pl.ds(j, rows, stride=ROW_TILES), :] for j in range(ROW_TILES)], axis=1)


def _router_kernel(x_ref, mod_ref, g_ref, wr_ref, h_ref, route_ref, cnt_ref, carry_ref):
    i = pl.program_id(0)
    tm = x_ref.shape[0]

    @pl.when(i == 0)
    def _():
        carry_ref[...] = jnp.zeros_like(carry_ref)

    h = _rms_mod(x_ref[...], g_ref[...], mod_ref[4:5, :], mod_ref[3:4, :])
    _store_token_rows(h_ref, h)
    h_hi = h.astype(BF16)
    h_lo = (h - h_hi.astype(F32)).astype(BF16)
    logits = (jnp.dot(h_hi, wr_ref[0], preferred_element_type=F32)
              + jnp.dot(h_lo, wr_ref[0], preferred_element_type=F32)
              + jnp.dot(h_hi, wr_ref[1], preferred_element_type=F32))
    lane = lax.broadcasted_iota(jnp.int32, logits.shape, 1).astype(F32)
    logits = jnp.where(lane < N_EXPERTS, logits, -jnp.inf)
    m1 = jnp.max(logits, axis=-1, keepdims=True)
    i1 = jnp.min(jnp.where(logits == m1, lane, float(V7X_LANES)), axis=-1, keepdims=True)
    rest = jnp.where(lane == i1, -jnp.inf, logits)
    m2 = jnp.max(rest, axis=-1, keepdims=True)
    i2 = jnp.min(jnp.where(rest == m2, lane, float(V7X_LANES)), axis=-1, keepdims=True)
    ex = jnp.exp(m2 - m1)
    g1 = 1.0 / (1.0 + ex)
    g2 = ex / (1.0 + ex)
    sel1 = lane == i1
    sel2 = lane == i2
    onehot = jnp.where(sel1 | sel2, 1.0, 0.0)
    row = lax.broadcasted_iota(jnp.int32, (tm, tm), 0)
    col = lax.broadcasted_iota(jnp.int32, (tm, tm), 1)
    lower = jnp.where(col < row, 1.0, 0.0).astype(BF16)
    before = jnp.dot(lower, onehot.astype(BF16), preferred_element_type=F32) + carry_ref[0:1, :]
    r1 = jnp.sum(jnp.where(sel1, before, 0.0), axis=-1, keepdims=True)
    r2 = jnp.sum(jnp.where(sel2, before, 0.0), axis=-1, keepdims=True)
    out_lane = lax.broadcasted_iota(jnp.int32, route_ref.shape, 1)
    vals = (i1.astype(F32), i2.astype(F32), r1, r2, g1, g2)
    packed = jnp.zeros(route_ref.shape, F32)
    for slot, v in enumerate(vals):
        packed = jnp.where(out_lane == slot, v, packed)
    route_ref[...] = packed
    carry = carry_ref[0:1, :] + jnp.sum(onehot, axis=0, keepdims=True)
    carry_ref[0:1, :] = carry
    cnt_ref[...] = jnp.broadcast_to(carry, cnt_ref.shape).astype(jnp.int32)


def _router(geom, x, mod_l, gvec, w_router, *, tm):
    t, d = x.shape
    w_pad = jnp.pad(w_router, ((0, 0), (0, V7X_LANES - N_EXPERTS)))
    w_hi = w_pad.astype(BF16)
    w_pad = jnp.stack([w_hi, (w_pad - w_hi.astype(F32)).astype(BF16)])
    return pl.pallas_call(
        _router_kernel,
        grid=(t // tm,),
        in_specs=[
            pl.BlockSpec((tm, d), lambda i: (i, 0)),
            pl.BlockSpec((None, 6, d), lambda i: (_tile_seq(geom, i, tm), 0, 0)),
            pl.BlockSpec((1, d), lambda i: (0, 0)),
            pl.BlockSpec((2, d, V7X_LANES), lambda i: (0, 0, 0)),
        ],
        out_specs=[
            pl.BlockSpec((tm * ROW_TILES, V7X_LANES), lambda i: (i, 0)),
            pl.BlockSpec((tm, V7X_LANES), lambda i: (i, 0)),
            pl.BlockSpec((V7X_SUBLANES, V7X_LANES), lambda i: (0, 0)),
        ],
        out_shape=[
            jax.ShapeDtypeStruct((t * ROW_TILES, V7X_LANES), F32),
            jax.ShapeDtypeStruct((t, V7X_LANES), F32),
            jax.ShapeDtypeStruct((V7X_SUBLANES, V7X_LANES), jnp.int32),
        ],
        scratch_shapes=[pltpu.VMEM((V7X_SUBLANES, V7X_LANES), F32)],
        compiler_params=_cparams(("arbitrary",)),
        name="router",
    )(x, mod_l, gvec, w_pad)


def _token_copy(src, src_tok, dst, dst_tok, sem):
    s = pl.multiple_of(src_tok * ROW_TILES, ROW_TILES)
    d = pl.multiple_of(dst_tok * ROW_TILES, ROW_TILES)
    return pltpu.make_async_copy(src.at[pl.ds(s, ROW_TILES), :], dst.at[pl.ds(d, ROW_TILES), :], sem)


def _dispatch_kernel(fill_ref, dest_ref, src_ref, out_hbm, zero_ref, sem, zero_sem, *, td):
    @pl.when(pl.program_id(0) == 0)
    def _():
        zero_ref[...] = jnp.zeros_like(zero_ref)
        for wait in (False, True):
            for e in range(N_EXPERTS):
                def pad_row(r, carry):
                    copy = _token_copy(zero_ref, 0, out_hbm, r, zero_sem)
                    copy.wait() if wait else copy.start()
                    return carry

                lax.fori_loop(fill_ref[0, e], fill_ref[1, e], pad_row, 0)

    def issue(t, carry):
        for k in range(TOP_K):
            _token_copy(src_ref, t, out_hbm, dest_ref[k, t], sem).start(priority=k % DMA_PRIORITIES)
        return carry

    lax.fori_loop(0, td, issue, 0, unroll=8)
    for _ in range(TOP_K):
        pltpu.make_async_copy(src_ref, out_hbm.at[pl.ds(0, td * ROW_TILES), :], sem).wait()


def _dispatch(h2_rows, dest, fill, n_rows, *, td):
    grid_spec = pltpu.PrefetchScalarGridSpec(
        num_scalar_prefetch=1,
        grid=(dest.shape[1] // td,),
        in_specs=[
            pl.BlockSpec((TOP_K, td), lambda i, fill: (0, i), memory_space=pltpu.SMEM),
            pl.BlockSpec((td * ROW_TILES, V7X_LANES), lambda i, fill: (i, 0)),
        ],
        out_specs=pl.BlockSpec(memory_space=pl.ANY),
        scratch_shapes=[pltpu.VMEM((ROW_TILES, V7X_LANES), h2_rows.dtype),
                        pltpu.SemaphoreType.DMA(()), pltpu.SemaphoreType.DMA(())],
    )
    return pl.pallas_call(
        functools.partial(_dispatch_kernel, td=td),
        grid_spec=grid_spec,
        out_shape=jax.ShapeDtypeStruct((n_rows * ROW_TILES, V7X_LANES), h2_rows.dtype),
        compiler_params=_cparams(("arbitrary",)),
        name="moe_dispatch",
    )(fill, dest, h2_rows)


V7X_MXU_COLS = 256


def _experts_kernel(blk_e_ref, nused_ref, x_ref, wg_ref, wu_ref, wd_ref, o_ref, hm_ref, *, ff_chunks):
    del blk_e_ref
    b = pl.program_id(0)
    bm = hm_ref.shape[0]

    @pl.when(b < nused_ref[0])
    def _():
        xb = _load_token_rows(x_ref, bm).astype(BF16)
        for c0, c1 in ff_chunks:
            a = jnp.dot(xb, wg_ref[:, c0:c1], preferred_element_type=F32)
            u = jnp.dot(xb, wu_ref[:, c0:c1], preferred_element_type=F32)
            hm_ref[:, c0:c1] = (_silu(a) * u).astype(BF16)
        for c0 in range(0, D_MODEL, V7X_MXU_COLS):
            y = jnp.dot(hm_ref[...], wd_ref[:, c0:c0 + V7X_MXU_COLS], preferred_element_type=F32)
            for k in range(V7X_MXU_COLS // V7X_LANES):
                o_ref[pl.ds(c0 // V7X_LANES + k, bm, stride=ROW_TILES), :] = y[:, k * V7X_LANES:(k + 1) * V7X_LANES]

    @pl.when(b >= nused_ref[0])
    def _():
        o_ref[...] = jnp.zeros_like(o_ref)


def _experts(xbuf, blk_e, n_used, wg, wu, wd, *, bm, tf):
    p = xbuf.shape[0] // ROW_TILES
    d = D_MODEL
    ff = wg.shape[2]
    ff_chunks = tuple((c, min(c + tf, ff)) for c in range(0, ff, tf))
    rows_blk = (bm * ROW_TILES, V7X_LANES)

    def expert(b, be, nu):
        return be[jnp.minimum(b, nu[0] - 1)]

    resident = lambda shape: pl.BlockSpec((None,) + shape, lambda b, be, nu: (expert(b, be, nu), 0, 0),
                                          pipeline_mode=pl.Buffered(1))
    grid_spec = pltpu.PrefetchScalarGridSpec(
        num_scalar_prefetch=2,
        grid=(p // bm,),
        in_specs=[
            pl.BlockSpec(rows_blk, lambda b, be, nu: (b, 0)),
            resident((d, ff)), resident((d, ff)), resident((ff, d)),
        ],
        out_specs=pl.BlockSpec(rows_blk, lambda b, be, nu: (b, 0)),
        scratch_shapes=[pltpu.VMEM((bm, ff), BF16)],
    )
    return pl.pallas_call(
        functools.partial(_experts_kernel, ff_chunks=ff_chunks),
        grid_spec=grid_spec,
        out_shape=jax.ShapeDtypeStruct(xbuf.shape, F32),
        compiler_params=_cparams(("arbitrary",)),
        name="moe_experts",
    )(blk_e, n_used, xbuf, wg, wu, wd)


def _moe_out_kernel(dest_ref, dest_next_ref, x_ref, route_ref, mod_ref, gf_ref, ybuf_hbm, op_ref, os_ref,
                    ya0_ref, ya1_ref, yb0_ref, yb1_ref, sems, *, final_norm, prompt_tiles):
    i = pl.program_id(0)
    tm = x_ref.shape[0]
    bufs = ((ya0_ref, ya1_ref), (yb0_ref, yb1_ref))

    def gather(d_ref, slot, inline=False):
        def issue(t, carry):
            for k in range(TOP_K):
                _token_copy(ybuf_hbm, d_ref[k, t], bufs[slot][k], t, sems.at[slot]).start(
                    priority=k % DMA_PRIORITIES)
            return carry

        if inline:
            for t in range(tm):
                issue(t, 0)
        else:
            lax.fori_loop(0, tm, issue, 0, unroll=8)

    def wait(slot):
        for k in range(TOP_K):
            pltpu.make_async_copy(ybuf_hbm.at[pl.ds(0, tm * ROW_TILES), :], bufs[slot][k], sems.at[slot]).wait()

    def combine(slot):
        g1 = route_ref[:, ROUTE_G0:ROUTE_G0 + 1]
        g2 = route_ref[:, ROUTE_G1:ROUTE_G1 + 1]
        y = _load_token_rows(bufs[slot][0], tm) * g1 + _load_token_rows(bufs[slot][1], tm) * g2
        x = x_ref[...] + mod_ref[5:6, :] * y
        if final_norm:
            ms = jnp.mean(x * x, axis=-1, keepdims=True)
            x = x * lax.rsqrt(ms + RMS_EPS) * gf_ref[...]

        @pl.when(i < prompt_tiles)
        def _():
            op_ref[...] = x

        @pl.when(i >= prompt_tiles)
        def _():
            os_ref[...] = x

    @pl.when(i == 0)
    def _():
        gather(dest_ref, 0)

    for slot in range(2):
        @pl.when(i % 2 == slot)
        def _():
            wait(slot)
            gather(dest_next_ref, 1 - slot, inline=True)
            combine(slot)

            @pl.when(i + 1 == pl.num_programs(0))
            def _():
                wait(1 - slot)


def _moe_out(geom, x, ybuf, dest, route, mod_l, g_final, *, tm, final_norm):
    t, d = x.shape
    n = t // tm
    prompt_tiles = geom.tokens_p // tm
    y_buf = pltpu.VMEM((tm * ROW_TILES, V7X_LANES), F32)
    return pl.pallas_call(
        functools.partial(_moe_out_kernel, final_norm=final_norm, prompt_tiles=prompt_tiles),
        grid=(n,),
        in_specs=[
            pl.BlockSpec((TOP_K, tm), lambda i: (0, i), memory_space=pltpu.SMEM),
            pl.BlockSpec((TOP_K, tm), lambda i: (0, jnp.minimum(i + 1, n - 1)), memory_space=pltpu.SMEM),
            pl.BlockSpec((tm, d), lambda i: (i, 0)),
            pl.BlockSpec((tm, V7X_LANES), lambda i: (i, 0)),
            pl.BlockSpec((None, 6, d), lambda i: (_tile_seq(geom, i, tm), 0, 0)),
            pl.BlockSpec((1, d), lambda i: (0, 0)),
            pl.BlockSpec(memory_space=pl.ANY),
        ],
        out_specs=[
            pl.BlockSpec((tm, d), lambda i: (jnp.minimum(i, prompt_tiles - 1), 0)),
            pl.BlockSpec((tm, d), lambda i: (jnp.maximum(i - prompt_tiles, 0), 0)),
        ],
        out_shape=[jax.ShapeDtypeStruct((geom.tokens_p, d), F32),
                   jax.ShapeDtypeStruct((t - geom.tokens_p, d), F32)],
        scratch_shapes=[y_buf, y_buf, y_buf, y_buf, pltpu.SemaphoreType.DMA((2,))],
        compiler_params=_cparams(("arbitrary",)),
        name="moe_out",
    )(dest, dest, x, route, mod_l, g_final, ybuf)


def _final_norm_kernel(x_ref, g_ref, o_ref):
    x = x_ref[...]
    o_ref[...] = x * lax.rsqrt(jnp.mean(x * x, axis=-1, keepdims=True) + RMS_EPS) * g_ref[...]


def _final_norm(x, g_final, *, tm):
    t, d = x.shape
    return pl.pallas_call(
        _final_norm_kernel,
        grid=(t // tm,),
        in_specs=[pl.BlockSpec((tm, d), lambda i: (i, 0)), pl.BlockSpec((1, d), lambda i: (0, 0))],
        out_specs=pl.BlockSpec((tm, d), lambda i: (i, 0)),
        out_shape=jax.ShapeDtypeStruct((t, d), F32),
        compiler_params=_cparams(("parallel",)),
        name="final_norm",
    )(x, g_final)


def _moe(geom, x, mod_l, gvec, w_router, wg, wu, wd, g_final, *, tiles, final_norm):
    t, d = x.shape
    bm = tiles["moe_rows"]
    h2, route, counts = _router(geom, x, mod_l, gvec, w_router, tm=tiles["router"])
    counts = counts[0, :N_EXPERTS]
    padded = ((counts + bm - 1) // bm) * bm
    pad_end = jnp.cumsum(padded)
    pad_start = pad_end - padded
    experts = route[:, ROUTE_E0:ROUTE_E1 + 1].astype(jnp.int32)
    ranks = route[:, ROUTE_R0:ROUTE_R1 + 1].astype(jnp.int32)
    dest = (pad_start[experts] + ranks).T
    n_rows = t * TOP_K + N_EXPERTS * bm
    n_blk = n_rows // bm
    blk_start = jnp.arange(n_blk, dtype=jnp.int32) * bm
    blk_e = jnp.minimum(jnp.sum((pad_end[None, :] <= blk_start[:, None]).astype(jnp.int32), axis=1),
                        N_EXPERTS - 1).astype(jnp.int32)
    n_used = (pad_end[-1:] // bm).astype(jnp.int32)
    fill = jnp.stack([pad_start + counts, pad_end.at[N_EXPERTS - 1].set(n_rows)]).astype(jnp.int32)
    xbuf = _dispatch(h2, dest, fill, n_rows, td=tiles["dma_rows"])
    ybuf = _experts(xbuf, blk_e, n_used, wg, wu, wd, bm=bm, tf=tiles["expert_ff"])
    return _moe_out(geom, x, ybuf, dest, route, mod_l, g_final, tm=tiles["token"], final_norm=final_norm)


def _pick_tiles(geom):
    s = math.gcd(geom.s_p, geom.s_s) if geom.n_p and geom.n_s else (geom.s_p if geom.n_p else geom.s_s)
    return {
        "proj": min(1024, s),
        "proj_cols": 2048,
        "ret_chunk": min(512, s),
        "lru": min(512, s),
        "token": min(512, s),
        "ffn_cols": 1536,
        "router": min(512, s),
        "moe_rows": 512,
        "expert_ff": 1792,
        "dma_rows": min(2048, s),
    }


def _trunk(geom, x, c_all, w_ada, b_ada, g_norm1, g_norm2, w_in, w_conv, b_conv, w_rg, b_rg, w_ig, b_ig,
           lru_lambda, w_ret_o, w_lru_o, w_out, w_ff_gate, w_ff_up, w_ff_down,
           w_router, w_e_gate, w_e_up, w_e_down, g_final, tiles):
    depth = w_in.shape[0]
    d = D_MODEL
    n_pad = -(-geom.n_seq // V7X_SUBLANES) * V7X_SUBLANES
    c_pad = jnp.pad(c_all, ((0, n_pad - geom.n_seq), (0, 0)))
    mod = _adaln_mod(c_pad, w_ada, b_ada)
    tables = _retention_tables(tiles["ret_chunk"], max(geom.s_p if geom.n_p else 0, geom.s_s if geom.n_s else 0))

    o_xl = 2 * RET_QK_W + 2 * RET_V_W
    o_ga = o_xl + 2 * LRU_WIDTH
    for l in range(depth):
        wl = w_in[l]
        w_a = jnp.concatenate([wl[:, :o_xl], wl[:, o_ga:]], axis=1).astype(BF16)
        w_b = wl[:, o_xl:o_ga].astype(BF16)
        g1 = g_norm1[l].reshape(1, d)
        g2 = g_norm2[l].reshape(1, d)
        proj_a = _norm_proj(geom, x, mod[l], g1, w_a, shift_row=0, scale_row=1,
                            tm=tiles["proj"], tn=tiles["proj_cols"])
        proj_b = _norm_proj(geom, x, mod[l], g1, w_b, shift_row=0, scale_row=1,
                            tm=tiles["proj"], tn=LRU_WIDTH)
        ya = _retention(geom, proj_a, w_ret_o[l].astype(BF16), tables, chunk=tiles["ret_chunk"])
        hf, hb = _lru(geom, proj_b, w_conv[l], b_conv[l], w_rg[l].astype(BF16), b_rg[l],
                      w_ig[l].astype(BF16), b_ig[l], lru_lambda[l], tl=tiles["lru"])
        x = _merge(geom, x, ya, hf, hb, proj_a, proj_b, mod[l], w_lru_o[l].astype(BF16),
                   w_out[l].astype(BF16), tm=tiles["token"])
        j = l // 2
        last = l == depth - 1
        if l % 2 == 0:
            x = _ffn(geom, x, mod[l], g2, w_ff_gate[j].astype(BF16), w_ff_up[j].astype(BF16),
                     w_ff_down[j].astype(BF16), tm=tiles["token"], tf=tiles["ffn_cols"])
            if last:
                x = _final_norm(x, g_final.reshape(1, d), tm=tiles["token"])
                return x[:geom.tokens_p], x[geom.tokens_p:]
        else:
            parts = _moe(geom, x, mod[l], g2, w_router[j], w_e_gate[j].astype(BF16), w_e_up[j].astype(BF16),
                         w_e_down[j].astype(BF16), g_final.reshape(1, d), tiles=tiles, final_norm=last)
            if last:
                return parts
            x = jnp.concatenate(parts, axis=0)


def kernel(x_prompt, x_sample, c_prompt, c_sample, w_ada, b_ada, g_norm1, g_norm2, w_in, w_conv, b_conv, w_rg, b_rg, w_ig, b_ig, lru_lambda, w_ret_o, w_lru_o, w_out, w_ff_gate, w_ff_up, w_ff_down, w_router, w_e_gate, w_e_up, w_e_down, g_final):
    n_p, s_p, d = x_prompt.shape
    n_s, s_s, _ = x_sample.shape
    geom = Geom(n_p, s_p, n_s, s_s)
    x = (x_prompt.reshape(-1, d), x_sample.reshape(-1, d))
    c_all = jnp.concatenate([c_prompt, c_sample], axis=0)
    yp, ys = _trunk(geom, x, c_all, w_ada, b_ada, g_norm1, g_norm2, w_in, w_conv, b_conv, w_rg, b_rg, w_ig, b_ig,
                    lru_lambda, w_ret_o, w_lru_o, w_out, w_ff_gate, w_ff_up, w_ff_down,
                    w_router, w_e_gate, w_e_up, w_e_down, g_final, _pick_tiles(geom))
    return (yp.reshape(n_p, s_p, d), ys.reshape(n_s, s_s, d))
```

```python
import functools
import math
from typing import NamedTuple

import jax
import jax.numpy as jnp
import numpy as np
from jax import lax
from jax.experimental import pallas as pl
from jax.experimental.pallas import tpu as pltpu

F32 = jnp.float32
BF16 = jnp.bfloat16

D_MODEL = 1024
RET_HEADS = 4
RET_DK = 256
RET_DV = 512
RET_HALF = RET_DK // 2
RET_QK_W = RET_HEADS * RET_DK
RET_V_W = RET_HEADS * RET_DV
ROPE_BASE = 10000.0
LRU_WIDTH = 1280
LRU_BLOCKS = 10
LRU_BW = LRU_WIDTH // LRU_BLOCKS
LRU_C = 8.0
CONV_WIDTH = 4
CONV_LEFT = 2
LRU_HALO = 16
N_EXPERTS = 8
TOP_K = 2
RMS_EPS = 1e-6
GN_EPS = 1e-5
SQRT_TINY = 1e-30

V7X_LANES = 128
V7X_SUBLANES = 8
V7X_VMEM_BYTES = 64 * 1024 * 1024
DMA_PRIORITIES = 2
VMEM_LIMIT = (V7X_VMEM_BYTES * 3) // 4
MIXER_VMEM_LIMIT = (V7X_VMEM_BYTES * 29) // 32

PA_W = 2 * RET_QK_W + 2 * RET_V_W + 2 * D_MODEL
PB_W = 2 * LRU_WIDTH


class Geom(NamedTuple):
    n_p: int
    s_p: int
    n_s: int
    s_s: int

    @property
    def tokens_p(self):
        return self.n_p * self.s_p

    @property
    def tokens(self):
        return self.n_p * self.s_p + self.n_s * self.s_s

    @property
    def n_seq(self):
        return self.n_p + self.n_s


def _tile_seq(g, i, tm):
    t0 = i * tm
    return jnp.where(t0 < g.tokens_p, t0 // g.s_p, g.n_p + (t0 - g.tokens_p) // g.s_s)


def _tile_pos(g, i, tm):
    t0 = i * tm
    return jnp.where(t0 < g.tokens_p, t0 % g.s_p, (t0 - g.tokens_p) % g.s_s)


def _tile_seq_len(g, i, tm):
    return jnp.where(i * tm < g.tokens_p, g.s_p, g.s_s)


def _cparams(sem):
    return pltpu.CompilerParams(dimension_semantics=sem, vmem_limit_bytes=VMEM_LIMIT)


def _sigmoid(x):
    return 0.5 * jnp.tanh(0.5 * x) + 0.5


def _silu(x):
    return x * _sigmoid(x)


def _rms_mod(x, gvec, scale, shift):
    ms = jnp.mean(x * x, axis=-1, keepdims=True)
    y = x * lax.rsqrt(ms + RMS_EPS) * gvec
    return y * (1.0 + scale) + shift


def _mod_kernel(c_ref, w_ref, b_ref, o_ref):
    c = c_ref[...]
    o_ref[...] = jnp.dot(_silu(c), w_ref[...], preferred_element_type=F32,
                         precision=lax.Precision.HIGHEST) + b_ref[...]


def _adaln_mod(c_pad, w_ada, b_ada):
    depth, d, _ = w_ada.shape
    n = c_pad.shape[0]
    out = pl.pallas_call(
        _mod_kernel,
        grid=(depth, 6),
        in_specs=[
            pl.BlockSpec((n, d), lambda l, j: (0, 0)),
            pl.BlockSpec((None, d, d), lambda l, j: (l, 0, j)),
            pl.BlockSpec((None, None, 1, d), lambda l, j: (l, j, 0, 0)),
        ],
        out_specs=pl.BlockSpec((None, None, n, d), lambda l, j: (l, j, 0, 0)),
        out_shape=jax.ShapeDtypeStruct((depth, 6, n, d), F32),
        compiler_params=_cparams(("arbitrary", "arbitrary")),
        name="adaln_mod",
    )(c_pad, w_ada, b_ada.reshape(depth, 6, 1, d))
    return out.transpose(0, 2, 1, 3)


def _token_parts(geom, x, tm):
    parts = tuple(p for p in x if p.shape[0]) if isinstance(x, tuple) else (x,)
    d = parts[0].shape[1]
    if len(parts) == 1:
        return parts, [pl.BlockSpec((tm, d), lambda i, *_: (i, 0))], None
    first = geom.tokens_p // tm
    specs = [pl.BlockSpec((tm, d), lambda i, *_: (jnp.minimum(i, first - 1), 0)),
             pl.BlockSpec((tm, d), lambda i, *_: (jnp.maximum(i - first, 0), 0))]
    return parts, specs, first


def _token_tile(x_refs, first):
    if len(x_refs) == 1:
        return x_refs[0][...]
    return jnp.where(pl.program_id(0) < first, x_refs[0][...], x_refs[1][...])


def _norm_proj_kernel(*refs, n_x, first, shift_row, scale_row):
    x_refs = refs[:n_x]
    mod_ref, g_ref, w_ref, o_ref, h_ref = refs[n_x:]

    @pl.when(pl.program_id(1) == 0)
    def _():
        h = _rms_mod(_token_tile(x_refs, first), g_ref[...], mod_ref[scale_row:scale_row + 1, :],
                     mod_ref[shift_row:shift_row + 1, :])
        h_ref[...] = h.astype(BF16)

    o_ref[...] = jnp.dot(h_ref[...], w_ref[...], preferred_element_type=F32).astype(o_ref.dtype)


def _norm_proj(geom, x, mod_l, gvec, w, *, shift_row, scale_row, tm, tn):
    parts, x_specs, first = _token_parts(geom, x, tm)
    t, d = geom.tokens, parts[0].shape[1]
    n = w.shape[1]
    return pl.pallas_call(
        functools.partial(_norm_proj_kernel, n_x=len(parts), first=first, shift_row=shift_row,
                          scale_row=scale_row),
        grid=(t // tm, n // tn),
        in_specs=x_specs + [
            pl.BlockSpec((None, 6, d), lambda i, j: (_tile_seq(geom, i, tm), 0, 0)),
            pl.BlockSpec((1, d), lambda i, j: (0, 0)),
            pl.BlockSpec((d, tn), lambda i, j: (0, j)),
        ],
        out_specs=pl.BlockSpec((tm, tn), lambda i, j: (i, j)),
        out_shape=jax.ShapeDtypeStruct((t, n), BF16),
        scratch_shapes=[pltpu.VMEM((tm, d), BF16)],
        compiler_params=_cparams(("parallel", "arbitrary")),
        name="norm_proj",
    )(*parts, mod_l, gvec, w)


def _rot_halves(ref, h, cos, sin):
    a = ref[:, h * RET_DK:h * RET_DK + RET_HALF].astype(F32)
    b = ref[:, h * RET_DK + RET_HALF:(h + 1) * RET_DK].astype(F32)
    return a * cos - b * sin, a * sin + b * cos


def _ret_bwd_state_kernel(k_ref, v_ref, cos_ref, sin_ref, kdb_ref, rb_ref, r_ref, *, geom, chunk, cdec):
    c = pl.num_programs(0) - 1 - pl.program_id(0)
    is_last = _tile_pos(geom, c, chunk) + chunk == _tile_seq_len(geom, c, chunk)

    @pl.when(is_last)
    def _():
        r_ref[...] = jnp.zeros_like(r_ref)

    cos = cos_ref[...]
    sin = sin_ref[...]
    for h in range(RET_HEADS):
        k1, k2 = _rot_halves(k_ref, h, cos, sin)
        dec = kdb_ref[h]
        kd = jnp.concatenate([k1 * dec, k2 * dec], axis=1).astype(BF16)
        vh = v_ref[:, h * RET_DV:(h + 1) * RET_DV].astype(BF16)
        r = r_ref[h]
        rb_ref[h] = r.astype(BF16)
        upd = lax.dot_general(kd, vh, (((0,), (0,)), ((), ())), preferred_element_type=F32)
        r_ref[h] = r * cdec[h] + upd


def _ret_heads(heads, acc, fillers, q_ref, k_ref, v_ref, g_ref, rb_ref, cos_ref, sin_ref, dmat_ref, qdf_ref,
               qdb_ref, kdf_ref, wo_ref, rf_ref, cdec):
    cos = cos_ref[...]
    sin = sin_ref[...]
    pending = iter(fillers)

    def fill():
        piece = next(pending, None)
        if piece is not None:
            piece()

    for h in heads:
        q1, q2 = _rot_halves(q_ref, h, cos, sin)
        k1, k2 = _rot_halves(k_ref, h, cos, sin)
        qr = jnp.concatenate([q1, q2], axis=1).astype(BF16)
        kr = jnp.concatenate([k1, k2], axis=1).astype(BF16)
        vh = v_ref[:, h * RET_DV:(h + 1) * RET_DV].astype(BF16)
        s = lax.dot_general(qr, kr, (((1,), (1,)), ((), ())), preferred_element_type=F32) * dmat_ref[h]
        fill()
        o = jnp.dot(s.astype(BF16), vh, preferred_element_type=F32)
        fill()
        qdf = qdf_ref[h]
        qf = jnp.concatenate([q1 * qdf, q2 * qdf], axis=1).astype(BF16)
        rf = rf_ref[h]
        o = o + jnp.dot(qf, rf.astype(BF16), preferred_element_type=F32)
        fill()
        qdb = qdb_ref[h]
        qb = jnp.concatenate([q1 * qdb, q2 * qdb], axis=1).astype(BF16)
        o = o + jnp.dot(qb, rb_ref[h], preferred_element_type=F32)
        fill()
        kdf = kdf_ref[h]
        kf = jnp.concatenate([k1 * kdf, k2 * kdf], axis=1).astype(BF16)
        rf_ref[h] = rf * cdec[h] + lax.dot_general(kf, vh, (((0,), (0,)), ((), ())),
                                                   preferred_element_type=F32)
        fill()
        oc = o - jnp.mean(o, axis=-1, keepdims=True)
        on = oc * lax.rsqrt(jnp.mean(oc * oc, axis=-1, keepdims=True) + GN_EPS)
        og = (on * _silu(g_ref[:, h * RET_DV:(h + 1) * RET_DV].astype(F32))).astype(BF16)
        part = jnp.dot(og, wo_ref[h * RET_DV:(h + 1) * RET_DV, :], preferred_element_type=F32)
        acc = part if acc is None else acc + part
    for piece in pending:
        piece()
    return acc


def _retention_tables(chunk, s_max):
    log_gamma = jnp.log1p(-jnp.exp2(-5.0 - jnp.arange(RET_HEADS, dtype=F32)))
    idx = jnp.arange(chunk, dtype=F32)
    dist = jnp.abs(idx[:, None] - idx[None, :])
    dmat = jnp.exp(log_gamma[:, None, None] * dist[None])

    def rows(e):
        return jnp.broadcast_to(jnp.exp(log_gamma[:, None] * e[None, :])[:, :, None],
                                (RET_HEADS, chunk, RET_HALF))

    qdf = rows(idx + 1.0)
    qdb = rows(chunk - idx)
    kdf = rows(chunk - 1.0 - idx)
    kdb = rows(idx)
    theta = 1.0 / (ROPE_BASE ** jnp.linspace(0.0, 1.0, RET_HALF, dtype=F32))
    ang = jnp.arange(s_max, dtype=F32)[:, None] * theta[None, :]
    k_scale = RET_DK ** -0.5
    return jnp.cos(ang), jnp.sin(ang), dmat * k_scale, qdf, qdb, kdf * k_scale, kdb * k_scale


def _chunk_decay(chunk):
    lg = np.log1p(-np.exp2(-5.0 - np.arange(RET_HEADS, dtype=np.float32))).astype(np.float32)
    return tuple(float(v) for v in np.exp(lg * np.float32(chunk)).astype(np.float32))


def _retention(geom, proj_a, w_ret_o, tables, *, chunk):
    t = proj_a.shape[0]
    n_chunks = t // chunk
    cos, sin, dmat, qdf, qdb, kdf, kdb = tables
    cdec = _chunk_decay(chunk)
    state_shape = (RET_HEADS, RET_DK, RET_DV)

    def pos_blk(c):
        return _tile_pos(geom, c, chunk) // chunk

    rev = lambda i: n_chunks - 1 - i
    tab_spec = pl.BlockSpec((RET_HEADS, chunk, RET_HALF), lambda i: (0, 0, 0))
    rb = pl.pallas_call(
        functools.partial(_ret_bwd_state_kernel, geom=geom, chunk=chunk, cdec=cdec),
        grid=(n_chunks,),
        in_specs=[
            pl.BlockSpec((chunk, RET_QK_W), lambda i: (rev(i), 1)),
            pl.BlockSpec((chunk, RET_V_W), lambda i: (rev(i), 1)),
            pl.BlockSpec((chunk, RET_HALF), lambda i: (pos_blk(rev(i)), 0)),
            pl.BlockSpec((chunk, RET_HALF), lambda i: (pos_blk(rev(i)), 0)),
            tab_spec,
        ],
        out_specs=pl.BlockSpec((None,) + state_shape, lambda i: (rev(i), 0, 0, 0)),
        out_shape=jax.ShapeDtypeStruct((n_chunks,) + state_shape, BF16),
        scratch_shapes=[pltpu.VMEM(state_shape, F32)],
        compiler_params=_cparams(("arbitrary",)),
        name="ret_bwd_state",
    )(proj_a, proj_a, cos, sin, kdb)

    once = lambda shape: pl.BlockSpec(shape, lambda i: (0,) * len(shape), pipeline_mode=pl.Buffered(1))
    tab_once = once((RET_HEADS, chunk, RET_HALF))
    fwd_specs = [
        pl.BlockSpec((chunk, RET_QK_W), lambda i: (i, 0)),
        pl.BlockSpec((chunk, RET_QK_W), lambda i: (i, 1)),
        pl.BlockSpec((chunk, RET_V_W), lambda i: (i, 1)),
        pl.BlockSpec((chunk, RET_V_W), lambda i: (i, 2)),
        pl.BlockSpec((None,) + state_shape, lambda i: (i, 0, 0, 0)),
        pl.BlockSpec((chunk, RET_HALF), lambda i: (pos_blk(i), 0)),
        pl.BlockSpec((chunk, RET_HALF), lambda i: (pos_blk(i), 0)),
        once((RET_HEADS, chunk, chunk)),
        tab_once, tab_once, tab_once,
        once((RET_V_W, D_MODEL)),
    ]
    fwd_args = (proj_a, proj_a, proj_a, proj_a, rb, cos, sin, dmat, qdf, qdb, kdf, w_ret_o)
    return fwd_args, fwd_specs, cdec, pltpu.VMEM(state_shape, F32)


def _log_sigmoid(x):
    return jnp.minimum(x, 0.0) - jnp.log1p(jnp.exp(-jnp.abs(x)))


LRU_SEGS = V7X_SUBLANES
LRU_LEAD = CONV_LEFT * LRU_SEGS
LRU_TAIL = (CONV_WIDTH - 1 - CONV_LEFT) * LRU_SEGS


def _lru_gate_steps(d, x_ref, prev_ref, next_ref, keep_prev, keep_next, xs_ref, wconv_ref, bconv_ref, wrg_ref,
                    brg_ref, wig_ref, big_ref, lam_ref, a_ref, u_ref):
    tl = x_ref.shape[0]
    seg = tl // LRU_SEGS
    prev_hi = prev_ref[...].astype(F32)[LRU_HALO - V7X_SUBLANES:] * keep_prev
    next_lo = next_ref[...].astype(F32)[:V7X_SUBLANES] * keep_next
    row8 = lax.broadcasted_iota(jnp.int32, (V7X_SUBLANES, LRU_BW), 0)
    last = V7X_SUBLANES - 1
    c = (0.5 * LRU_C * math.log2(math.e)) * _log_sigmoid(lam_ref[d])

    def block(n):
        sl = slice(n * LRU_BW, (n + 1) * LRU_BW)
        x = x_ref[:, sl].astype(F32)
        for s in range(LRU_SEGS):
            xs_ref[n, pl.ds(LRU_LEAD + s, seg, stride=LRU_SEGS), :] = x[s * seg:(s + 1) * seg]
        slab = lambda t: xs_ref[n, LRU_LEAD + t * LRU_SEGS:LRU_LEAD + (t + 1) * LRU_SEGS, :]
        p = prev_hi[:, sl]
        m1 = pltpu.roll(jnp.where(row8 == last, p, slab(seg - 1)), 1, axis=0)
        m2 = pltpu.roll(jnp.where(row8 == last, pltpu.roll(p, 1, axis=0), slab(seg - 2)), 1, axis=0)
        p1 = pltpu.roll(jnp.where(row8 == 0, next_lo[:, sl], slab(0)), last, axis=0)
        xs_ref[n, 0:LRU_SEGS, :] = m2
        xs_ref[n, LRU_SEGS:LRU_LEAD, :] = m1
        xs_ref[n, LRU_LEAD + tl:LRU_LEAD + tl + LRU_TAIL, :] = p1

        xh = bconv_ref[:, sl] + xs_ref[n, 0:tl, :] * wconv_ref[0:1, sl]
        for j in range(1, CONV_WIDTH):
            xh = xh + xs_ref[n, j * LRU_SEGS:j * LRU_SEGS + tl, :] * wconv_ref[j:j + 1, sl]
        xb = xh.astype(BF16)
        th_r = jnp.tanh(jnp.dot(xb, wrg_ref[d, n], preferred_element_type=F32) + brg_ref[d][:, sl])
        th_i = jnp.tanh(jnp.dot(xb, wig_ref[d, n], preferred_element_type=F32) + big_ref[d][:, sl])
        a = jnp.exp2(c[:, sl] * th_r + c[:, sl])
        a_ref[n] = a
        y = 1.0 - a * a
        u_ref[n] = (y * lax.rsqrt(jnp.maximum(y, SQRT_TINY))) * ((th_i + 1.0) * xh)

    return [functools.partial(block, n) for n in range(LRU_BLOCKS)]


def _lru_scan_loop(a_ref, u_ref, hs_ref, as_ref, reverse):
    tl = a_ref.shape[1]
    seg = tl // LRU_SEGS
    slab_shape = (LRU_SEGS, LRU_BW)

    def step(k, carry):
        t = seg - 1 - k if reverse else k
        rows = pl.ds(pl.multiple_of(t * LRU_SEGS, LRU_SEGS), LRU_SEGS)
        hs, decays = [], []
        for n in range(LRU_BLOCKS):
            a = a_ref[n, rows, :]
            h = a * carry[0][n] + u_ref[n, rows, :]
            decay = a * carry[1][n]
            hs_ref[n, rows, :] = h
            as_ref[n, rows, :] = decay
            hs.append(h)
            decays.append(decay)
        return tuple(hs), tuple(decays)

    init = (tuple(jnp.zeros(slab_shape, F32) for _ in range(LRU_BLOCKS)),
            tuple(jnp.ones(slab_shape, F32) for _ in range(LRU_BLOCKS)))
    return lax.fori_loop(0, seg, step, init, unroll=4)


def _lru_scan_finish(ends, hs_ref, as_ref, fin_ref, carry_ref, o_ref, reverse):
    h_end, decay_end = ends
    tl = as_ref.shape[1]
    seg = tl // LRU_SEGS
    order = range(LRU_SEGS - 1, -1, -1) if reverse else range(LRU_SEGS)
    for n in range(LRU_BLOCKS):
        sl = slice(n * LRU_BW, (n + 1) * LRU_BW)
        fin_ref[0] = h_end[n]
        fin_ref[1] = decay_end[n]
        c = carry_ref[n, 0:1, :]
        for s in order:
            fin_ref[2, s:s + 1, :] = c
            c = fin_ref[0, s:s + 1, :] + fin_ref[1, s:s + 1, :] * c
        carry_ref[n, 0:1, :] = c
        for s in range(LRU_SEGS):
            seg_rows = pl.ds(s, seg, stride=LRU_SEGS)
            h = hs_ref[n, seg_rows, :] + as_ref[n, seg_rows, :] * fin_ref[2, s:s + 1, :]
            o_ref[s * seg:(s + 1) * seg, sl] = h.astype(o_ref.dtype)


N_RET_REFS = 12
N_LRU_REFS = 13


def _mixer_kernel(*refs, geom, tl, cdec):
    ret = refs[:N_RET_REFS]
    (xf_ref, xfp_ref, xfn_ref, xb_ref, xbp_ref, xbn_ref, wconv_ref, bconv_ref, wrg_ref, brg_ref, wig_ref, big_ref,
     lam_ref) = refs[N_RET_REFS:N_RET_REFS + N_LRU_REFS]
    (ya_ref, hf_ref, hb_ref, rf_ref, acc_ref, xs_ref, a_ref, u_ref, as_ref, fin_ref, cf_ref,
     cb_ref) = refs[N_RET_REFS + N_LRU_REFS:]
    hs_ref = xs_ref
    i = pl.program_id(0)
    ib = pl.num_programs(0) - 1 - i

    def flags(tile):
        pos = _tile_pos(geom, tile, tl)
        return pos == 0, pos + tl == _tile_seq_len(geom, tile, tl)

    def keep(flag):
        return jnp.where(flag, 0.0, 1.0)

    def gates(d, x_ref, prev_ref, next_ref, first, last):
        return _lru_gate_steps(d, x_ref, prev_ref, next_ref, keep(first), keep(last), xs_ref, wconv_ref, bconv_ref,
                               wrg_ref, brg_ref, wig_ref, big_ref, lam_ref, a_ref, u_ref)

    def heads(which, acc, fillers):
        return _ret_heads(which, acc, fillers, *ret, rf_ref, cdec)

    first_f, last_f = flags(i)
    first_b, last_b = flags(ib)

    @pl.when(first_f)
    def _():
        rf_ref[...] = jnp.zeros_like(rf_ref)
        cf_ref[...] = jnp.zeros_like(cf_ref)

    @pl.when(last_b)
    def _():
        cb_ref[...] = jnp.zeros_like(cb_ref)

    half = RET_HEADS // 2
    acc_ref[...] = heads(range(half), None, gates(0, xf_ref, xfp_ref, xfn_ref, first_f, last_f))
    ends = _lru_scan_loop(a_ref, u_ref, hs_ref, as_ref, reverse=False)
    _lru_scan_finish(ends, hs_ref, as_ref, fin_ref, cf_ref, hf_ref, reverse=False)
    ya_ref[...] = heads(range(half, RET_HEADS), acc_ref[...],
                        gates(1, xb_ref, xbp_ref, xbn_ref, first_b, last_b)).astype(ya_ref.dtype)
    ends = _lru_scan_loop(a_ref, u_ref, hs_ref, as_ref, reverse=True)
    _lru_scan_finish(ends, hs_ref, as_ref, fin_ref, cb_ref, hb_ref, reverse=True)


def _mixer(geom, proj_a, proj_b, w_ret_o, tables, w_conv, b_conv, w_rg, b_rg, w_ig, b_ig, lam, *, tl):
    t = proj_b.shape[0]
    n = t // tl
    r8 = tl // LRU_HALO
    n8 = t // LRU_HALO
    w = LRU_WIDTH
    rev = lambda i: n - 1 - i
    prev_blk = lambda i: jnp.maximum(i * r8 - 1, 0)
    next_blk = lambda i: jnp.minimum((i + 1) * r8, n8 - 1)
    full = lambda a: pl.BlockSpec(a.shape, lambda i: (0,) * a.ndim)
    w_conv = 0.5 * w_conv
    b_conv2 = 0.5 * b_conv.reshape(1, w)
    b_rg3 = 0.5 * b_rg.reshape(2, 1, w)
    b_ig3 = 0.5 * b_ig.reshape(2, 1, w)
    lam3 = lam.reshape(2, 1, w)
    ret_args, ret_specs, cdec, ret_state = _retention(geom, proj_a, w_ret_o, tables, chunk=tl)
    lru_args = (proj_b, proj_b, proj_b, proj_b, proj_b, proj_b, w_conv, b_conv2, w_rg, b_rg3, w_ig, b_ig3, lam3)
    assert (len(ret_args), len(lru_args)) == (N_RET_REFS, N_LRU_REFS)
    return pl.pallas_call(
        functools.partial(_mixer_kernel, geom=geom, tl=tl, cdec=cdec),
        grid=(n,),
        in_specs=ret_specs + [
            pl.BlockSpec((tl, w), lambda i: (i, 0)),
            pl.BlockSpec((LRU_HALO, w), lambda i: (prev_blk(i), 0)),
            pl.BlockSpec((LRU_HALO, w), lambda i: (next_blk(i), 0)),
            pl.BlockSpec((tl, w), lambda i: (rev(i), 0)),
            pl.BlockSpec((LRU_HALO, w), lambda i: (prev_blk(rev(i)), 0)),
            pl.BlockSpec((LRU_HALO, w), lambda i: (next_blk(rev(i)), 0)),
            full(w_conv), full(b_conv2), full(w_rg), full(b_rg3), full(w_ig), full(b_ig3), full(lam3),
        ],
        out_specs=[
            pl.BlockSpec((tl, D_MODEL), lambda i: (i, 0)),
            pl.BlockSpec((tl, w), lambda i: (i, 0)),
            pl.BlockSpec((tl, w), lambda i: (rev(i), 0)),
        ],
        out_shape=[jax.ShapeDtypeStruct((t, D_MODEL), BF16), jax.ShapeDtypeStruct((t, w), BF16),
                   jax.ShapeDtypeStruct((t, w), BF16)],
        scratch_shapes=[ret_state, pltpu.VMEM((tl, D_MODEL), F32),
                        pltpu.VMEM((LRU_BLOCKS, LRU_LEAD + tl + LRU_TAIL, LRU_BW), F32)]
        + [pltpu.VMEM((LRU_BLOCKS, tl, LRU_BW), F32)] * 3
        + [pltpu.VMEM((3, LRU_SEGS, LRU_BW), F32)]
        + [pltpu.VMEM((LRU_BLOCKS, V7X_SUBLANES, LRU_BW), F32)] * 2,
        compiler_params=pltpu.CompilerParams(dimension_semantics=("arbitrary",),
                                             vmem_limit_bytes=MIXER_VMEM_LIMIT),
        name="mixer",
    )(*ret_args, *lru_args)


def _gelu_tanh(x):
    return 0.5 * x * (1.0 + jnp.tanh(math.sqrt(2.0 / math.pi) * (x + 0.044715 * (x * x * x))))


def _merge_kernel(*refs, n_x, first):
    x_refs = refs[:n_x]
    ya_ref, hf_ref, hb_ref, gl_ref, ga_ref, gb_ref, mod_ref, wl_ref, wo_ref, o_ref = refs[n_x:]
    f32 = lambda ref: ref[...].astype(F32)
    y = ((f32(hf_ref) + f32(hb_ref)) * _gelu_tanh(f32(gl_ref))).astype(BF16)
    yb = jnp.dot(y, wl_ref[...], preferred_element_type=F32)
    m = _sigmoid(f32(ga_ref)) * f32(ya_ref) + _sigmoid(f32(gb_ref)) * yb
    mix = jnp.dot(m.astype(BF16), wo_ref[...], preferred_element_type=F32)
    o_ref[...] = _token_tile(x_refs, first) + mod_ref[2:3, :] * mix


def _merge(geom, x, ya, hf, hb, proj_a, proj_b, mod_l, w_lru_o, w_out, *, tm):
    parts, x_specs, first = _token_parts(geom, x, tm)
    t, d = geom.tokens, parts[0].shape[1]
    gate_a_blk = (2 * RET_QK_W + 2 * RET_V_W) // d
    tok = lambda w: pl.BlockSpec((tm, w), lambda i: (i, 0))
    return pl.pallas_call(
        functools.partial(_merge_kernel, n_x=len(parts), first=first),
        grid=(t // tm,),
        in_specs=x_specs + [
            tok(d), tok(LRU_WIDTH), tok(LRU_WIDTH),
            pl.BlockSpec((tm, LRU_WIDTH), lambda i: (i, 1)),
            pl.BlockSpec((tm, d), lambda i: (i, gate_a_blk)),
            pl.BlockSpec((tm, d), lambda i: (i, gate_a_blk + 1)),
            pl.BlockSpec((None, 6, d), lambda i: (_tile_seq(geom, i, tm), 0, 0)),
            pl.BlockSpec((LRU_WIDTH, d), lambda i: (0, 0)),
            pl.BlockSpec((d, d), lambda i: (0, 0)),
        ],
        out_specs=tok(d),
        out_shape=jax.ShapeDtypeStruct((t, d), F32),
        compiler_params=_cparams(("parallel",)),
        name="merge",
    )(*parts, ya, hf, hb, proj_b, proj_a, proj_a, mod_l, w_lru_o, w_out)


def _ffn_kernel(x_ref, mod_ref, g_ref, wg_ref, wu_ref, wd_ref, o_ref, *, col_chunks):
    x = x_ref[...]
    h = _rms_mod(x, g_ref[...], mod_ref[4:5, :], mod_ref[3:4, :]).astype(BF16)
    acc = None
    for c0, c1 in col_chunks:
        a = jnp.dot(h, wg_ref[:, c0:c1], preferred_element_type=F32)
        u = jnp.dot(h, wu_ref[:, c0:c1], preferred_element_type=F32)
        part = jnp.dot((_silu(a) * u).astype(BF16), wd_ref[c0:c1, :], preferred_element_type=F32)
        acc = part if acc is None else acc + part
    o_ref[...] = x + mod_ref[5:6, :] * acc


def _ffn(geom, x, mod_l, gvec, wg, wu, wd, *, tm, tf):
    t, d = x.shape
    ff = wg.shape[1]
    col_chunks = tuple((c, min(c + tf, ff)) for c in range(0, ff, tf))
    resident = lambda shape: pl.BlockSpec(shape, lambda i: (0, 0), pipeline_mode=pl.Buffered(1))
    return pl.pallas_call(
        functools.partial(_ffn_kernel, col_chunks=col_chunks),
        grid=(t // tm,),
        in_specs=[
            pl.BlockSpec((tm, d), lambda i: (i, 0)),
            pl.BlockSpec((None, 6, d), lambda i: (_tile_seq(geom, i, tm), 0, 0)),
            pl.BlockSpec((1, d), lambda i: (0, 0)),
            resident((d, ff)), resident((d, ff)), resident((ff, d)),
        ],
        out_specs=pl.BlockSpec((tm, d), lambda i: (i, 0)),
        out_shape=jax.ShapeDtypeStruct((t, d), F32),
        compiler_params=_cparams(("parallel",)),
        name="ffn_dense",
    )(x, mod_l, gvec, wg, wu, wd)


ROUTE_E0, ROUTE_E1, ROUTE_R0, ROUTE_R1, ROUTE_G0, ROUTE_G1 = range(6)

ROW_TILES = D_MODEL // V7X_LANES


def _store_token_rows(ref, val):
    rows = val.shape[0]
    for j in range(ROW_TILES):
        ref[pl.ds(j, rows, stride=ROW_TILES), :] = val[:, j * V7X_LANES:(j + 1) * V7X_LANES]


def _load_token_rows(ref, rows):
    return jnp.concatenate([ref[pl.ds(j, rows, stride=ROW_TILES), :] for j in range(ROW_TILES)], axis=1)


def _router_kernel(x_ref, mod_ref, g_ref, wr_ref, h_ref, route_ref, cnt_ref, carry_ref):
    i = pl.program_id(0)
    tm = x_ref.shape[0]

    @pl.when(i == 0)
    def _():
        carry_ref[...] = jnp.zeros_like(carry_ref)

    h = _rms_mod(x_ref[...], g_ref[...], mod_ref[4:5, :], mod_ref[3:4, :])
    _store_token_rows(h_ref, h)
    h_hi = h.astype(BF16)
    h_lo = (h - h_hi.astype(F32)).astype(BF16)
    logits = (jnp.dot(h_hi, wr_ref[0], preferred_element_type=F32)
              + jnp.dot(h_lo, wr_ref[0], preferred_element_type=F32)
              + jnp.dot(h_hi, wr_ref[1], preferred_element_type=F32))
    lane = lax.broadcasted_iota(jnp.int32, logits.shape, 1).astype(F32)
    logits = jnp.where(lane < N_EXPERTS, logits, -jnp.inf)
    m1 = jnp.max(logits, axis=-1, keepdims=True)
    i1 = jnp.min(jnp.where(logits == m1, lane, float(V7X_LANES)), axis=-1, keepdims=True)
    rest = jnp.where(lane == i1, -jnp.inf, logits)
    m2 = jnp.max(rest, axis=-1, keepdims=True)
    i2 = jnp.min(jnp.where(rest == m2, lane, float(V7X_LANES)), axis=-1, keepdims=True)
    ex = jnp.exp(m2 - m1)
    g1 = 1.0 / (1.0 + ex)
    g2 = ex / (1.0 + ex)
    sel1 = lane == i1
    sel2 = lane == i2
    onehot = jnp.where(sel1 | sel2, 1.0, 0.0)
    row = lax.broadcasted_iota(jnp.int32, (tm, tm), 0)
    col = lax.broadcasted_iota(jnp.int32, (tm, tm), 1)
    lower = jnp.where(col < row, 1.0, 0.0).astype(BF16)
    before = jnp.dot(lower, onehot.astype(BF16), preferred_element_type=F32) + carry_ref[0:1, :]
    r1 = jnp.sum(jnp.where(sel1, before, 0.0), axis=-1, keepdims=True)
    r2 = jnp.sum(jnp.where(sel2, before, 0.0), axis=-1, keepdims=True)
    out_lane = lax.broadcasted_iota(jnp.int32, route_ref.shape, 1)
    vals = (i1.astype(F32), i2.astype(F32), r1, r2, g1, g2)
    packed = jnp.zeros(route_ref.shape, F32)
    for slot, v in enumerate(vals):
        packed = jnp.where(out_lane == slot, v, packed)
    route_ref[...] = packed
    carry = carry_ref[0:1, :] + jnp.sum(onehot, axis=0, keepdims=True)
    carry_ref[0:1, :] = carry
    cnt_ref[...] = jnp.broadcast_to(carry, cnt_ref.shape).astype(jnp.int32)


def _router(geom, x, mod_l, gvec, w_router, *, tm):
    t, d = x.shape
    w_pad = jnp.pad(w_router, ((0, 0), (0, V7X_LANES - N_EXPERTS)))
    w_hi = w_pad.astype(BF16)
    w_pad = jnp.stack([w_hi, (w_pad - w_hi.astype(F32)).astype(BF16)])
    return pl.pallas_call(
        _router_kernel,
        grid=(t // tm,),
        in_specs=[
            pl.BlockSpec((tm, d), lambda i: (i, 0)),
            pl.BlockSpec((None, 6, d), lambda i: (_tile_seq(geom, i, tm), 0, 0)),
            pl.BlockSpec((1, d), lambda i: (0, 0)),
            pl.BlockSpec((2, d, V7X_LANES), lambda i: (0, 0, 0)),
        ],
        out_specs=[
            pl.BlockSpec((tm * ROW_TILES, V7X_LANES), lambda i: (i, 0)),
            pl.BlockSpec((tm, V7X_LANES), lambda i: (i, 0)),
            pl.BlockSpec((V7X_SUBLANES, V7X_LANES), lambda i: (0, 0)),
        ],
        out_shape=[
            jax.ShapeDtypeStruct((t * ROW_TILES, V7X_LANES), F32),
            jax.ShapeDtypeStruct((t, V7X_LANES), F32),
            jax.ShapeDtypeStruct((V7X_SUBLANES, V7X_LANES), jnp.int32),
        ],
        scratch_shapes=[pltpu.VMEM((V7X_SUBLANES, V7X_LANES), F32)],
        compiler_params=_cparams(("arbitrary",)),
        name="router",
    )(x, mod_l, gvec, w_pad)


def _token_copy(src, src_tok, dst, dst_tok, sem):
    s = pl.multiple_of(src_tok * ROW_TILES, ROW_TILES)
    d = pl.multiple_of(dst_tok * ROW_TILES, ROW_TILES)
    return pltpu.make_async_copy(src.at[pl.ds(s, ROW_TILES), :], dst.at[pl.ds(d, ROW_TILES), :], sem)


def _dispatch_kernel(fill_ref, dest_ref, src_ref, out_hbm, zero_ref, sem, zero_sem, *, td):
    @pl.when(pl.program_id(0) == 0)
    def _():
        zero_ref[...] = jnp.zeros_like(zero_ref)
        for wait in (False, True):
            for e in range(N_EXPERTS):
                def pad_row(r, carry):
                    copy = _token_copy(zero_ref, 0, out_hbm, r, zero_sem)
                    copy.wait() if wait else copy.start()
                    return carry

                lax.fori_loop(fill_ref[0, e], fill_ref[1, e], pad_row, 0)

    def issue(t, carry):
        for k in range(TOP_K):
            _token_copy(src_ref, t, out_hbm, dest_ref[k, t], sem).start(priority=k % DMA_PRIORITIES)
        return carry

    lax.fori_loop(0, td, issue, 0, unroll=8)
    for _ in range(TOP_K):
        pltpu.make_async_copy(src_ref, out_hbm.at[pl.ds(0, td * ROW_TILES), :], sem).wait()


def _dispatch(h2_rows, dest, fill, n_rows, *, td):
    grid_spec = pltpu.PrefetchScalarGridSpec(
        num_scalar_prefetch=1,
        grid=(dest.shape[1] // td,),
        in_specs=[
            pl.BlockSpec((TOP_K, td), lambda i, fill: (0, i), memory_space=pltpu.SMEM),
            pl.BlockSpec((td * ROW_TILES, V7X_LANES), lambda i, fill: (i, 0)),
        ],
        out_specs=pl.BlockSpec(memory_space=pl.ANY),
        scratch_shapes=[pltpu.VMEM((ROW_TILES, V7X_LANES), h2_rows.dtype),
                        pltpu.SemaphoreType.DMA(()), pltpu.SemaphoreType.DMA(())],
    )
    return pl.pallas_call(
        functools.partial(_dispatch_kernel, td=td),
        grid_spec=grid_spec,
        out_shape=jax.ShapeDtypeStruct((n_rows * ROW_TILES, V7X_LANES), h2_rows.dtype),
        compiler_params=_cparams(("arbitrary",)),
        name="moe_dispatch",
    )(fill, dest, h2_rows)


V7X_MXU_COLS = 256


def _experts_kernel(blk_e_ref, nused_ref, x_ref, wg_ref, wu_ref, wd_ref, o_ref, hm_ref, *, ff_chunks):
    del blk_e_ref
    b = pl.program_id(0)
    bm = hm_ref.shape[0]

    @pl.when(b < nused_ref[0])
    def _():
        xb = _load_token_rows(x_ref, bm).astype(BF16)
        for c0, c1 in ff_chunks:
            a = jnp.dot(xb, wg_ref[:, c0:c1], preferred_element_type=F32)
            u = jnp.dot(xb, wu_ref[:, c0:c1], preferred_element_type=F32)
            hm_ref[:, c0:c1] = (_silu(a) * u).astype(BF16)
        for c0 in range(0, D_MODEL, V7X_MXU_COLS):
            y = jnp.dot(hm_ref[...], wd_ref[:, c0:c0 + V7X_MXU_COLS], preferred_element_type=F32)
            for k in range(V7X_MXU_COLS // V7X_LANES):
                o_ref[pl.ds(c0 // V7X_LANES + k, bm, stride=ROW_TILES), :] = y[:, k * V7X_LANES:(k + 1) * V7X_LANES]

    @pl.when(b >= nused_ref[0])
    def _():
        o_ref[...] = jnp.zeros_like(o_ref)


def _experts(xbuf, blk_e, n_used, wg, wu, wd, *, bm, tf):
    p = xbuf.shape[0] // ROW_TILES
    d = D_MODEL
    ff = wg.shape[2]
    ff_chunks = tuple((c, min(c + tf, ff)) for c in range(0, ff, tf))
    rows_blk = (bm * ROW_TILES, V7X_LANES)

    def expert(b, be, nu):
        return be[jnp.minimum(b, nu[0] - 1)]

    resident = lambda shape: pl.BlockSpec((None,) + shape, lambda b, be, nu: (expert(b, be, nu), 0, 0),
                                          pipeline_mode=pl.Buffered(1))
    grid_spec = pltpu.PrefetchScalarGridSpec(
        num_scalar_prefetch=2,
        grid=(p // bm,),
        in_specs=[
            pl.BlockSpec(rows_blk, lambda b, be, nu: (b, 0)),
            resident((d, ff)), resident((d, ff)), resident((ff, d)),
        ],
        out_specs=pl.BlockSpec(rows_blk, lambda b, be, nu: (b, 0)),
        scratch_shapes=[pltpu.VMEM((bm, ff), BF16)],
    )
    return pl.pallas_call(
        functools.partial(_experts_kernel, ff_chunks=ff_chunks),
        grid_spec=grid_spec,
        out_shape=jax.ShapeDtypeStruct(xbuf.shape, F32),
        compiler_params=_cparams(("arbitrary",)),
        name="moe_experts",
    )(blk_e, n_used, xbuf, wg, wu, wd)


def _moe_out_kernel(dest_ref, dest_next_ref, x_ref, route_ref, mod_ref, gf_ref, ybuf_hbm, op_ref, os_ref,
                    ya0_ref, ya1_ref, yb0_ref, yb1_ref, sems, *, final_norm, prompt_tiles):
    i = pl.program_id(0)
    tm = x_ref.shape[0]
    bufs = ((ya0_ref, ya1_ref), (yb0_ref, yb1_ref))

    def gather(d_ref, slot, inline=False):
        def issue(t, carry):
            for k in range(TOP_K):
                _token_copy(ybuf_hbm, d_ref[k, t], bufs[slot][k], t, sems.at[slot]).start(
                    priority=k % DMA_PRIORITIES)
            return carry

        if inline:
            for t in range(tm):
                issue(t, 0)
        else:
            lax.fori_loop(0, tm, issue, 0, unroll=8)

    def wait(slot):
        for k in range(TOP_K):
            pltpu.make_async_copy(ybuf_hbm.at[pl.ds(0, tm * ROW_TILES), :], bufs[slot][k], sems.at[slot]).wait()

    def combine(slot):
        g1 = route_ref[:, ROUTE_G0:ROUTE_G0 + 1]
        g2 = route_ref[:, ROUTE_G1:ROUTE_G1 + 1]
        y = _load_token_rows(bufs[slot][0], tm) * g1 + _load_token_rows(bufs[slot][1], tm) * g2
        x = x_ref[...] + mod_ref[5:6, :] * y
        if final_norm:
            ms = jnp.mean(x * x, axis=-1, keepdims=True)
            x = x * lax.rsqrt(ms + RMS_EPS) * gf_ref[...]

        @pl.when(i < prompt_tiles)
        def _():
            op_ref[...] = x

        @pl.when(i >= prompt_tiles)
        def _():
            os_ref[...] = x

    @pl.when(i == 0)
    def _():
        gather(dest_ref, 0)

    for slot in range(2):
        @pl.when(i % 2 == slot)
        def _():
            wait(slot)
            gather(dest_next_ref, 1 - slot, inline=True)
            combine(slot)

            @pl.when(i + 1 == pl.num_programs(0))
            def _():
                wait(1 - slot)


def _moe_out(geom, x, ybuf, dest, route, mod_l, g_final, *, tm, final_norm):
    t, d = x.shape
    n = t // tm
    prompt_tiles = geom.tokens_p // tm
    y_buf = pltpu.VMEM((tm * ROW_TILES, V7X_LANES), F32)
    return pl.pallas_call(
        functools.partial(_moe_out_kernel, final_norm=final_norm, prompt_tiles=prompt_tiles),
        grid=(n,),
        in_specs=[
            pl.BlockSpec((TOP_K, tm), lambda i: (0, i), memory_space=pltpu.SMEM),
            pl.BlockSpec((TOP_K, tm), lambda i: (0, jnp.minimum(i + 1, n - 1)), memory_space=pltpu.SMEM),
            pl.BlockSpec((tm, d), lambda i: (i, 0)),
            pl.BlockSpec((tm, V7X_LANES), lambda i: (i, 0)),
            pl.BlockSpec((None, 6, d), lambda i: (_tile_seq(geom, i, tm), 0, 0)),
            pl.BlockSpec((1, d), lambda i: (0, 0)),
            pl.BlockSpec(memory_space=pl.ANY),
        ],
        out_specs=[
            pl.BlockSpec((tm, d), lambda i: (jnp.minimum(i, prompt_tiles - 1), 0)),
            pl.BlockSpec((tm, d), lambda i: (jnp.maximum(i - prompt_tiles, 0), 0)),
        ],
        out_shape=[jax.ShapeDtypeStruct((geom.tokens_p, d), F32),
                   jax.ShapeDtypeStruct((t - geom.tokens_p, d), F32)],
        scratch_shapes=[y_buf, y_buf, y_buf, y_buf, pltpu.SemaphoreType.DMA((2,))],
        compiler_params=_cparams(("arbitrary",)),
        name="moe_out",
    )(dest, dest, x, route, mod_l, g_final, ybuf)


def _final_norm_kernel(x_ref, g_ref, o_ref):
    x = x_ref[...]
    o_ref[...] = x * lax.rsqrt(jnp.mean(x * x, axis=-1, keepdims=True) + RMS_EPS) * g_ref[...]


def _final_norm(x, g_final, *, tm):
    t, d = x.shape
    return pl.pallas_call(
        _final_norm_kernel,
        grid=(t // tm,),
        in_specs=[pl.BlockSpec((tm, d), lambda i: (i, 0)), pl.BlockSpec((1, d), lambda i: (0, 0))],
        out_specs=pl.BlockSpec((tm, d), lambda i: (i, 0)),
        out_shape=jax.ShapeDtypeStruct((t, d), F32),
        compiler_params=_cparams(("parallel",)),
        name="final_norm",
    )(x, g_final)


def _moe(geom, x, mod_l, gvec, w_router, wg, wu, wd, g_final, *, tiles, final_norm):
    t, d = x.shape
    bm = tiles["moe_rows"]
    h2, route, counts = _router(geom, x, mod_l, gvec, w_router, tm=tiles["router"])
    counts = counts[0, :N_EXPERTS]
    padded = ((counts + bm - 1) // bm) * bm
    pad_end = jnp.cumsum(padded)
    pad_start = pad_end - padded
    experts = route[:, ROUTE_E0:ROUTE_E1 + 1].astype(jnp.int32)
    ranks = route[:, ROUTE_R0:ROUTE_R1 + 1].astype(jnp.int32)
    dest = (pad_start[experts] + ranks).T
    n_rows = t * TOP_K + N_EXPERTS * bm
    n_blk = n_rows // bm
    blk_start = jnp.arange(n_blk, dtype=jnp.int32) * bm
    blk_e = jnp.minimum(jnp.sum((pad_end[None, :] <= blk_start[:, None]).astype(jnp.int32), axis=1),
                        N_EXPERTS - 1).astype(jnp.int32)
    n_used = (pad_end[-1:] // bm).astype(jnp.int32)
    fill = jnp.stack([pad_start + counts, pad_end.at[N_EXPERTS - 1].set(n_rows)]).astype(jnp.int32)
    xbuf = _dispatch(h2, dest, fill, n_rows, td=tiles["dma_rows"])
    ybuf = _experts(xbuf, blk_e, n_used, wg, wu, wd, bm=bm, tf=tiles["expert_ff"])
    return _moe_out(geom, x, ybuf, dest, route, mod_l, g_final, tm=tiles["token"], final_norm=final_norm)


def _pick_tiles(geom):
    s = math.gcd(geom.s_p, geom.s_s) if geom.n_p and geom.n_s else (geom.s_p if geom.n_p else geom.s_s)
    return {
        "proj": min(1024, s),
        "proj_cols": 2048,
        "mixer": min(512, s),
        "token": min(512, s),
        "ffn_cols": 1536,
        "router": min(512, s),
        "moe_rows": 512,
        "expert_ff": 1792,
        "dma_rows": min(2048, s),
    }


def _trunk(geom, x, c_all, w_ada, b_ada, g_norm1, g_norm2, w_in, w_conv, b_conv, w_rg, b_rg, w_ig, b_ig,
           lru_lambda, w_ret_o, w_lru_o, w_out, w_ff_gate, w_ff_up, w_ff_down,
           w_router, w_e_gate, w_e_up, w_e_down, g_final, tiles):
    depth = w_in.shape[0]
    d = D_MODEL
    n_pad = -(-geom.n_seq // V7X_SUBLANES) * V7X_SUBLANES
    c_pad = jnp.pad(c_all, ((0, n_pad - geom.n_seq), (0, 0)))
    mod = _adaln_mod(c_pad, w_ada, b_ada)
    tables = _retention_tables(tiles["mixer"], max(geom.s_p if geom.n_p else 0, geom.s_s if geom.n_s else 0))

    o_xl = 2 * RET_QK_W + 2 * RET_V_W
    o_ga = o_xl + 2 * LRU_WIDTH
    for l in range(depth):
        wl = w_in[l]
        w_a = jnp.concatenate([wl[:, :o_xl], wl[:, o_ga:]], axis=1).astype(BF16)
        w_b = wl[:, o_xl:o_ga].astype(BF16)
        g1 = g_norm1[l].reshape(1, d)
        g2 = g_norm2[l].reshape(1, d)
        proj_a = _norm_proj(geom, x, mod[l], g1, w_a, shift_row=0, scale_row=1,
                            tm=tiles["proj"], tn=tiles["proj_cols"])
        proj_b = _norm_proj(geom, x, mod[l], g1, w_b, shift_row=0, scale_row=1,
                            tm=tiles["proj"], tn=LRU_WIDTH)
        ya, hf, hb = _mixer(geom, proj_a, proj_b, w_ret_o[l].astype(BF16), tables, w_conv[l], b_conv[l],
                            w_rg[l].astype(BF16), b_rg[l], w_ig[l].astype(BF16), b_ig[l], lru_lambda[l],
                            tl=tiles["mixer"])
        x = _merge(geom, x, ya, hf, hb, proj_a, proj_b, mod[l], w_lru_o[l].astype(BF16),
                   w_out[l].astype(BF16), tm=tiles["token"])
        j = l // 2
        last = l == depth - 1
        if l % 2 == 0:
            x = _ffn(geom, x, mod[l], g2, w_ff_gate[j].astype(BF16), w_ff_up[j].astype(BF16),
                     w_ff_down[j].astype(BF16), tm=tiles["token"], tf=tiles["ffn_cols"])
            if last:
                x = _final_norm(x, g_final.reshape(1, d), tm=tiles["token"])
                return x[:geom.tokens_p], x[geom.tokens_p:]
        else:
            parts = _moe(geom, x, mod[l], g2, w_router[j], w_e_gate[j].astype(BF16), w_e_up[j].astype(BF16),
                         w_e_down[j].astype(BF16), g_final.reshape(1, d), tiles=tiles, final_norm=last)
            if last:
                return parts
            x = jnp.concatenate(parts, axis=0)


def kernel(x_prompt, x_sample, c_prompt, c_sample, w_ada, b_ada, g_norm1, g_norm2, w_in, w_conv, b_conv, w_rg, b_rg, w_ig, b_ig, lru_lambda, w_ret_o, w_lru_o, w_out, w_ff_gate, w_ff_up, w_ff_down, w_router, w_e_gate, w_e_up, w_e_down, g_final):
    n_p, s_p, d = x_prompt.shape
    n_s, s_s, _ = x_sample.shape
    geom = Geom(n_p, s_p, n_s, s_s)
    x = (x_prompt.reshape(-1, d), x_sample.reshape(-1, d))
    c_all = jnp.concatenate([c_prompt, c_sample], axis=0)
    yp, ys = _trunk(geom, x, c_all, w_ada, b_ada, g_norm1, g_norm2, w_in, w_conv, b_conv, w_rg, b_rg, w_ig, b_ig,
                    lru_lambda, w_ret_o, w_lru_o, w_out, w_ff_gate, w_ff_up, w_ff_down,
                    w_router, w_e_gate, w_e_up, w_e_down, g_final, _pick_tiles(geom))
    return (yp.reshape(n_p, s_p, d), ys.reshape(n_s, s_s, d))
```

```python
import functools
import math
from typing import NamedTuple

import jax
import jax.numpy as jnp
import numpy as np
from jax import lax
from jax.experimental import pallas as pl
from jax.experimental.pallas import tpu as pltpu

F32 = jnp.float32
BF16 = jnp.bfloat16

D_MODEL = 1024
RET_HEADS = 4
RET_DK = 256
RET_DV = 512
RET_HALF = RET_DK // 2
RET_QK_W = RET_HEADS * RET_DK
RET_V_W = RET_HEADS * RET_DV
ROPE_BASE = 10000.0
LRU_WIDTH = 1280
LRU_BLOCKS = 10
LRU_BW = LRU_WIDTH // LRU_BLOCKS
LRU_C = 8.0
CONV_WIDTH = 4
CONV_LEFT = 2
LRU_HALO = 16
N_EXPERTS = 8
TOP_K = 2
RMS_EPS = 1e-6
GN_EPS = 1e-5
SQRT_TINY = 1e-30

V7X_LANES = 128
V7X_SUBLANES = 8
V7X_VMEM_BYTES = 64 * 1024 * 1024
DMA_PRIORITIES = 2
VMEM_LIMIT = (V7X_VMEM_BYTES * 3) // 4
MIXER_VMEM_LIMIT = (V7X_VMEM_BYTES * 29) // 32

PA_W = 2 * RET_QK_W + 2 * RET_V_W + 2 * D_MODEL
PB_W = 2 * LRU_WIDTH


class Geom(NamedTuple):
    n_p: int
    s_p: int
    n_s: int
    s_s: int

    @property
    def tokens_p(self):
        return self.n_p * self.s_p

    @property
    def tokens(self):
        return self.n_p * self.s_p + self.n_s * self.s_s

    @property
    def n_seq(self):
        return self.n_p + self.n_s


def _tile_seq(g, i, tm):
    t0 = i * tm
    return jnp.where(t0 < g.tokens_p, t0 // g.s_p, g.n_p + (t0 - g.tokens_p) // g.s_s)


def _tile_pos(g, i, tm):
    t0 = i * tm
    return jnp.where(t0 < g.tokens_p, t0 % g.s_p, (t0 - g.tokens_p) % g.s_s)


def _tile_seq_len(g, i, tm):
    return jnp.where(i * tm < g.tokens_p, g.s_p, g.s_s)


def _cparams(sem):
    return pltpu.CompilerParams(dimension_semantics=sem, vmem_limit_bytes=VMEM_LIMIT)


def _sigmoid(x):
    return 0.5 * jnp.tanh(0.5 * x) + 0.5


def _silu(x):
    return x * _sigmoid(x)


def _rms_mod(x, gvec, scale, shift):
    ms = jnp.mean(x * x, axis=-1, keepdims=True)
    y = x * lax.rsqrt(ms + RMS_EPS) * gvec
    return y * (1.0 + scale) + shift


def _mod_kernel(c_ref, w_ref, b_ref, o_ref):
    c = c_ref[...]
    o_ref[...] = jnp.dot(_silu(c), w_ref[...], preferred_element_type=F32,
                         precision=lax.Precision.HIGHEST) + b_ref[...]


def _adaln_mod(c_pad, w_ada, b_ada):
    depth, d, _ = w_ada.shape
    n = c_pad.shape[0]
    out = pl.pallas_call(
        _mod_kernel,
        grid=(depth, 6),
        in_specs=[
            pl.BlockSpec((n, d), lambda l, j: (0, 0)),
            pl.BlockSpec((None, d, d), lambda l, j: (l, 0, j)),
            pl.BlockSpec((None, None, 1, d), lambda l, j: (l, j, 0, 0)),
        ],
        out_specs=pl.BlockSpec((None, None, n, d), lambda l, j: (l, j, 0, 0)),
        out_shape=jax.ShapeDtypeStruct((depth, 6, n, d), F32),
        compiler_params=_cparams(("arbitrary", "arbitrary")),
        name="adaln_mod",
    )(c_pad, w_ada, b_ada.reshape(depth, 6, 1, d))
    return out.transpose(0, 2, 1, 3)


def _token_parts(geom, x, tm):
    parts = tuple(p for p in x if p.shape[0]) if isinstance(x, tuple) else (x,)
    d = parts[0].shape[1]
    if len(parts) == 1:
        return parts, [pl.BlockSpec((tm, d), lambda i, *_: (i, 0))], None
    first = geom.tokens_p // tm
    specs = [pl.BlockSpec((tm, d), lambda i, *_: (jnp.minimum(i, first - 1), 0)),
             pl.BlockSpec((tm, d), lambda i, *_: (jnp.maximum(i - first, 0), 0))]
    return parts, specs, first


def _token_tile(x_refs, first):
    if len(x_refs) == 1:
        return x_refs[0][...]
    return jnp.where(pl.program_id(0) < first, x_refs[0][...], x_refs[1][...])


def _norm_proj_kernel(*refs, n_x, first, shift_row, scale_row):
    x_refs = refs[:n_x]
    mod_ref, g_ref, w_ref, o_ref, h_ref = refs[n_x:]

    @pl.when(pl.program_id(1) == 0)
    def _():
        h = _rms_mod(_token_tile(x_refs, first), g_ref[...], mod_ref[scale_row:scale_row + 1, :],
                     mod_ref[shift_row:shift_row + 1, :])
        h_ref[...] = h.astype(BF16)

    o_ref[...] = jnp.dot(h_ref[...], w_ref[...], preferred_element_type=F32).astype(o_ref.dtype)


def _norm_proj(geom, x, mod_l, gvec, w, *, shift_row, scale_row, tm, tn):
    parts, x_specs, first = _token_parts(geom, x, tm)
    t, d = geom.tokens, parts[0].shape[1]
    n = w.shape[1]
    return pl.pallas_call(
        functools.partial(_norm_proj_kernel, n_x=len(parts), first=first, shift_row=shift_row,
                          scale_row=scale_row),
        grid=(t // tm, n // tn),
        in_specs=x_specs + [
            pl.BlockSpec((None, 6, d), lambda i, j: (_tile_seq(geom, i, tm), 0, 0)),
            pl.BlockSpec((1, d), lambda i, j: (0, 0)),
            pl.BlockSpec((d, tn), lambda i, j: (0, j)),
        ],
        out_specs=pl.BlockSpec((tm, tn), lambda i, j: (i, j)),
        out_shape=jax.ShapeDtypeStruct((t, n), BF16),
        scratch_shapes=[pltpu.VMEM((tm, d), BF16)],
        compiler_params=_cparams(("parallel", "arbitrary")),
        name="norm_proj",
    )(*parts, mod_l, gvec, w)


def _rot_halves(ref, h, cos, sin):
    a = ref[:, h * RET_DK:h * RET_DK + RET_HALF].astype(F32)
    b = ref[:, h * RET_DK + RET_HALF:(h + 1) * RET_DK].astype(F32)
    return a * cos - b * sin, a * sin + b * cos


def _ret_bwd_state_kernel(k_ref, v_ref, cos_ref, sin_ref, kdb_ref, rb_ref, r_ref, *, geom, chunk, cdec):
    c = pl.num_programs(0) - 1 - pl.program_id(0)
    is_last = _tile_pos(geom, c, chunk) + chunk == _tile_seq_len(geom, c, chunk)

    @pl.when(is_last)
    def _():
        r_ref[...] = jnp.zeros_like(r_ref)

    cos = cos_ref[...]
    sin = sin_ref[...]
    for h in range(RET_HEADS):
        k1, k2 = _rot_halves(k_ref, h, cos, sin)
        dec = kdb_ref[h]
        kd = jnp.concatenate([k1 * dec, k2 * dec], axis=1).astype(BF16)
        vh = v_ref[:, h * RET_DV:(h + 1) * RET_DV].astype(BF16)
        r = r_ref[h]
        rb_ref[h] = r.astype(BF16)
        upd = lax.dot_general(kd, vh, (((0,), (0,)), ((), ())), preferred_element_type=F32)
        r_ref[h] = r * cdec[h] + upd


def _ret_heads(heads, acc, fillers, q_ref, k_ref, v_ref, g_ref, rb_ref, cos_ref, sin_ref, dmat_ref, qdf_ref,
               qdb_ref, kdf_ref, wo_ref, rf_ref, cdec):
    cos = cos_ref[...]
    sin = sin_ref[...]
    pending = iter(fillers)

    def fill():
        piece = next(pending, None)
        if piece is not None:
            piece()

    for h in heads:
        q1, q2 = _rot_halves(q_ref, h, cos, sin)
        k1, k2 = _rot_halves(k_ref, h, cos, sin)
        qr = jnp.concatenate([q1, q2], axis=1).astype(BF16)
        kr = jnp.concatenate([k1, k2], axis=1).astype(BF16)
        vh = v_ref[:, h * RET_DV:(h + 1) * RET_DV].astype(BF16)
        s = lax.dot_general(qr, kr, (((1,), (1,)), ((), ())), preferred_element_type=F32) * dmat_ref[h]
        fill()
        o = jnp.dot(s.astype(BF16), vh, preferred_element_type=F32)
        fill()
        qdf = qdf_ref[h]
        qf = jnp.concatenate([q1 * qdf, q2 * qdf], axis=1).astype(BF16)
        rf = rf_ref[h]
        o = o + jnp.dot(qf, rf.astype(BF16), preferred_element_type=F32)
        fill()
        qdb = qdb_ref[h]
        qb = jnp.concatenate([q1 * qdb, q2 * qdb], axis=1).astype(BF16)
        o = o + jnp.dot(qb, rb_ref[h], preferred_element_type=F32)
        fill()
        kdf = kdf_ref[h]
        kf = jnp.concatenate([k1 * kdf, k2 * kdf], axis=1).astype(BF16)
        rf_ref[h] = rf * cdec[h] + lax.dot_general(kf, vh, (((0,), (0,)), ((), ())),
                                                   preferred_element_type=F32)
        fill()
        oc = o - jnp.mean(o, axis=-1, keepdims=True)
        on = oc * lax.rsqrt(jnp.mean(oc * oc, axis=-1, keepdims=True) + GN_EPS)
        og = (on * _silu(g_ref[:, h * RET_DV:(h + 1) * RET_DV].astype(F32))).astype(BF16)
        part = jnp.dot(og, wo_ref[h * RET_DV:(h + 1) * RET_DV, :], preferred_element_type=F32)
        acc = part if acc is None else acc + part
    for piece in pending:
        piece()
    return acc


def _retention_tables(chunk, s_max):
    log_gamma = jnp.log1p(-jnp.exp2(-5.0 - jnp.arange(RET_HEADS, dtype=F32)))
    idx = jnp.arange(chunk, dtype=F32)
    dist = jnp.abs(idx[:, None] - idx[None, :])
    dmat = jnp.exp(log_gamma[:, None, None] * dist[None])

    def rows(e):
        return jnp.broadcast_to(jnp.exp(log_gamma[:, None] * e[None, :])[:, :, None],
                                (RET_HEADS, chunk, RET_HALF))

    qdf = rows(idx + 1.0)
    qdb = rows(chunk - idx)
    kdf = rows(chunk - 1.0 - idx)
    kdb = rows(idx)
    theta = 1.0 / (ROPE_BASE ** jnp.linspace(0.0, 1.0, RET_HALF, dtype=F32))
    ang = jnp.arange(s_max, dtype=F32)[:, None] * theta[None, :]
    k_scale = RET_DK ** -0.5
    return jnp.cos(ang), jnp.sin(ang), dmat * k_scale, qdf, qdb, kdf * k_scale, kdb * k_scale


def _chunk_decay(chunk):
    lg = np.log1p(-np.exp2(-5.0 - np.arange(RET_HEADS, dtype=np.float32))).astype(np.float32)
    return tuple(float(v) for v in np.exp(lg * np.float32(chunk)).astype(np.float32))


def _retention(geom, proj_a, w_ret_o, tables, *, chunk):
    t = proj_a.shape[0]
    n_chunks = t // chunk
    cos, sin, dmat, qdf, qdb, kdf, kdb = tables
    cdec = _chunk_decay(chunk)
    state_shape = (RET_HEADS, RET_DK, RET_DV)

    def pos_blk(c):
        return _tile_pos(geom, c, chunk) // chunk

    rev = lambda i: n_chunks - 1 - i
    tab_spec = pl.BlockSpec((RET_HEADS, chunk, RET_HALF), lambda i: (0, 0, 0))
    rb = pl.pallas_call(
        functools.partial(_ret_bwd_state_kernel, geom=geom, chunk=chunk, cdec=cdec),
        grid=(n_chunks,),
        in_specs=[
            pl.BlockSpec((chunk, RET_QK_W), lambda i: (rev(i), 1)),
            pl.BlockSpec((chunk, RET_V_W), lambda i: (rev(i), 1)),
            pl.BlockSpec((chunk, RET_HALF), lambda i: (pos_blk(rev(i)), 0)),
            pl.BlockSpec((chunk, RET_HALF), lambda i: (pos_blk(rev(i)), 0)),
            tab_spec,
        ],
        out_specs=pl.BlockSpec((None,) + state_shape, lambda i: (rev(i), 0, 0, 0)),
        out_shape=jax.ShapeDtypeStruct((n_chunks,) + state_shape, BF16),
        scratch_shapes=[pltpu.VMEM(state_shape, F32)],
        compiler_params=_cparams(("arbitrary",)),
        name="ret_bwd_state",
    )(proj_a, proj_a, cos, sin, kdb)

    once = lambda shape: pl.BlockSpec(shape, lambda i: (0,) * len(shape), pipeline_mode=pl.Buffered(1))
    tab_once = once((RET_HEADS, chunk, RET_HALF))
    fwd_specs = [
        pl.BlockSpec((chunk, RET_QK_W), lambda i: (i, 0)),
        pl.BlockSpec((chunk, RET_QK_W), lambda i: (i, 1)),
        pl.BlockSpec((chunk, RET_V_W), lambda i: (i, 1)),
        pl.BlockSpec((chunk, RET_V_W), lambda i: (i, 2)),
        pl.BlockSpec((None,) + state_shape, lambda i: (i, 0, 0, 0)),
        pl.BlockSpec((chunk, RET_HALF), lambda i: (pos_blk(i), 0)),
        pl.BlockSpec((chunk, RET_HALF), lambda i: (pos_blk(i), 0)),
        once((RET_HEADS, chunk, chunk)),
        tab_once, tab_once, tab_once,
        once((RET_V_W, D_MODEL)),
    ]
    fwd_args = (proj_a, proj_a, proj_a, proj_a, rb, cos, sin, dmat, qdf, qdb, kdf, w_ret_o)
    return fwd_args, fwd_specs, cdec, pltpu.VMEM(state_shape, F32)


def _log_sigmoid(x):
    return jnp.minimum(x, 0.0) - jnp.log1p(jnp.exp(-jnp.abs(x)))


LRU_SEGS = V7X_SUBLANES
LRU_LEAD = CONV_LEFT * LRU_SEGS
LRU_TAIL = (CONV_WIDTH - 1 - CONV_LEFT) * LRU_SEGS


def _lru_gate_steps(d, x_ref, prev_ref, next_ref, keep_prev, keep_next, xs_ref, wconv_ref, bconv_ref, wrg_ref,
                    brg_ref, wig_ref, big_ref, lam_ref, a_ref, u_ref):
    tl = x_ref.shape[0]
    seg = tl // LRU_SEGS
    prev_hi = prev_ref[...].astype(F32)[LRU_HALO - V7X_SUBLANES:] * keep_prev
    next_lo = next_ref[...].astype(F32)[:V7X_SUBLANES] * keep_next
    row8 = lax.broadcasted_iota(jnp.int32, (V7X_SUBLANES, LRU_BW), 0)
    last = V7X_SUBLANES - 1
    c = (0.5 * LRU_C * math.log2(math.e)) * _log_sigmoid(lam_ref[d])

    def block(n):
        sl = slice(n * LRU_BW, (n + 1) * LRU_BW)
        x = x_ref[:, sl].astype(F32)
        for s in range(LRU_SEGS):
            xs_ref[n, pl.ds(LRU_LEAD + s, seg, stride=LRU_SEGS), :] = x[s * seg:(s + 1) * seg]
        slab = lambda t: xs_ref[n, LRU_LEAD + t * LRU_SEGS:LRU_LEAD + (t + 1) * LRU_SEGS, :]
        p = prev_hi[:, sl]
        m1 = pltpu.roll(jnp.where(row8 == last, p, slab(seg - 1)), 1, axis=0)
        m2 = pltpu.roll(jnp.where(row8 == last, pltpu.roll(p, 1, axis=0), slab(seg - 2)), 1, axis=0)
        p1 = pltpu.roll(jnp.where(row8 == 0, next_lo[:, sl], slab(0)), last, axis=0)
        xs_ref[n, 0:LRU_SEGS, :] = m2
        xs_ref[n, LRU_SEGS:LRU_LEAD, :] = m1
        xs_ref[n, LRU_LEAD + tl:LRU_LEAD + tl + LRU_TAIL, :] = p1

        xh = bconv_ref[:, sl] + xs_ref[n, 0:tl, :] * wconv_ref[0:1, sl]
        for j in range(1, CONV_WIDTH):
            xh = xh + xs_ref[n, j * LRU_SEGS:j * LRU_SEGS + tl, :] * wconv_ref[j:j + 1, sl]
        xb = xh.astype(BF16)
        th_r = jnp.tanh(jnp.dot(xb, wrg_ref[d, n], preferred_element_type=F32) + brg_ref[d][:, sl])
        th_i = jnp.tanh(jnp.dot(xb, wig_ref[d, n], preferred_element_type=F32) + big_ref[d][:, sl])
        a = jnp.exp2(c[:, sl] * th_r + c[:, sl])
        a_ref[n] = a
        y = 1.0 - a * a
        u_ref[n] = (y * lax.rsqrt(jnp.maximum(y, SQRT_TINY))) * ((th_i + 1.0) * xh)

    return [functools.partial(block, n) for n in range(LRU_BLOCKS)]


def _lru_scan_loop(a_ref, u_ref, hs_ref, as_ref, reverse):
    tl = a_ref.shape[1]
    seg = tl // LRU_SEGS
    slab_shape = (LRU_SEGS, LRU_BW)

    def step(k, carry):
        t = seg - 1 - k if reverse else k
        rows = pl.ds(pl.multiple_of(t * LRU_SEGS, LRU_SEGS), LRU_SEGS)
        hs, decays = [], []
        for n in range(LRU_BLOCKS):
            a = a_ref[n, rows, :]
            h = a * carry[0][n] + u_ref[n, rows, :]
            decay = a * carry[1][n]
            hs_ref[n, rows, :] = h
            as_ref[n, rows, :] = decay
            hs.append(h)
            decays.append(decay)
        return tuple(hs), tuple(decays)

    init = (tuple(jnp.zeros(slab_shape, F32) for _ in range(LRU_BLOCKS)),
            tuple(jnp.ones(slab_shape, F32) for _ in range(LRU_BLOCKS)))
    return lax.fori_loop(0, seg, step, init, unroll=4)


def _lru_scan_finish(ends, hs_ref, as_ref, fin_ref, carry_ref, o_ref, reverse):
    h_end, decay_end = ends
    tl = as_ref.shape[1]
    seg = tl // LRU_SEGS
    order = range(LRU_SEGS - 1, -1, -1) if reverse else range(LRU_SEGS)
    for n in range(LRU_BLOCKS):
        sl = slice(n * LRU_BW, (n + 1) * LRU_BW)
        fin_ref[0] = h_end[n]
        fin_ref[1] = decay_end[n]
        c = carry_ref[n, 0:1, :]
        for s in order:
            fin_ref[2, s:s + 1, :] = c
            c = fin_ref[0, s:s + 1, :] + fin_ref[1, s:s + 1, :] * c
        carry_ref[n, 0:1, :] = c
        for s in range(LRU_SEGS):
            seg_rows = pl.ds(s, seg, stride=LRU_SEGS)
            h = hs_ref[n, seg_rows, :] + as_ref[n, seg_rows, :] * fin_ref[2, s:s + 1, :]
            o_ref[s * seg:(s + 1) * seg, sl] = h.astype(o_ref.dtype)


N_RET_REFS = 12
N_LRU_REFS = 13


def _mixer_kernel(*refs, geom, tl, cdec):
    ret = refs[:N_RET_REFS]
    (xf_ref, xfp_ref, xfn_ref, xb_ref, xbp_ref, xbn_ref, wconv_ref, bconv_ref, wrg_ref, brg_ref, wig_ref, big_ref,
     lam_ref) = refs[N_RET_REFS:N_RET_REFS + N_LRU_REFS]
    (ya_ref, hf_ref, hb_ref, rf_ref, acc_ref, xs_ref, a_ref, u_ref, as_ref, fin_ref, cf_ref,
     cb_ref) = refs[N_RET_REFS + N_LRU_REFS:]
    hs_ref = xs_ref
    i = pl.program_id(0)
    ib = pl.num_programs(0) - 1 - i

    def flags(tile):
        pos = _tile_pos(geom, tile, tl)
        return pos == 0, pos + tl == _tile_seq_len(geom, tile, tl)

    def keep(flag):
        return jnp.where(flag, 0.0, 1.0)

    def gates(d, x_ref, prev_ref, next_ref, first, last):
        return _lru_gate_steps(d, x_ref, prev_ref, next_ref, keep(first), keep(last), xs_ref, wconv_ref, bconv_ref,
                               wrg_ref, brg_ref, wig_ref, big_ref, lam_ref, a_ref, u_ref)

    def heads(which, acc, fillers):
        return _ret_heads(which, acc, fillers, *ret, rf_ref, cdec)

    first_f, last_f = flags(i)
    first_b, last_b = flags(ib)

    @pl.when(first_f)
    def _():
        rf_ref[...] = jnp.zeros_like(rf_ref)
        cf_ref[...] = jnp.zeros_like(cf_ref)

    @pl.when(last_b)
    def _():
        cb_ref[...] = jnp.zeros_like(cb_ref)

    half = RET_HEADS // 2
    acc_ref[...] = heads(range(half), None, gates(0, xf_ref, xfp_ref, xfn_ref, first_f, last_f))
    ends = _lru_scan_loop(a_ref, u_ref, hs_ref, as_ref, reverse=False)
    _lru_scan_finish(ends, hs_ref, as_ref, fin_ref, cf_ref, hf_ref, reverse=False)
    ya_ref[...] = heads(range(half, RET_HEADS), acc_ref[...],
                        gates(1, xb_ref, xbp_ref, xbn_ref, first_b, last_b)).astype(ya_ref.dtype)
    ends = _lru_scan_loop(a_ref, u_ref, hs_ref, as_ref, reverse=True)
    _lru_scan_finish(ends, hs_ref, as_ref, fin_ref, cb_ref, hb_ref, reverse=True)


def _mixer(geom, proj_a, proj_b, w_ret_o, tables, w_conv, b_conv, w_rg, b_rg, w_ig, b_ig, lam, *, tl):
    t = proj_b.shape[0]
    n = t // tl
    r8 = tl // LRU_HALO
    n8 = t // LRU_HALO
    w = LRU_WIDTH
    rev = lambda i: n - 1 - i
    prev_blk = lambda i: jnp.maximum(i * r8 - 1, 0)
    next_blk = lambda i: jnp.minimum((i + 1) * r8, n8 - 1)
    full = lambda a: pl.BlockSpec(a.shape, lambda i: (0,) * a.ndim)
    w_conv = 0.5 * w_conv
    b_conv2 = 0.5 * b_conv.reshape(1, w)
    b_rg3 = 0.5 * b_rg.reshape(2, 1, w)
    b_ig3 = 0.5 * b_ig.reshape(2, 1, w)
    lam3 = lam.reshape(2, 1, w)
    ret_args, ret_specs, cdec, ret_state = _retention(geom, proj_a, w_ret_o, tables, chunk=tl)
    lru_args = (proj_b, proj_b, proj_b, proj_b, proj_b, proj_b, w_conv, b_conv2, w_rg, b_rg3, w_ig, b_ig3, lam3)
    assert (len(ret_args), len(lru_args)) == (N_RET_REFS, N_LRU_REFS)
    return pl.pallas_call(
        functools.partial(_mixer_kernel, geom=geom, tl=tl, cdec=cdec),
        grid=(n,),
        in_specs=ret_specs + [
            pl.BlockSpec((tl, w), lambda i: (i, 0)),
            pl.BlockSpec((LRU_HALO, w), lambda i: (prev_blk(i), 0)),
            pl.BlockSpec((LRU_HALO, w), lambda i: (next_blk(i), 0)),
            pl.BlockSpec((tl, w), lambda i: (rev(i), 0)),
            pl.BlockSpec((LRU_HALO, w), lambda i: (prev_blk(rev(i)), 0)),
            pl.BlockSpec((LRU_HALO, w), lambda i: (next_blk(rev(i)), 0)),
            full(w_conv), full(b_conv2), full(w_rg), full(b_rg3), full(w_ig), full(b_ig3), full(lam3),
        ],
        out_specs=[
            pl.BlockSpec((tl, D_MODEL), lambda i: (i, 0)),
            pl.BlockSpec((tl, w), lambda i: (i, 0)),
            pl.BlockSpec((tl, w), lambda i: (rev(i), 0)),
        ],
        out_shape=[jax.ShapeDtypeStruct((t, D_MODEL), BF16), jax.ShapeDtypeStruct((t, w), BF16),
                   jax.ShapeDtypeStruct((t, w), BF16)],
        scratch_shapes=[ret_state, pltpu.VMEM((tl, D_MODEL), F32),
                        pltpu.VMEM((LRU_BLOCKS, LRU_LEAD + tl + LRU_TAIL, LRU_BW), F32)]
        + [pltpu.VMEM((LRU_BLOCKS, tl, LRU_BW), F32)] * 3
        + [pltpu.VMEM((3, LRU_SEGS, LRU_BW), F32)]
        + [pltpu.VMEM((LRU_BLOCKS, V7X_SUBLANES, LRU_BW), F32)] * 2,
        compiler_params=pltpu.CompilerParams(dimension_semantics=("arbitrary",),
                                             vmem_limit_bytes=MIXER_VMEM_LIMIT),
        name="mixer",
    )(*ret_args, *lru_args)


def _gelu_tanh(x):
    return 0.5 * x * (1.0 + jnp.tanh(math.sqrt(2.0 / math.pi) * (x + 0.044715 * (x * x * x))))


def _merge_kernel(*refs, n_x, first):
    x_refs = refs[:n_x]
    ya_ref, hf_ref, hb_ref, gl_ref, ga_ref, gb_ref, mod_ref, wl_ref, wo_ref, o_ref = refs[n_x:]
    f32 = lambda ref: ref[...].astype(F32)
    y = ((f32(hf_ref) + f32(hb_ref)) * _gelu_tanh(f32(gl_ref))).astype(BF16)
    yb = jnp.dot(y, wl_ref[...], preferred_element_type=F32)
    m = _sigmoid(f32(ga_ref)) * f32(ya_ref) + _sigmoid(f32(gb_ref)) * yb
    mix = jnp.dot(m.astype(BF16), wo_ref[...], preferred_element_type=F32)
    o_ref[...] = _token_tile(x_refs, first) + mod_ref[2:3, :] * mix


def _merge(geom, x, ya, hf, hb, proj_a, proj_b, mod_l, w_lru_o, w_out, *, tm):
    parts, x_specs, first = _token_parts(geom, x, tm)
    t, d = geom.tokens, parts[0].shape[1]
    gate_a_blk = (2 * RET_QK_W + 2 * RET_V_W) // d
    tok = lambda w: pl.BlockSpec((tm, w), lambda i: (i, 0))
    return pl.pallas_call(
        functools.partial(_merge_kernel, n_x=len(parts), first=first),
        grid=(t // tm,),
        in_specs=x_specs + [
            tok(d), tok(LRU_WIDTH), tok(LRU_WIDTH),
            pl.BlockSpec((tm, LRU_WIDTH), lambda i: (i, 1)),
            pl.BlockSpec((tm, d), lambda i: (i, gate_a_blk)),
            pl.BlockSpec((tm, d), lambda i: (i, gate_a_blk + 1)),
            pl.BlockSpec((None, 6, d), lambda i: (_tile_seq(geom, i, tm), 0, 0)),
            pl.BlockSpec((LRU_WIDTH, d), lambda i: (0, 0)),
            pl.BlockSpec((d, d), lambda i: (0, 0)),
        ],
        out_specs=tok(d),
        out_shape=jax.ShapeDtypeStruct((t, d), F32),
        compiler_params=_cparams(("parallel",)),
        name="merge",
    )(*parts, ya, hf, hb, proj_b, proj_a, proj_a, mod_l, w_lru_o, w_out)


def _ffn_kernel(x_ref, mod_ref, g_ref, wg_ref, wu_ref, wd_ref, o_ref, *, col_chunks):
    x = x_ref[...]
    h = _rms_mod(x, g_ref[...], mod_ref[4:5, :], mod_ref[3:4, :]).astype(BF16)
    acc = None
    for c0, c1 in col_chunks:
        a = jnp.dot(h, wg_ref[:, c0:c1], preferred_element_type=F32)
        u = jnp.dot(h, wu_ref[:, c0:c1], preferred_element_type=F32)
        part = jnp.dot((_silu(a) * u).astype(BF16), wd_ref[c0:c1, :], preferred_element_type=F32)
        acc = part if acc is None else acc + part
    o_ref[...] = x + mod_ref[5:6, :] * acc


def _ffn(geom, x, mod_l, gvec, wg, wu, wd, *, tm, tf):
    t, d = x.shape
    ff = wg.shape[1]
    col_chunks = tuple((c, min(c + tf, ff)) for c in range(0, ff, tf))
    resident = lambda shape: pl.BlockSpec(shape, lambda i: (0, 0), pipeline_mode=pl.Buffered(1))
    return pl.pallas_call(
        functools.partial(_ffn_kernel, col_chunks=col_chunks),
        grid=(t // tm,),
        in_specs=[
            pl.BlockSpec((tm, d), lambda i: (i, 0)),
            pl.BlockSpec((None, 6, d), lambda i: (_tile_seq(geom, i, tm), 0, 0)),
            pl.BlockSpec((1, d), lambda i: (0, 0)),
            resident((d, ff)), resident((d, ff)), resident((ff, d)),
        ],
        out_specs=pl.BlockSpec((tm, d), lambda i: (i, 0)),
        out_shape=jax.ShapeDtypeStruct((t, d), F32),
        compiler_params=_cparams(("parallel",)),
        name="ffn_dense",
    )(x, mod_l, gvec, wg, wu, wd)


ROUTE_E0, ROUTE_E1, ROUTE_R0, ROUTE_R1, ROUTE_G0, ROUTE_G1 = range(6)

ROW_TILES = D_MODEL // V7X_LANES


def _store_token_rows(ref, val):
    rows = val.shape[0]
    for j in range(ROW_TILES):
        ref[pl.ds(j, rows, stride=ROW_TILES), :] = val[:, j * V7X_LANES:(j + 1) * V7X_LANES]


def _load_token_rows(ref, rows):
    return jnp.concatenate([ref[pl.ds(j, rows, stride=ROW_TILES), :] for j in range(ROW_TILES)], axis=1)


def _router_kernel(x_ref, mod_ref, g_ref, wr_ref, h_ref, route_ref, cnt_ref, carry_ref):
    i = pl.program_id(0)
    tm = x_ref.shape[0]

    @pl.when(i == 0)
    def _():
        carry_ref[...] = jnp.zeros_like(carry_ref)

    h = _rms_mod(x_ref[...], g_ref[...], mod_ref[4:5, :], mod_ref[3:4, :])
    _store_token_rows(h_ref, h)
    h_hi = h.astype(BF16)
    h_lo = (h - h_hi.astype(F32)).astype(BF16)
    logits = (jnp.dot(h_hi, wr_ref[0], preferred_element_type=F32)
              + jnp.dot(h_lo, wr_ref[0], preferred_element_type=F32)
              + jnp.dot(h_hi, wr_ref[1], preferred_element_type=F32))
    lane = lax.broadcasted_iota(jnp.int32, logits.shape, 1).astype(F32)
    logits = jnp.where(lane < N_EXPERTS, logits, -jnp.inf)
    m1 = jnp.max(logits, axis=-1, keepdims=True)
    i1 = jnp.min(jnp.where(logits == m1, lane, float(V7X_LANES)), axis=-1, keepdims=True)
    rest = jnp.where(lane == i1, -jnp.inf, logits)
    m2 = jnp.max(rest, axis=-1, keepdims=True)
    i2 = jnp.min(jnp.where(rest == m2, lane, float(V7X_LANES)), axis=-1, keepdims=True)
    ex = jnp.exp(m2 - m1)
    g1 = 1.0 / (1.0 + ex)
    g2 = ex / (1.0 + ex)
    sel1 = lane == i1
    sel2 = lane == i2
    onehot = jnp.where(sel1 | sel2, 1.0, 0.0)
    row = lax.broadcasted_iota(jnp.int32, (tm, tm), 0)
    col = lax.broadcasted_iota(jnp.int32, (tm, tm), 1)
    lower = jnp.where(col < row, 1.0, 0.0).astype(BF16)
    before = jnp.dot(lower, onehot.astype(BF16), preferred_element_type=F32) + carry_ref[0:1, :]
    r1 = jnp.sum(jnp.where(sel1, before, 0.0), axis=-1, keepdims=True)
    r2 = jnp.sum(jnp.where(sel2, before, 0.0), axis=-1, keepdims=True)
    out_lane = lax.broadcasted_iota(jnp.int32, route_ref.shape, 1)
    vals = (i1.astype(F32), i2.astype(F32), r1, r2, g1, g2)
    packed = jnp.zeros(route_ref.shape, F32)
    for slot, v in enumerate(vals):
        packed = jnp.where(out_lane == slot, v, packed)
    route_ref[...] = packed
    carry = carry_ref[0:1, :] + jnp.sum(onehot, axis=0, keepdims=True)
    carry_ref[0:1, :] = carry
    cnt_ref[...] = jnp.broadcast_to(carry, cnt_ref.shape).astype(jnp.int32)


def _router(geom, x, mod_l, gvec, w_router, *, tm):
    t, d = x.shape
    w_pad = jnp.pad(w_router, ((0, 0), (0, V7X_LANES - N_EXPERTS)))
    w_hi = w_pad.astype(BF16)
    w_pad = jnp.stack([w_hi, (w_pad - w_hi.astype(F32)).astype(BF16)])
    return pl.pallas_call(
        _router_kernel,
        grid=(t // tm,),
        in_specs=[
            pl.BlockSpec((tm, d), lambda i: (i, 0)),
            pl.BlockSpec((None, 6, d), lambda i: (_tile_seq(geom, i, tm), 0, 0)),
            pl.BlockSpec((1, d), lambda i: (0, 0)),
            pl.BlockSpec((2, d, V7X_LANES), lambda i: (0, 0, 0)),
        ],
        out_specs=[
            pl.BlockSpec((tm * ROW_TILES, V7X_LANES), lambda i: (i, 0)),
            pl.BlockSpec((tm, V7X_LANES), lambda i: (i, 0)),
            pl.BlockSpec((V7X_SUBLANES, V7X_LANES), lambda i: (0, 0)),
        ],
        out_shape=[
            jax.ShapeDtypeStruct((t * ROW_TILES, V7X_LANES), F32),
            jax.ShapeDtypeStruct((t, V7X_LANES), F32),
            jax.ShapeDtypeStruct((V7X_SUBLANES, V7X_LANES), jnp.int32),
        ],
        scratch_shapes=[pltpu.VMEM((V7X_SUBLANES, V7X_LANES), F32)],
        compiler_params=_cparams(("arbitrary",)),
        name="router",
    )(x, mod_l, gvec, w_pad)


def _token_copy(src, src_tok, dst, dst_tok, sem):
    s = pl.multiple_of(src_tok * ROW_TILES, ROW_TILES)
    d = pl.multiple_of(dst_tok * ROW_TILES, ROW_TILES)
    return pltpu.make_async_copy(src.at[pl.ds(s, ROW_TILES), :], dst.at[pl.ds(d, ROW_TILES), :], sem)


def _dispatch_kernel(fill_ref, dest_ref, src_ref, out_hbm, zero_ref, sem, zero_sem, *, td):
    @pl.when(pl.program_id(0) == 0)
    def _():
        zero_ref[...] = jnp.zeros_like(zero_ref)
        for wait in (False, True):
            for e in range(N_EXPERTS):
                def pad_row(r, carry):
                    copy = _token_copy(zero_ref, 0, out_hbm, r, zero_sem)
                    copy.wait() if wait else copy.start()
                    return carry

                lax.fori_loop(fill_ref[0, e], fill_ref[1, e], pad_row, 0)

    def issue(t, carry):
        for k in range(TOP_K):
            _token_copy(src_ref, t, out_hbm, dest_ref[k, t], sem).start(priority=k % DMA_PRIORITIES)
        return carry

    lax.fori_loop(0, td, issue, 0, unroll=8)
    for _ in range(TOP_K):
        pltpu.make_async_copy(src_ref, out_hbm.at[pl.ds(0, td * ROW_TILES), :], sem).wait()


def _dispatch(h2_rows, dest, fill, n_rows, *, td):
    grid_spec = pltpu.PrefetchScalarGridSpec(
        num_scalar_prefetch=1,
        grid=(dest.shape[1] // td,),
        in_specs=[
            pl.BlockSpec((TOP_K, td), lambda i, fill: (0, i), memory_space=pltpu.SMEM),
            pl.BlockSpec((td * ROW_TILES, V7X_LANES), lambda i, fill: (i, 0)),
        ],
        out_specs=pl.BlockSpec(memory_space=pl.ANY),
        scratch_shapes=[pltpu.VMEM((ROW_TILES, V7X_LANES), h2_rows.dtype),
                        pltpu.SemaphoreType.DMA(()), pltpu.SemaphoreType.DMA(())],
    )
    return pl.pallas_call(
        functools.partial(_dispatch_kernel, td=td),
        grid_spec=grid_spec,
        out_shape=jax.ShapeDtypeStruct((n_rows * ROW_TILES, V7X_LANES), h2_rows.dtype),
        compiler_params=_cparams(("arbitrary",)),
        name="moe_dispatch",
    )(fill, dest, h2_rows)


V7X_MXU_COLS = 256


def _experts_kernel(blk_e_ref, nused_ref, x_ref, wg_ref, wu_ref, wd_ref, o_ref, hm_ref, *, ff_chunks):
    del blk_e_ref
    b = pl.program_id(0)
    bm = hm_ref.shape[0]

    @pl.when(b < nused_ref[0])
    def _():
        xb = _load_token_rows(x_ref, bm).astype(BF16)
        for c0, c1 in ff_chunks:
            a = jnp.dot(xb, wg_ref[:, c0:c1], preferred_element_type=F32)
            u = jnp.dot(xb, wu_ref[:, c0:c1], preferred_element_type=F32)
            hm_ref[:, c0:c1] = (_silu(a) * u).astype(BF16)
        for c0 in range(0, D_MODEL, V7X_MXU_COLS):
            y = jnp.dot(hm_ref[...], wd_ref[:, c0:c0 + V7X_MXU_COLS], preferred_element_type=F32)
            for k in range(V7X_MXU_COLS // V7X_LANES):
                o_ref[pl.ds(c0 // V7X_LANES + k, bm, stride=ROW_TILES), :] = y[:, k * V7X_LANES:(k + 1) * V7X_LANES]

    @pl.when(b >= nused_ref[0])
    def _():
        o_ref[...] = jnp.zeros_like(o_ref)


def _experts(xbuf, blk_e, n_used, wg, wu, wd, *, bm, tf):
    p = xbuf.shape[0] // ROW_TILES
    d = D_MODEL
    ff = wg.shape[2]
    ff_chunks = tuple((c, min(c + tf, ff)) for c in range(0, ff, tf))
    rows_blk = (bm * ROW_TILES, V7X_LANES)

    def expert(b, be, nu):
        return be[jnp.minimum(b, nu[0] - 1)]

    resident = lambda shape: pl.BlockSpec((None,) + shape, lambda b, be, nu: (expert(b, be, nu), 0, 0),
                                          pipeline_mode=pl.Buffered(1))
    grid_spec = pltpu.PrefetchScalarGridSpec(
        num_scalar_prefetch=2,
        grid=(p // bm,),
        in_specs=[
            pl.BlockSpec(rows_blk, lambda b, be, nu: (b, 0)),
            resident((d, ff)), resident((d, ff)), resident((ff, d)),
        ],
        out_specs=pl.BlockSpec(rows_blk, lambda b, be, nu: (b, 0)),
        scratch_shapes=[pltpu.VMEM((bm, ff), BF16)],
    )
    return pl.pallas_call(
        functools.partial(_experts_kernel, ff_chunks=ff_chunks),
        grid_spec=grid_spec,
        out_shape=jax.ShapeDtypeStruct(xbuf.shape, F32),
        compiler_params=_cparams(("arbitrary",)),
        name="moe_experts",
    )(blk_e, n_used, xbuf, wg, wu, wd)


def _moe_out_kernel(dest_ref, dest_next_ref, x_ref, route_ref, mod_ref, gf_ref, ybuf_hbm, op_ref, os_ref,
                    ya0_ref, ya1_ref, yb0_ref, yb1_ref, sems, *, final_norm, prompt_tiles):
    i = pl.program_id(0)
    tm = x_ref.shape[0]
    bufs = ((ya0_ref, ya1_ref), (yb0_ref, yb1_ref))

    def gather(d_ref, slot, inline=False):
        def issue(t, carry):
            for k in range(TOP_K):
                _token_copy(ybuf_hbm, d_ref[k, t], bufs[slot][k], t, sems.at[slot]).start(
                    priority=k % DMA_PRIORITIES)
            return carry

        if inline:
            for t in range(tm):
                issue(t, 0)
        else:
            lax.fori_loop(0, tm, issue, 0, unroll=8)

    def wait(slot):
        for k in range(TOP_K):
            pltpu.make_async_copy(ybuf_hbm.at[pl.ds(0, tm * ROW_TILES), :], bufs[slot][k], sems.at[slot]).wait()

    def combine(slot):
        g1 = route_ref[:, ROUTE_G0:ROUTE_G0 + 1]
        g2 = route_ref[:, ROUTE_G1:ROUTE_G1 + 1]
        y = _load_token_rows(bufs[slot][0], tm) * g1 + _load_token_rows(bufs[slot][1], tm) * g2
        x = x_ref[...] + mod_ref[5:6, :] * y
        if final_norm:
            ms = jnp.mean(x * x, axis=-1, keepdims=True)
            x = x * lax.rsqrt(ms + RMS_EPS) * gf_ref[...]

        @pl.when(i < prompt_tiles)
        def _():
            op_ref[...] = x

        @pl.when(i >= prompt_tiles)
        def _():
            os_ref[...] = x

    @pl.when(i == 0)
    def _():
        gather(dest_ref, 0)

    for slot in range(2):
        @pl.when(i % 2 == slot)
        def _():
            wait(slot)
            gather(dest_next_ref, 1 - slot, inline=True)
            combine(slot)

            @pl.when(i + 1 == pl.num_programs(0))
            def _():
                wait(1 - slot)


def _moe_out(geom, x, ybuf, dest, route, mod_l, g_final, *, tm, final_norm):
    t, d = x.shape
    n = t // tm
    prompt_tiles = geom.tokens_p // tm
    y_buf = pltpu.VMEM((tm * ROW_TILES, V7X_LANES), F32)
    return pl.pallas_call(
        functools.partial(_moe_out_kernel, final_norm=final_norm, prompt_tiles=prompt_tiles),
        grid=(n,),
        in_specs=[
            pl.BlockSpec((TOP_K, tm), lambda i: (0, i), memory_space=pltpu.SMEM),
            pl.BlockSpec((TOP_K, tm), lambda i: (0, jnp.minimum(i + 1, n - 1)), memory_space=pltpu.SMEM),
            pl.BlockSpec((tm, d), lambda i: (i, 0)),
            pl.BlockSpec((tm, V7X_LANES), lambda i: (i, 0)),
            pl.BlockSpec((None, 6, d), lambda i: (_tile_seq(geom, i, tm), 0, 0)),
            pl.BlockSpec((1, d), lambda i: (0, 0)),
            pl.BlockSpec(memory_space=pl.ANY),
        ],
        out_specs=[
            pl.BlockSpec((tm, d), lambda i: (jnp.minimum(i, prompt_tiles - 1), 0)),
            pl.BlockSpec((tm, d), lambda i: (jnp.maximum(i - prompt_tiles, 0), 0)),
        ],
        out_shape=[jax.ShapeDtypeStruct((geom.tokens_p, d), F32),
                   jax.ShapeDtypeStruct((t - geom.tokens_p, d), F32)],
        scratch_shapes=[y_buf, y_buf, y_buf, y_buf, pltpu.SemaphoreType.DMA((2,))],
        compiler_params=_cparams(("arbitrary",)),
        name="moe_out",
    )(dest, dest, x, route, mod_l, g_final, ybuf)


def _final_norm_kernel(x_ref, g_ref, o_ref):
    x = x_ref[...]
    o_ref[...] = x * lax.rsqrt(jnp.mean(x * x, axis=-1, keepdims=True) + RMS_EPS) * g_ref[...]


def _final_norm(x, g_final, *, tm):
    t, d = x.shape
    return pl.pallas_call(
        _final_norm_kernel,
        grid=(t // tm,),
        in_specs=[pl.BlockSpec((tm, d), lambda i: (i, 0)), pl.BlockSpec((1, d), lambda i: (0, 0))],
        out_specs=pl.BlockSpec((tm, d), lambda i: (i, 0)),
        out_shape=jax.ShapeDtypeStruct((t, d), F32),
        compiler_params=_cparams(("parallel",)),
        name="final_norm",
    )(x, g_final)


def _moe(geom, x, mod_l, gvec, w_router, wg, wu, wd, g_final, *, tiles, final_norm):
    t, d = x.shape
    bm = tiles["moe_rows"]
    h2, route, counts = _router(geom, x, mod_l, gvec, w_router, tm=tiles["router"])
    counts = counts[0, :N_EXPERTS]
    padded = ((counts + bm - 1) // bm) * bm
    pad_end = jnp.cumsum(padded)
    pad_start = pad_end - padded
    experts = route[:, ROUTE_E0:ROUTE_E1 + 1].astype(jnp.int32)
    ranks = route[:, ROUTE_R0:ROUTE_R1 + 1].astype(jnp.int32)
    dest = (pad_start[experts] + ranks).T
    n_rows = t * TOP_K + N_EXPERTS * bm
    n_blk = n_rows // bm
    blk_start = jnp.arange(n_blk, dtype=jnp.int32) * bm
    blk_e = jnp.minimum(jnp.sum((pad_end[None, :] <= blk_start[:, None]).astype(jnp.int32), axis=1),
                        N_EXPERTS - 1).astype(jnp.int32)
    n_used = (pad_end[-1:] // bm).astype(jnp.int32)
    fill = jnp.stack([pad_start + counts, pad_end.at[N_EXPERTS - 1].set(n_rows)]).astype(jnp.int32)
    xbuf = _dispatch(h2, dest, fill, n_rows, td=tiles["dma_rows"])
    ybuf = _experts(xbuf, blk_e, n_used, wg, wu, wd, bm=bm, tf=tiles["expert_ff"])
    return _moe_out(geom, x, ybuf, dest, route, mod_l, g_final, tm=tiles["combine_rows"], final_norm=final_norm)


def _pick_tiles(geom):
    s = math.gcd(geom.s_p, geom.s_s) if geom.n_p and geom.n_s else (geom.s_p if geom.n_p else geom.s_s)
    return {
        "proj": min(1024, s),
        "proj_cols": 2048,
        "mixer": min(512, s),
        "token": min(512, s),
        "ffn_rows": min(1024, s),
        "combine_rows": min(1024, s),
        "ffn_cols": 1536,
        "router": min(512, s),
        "moe_rows": 512,
        "expert_ff": 1792,
        "dma_rows": min(2048, s),
    }


def _trunk(geom, x, c_all, w_ada, b_ada, g_norm1, g_norm2, w_in, w_conv, b_conv, w_rg, b_rg, w_ig, b_ig,
           lru_lambda, w_ret_o, w_lru_o, w_out, w_ff_gate, w_ff_up, w_ff_down,
           w_router, w_e_gate, w_e_up, w_e_down, g_final, tiles):
    depth = w_in.shape[0]
    d = D_MODEL
    n_pad = -(-geom.n_seq // V7X_SUBLANES) * V7X_SUBLANES
    c_pad = jnp.pad(c_all, ((0, n_pad - geom.n_seq), (0, 0)))
    mod = _adaln_mod(c_pad, w_ada, b_ada)
    tables = _retention_tables(tiles["mixer"], max(geom.s_p if geom.n_p else 0, geom.s_s if geom.n_s else 0))

    o_xl = 2 * RET_QK_W + 2 * RET_V_W
    o_ga = o_xl + 2 * LRU_WIDTH
    for l in range(depth):
        wl = w_in[l]
        w_a = jnp.concatenate([wl[:, :o_xl], wl[:, o_ga:]], axis=1).astype(BF16)
        w_b = wl[:, o_xl:o_ga].astype(BF16)
        g1 = g_norm1[l].reshape(1, d)
        g2 = g_norm2[l].reshape(1, d)
        proj_a = _norm_proj(geom, x, mod[l], g1, w_a, shift_row=0, scale_row=1,
                            tm=tiles["proj"], tn=tiles["proj_cols"])
        proj_b = _norm_proj(geom, x, mod[l], g1, w_b, shift_row=0, scale_row=1,
                            tm=tiles["proj"], tn=PB_W)
        ya, hf, hb = _mixer(geom, proj_a, proj_b, w_ret_o[l].astype(BF16), tables, w_conv[l], b_conv[l],
                            w_rg[l].astype(BF16), b_rg[l], w_ig[l].astype(BF16), b_ig[l], lru_lambda[l],
                            tl=tiles["mixer"])
        x = _merge(geom, x, ya, hf, hb, proj_a, proj_b, mod[l], w_lru_o[l].astype(BF16),
                   w_out[l].astype(BF16), tm=tiles["token"])
        j = l // 2
        last = l == depth - 1
        if l % 2 == 0:
            x = _ffn(geom, x, mod[l], g2, w_ff_gate[j].astype(BF16), w_ff_up[j].astype(BF16),
                     w_ff_down[j].astype(BF16), tm=tiles["ffn_rows"], tf=tiles["ffn_cols"])
            if last:
                x = _final_norm(x, g_final.reshape(1, d), tm=tiles["token"])
                return x[:geom.tokens_p], x[geom.tokens_p:]
        else:
            parts = _moe(geom, x, mod[l], g2, w_router[j], w_e_gate[j].astype(BF16), w_e_up[j].astype(BF16),
                         w_e_down[j].astype(BF16), g_final.reshape(1, d), tiles=tiles, final_norm=last)
            if last:
                return parts
            x = jnp.concatenate(parts, axis=0)


def kernel(x_prompt, x_sample, c_prompt, c_sample, w_ada, b_ada, g_norm1, g_norm2, w_in, w_conv, b_conv, w_rg, b_rg, w_ig, b_ig, lru_lambda, w_ret_o, w_lru_o, w_out, w_ff_gate, w_ff_up, w_ff_down, w_router, w_e_gate, w_e_up, w_e_down, g_final):
    n_p, s_p, d = x_prompt.shape
    n_s, s_s, _ = x_sample.shape
    geom = Geom(n_p, s_p, n_s, s_s)
    x = (x_prompt.reshape(-1, d), x_sample.reshape(-1, d))
    c_all = jnp.concatenate([c_prompt, c_sample], axis=0)
    yp, ys = _trunk(geom, x, c_all, w_ada, b_ada, g_norm1, g_norm2, w_in, w_conv, b_conv, w_rg, b_rg, w_ig, b_ig,
                    lru_lambda, w_ret_o, w_lru_o, w_out, w_ff_gate, w_ff_up, w_ff_down,
                    w_router, w_e_gate, w_e_up, w_e_down, g_final, _pick_tiles(geom))
    return (yp.reshape(n_p, s_p, d), ys.reshape(n_s, s_s, d))
```

```python
import functools
import math
from typing import NamedTuple

import jax
import jax.numpy as jnp
import numpy as np
from jax import lax
from jax.experimental import pallas as pl
from jax.experimental.pallas import tpu as pltpu

F32 = jnp.float32
BF16 = jnp.bfloat16

D_MODEL = 1024
RET_HEADS = 4
RET_DK = 256
RET_DV = 512
RET_HALF = RET_DK // 2
RET_QK_W = RET_HEADS * RET_DK
RET_V_W = RET_HEADS * RET_DV
ROPE_BASE = 10000.0
LRU_WIDTH = 1280
LRU_BLOCKS = 10
LRU_BW = LRU_WIDTH // LRU_BLOCKS
LRU_C = 8.0
CONV_WIDTH = 4
CONV_LEFT = 2
LRU_HALO = 16
N_EXPERTS = 8
TOP_K = 2
RMS_EPS = 1e-6
GN_EPS = 1e-5
SQRT_TINY = 1e-30

V7X_LANES = 128
V7X_SUBLANES = 8
V7X_VMEM_BYTES = 64 * 1024 * 1024
DMA_PRIORITIES = 2
VMEM_LIMIT = (V7X_VMEM_BYTES * 3) // 4
MIXER_VMEM_LIMIT = (V7X_VMEM_BYTES * 29) // 32

PA_W = 2 * RET_QK_W + 2 * RET_V_W + 2 * D_MODEL
PB_W = 2 * LRU_WIDTH


class Geom(NamedTuple):
    n_p: int
    s_p: int
    n_s: int
    s_s: int

    @property
    def tokens_p(self):
        return self.n_p * self.s_p

    @property
    def tokens(self):
        return self.n_p * self.s_p + self.n_s * self.s_s

    @property
    def n_seq(self):
        return self.n_p + self.n_s


def _tile_seq(g, i, tm):
    t0 = i * tm
    return jnp.where(t0 < g.tokens_p, t0 // g.s_p, g.n_p + (t0 - g.tokens_p) // g.s_s)


def _tile_pos(g, i, tm):
    t0 = i * tm
    return jnp.where(t0 < g.tokens_p, t0 % g.s_p, (t0 - g.tokens_p) % g.s_s)


def _tile_seq_len(g, i, tm):
    return jnp.where(i * tm < g.tokens_p, g.s_p, g.s_s)


def _cparams(sem):
    return pltpu.CompilerParams(dimension_semantics=sem, vmem_limit_bytes=VMEM_LIMIT)


def _sigmoid(x):
    return 0.5 * jnp.tanh(0.5 * x) + 0.5


def _silu(x):
    return x * _sigmoid(x)


def _rms_mod(x, gvec, scale, shift):
    ms = jnp.mean(x * x, axis=-1, keepdims=True)
    y = x * lax.rsqrt(ms + RMS_EPS) * gvec
    return y * (1.0 + scale) + shift


def _mod_kernel(c_ref, w_ref, b_ref, o_ref):
    c = c_ref[...]
    o_ref[...] = jnp.dot(_silu(c), w_ref[...], preferred_element_type=F32,
                         precision=lax.Precision.HIGHEST) + b_ref[...]


def _adaln_mod(c_pad, w_ada, b_ada):
    depth, d, _ = w_ada.shape
    n = c_pad.shape[0]
    out = pl.pallas_call(
        _mod_kernel,
        grid=(depth, 6),
        in_specs=[
            pl.BlockSpec((n, d), lambda l, j: (0, 0)),
            pl.BlockSpec((None, d, d), lambda l, j: (l, 0, j)),
            pl.BlockSpec((None, None, 1, d), lambda l, j: (l, j, 0, 0)),
        ],
        out_specs=pl.BlockSpec((None, None, n, d), lambda l, j: (l, j, 0, 0)),
        out_shape=jax.ShapeDtypeStruct((depth, 6, n, d), F32),
        compiler_params=_cparams(("arbitrary", "arbitrary")),
        name="adaln_mod",
    )(c_pad, w_ada, b_ada.reshape(depth, 6, 1, d))
    return out.transpose(0, 2, 1, 3)


def _token_parts(geom, x, tm):
    parts = tuple(p for p in x if p.shape[0]) if isinstance(x, tuple) else (x,)
    d = parts[0].shape[1]
    if len(parts) == 1:
        return parts, [pl.BlockSpec((tm, d), lambda i, *_: (i, 0))], None
    first = geom.tokens_p // tm
    specs = [pl.BlockSpec((tm, d), lambda i, *_: (jnp.minimum(i, first - 1), 0)),
             pl.BlockSpec((tm, d), lambda i, *_: (jnp.maximum(i - first, 0), 0))]
    return parts, specs, first


def _token_tile(x_refs, first):
    if len(x_refs) == 1:
        return x_refs[0][...]
    return jnp.where(pl.program_id(0) < first, x_refs[0][...], x_refs[1][...])


def _norm_proj_kernel(*refs, n_x, first, shift_row, scale_row):
    x_refs = refs[:n_x]
    mod_ref, g_ref, w_ref, o_ref, h_ref = refs[n_x:]

    @pl.when(pl.program_id(1) == 0)
    def _():
        h = _rms_mod(_token_tile(x_refs, first), g_ref[...], mod_ref[scale_row:scale_row + 1, :],
                     mod_ref[shift_row:shift_row + 1, :])
        h_ref[...] = h.astype(BF16)

    o_ref[...] = jnp.dot(h_ref[...], w_ref[...], preferred_element_type=F32).astype(o_ref.dtype)


def _norm_proj(geom, x, mod_l, gvec, w, *, shift_row, scale_row, tm, tn):
    parts, x_specs, first = _token_parts(geom, x, tm)
    t, d = geom.tokens, parts[0].shape[1]
    n = w.shape[1]
    return pl.pallas_call(
        functools.partial(_norm_proj_kernel, n_x=len(parts), first=first, shift_row=shift_row,
                          scale_row=scale_row),
        grid=(t // tm, n // tn),
        in_specs=x_specs + [
            pl.BlockSpec((None, 6, d), lambda i, j: (_tile_seq(geom, i, tm), 0, 0)),
            pl.BlockSpec((1, d), lambda i, j: (0, 0)),
            pl.BlockSpec((d, tn), lambda i, j: (0, j), **({"pipeline_mode": pl.Buffered(1)} if tn == n else {})),
        ],
        out_specs=pl.BlockSpec((tm, tn), lambda i, j: (i, j)),
        out_shape=jax.ShapeDtypeStruct((t, n), BF16),
        scratch_shapes=[pltpu.VMEM((tm, d), BF16)],
        compiler_params=_cparams(("parallel", "arbitrary")),
        name="norm_proj",
    )(*parts, mod_l, gvec, w)


def _rot_halves(ref, h, cos, sin):
    a = ref[:, h * RET_DK:h * RET_DK + RET_HALF].astype(F32)
    b = ref[:, h * RET_DK + RET_HALF:(h + 1) * RET_DK].astype(F32)
    return a * cos - b * sin, a * sin + b * cos


def _ret_bwd_state_kernel(k_ref, v_ref, cos_ref, sin_ref, kdb_ref, rb_ref, r_ref, *, geom, chunk, cdec):
    c = pl.num_programs(0) - 1 - pl.program_id(0)
    is_last = _tile_pos(geom, c, chunk) + chunk == _tile_seq_len(geom, c, chunk)

    @pl.when(is_last)
    def _():
        r_ref[...] = jnp.zeros_like(r_ref)

    cos = cos_ref[...]
    sin = sin_ref[...]
    for h in range(RET_HEADS):
        k1, k2 = _rot_halves(k_ref, h, cos, sin)
        dec = kdb_ref[h]
        kd = jnp.concatenate([k1 * dec, k2 * dec], axis=1).astype(BF16)
        vh = v_ref[:, h * RET_DV:(h + 1) * RET_DV].astype(BF16)
        r = r_ref[h]
        rb_ref[h] = r.astype(BF16)
        upd = lax.dot_general(kd, vh, (((0,), (0,)), ((), ())), preferred_element_type=F32)
        r_ref[h] = r * cdec[h] + upd


def _ret_heads(heads, acc, fillers, q_ref, k_ref, v_ref, g_ref, rb_ref, cos_ref, sin_ref, dmat_ref, qdf_ref,
               qdb_ref, kdf_ref, wo_ref, rf_ref, cdec):
    cos = cos_ref[...]
    sin = sin_ref[...]
    pending = iter(fillers)

    def fill():
        piece = next(pending, None)
        if piece is not None:
            piece()

    for h in heads:
        q1, q2 = _rot_halves(q_ref, h, cos, sin)
        k1, k2 = _rot_halves(k_ref, h, cos, sin)
        qr = jnp.concatenate([q1, q2], axis=1).astype(BF16)
        kr = jnp.concatenate([k1, k2], axis=1).astype(BF16)
        vh = v_ref[:, h * RET_DV:(h + 1) * RET_DV].astype(BF16)
        s = lax.dot_general(qr, kr, (((1,), (1,)), ((), ())), preferred_element_type=F32) * dmat_ref[h]
        fill()
        o = jnp.dot(s.astype(BF16), vh, preferred_element_type=F32)
        fill()
        qdf = qdf_ref[h]
        qf = jnp.concatenate([q1 * qdf, q2 * qdf], axis=1).astype(BF16)
        rf = rf_ref[h]
        o = o + jnp.dot(qf, rf.astype(BF16), preferred_element_type=F32)
        fill()
        qdb = qdb_ref[h]
        qb = jnp.concatenate([q1 * qdb, q2 * qdb], axis=1).astype(BF16)
        o = o + jnp.dot(qb, rb_ref[h], preferred_element_type=F32)
        fill()
        kdf = kdf_ref[h]
        kf = jnp.concatenate([k1 * kdf, k2 * kdf], axis=1).astype(BF16)
        rf_ref[h] = rf * cdec[h] + lax.dot_general(kf, vh, (((0,), (0,)), ((), ())),
                                                   preferred_element_type=F32)
        fill()
        oc = o - jnp.mean(o, axis=-1, keepdims=True)
        on = oc * lax.rsqrt(jnp.mean(oc * oc, axis=-1, keepdims=True) + GN_EPS)
        og = (on * _silu(g_ref[:, h * RET_DV:(h + 1) * RET_DV].astype(F32))).astype(BF16)
        part = jnp.dot(og, wo_ref[h * RET_DV:(h + 1) * RET_DV, :], preferred_element_type=F32)
        acc = part if acc is None else acc + part
    for piece in pending:
        piece()
    return acc


def _retention_tables(chunk, s_max):
    log_gamma = jnp.log1p(-jnp.exp2(-5.0 - jnp.arange(RET_HEADS, dtype=F32)))
    idx = jnp.arange(chunk, dtype=F32)
    dist = jnp.abs(idx[:, None] - idx[None, :])
    dmat = jnp.exp(log_gamma[:, None, None] * dist[None])

    def rows(e):
        return jnp.broadcast_to(jnp.exp(log_gamma[:, None] * e[None, :])[:, :, None],
                                (RET_HEADS, chunk, RET_HALF))

    qdf = rows(idx + 1.0)
    qdb = rows(chunk - idx)
    kdf = rows(chunk - 1.0 - idx)
    kdb = rows(idx)
    theta = 1.0 / (ROPE_BASE ** jnp.linspace(0.0, 1.0, RET_HALF, dtype=F32))
    ang = jnp.arange(s_max, dtype=F32)[:, None] * theta[None, :]
    k_scale = RET_DK ** -0.5
    return jnp.cos(ang), jnp.sin(ang), dmat * k_scale, qdf, qdb, kdf * k_scale, kdb * k_scale


def _chunk_decay(chunk):
    lg = np.log1p(-np.exp2(-5.0 - np.arange(RET_HEADS, dtype=np.float32))).astype(np.float32)
    return tuple(float(v) for v in np.exp(lg * np.float32(chunk)).astype(np.float32))


def _retention(geom, proj_a, w_ret_o, tables, *, chunk):
    t = proj_a.shape[0]
    n_chunks = t // chunk
    cos, sin, dmat, qdf, qdb, kdf, kdb = tables
    cdec = _chunk_decay(chunk)
    state_shape = (RET_HEADS, RET_DK, RET_DV)

    def pos_blk(c):
        return _tile_pos(geom, c, chunk) // chunk

    rev = lambda i: n_chunks - 1 - i
    tab_spec = pl.BlockSpec((RET_HEADS, chunk, RET_HALF), lambda i: (0, 0, 0))
    rb = pl.pallas_call(
        functools.partial(_ret_bwd_state_kernel, geom=geom, chunk=chunk, cdec=cdec),
        grid=(n_chunks,),
        in_specs=[
            pl.BlockSpec((chunk, RET_QK_W), lambda i: (rev(i), 1)),
            pl.BlockSpec((chunk, RET_V_W), lambda i: (rev(i), 1)),
            pl.BlockSpec((chunk, RET_HALF), lambda i: (pos_blk(rev(i)), 0)),
            pl.BlockSpec((chunk, RET_HALF), lambda i: (pos_blk(rev(i)), 0)),
            tab_spec,
        ],
        out_specs=pl.BlockSpec((None,) + state_shape, lambda i: (rev(i), 0, 0, 0)),
        out_shape=jax.ShapeDtypeStruct((n_chunks,) + state_shape, BF16),
        scratch_shapes=[pltpu.VMEM(state_shape, F32)],
        compiler_params=_cparams(("arbitrary",)),
        name="ret_bwd_state",
    )(proj_a, proj_a, cos, sin, kdb)

    once = lambda shape: pl.BlockSpec(shape, lambda i: (0,) * len(shape), pipeline_mode=pl.Buffered(1))
    tab_once = once((RET_HEADS, chunk, RET_HALF))
    fwd_specs = [
        pl.BlockSpec((chunk, RET_QK_W), lambda i: (i, 0)),
        pl.BlockSpec((chunk, RET_QK_W), lambda i: (i, 1)),
        pl.BlockSpec((chunk, RET_V_W), lambda i: (i, 1)),
        pl.BlockSpec((chunk, RET_V_W), lambda i: (i, 2)),
        pl.BlockSpec((None,) + state_shape, lambda i: (i, 0, 0, 0)),
        pl.BlockSpec((chunk, RET_HALF), lambda i: (pos_blk(i), 0)),
        pl.BlockSpec((chunk, RET_HALF), lambda i: (pos_blk(i), 0)),
        once((RET_HEADS, chunk, chunk)),
        tab_once, tab_once, tab_once,
        once((RET_V_W, D_MODEL)),
    ]
    fwd_args = (proj_a, proj_a, proj_a, proj_a, rb, cos, sin, dmat, qdf, qdb, kdf, w_ret_o)
    return fwd_args, fwd_specs, cdec, pltpu.VMEM(state_shape, F32)


def _log_sigmoid(x):
    return jnp.minimum(x, 0.0) - jnp.log1p(jnp.exp(-jnp.abs(x)))


LRU_SEGS = V7X_SUBLANES
LRU_LEAD = CONV_LEFT * LRU_SEGS
LRU_TAIL = (CONV_WIDTH - 1 - CONV_LEFT) * LRU_SEGS


def _lru_gate_steps(d, x_ref, prev_ref, next_ref, keep_prev, keep_next, xs_ref, wconv_ref, bconv_ref, wrg_ref,
                    brg_ref, wig_ref, big_ref, lam_ref, a_ref, u_ref):
    tl = x_ref.shape[0]
    seg = tl // LRU_SEGS
    prev_hi = prev_ref[...].astype(F32)[LRU_HALO - V7X_SUBLANES:] * keep_prev
    next_lo = next_ref[...].astype(F32)[:V7X_SUBLANES] * keep_next
    row8 = lax.broadcasted_iota(jnp.int32, (V7X_SUBLANES, LRU_BW), 0)
    last = V7X_SUBLANES - 1
    c = (0.5 * LRU_C * math.log2(math.e)) * _log_sigmoid(lam_ref[d])

    def block(n):
        sl = slice(n * LRU_BW, (n + 1) * LRU_BW)
        x = x_ref[:, sl].astype(F32)
        for s in range(LRU_SEGS):
            xs_ref[n, pl.ds(LRU_LEAD + s, seg, stride=LRU_SEGS), :] = x[s * seg:(s + 1) * seg]
        slab = lambda t: xs_ref[n, LRU_LEAD + t * LRU_SEGS:LRU_LEAD + (t + 1) * LRU_SEGS, :]
        p = prev_hi[:, sl]
        m1 = pltpu.roll(jnp.where(row8 == last, p, slab(seg - 1)), 1, axis=0)
        m2 = pltpu.roll(jnp.where(row8 == last, pltpu.roll(p, 1, axis=0), slab(seg - 2)), 1, axis=0)
        p1 = pltpu.roll(jnp.where(row8 == 0, next_lo[:, sl], slab(0)), last, axis=0)
        xs_ref[n, 0:LRU_SEGS, :] = m2
        xs_ref[n, LRU_SEGS:LRU_LEAD, :] = m1
        xs_ref[n, LRU_LEAD + tl:LRU_LEAD + tl + LRU_TAIL, :] = p1

        xh = bconv_ref[:, sl] + xs_ref[n, 0:tl, :] * wconv_ref[0:1, sl]
        for j in range(1, CONV_WIDTH):
            xh = xh + xs_ref[n, j * LRU_SEGS:j * LRU_SEGS + tl, :] * wconv_ref[j:j + 1, sl]
        xb = xh.astype(BF16)
        th_r = jnp.tanh(jnp.dot(xb, wrg_ref[d, n], preferred_element_type=F32) + brg_ref[d][:, sl])
        th_i = jnp.tanh(jnp.dot(xb, wig_ref[d, n], preferred_element_type=F32) + big_ref[d][:, sl])
        a = jnp.exp2(c[:, sl] * th_r + c[:, sl])
        a_ref[n] = a
        y = 1.0 - a * a
        u_ref[n] = (y * lax.rsqrt(jnp.maximum(y, SQRT_TINY))) * ((th_i + 1.0) * xh)

    return [functools.partial(block, n) for n in range(LRU_BLOCKS)]


def _lru_scan_loop(a_ref, u_ref, hs_ref, as_ref, reverse):
    tl = a_ref.shape[1]
    seg = tl // LRU_SEGS
    slab_shape = (LRU_SEGS, LRU_BW)

    def step(k, carry):
        t = seg - 1 - k if reverse else k
        rows = pl.ds(pl.multiple_of(t * LRU_SEGS, LRU_SEGS), LRU_SEGS)
        hs, decays = [], []
        for n in range(LRU_BLOCKS):
            a = a_ref[n, rows, :]
            h = a * carry[0][n] + u_ref[n, rows, :]
            decay = a * carry[1][n]
            hs_ref[n, rows, :] = h
            as_ref[n, rows, :] = decay
            hs.append(h)
            decays.append(decay)
        return tuple(hs), tuple(decays)

    init = (tuple(jnp.zeros(slab_shape, F32) for _ in range(LRU_BLOCKS)),
            tuple(jnp.ones(slab_shape, F32) for _ in range(LRU_BLOCKS)))
    return lax.fori_loop(0, seg, step, init, unroll=4)


def _lru_scan_finish(ends, hs_ref, as_ref, fin_ref, carry_ref, o_ref, reverse):
    h_end, decay_end = ends
    tl = as_ref.shape[1]
    seg = tl // LRU_SEGS
    order = range(LRU_SEGS - 1, -1, -1) if reverse else range(LRU_SEGS)
    for n in range(LRU_BLOCKS):
        sl = slice(n * LRU_BW, (n + 1) * LRU_BW)
        fin_ref[0] = h_end[n]
        fin_ref[1] = decay_end[n]
        c = carry_ref[n, 0:1, :]
        for s in order:
            fin_ref[2, s:s + 1, :] = c
            c = fin_ref[0, s:s + 1, :] + fin_ref[1, s:s + 1, :] * c
        carry_ref[n, 0:1, :] = c
        for s in range(LRU_SEGS):
            seg_rows = pl.ds(s, seg, stride=LRU_SEGS)
            h = hs_ref[n, seg_rows, :] + as_ref[n, seg_rows, :] * fin_ref[2, s:s + 1, :]
            o_ref[s * seg:(s + 1) * seg, sl] = h.astype(o_ref.dtype)


N_RET_REFS = 12
N_LRU_REFS = 13


def _mixer_kernel(*refs, geom, tl, cdec):
    ret = refs[:N_RET_REFS]
    (xf_ref, xfp_ref, xfn_ref, xb_ref, xbp_ref, xbn_ref, wconv_ref, bconv_ref, wrg_ref, brg_ref, wig_ref, big_ref,
     lam_ref) = refs[N_RET_REFS:N_RET_REFS + N_LRU_REFS]
    (ya_ref, hf_ref, hb_ref, rf_ref, acc_ref, xs_ref, a_ref, u_ref, as_ref, fin_ref, cf_ref,
     cb_ref) = refs[N_RET_REFS + N_LRU_REFS:]
    hs_ref = xs_ref
    i = pl.program_id(0)
    ib = pl.num_programs(0) - 1 - i

    def flags(tile):
        pos = _tile_pos(geom, tile, tl)
        return pos == 0, pos + tl == _tile_seq_len(geom, tile, tl)

    def keep(flag):
        return jnp.where(flag, 0.0, 1.0)

    def gates(d, x_ref, prev_ref, next_ref, first, last):
        return _lru_gate_steps(d, x_ref, prev_ref, next_ref, keep(first), keep(last), xs_ref, wconv_ref, bconv_ref,
                               wrg_ref, brg_ref, wig_ref, big_ref, lam_ref, a_ref, u_ref)

    def heads(which, acc, fillers):
        return _ret_heads(which, acc, fillers, *ret, rf_ref, cdec)

    first_f, last_f = flags(i)
    first_b, last_b = flags(ib)

    @pl.when(first_f)
    def _():
        rf_ref[...] = jnp.zeros_like(rf_ref)
        cf_ref[...] = jnp.zeros_like(cf_ref)

    @pl.when(last_b)
    def _():
        cb_ref[...] = jnp.zeros_like(cb_ref)

    half = RET_HEADS // 2
    acc_ref[...] = heads(range(half), None, gates(0, xf_ref, xfp_ref, xfn_ref, first_f, last_f))
    ends = _lru_scan_loop(a_ref, u_ref, hs_ref, as_ref, reverse=False)
    _lru_scan_finish(ends, hs_ref, as_ref, fin_ref, cf_ref, hf_ref, reverse=False)
    ya_ref[...] = heads(range(half, RET_HEADS), acc_ref[...],
                        gates(1, xb_ref, xbp_ref, xbn_ref, first_b, last_b)).astype(ya_ref.dtype)
    ends = _lru_scan_loop(a_ref, u_ref, hs_ref, as_ref, reverse=True)
    _lru_scan_finish(ends, hs_ref, as_ref, fin_ref, cb_ref, hb_ref, reverse=True)


def _mixer(geom, proj_a, proj_b, w_ret_o, tables, w_conv, b_conv, w_rg, b_rg, w_ig, b_ig, lam, *, tl):
    t = proj_b.shape[0]
    n = t // tl
    r8 = tl // LRU_HALO
    n8 = t // LRU_HALO
    w = LRU_WIDTH
    rev = lambda i: n - 1 - i
    prev_blk = lambda i: jnp.maximum(i * r8 - 1, 0)
    next_blk = lambda i: jnp.minimum((i + 1) * r8, n8 - 1)
    full = lambda a: pl.BlockSpec(a.shape, lambda i: (0,) * a.ndim)
    w_conv = 0.5 * w_conv
    b_conv2 = 0.5 * b_conv.reshape(1, w)
    b_rg3 = 0.5 * b_rg.reshape(2, 1, w)
    b_ig3 = 0.5 * b_ig.reshape(2, 1, w)
    lam3 = lam.reshape(2, 1, w)
    ret_args, ret_specs, cdec, ret_state = _retention(geom, proj_a, w_ret_o, tables, chunk=tl)
    lru_args = (proj_b, proj_b, proj_b, proj_b, proj_b, proj_b, w_conv, b_conv2, w_rg, b_rg3, w_ig, b_ig3, lam3)
    assert (len(ret_args), len(lru_args)) == (N_RET_REFS, N_LRU_REFS)
    return pl.pallas_call(
        functools.partial(_mixer_kernel, geom=geom, tl=tl, cdec=cdec),
        grid=(n,),
        in_specs=ret_specs + [
            pl.BlockSpec((tl, w), lambda i: (i, 0)),
            pl.BlockSpec((LRU_HALO, w), lambda i: (prev_blk(i), 0)),
            pl.BlockSpec((LRU_HALO, w), lambda i: (next_blk(i), 0)),
            pl.BlockSpec((tl, w), lambda i: (rev(i), 0)),
            pl.BlockSpec((LRU_HALO, w), lambda i: (prev_blk(rev(i)), 0)),
            pl.BlockSpec((LRU_HALO, w), lambda i: (next_blk(rev(i)), 0)),
            full(w_conv), full(b_conv2), full(w_rg), full(b_rg3), full(w_ig), full(b_ig3), full(lam3),
        ],
        out_specs=[
            pl.BlockSpec((tl, D_MODEL), lambda i: (i, 0)),
            pl.BlockSpec((tl, w), lambda i: (i, 0)),
            pl.BlockSpec((tl, w), lambda i: (rev(i), 0)),
        ],
        out_shape=[jax.ShapeDtypeStruct((t, D_MODEL), BF16), jax.ShapeDtypeStruct((t, w), BF16),
                   jax.ShapeDtypeStruct((t, w), BF16)],
        scratch_shapes=[ret_state, pltpu.VMEM((tl, D_MODEL), F32),
                        pltpu.VMEM((LRU_BLOCKS, LRU_LEAD + tl + LRU_TAIL, LRU_BW), F32)]
        + [pltpu.VMEM((LRU_BLOCKS, tl, LRU_BW), F32)] * 3
        + [pltpu.VMEM((3, LRU_SEGS, LRU_BW), F32)]
        + [pltpu.VMEM((LRU_BLOCKS, V7X_SUBLANES, LRU_BW), F32)] * 2,
        compiler_params=pltpu.CompilerParams(dimension_semantics=("arbitrary",),
                                             vmem_limit_bytes=MIXER_VMEM_LIMIT),
        name="mixer",
    )(*ret_args, *lru_args)


def _gelu_tanh(x):
    return 0.5 * x * (1.0 + jnp.tanh(math.sqrt(2.0 / math.pi) * (x + 0.044715 * (x * x * x))))


def _merge_kernel(*refs, n_x, first):
    x_refs = refs[:n_x]
    ya_ref, hf_ref, hb_ref, gl_ref, ga_ref, gb_ref, mod_ref, wl_ref, wo_ref, o_ref = refs[n_x:]
    f32 = lambda ref: ref[...].astype(F32)
    y = ((f32(hf_ref) + f32(hb_ref)) * _gelu_tanh(f32(gl_ref))).astype(BF16)
    yb = jnp.dot(y, wl_ref[...], preferred_element_type=F32)
    m = _sigmoid(f32(ga_ref)) * f32(ya_ref) + _sigmoid(f32(gb_ref)) * yb
    mix = jnp.dot(m.astype(BF16), wo_ref[...], preferred_element_type=F32)
    o_ref[...] = _token_tile(x_refs, first) + mod_ref[2:3, :] * mix


def _merge(geom, x, ya, hf, hb, proj_a, proj_b, mod_l, w_lru_o, w_out, *, tm):
    parts, x_specs, first = _token_parts(geom, x, tm)
    t, d = geom.tokens, parts[0].shape[1]
    gate_a_blk = (2 * RET_QK_W + 2 * RET_V_W) // d
    tok = lambda w: pl.BlockSpec((tm, w), lambda i: (i, 0))
    return pl.pallas_call(
        functools.partial(_merge_kernel, n_x=len(parts), first=first),
        grid=(t // tm,),
        in_specs=x_specs + [
            tok(d), tok(LRU_WIDTH), tok(LRU_WIDTH),
            pl.BlockSpec((tm, LRU_WIDTH), lambda i: (i, 1)),
            pl.BlockSpec((tm, d), lambda i: (i, gate_a_blk)),
            pl.BlockSpec((tm, d), lambda i: (i, gate_a_blk + 1)),
            pl.BlockSpec((None, 6, d), lambda i: (_tile_seq(geom, i, tm), 0, 0)),
            pl.BlockSpec((LRU_WIDTH, d), lambda i: (0, 0)),
            pl.BlockSpec((d, d), lambda i: (0, 0)),
        ],
        out_specs=tok(d),
        out_shape=jax.ShapeDtypeStruct((t, d), F32),
        compiler_params=_cparams(("parallel",)),
        name="merge",
    )(*parts, ya, hf, hb, proj_b, proj_a, proj_a, mod_l, w_lru_o, w_out)


def _ffn_kernel(x_ref, mod_ref, g_ref, wg_ref, wu_ref, wd_ref, o_ref, *, col_chunks):
    x = x_ref[...]
    h = _rms_mod(x, g_ref[...], mod_ref[4:5, :], mod_ref[3:4, :]).astype(BF16)
    acc = None
    for c0, c1 in col_chunks:
        a = jnp.dot(h, wg_ref[:, c0:c1], preferred_element_type=F32)
        u = jnp.dot(h, wu_ref[:, c0:c1], preferred_element_type=F32)
        part = jnp.dot((_silu(a) * u).astype(BF16), wd_ref[c0:c1, :], preferred_element_type=F32)
        acc = part if acc is None else acc + part
    o_ref[...] = x + mod_ref[5:6, :] * acc


def _ffn(geom, x, mod_l, gvec, wg, wu, wd, *, tm, tf):
    t, d = x.shape
    ff = wg.shape[1]
    col_chunks = tuple((c, min(c + tf, ff)) for c in range(0, ff, tf))
    resident = lambda shape: pl.BlockSpec(shape, lambda i: (0, 0), pipeline_mode=pl.Buffered(1))
    return pl.pallas_call(
        functools.partial(_ffn_kernel, col_chunks=col_chunks),
        grid=(t // tm,),
        in_specs=[
            pl.BlockSpec((tm, d), lambda i: (i, 0)),
            pl.BlockSpec((None, 6, d), lambda i: (_tile_seq(geom, i, tm), 0, 0)),
            pl.BlockSpec((1, d), lambda i: (0, 0)),
            resident((d, ff)), resident((d, ff)), resident((ff, d)),
        ],
        out_specs=pl.BlockSpec((tm, d), lambda i: (i, 0)),
        out_shape=jax.ShapeDtypeStruct((t, d), F32),
        compiler_params=_cparams(("parallel",)),
        name="ffn_dense",
    )(x, mod_l, gvec, wg, wu, wd)


ROUTE_E0, ROUTE_E1, ROUTE_R0, ROUTE_R1, ROUTE_G0, ROUTE_G1 = range(6)

ROW_TILES = D_MODEL // V7X_LANES


def _store_token_rows(ref, val):
    rows = val.shape[0]
    for j in range(ROW_TILES):
        ref[pl.ds(j, rows, stride=ROW_TILES), :] = val[:, j * V7X_LANES:(j + 1) * V7X_LANES]


def _load_token_rows(ref, rows):
    return jnp.concatenate([ref[pl.ds(j, rows, stride=ROW_TILES), :] for j in range(ROW_TILES)], axis=1)


def _router_kernel(x_ref, mod_ref, g_ref, wr_ref, h_ref, route_ref, cnt_ref, carry_ref):
    i = pl.program_id(0)
    tm = x_ref.shape[0]

    @pl.when(i == 0)
    def _():
        carry_ref[...] = jnp.zeros_like(carry_ref)

    h = _rms_mod(x_ref[...], g_ref[...], mod_ref[4:5, :], mod_ref[3:4, :])
    _store_token_rows(h_ref, h)
    h_hi = h.astype(BF16)
    h_lo = (h - h_hi.astype(F32)).astype(BF16)
    logits = (jnp.dot(h_hi, wr_ref[0], preferred_element_type=F32)
              + jnp.dot(h_lo, wr_ref[0], preferred_element_type=F32)
              + jnp.dot(h_hi, wr_ref[1], preferred_element_type=F32))
    lane = lax.broadcasted_iota(jnp.int32, logits.shape, 1).astype(F32)
    logits = jnp.where(lane < N_EXPERTS, logits, -jnp.inf)
    m1 = jnp.max(logits, axis=-1, keepdims=True)
    i1 = jnp.min(jnp.where(logits == m1, lane, float(V7X_LANES)), axis=-1, keepdims=True)
    rest = jnp.where(lane == i1, -jnp.inf, logits)
    m2 = jnp.max(rest, axis=-1, keepdims=True)
    i2 = jnp.min(jnp.where(rest == m2, lane, float(V7X_LANES)), axis=-1, keepdims=True)
    ex = jnp.exp(m2 - m1)
    g1 = 1.0 / (1.0 + ex)
    g2 = ex / (1.0 + ex)
    sel1 = lane == i1
    sel2 = lane == i2
    onehot = jnp.where(sel1 | sel2, 1.0, 0.0)
    row = lax.broadcasted_iota(jnp.int32, (tm, tm), 0)
    col = lax.broadcasted_iota(jnp.int32, (tm, tm), 1)
    lower = jnp.where(col < row, 1.0, 0.0).astype(BF16)
    before = jnp.dot(lower, onehot.astype(BF16), preferred_element_type=F32) + carry_ref[0:1, :]
    r1 = jnp.sum(jnp.where(sel1, before, 0.0), axis=-1, keepdims=True)
    r2 = jnp.sum(jnp.where(sel2, before, 0.0), axis=-1, keepdims=True)
    out_lane = lax.broadcasted_iota(jnp.int32, route_ref.shape, 1)
    vals = (i1.astype(F32), i2.astype(F32), r1, r2, g1, g2)
    packed = jnp.zeros(route_ref.shape, F32)
    for slot, v in enumerate(vals):
        packed = jnp.where(out_lane == slot, v, packed)
    route_ref[...] = packed
    carry = carry_ref[0:1, :] + jnp.sum(onehot, axis=0, keepdims=True)
    carry_ref[0:1, :] = carry
    cnt_ref[...] = jnp.broadcast_to(carry, cnt_ref.shape).astype(jnp.int32)


def _router(geom, x, mod_l, gvec, w_router, *, tm):
    t, d = x.shape
    w_pad = jnp.pad(w_router, ((0, 0), (0, V7X_LANES - N_EXPERTS)))
    w_hi = w_pad.astype(BF16)
    w_pad = jnp.stack([w_hi, (w_pad - w_hi.astype(F32)).astype(BF16)])
    return pl.pallas_call(
        _router_kernel,
        grid=(t // tm,),
        in_specs=[
            pl.BlockSpec((tm, d), lambda i: (i, 0)),
            pl.BlockSpec((None, 6, d), lambda i: (_tile_seq(geom, i, tm), 0, 0)),
            pl.BlockSpec((1, d), lambda i: (0, 0)),
            pl.BlockSpec((2, d, V7X_LANES), lambda i: (0, 0, 0)),
        ],
        out_specs=[
            pl.BlockSpec((tm * ROW_TILES, V7X_LANES), lambda i: (i, 0)),
            pl.BlockSpec((tm, V7X_LANES), lambda i: (i, 0)),
            pl.BlockSpec((V7X_SUBLANES, V7X_LANES), lambda i: (0, 0)),
        ],
        out_shape=[
            jax.ShapeDtypeStruct((t * ROW_TILES, V7X_LANES), F32),
            jax.ShapeDtypeStruct((t, V7X_LANES), F32),
            jax.ShapeDtypeStruct((V7X_SUBLANES, V7X_LANES), jnp.int32),
        ],
        scratch_shapes=[pltpu.VMEM((V7X_SUBLANES, V7X_LANES), F32)],
        compiler_params=_cparams(("arbitrary",)),
        name="router",
    )(x, mod_l, gvec, w_pad)


def _token_copy(src, src_tok, dst, dst_tok, sem):
    s = pl.multiple_of(src_tok * ROW_TILES, ROW_TILES)
    d = pl.multiple_of(dst_tok * ROW_TILES, ROW_TILES)
    return pltpu.make_async_copy(src.at[pl.ds(s, ROW_TILES), :], dst.at[pl.ds(d, ROW_TILES), :], sem)


def _dispatch_kernel(fill_ref, dest_ref, src_ref, out_hbm, zero_ref, sem, zero_sem, *, td):
    @pl.when(pl.program_id(0) == 0)
    def _():
        zero_ref[...] = jnp.zeros_like(zero_ref)
        for wait in (False, True):
            for e in range(N_EXPERTS):
                def pad_row(r, carry):
                    copy = _token_copy(zero_ref, 0, out_hbm, r, zero_sem)
                    copy.wait() if wait else copy.start()
                    return carry

                lax.fori_loop(fill_ref[0, e], fill_ref[1, e], pad_row, 0)

    def issue(t, carry):
        for k in range(TOP_K):
            _token_copy(src_ref, t, out_hbm, dest_ref[k, t], sem).start(priority=k % DMA_PRIORITIES)
        return carry

    lax.fori_loop(0, td, issue, 0, unroll=8)
    for _ in range(TOP_K):
        pltpu.make_async_copy(src_ref, out_hbm.at[pl.ds(0, td * ROW_TILES), :], sem).wait()


def _dispatch(h2_rows, dest, fill, n_rows, *, td):
    grid_spec = pltpu.PrefetchScalarGridSpec(
        num_scalar_prefetch=1,
        grid=(dest.shape[1] // td,),
        in_specs=[
            pl.BlockSpec((TOP_K, td), lambda i, fill: (0, i), memory_space=pltpu.SMEM),
            pl.BlockSpec((td * ROW_TILES, V7X_LANES), lambda i, fill: (i, 0)),
        ],
        out_specs=pl.BlockSpec(memory_space=pl.ANY),
        scratch_shapes=[pltpu.VMEM((ROW_TILES, V7X_LANES), h2_rows.dtype),
                        pltpu.SemaphoreType.DMA(()), pltpu.SemaphoreType.DMA(())],
    )
    return pl.pallas_call(
        functools.partial(_dispatch_kernel, td=td),
        grid_spec=grid_spec,
        out_shape=jax.ShapeDtypeStruct((n_rows * ROW_TILES, V7X_LANES), h2_rows.dtype),
        compiler_params=_cparams(("arbitrary",)),
        name="moe_dispatch",
    )(fill, dest, h2_rows)


V7X_MXU_COLS = 256


def _experts_kernel(blk_e_ref, nused_ref, x_ref, wg_ref, wu_ref, wd_ref, o_ref, hm_ref, *, ff_chunks):
    del blk_e_ref
    b = pl.program_id(0)
    bm = hm_ref.shape[0]

    @pl.when(b < nused_ref[0])
    def _():
        xb = _load_token_rows(x_ref, bm).astype(BF16)
        for c0, c1 in ff_chunks:
            a = jnp.dot(xb, wg_ref[:, c0:c1], preferred_element_type=F32)
            u = jnp.dot(xb, wu_ref[:, c0:c1], preferred_element_type=F32)
            hm_ref[:, c0:c1] = (_silu(a) * u).astype(BF16)
        for c0 in range(0, D_MODEL, V7X_MXU_COLS):
            y = jnp.dot(hm_ref[...], wd_ref[:, c0:c0 + V7X_MXU_COLS], preferred_element_type=F32)
            for k in range(V7X_MXU_COLS // V7X_LANES):
                o_ref[pl.ds(c0 // V7X_LANES + k, bm, stride=ROW_TILES), :] = y[:, k * V7X_LANES:(k + 1) * V7X_LANES]

    @pl.when(b >= nused_ref[0])
    def _():
        o_ref[...] = jnp.zeros_like(o_ref)


def _experts(xbuf, blk_e, n_used, wg, wu, wd, *, bm, tf):
    p = xbuf.shape[0] // ROW_TILES
    d = D_MODEL
    ff = wg.shape[2]
    ff_chunks = tuple((c, min(c + tf, ff)) for c in range(0, ff, tf))
    rows_blk = (bm * ROW_TILES, V7X_LANES)

    def expert(b, be, nu):
        return be[jnp.minimum(b, nu[0] - 1)]

    resident = lambda shape: pl.BlockSpec((None,) + shape, lambda b, be, nu: (expert(b, be, nu), 0, 0),
                                          pipeline_mode=pl.Buffered(1))
    grid_spec = pltpu.PrefetchScalarGridSpec(
        num_scalar_prefetch=2,
        grid=(p // bm,),
        in_specs=[
            pl.BlockSpec(rows_blk, lambda b, be, nu: (b, 0)),
            resident((d, ff)), resident((d, ff)), resident((ff, d)),
        ],
        out_specs=pl.BlockSpec(rows_blk, lambda b, be, nu: (b, 0)),
        scratch_shapes=[pltpu.VMEM((bm, ff), BF16)],
    )
    return pl.pallas_call(
        functools.partial(_experts_kernel, ff_chunks=ff_chunks),
        grid_spec=grid_spec,
        out_shape=jax.ShapeDtypeStruct(xbuf.shape, F32),
        compiler_params=_cparams(("arbitrary",)),
        name="moe_experts",
    )(blk_e, n_used, xbuf, wg, wu, wd)


def _moe_out_kernel(dest_ref, dest_next_ref, x_ref, route_ref, mod_ref, gf_ref, ybuf_hbm, op_ref, os_ref,
                    ya0_ref, ya1_ref, yb0_ref, yb1_ref, sems, *, final_norm, prompt_tiles):
    i = pl.program_id(0)
    tm = x_ref.shape[0]
    bufs = ((ya0_ref, ya1_ref), (yb0_ref, yb1_ref))

    def gather(d_ref, slot, inline=False):
        def issue(t, carry):
            for k in range(TOP_K):
                _token_copy(ybuf_hbm, d_ref[k, t], bufs[slot][k], t, sems.at[slot]).start(
                    priority=k % DMA_PRIORITIES)
            return carry

        if inline:
            for t in range(tm):
                issue(t, 0)
        else:
            lax.fori_loop(0, tm, issue, 0, unroll=8)

    def wait(slot):
        for k in range(TOP_K):
            pltpu.make_async_copy(ybuf_hbm.at[pl.ds(0, tm * ROW_TILES), :], bufs[slot][k], sems.at[slot]).wait()

    def combine(slot):
        g1 = route_ref[:, ROUTE_G0:ROUTE_G0 + 1]
        g2 = route_ref[:, ROUTE_G1:ROUTE_G1 + 1]
        y = _load_token_rows(bufs[slot][0], tm) * g1 + _load_token_rows(bufs[slot][1], tm) * g2
        x = x_ref[...] + mod_ref[5:6, :] * y
        if final_norm:
            ms = jnp.mean(x * x, axis=-1, keepdims=True)
            x = x * lax.rsqrt(ms + RMS_EPS) * gf_ref[...]

        @pl.when(i < prompt_tiles)
        def _():
            op_ref[...] = x

        @pl.when(i >= prompt_tiles)
        def _():
            os_ref[...] = x

    @pl.when(i == 0)
    def _():
        gather(dest_ref, 0)

    for slot in range(2):
        @pl.when(i % 2 == slot)
        def _():
            wait(slot)
            gather(dest_next_ref, 1 - slot, inline=True)
            combine(slot)

            @pl.when(i + 1 == pl.num_programs(0))
            def _():
                wait(1 - slot)


def _moe_out(geom, x, ybuf, dest, route, mod_l, g_final, *, tm, final_norm):
    t, d = x.shape
    n = t // tm
    prompt_tiles = geom.tokens_p // tm
    y_buf = pltpu.VMEM((tm * ROW_TILES, V7X_LANES), F32)
    return pl.pallas_call(
        functools.partial(_moe_out_kernel, final_norm=final_norm, prompt_tiles=prompt_tiles),
        grid=(n,),
        in_specs=[
            pl.BlockSpec((TOP_K, tm), lambda i: (0, i), memory_space=pltpu.SMEM),
            pl.BlockSpec((TOP_K, tm), lambda i: (0, jnp.minimum(i + 1, n - 1)), memory_space=pltpu.SMEM),
            pl.BlockSpec((tm, d), lambda i: (i, 0)),
            pl.BlockSpec((tm, V7X_LANES), lambda i: (i, 0)),
            pl.BlockSpec((None, 6, d), lambda i: (_tile_seq(geom, i, tm), 0, 0)),
            pl.BlockSpec((1, d), lambda i: (0, 0)),
            pl.BlockSpec(memory_space=pl.ANY),
        ],
        out_specs=[
            pl.BlockSpec((tm, d), lambda i: (jnp.minimum(i, prompt_tiles - 1), 0)),
            pl.BlockSpec((tm, d), lambda i: (jnp.maximum(i - prompt_tiles, 0), 0)),
        ],
        out_shape=[jax.ShapeDtypeStruct((geom.tokens_p, d), F32),
                   jax.ShapeDtypeStruct((t - geom.tokens_p, d), F32)],
        scratch_shapes=[y_buf, y_buf, y_buf, y_buf, pltpu.SemaphoreType.DMA((2,))],
        compiler_params=_cparams(("arbitrary",)),
        name="moe_out",
    )(dest, dest, x, route, mod_l, g_final, ybuf)


def _final_norm_kernel(x_ref, g_ref, o_ref):
    x = x_ref[...]
    o_ref[...] = x * lax.rsqrt(jnp.mean(x * x, axis=-1, keepdims=True) + RMS_EPS) * g_ref[...]


def _final_norm(x, g_final, *, tm):
    t, d = x.shape
    return pl.pallas_call(
        _final_norm_kernel,
        grid=(t // tm,),
        in_specs=[pl.BlockSpec((tm, d), lambda i: (i, 0)), pl.BlockSpec((1, d), lambda i: (0, 0))],
        out_specs=pl.BlockSpec((tm, d), lambda i: (i, 0)),
        out_shape=jax.ShapeDtypeStruct((t, d), F32),
        compiler_params=_cparams(("parallel",)),
        name="final_norm",
    )(x, g_final)


def _moe(geom, x, mod_l, gvec, w_router, wg, wu, wd, g_final, *, tiles, final_norm):
    t, d = x.shape
    bm = tiles["moe_rows"]
    h2, route, counts = _router(geom, x, mod_l, gvec, w_router, tm=tiles["router"])
    counts = counts[0, :N_EXPERTS]
    padded = ((counts + bm - 1) // bm) * bm
    pad_end = jnp.cumsum(padded)
    pad_start = pad_end - padded
    experts = route[:, ROUTE_E0:ROUTE_E1 + 1].astype(jnp.int32)
    ranks = route[:, ROUTE_R0:ROUTE_R1 + 1].astype(jnp.int32)
    dest = (pad_start[experts] + ranks).T
    n_rows = t * TOP_K + N_EXPERTS * bm
    n_blk = n_rows // bm
    blk_start = jnp.arange(n_blk, dtype=jnp.int32) * bm
    blk_e = jnp.minimum(jnp.sum((pad_end[None, :] <= blk_start[:, None]).astype(jnp.int32), axis=1),
                        N_EXPERTS - 1).astype(jnp.int32)
    n_used = (pad_end[-1:] // bm).astype(jnp.int32)
    fill = jnp.stack([pad_start + counts, pad_end.at[N_EXPERTS - 1].set(n_rows)]).astype(jnp.int32)
    xbuf = _dispatch(h2, dest, fill, n_rows, td=tiles["dma_rows"])
    ybuf = _experts(xbuf, blk_e, n_used, wg, wu, wd, bm=bm, tf=tiles["expert_ff"])
    return _moe_out(geom, x, ybuf, dest, route, mod_l, g_final, tm=tiles["combine_rows"], final_norm=final_norm)


def _pick_tiles(geom):
    s = math.gcd(geom.s_p, geom.s_s) if geom.n_p and geom.n_s else (geom.s_p if geom.n_p else geom.s_s)
    return {
        "proj": min(1024, s),
        "proj_a_rows": min(512, s),
        "mixer": min(512, s),
        "token": min(512, s),
        "ffn_rows": min(1024, s),
        "combine_rows": min(512, s),
        "ffn_cols": 1536,
        "router": min(512, s),
        "moe_rows": 512,
        "expert_ff": 1792,
        "dma_rows": min(2048, s),
    }


def _trunk(geom, x, c_all, w_ada, b_ada, g_norm1, g_norm2, w_in, w_conv, b_conv, w_rg, b_rg, w_ig, b_ig,
           lru_lambda, w_ret_o, w_lru_o, w_out, w_ff_gate, w_ff_up, w_ff_down,
           w_router, w_e_gate, w_e_up, w_e_down, g_final, tiles):
    depth = w_in.shape[0]
    d = D_MODEL
    n_pad = -(-geom.n_seq // V7X_SUBLANES) * V7X_SUBLANES
    c_pad = jnp.pad(c_all, ((0, n_pad - geom.n_seq), (0, 0)))
    mod = _adaln_mod(c_pad, w_ada, b_ada)
    tables = _retention_tables(tiles["mixer"], max(geom.s_p if geom.n_p else 0, geom.s_s if geom.n_s else 0))

    o_xl = 2 * RET_QK_W + 2 * RET_V_W
    o_ga = o_xl + 2 * LRU_WIDTH
    for l in range(depth):
        wl = w_in[l]
        w_a = jnp.concatenate([wl[:, :o_xl], wl[:, o_ga:]], axis=1).astype(BF16)
        w_b = wl[:, o_xl:o_ga].astype(BF16)
        g1 = g_norm1[l].reshape(1, d)
        g2 = g_norm2[l].reshape(1, d)
        proj_a = _norm_proj(geom, x, mod[l], g1, w_a, shift_row=0, scale_row=1,
                            tm=tiles["proj_a_rows"], tn=PA_W)
        proj_b = _norm_proj(geom, x, mod[l], g1, w_b, shift_row=0, scale_row=1,
                            tm=tiles["proj"], tn=PB_W)
        ya, hf, hb = _mixer(geom, proj_a, proj_b, w_ret_o[l].astype(BF16), tables, w_conv[l], b_conv[l],
                            w_rg[l].astype(BF16), b_rg[l], w_ig[l].astype(BF16), b_ig[l], lru_lambda[l],
                            tl=tiles["mixer"])
        x = _merge(geom, x, ya, hf, hb, proj_a, proj_b, mod[l], w_lru_o[l].astype(BF16),
                   w_out[l].astype(BF16), tm=tiles["token"])
        j = l // 2
        last = l == depth - 1
        if l % 2 == 0:
            x = _ffn(geom, x, mod[l], g2, w_ff_gate[j].astype(BF16), w_ff_up[j].astype(BF16),
                     w_ff_down[j].astype(BF16), tm=tiles["ffn_rows"], tf=tiles["ffn_cols"])
            if last:
                x = _final_norm(x, g_final.reshape(1, d), tm=tiles["token"])
                return x[:geom.tokens_p], x[geom.tokens_p:]
        else:
            parts = _moe(geom, x, mod[l], g2, w_router[j], w_e_gate[j].astype(BF16), w_e_up[j].astype(BF16),
                         w_e_down[j].astype(BF16), g_final.reshape(1, d), tiles=tiles, final_norm=last)
            if last:
                return parts
            x = jnp.concatenate(parts, axis=0)


def kernel(x_prompt, x_sample, c_prompt, c_sample, w_ada, b_ada, g_norm1, g_norm2, w_in, w_conv, b_conv, w_rg, b_rg, w_ig, b_ig, lru_lambda, w_ret_o, w_lru_o, w_out, w_ff_gate, w_ff_up, w_ff_down, w_router, w_e_gate, w_e_up, w_e_down, g_final):
    n_p, s_p, d = x_prompt.shape
    n_s, s_s, _ = x_sample.shape
    geom = Geom(n_p, s_p, n_s, s_s)
    x = (x_prompt.reshape(-1, d), x_sample.reshape(-1, d))
    c_all = jnp.concatenate([c_prompt, c_sample], axis=0)
    yp, ys = _trunk(geom, x, c_all, w_ada, b_ada, g_norm1, g_norm2, w_in, w_conv, b_conv, w_rg, b_rg, w_ig, b_ig,
                    lru_lambda, w_ret_o, w_lru_o, w_out, w_ff_gate, w_ff_up, w_ff_down,
                    w_router, w_e_gate, w_e_up, w_e_down, g_final, _pick_tiles(geom))
    return (yp.reshape(n_p, s_p, d), ys.reshape(n_s, s_s, d))
```
